```python
import jax, jax.numpy as jnp
from jax import lax
import numpy as np

D_MODEL = 1024
BATCH = 8
SEQ = 2048
DEPTH = 1

GRID_W = 64
CTX_LEN = 256
D_RNN = 1024
N_LRU_BLOCKS = 16
LRU_BLOCK = D_RNN // N_LRU_BLOCKS
LRU_C = 8.0
CONV_W = 4
CONV_PAD_LEFT = 2
N_HEADS = 16
HEAD_DIM = 64
D_ATT = N_HEADS * HEAD_DIM
WIN_ROWS = 8
WIN_COLS = 16
Q_BLOCK_COLS = 16
K_BLOCK_COLS = Q_BLOCK_COLS + WIN_COLS
D_FF = 4 * D_MODEL
N_MOD = 6
EPS = 1e-6
NEG = -1e30
D_IN = 2 * D_RNN + 3 * D_ATT + 2 * D_MODEL
SPLITS = (D_RNN, 2 * D_RNN, 2 * D_RNN + D_ATT, 2 * D_RNN + 2 * D_ATT, 2 * D_RNN + 3 * D_ATT)

kernel_name = 'hybrid_rglru_natten_dit_block'


def rms_norm(x, g):
    xf = x.astype(jnp.float32)
    y = xf * lax.rsqrt(jnp.mean(xf * xf, axis=-1, keepdims=True) + EPS)
    return (y * g.astype(jnp.float32)).astype(x.dtype)


def modulate(x, g, shift, scale):
    return rms_norm(x, g) * (1.0 + scale) + shift


def short_conv(x, w, b):
    L = x.shape[1]
    xp = jnp.pad(x, ((0, 0), (CONV_PAD_LEFT, CONV_W - 1 - CONV_PAD_LEFT), (0, 0)))
    out = b
    for k in range(CONV_W):
        out = out + w[k] * xp[:, k:k + L]
    return out


def rglru_coeffs(xc, w_rg, b_rg, lam):
    Bn, L, _ = xc.shape
    xb = xc.reshape(Bn, L, N_LRU_BLOCKS, LRU_BLOCK)
    gates = jnp.einsum('blhi,ghij->gblhj', xb, w_rg).reshape(2, Bn, L, D_RNN)
    gates = jax.nn.sigmoid(gates + b_rg[:, None, None, :])
    r, i = gates[0], gates[1]
    log_a = -LRU_C * r * jax.nn.softplus(-lam)
    a = jnp.exp(log_a)
    b = jnp.sqrt(-jnp.expm1(2.0 * log_a)) * (i * xc)
    return a, b


def _lin_combine(e, l):
    a1, b1 = e
    a2, b2 = l
    return a1 * a2, a2 * b1 + b2


def linear_scan(a, b, h0, reverse):
    idx = -1 if reverse else 0
    b = b.at[:, idx].add(a[:, idx] * h0)
    _, h = lax.associative_scan(_lin_combine, (a, b), axis=1, reverse=reverse)
    return h


def rglru_branch(xr, xr_c, conv_w, conv_b, w_rg, b_rg, lam):
    xc = short_conv(xr, conv_w, conv_b)
    xc_c = short_conv(xr_c, conv_w, conv_b)
    h_lat = jnp.zeros_like(xc)
    h_ctx = jnp.zeros_like(xc_c)
    for d, rev in enumerate((False, True)):
        a_c, b_c = rglru_coeffs(xc_c, w_rg[d], b_rg[d], lam[d])
        hc = linear_scan(a_c, b_c, jnp.zeros_like(b_c[:, 0]), rev)
        h_final = hc[:, 0] if rev else hc[:, -1]
        a_l, b_l = rglru_coeffs(xc, w_rg[d], b_rg[d], lam[d])
        h_lat = h_lat + linear_scan(a_l, b_l, h_final, rev)
        h_ctx = h_ctx + hc
    return h_lat, h_ctx


def na_attention(q, k, v, k_c, v_c, rpb):
    Bn, S = q.shape[0], q.shape[1]
    rows = S // GRID_W
    wr = min(WIN_ROWS, rows)
    n_cb = GRID_W // Q_BLOCK_COLS
    scale = HEAD_DIM ** -0.5
    qg = (q * scale).reshape(Bn, rows, GRID_W, N_HEADS, HEAD_DIM)
    kg = k.reshape(Bn, rows, GRID_W, N_HEADS, HEAD_DIM)
    vg = v.reshape(Bn, rows, GRID_W, N_HEADS, HEAD_DIM)
    q_col = np.arange(GRID_W).reshape(n_cb, Q_BLOCK_COLS)
    col_start = np.clip(q_col - WIN_COLS // 2, 0, GRID_W - WIN_COLS)
    band_start = np.clip(np.arange(n_cb) * Q_BLOCK_COLS - WIN_COLS // 2, 0, GRID_W - K_BLOCK_COLS)
    k_col = band_start[:, None] + np.arange(K_BLOCK_COLS)
    dc = k_col[:, None, :] - q_col[:, :, None]
    col_ok = (k_col[:, None, :] >= col_start[:, :, None]) & (k_col[:, None, :] < col_start[:, :, None] + WIN_COLS)
    col_bias = rpb.astype(jnp.float32)[:, :, np.clip(dc + WIN_COLS - 1, 0, 2 * WIN_COLS - 2)]
    col_bias = jnp.where(col_ok[None, None], col_bias, NEG).transpose(0, 2, 3, 1, 4)

    def row_block(r):
        q_r = lax.dynamic_index_in_dim(qg, r, axis=1, keepdims=False).reshape(Bn, n_cb, Q_BLOCK_COLS, N_HEADS, HEAD_DIM)
        r0 = jnp.clip(r - wr // 2, 0, rows - wr)
        k_blk = lax.dynamic_slice_in_dim(kg, r0, wr, axis=1)[:, :, k_col]
        v_blk = lax.dynamic_slice_in_dim(vg, r0, wr, axis=1)[:, :, k_col]
        dr = r0 + jnp.arange(wr) - r + (WIN_ROWS - 1)
        bias = jnp.take(col_bias, dr, axis=3)
        s_loc = jnp.einsum('bjqhd,bwjkhd->bhjqwk', q_r, k_blk, preferred_element_type=jnp.float32) + bias[None]
        s_loc = s_loc.reshape(Bn, N_HEADS, n_cb, Q_BLOCK_COLS, wr * K_BLOCK_COLS)
        s_ctx = jnp.einsum('bjqhd,bchd->bhjqc', q_r, k_c, preferred_element_type=jnp.float32)
        p = jax.nn.softmax(jnp.concatenate([s_loc, s_ctx], axis=-1), axis=-1)
        p_loc = p[..., :wr * K_BLOCK_COLS].reshape(Bn, N_HEADS, n_cb, Q_BLOCK_COLS, wr, K_BLOCK_COLS).astype(v.dtype)
        p_ctx = p[..., wr * K_BLOCK_COLS:].astype(v.dtype)
        o = jnp.einsum('bhjqwk,bwjkhd->bjqhd', p_loc, v_blk) + jnp.einsum('bhjqc,bchd->bjqhd', p_ctx, v_c)
        return o.reshape(Bn, GRID_W, D_ATT)

    out = lax.map(row_block, jnp.arange(rows))
    return out.transpose(1, 0, 2, 3).reshape(Bn, S, D_ATT)


def ctx_attention(q_c, k_c, v_c):
    s = jnp.einsum('bqhd,bkhd->bhqk', q_c * HEAD_DIM ** -0.5, k_c, preferred_element_type=jnp.float32)
    p = jax.nn.softmax(s, axis=-1).astype(v_c.dtype)
    o = jnp.einsum('bhqk,bkhd->bqhd', p, v_c)
    return o.reshape(o.shape[0], o.shape[1], D_ATT)


def heads(t):
    return t.reshape(t.shape[0], t.shape[1], N_HEADS, HEAD_DIM)


def token_mixer(u, u_c, w_in, b_gate, conv_w, conv_b, w_rg, b_rg, lam, rpb, w_lru_out, w_na_out, w_o, ctx_out):
    xr, gr, q, k, v, gl = jnp.split(u @ w_in, SPLITS, axis=-1)
    xr_c, gr_c, q_c, k_c, v_c, gl_c = jnp.split(u_c @ w_in, SPLITS, axis=-1)
    h_lat, h_ctx = rglru_branch(xr, xr_c, conv_w, conv_b, w_rg, b_rg, lam)
    y_lru = (h_lat * jax.nn.gelu(gr)) @ w_lru_out
    y_na = na_attention(heads(q), heads(k), heads(v), heads(k_c), heads(v_c), rpb) @ w_na_out
    g_lru, g_na = jnp.split(jax.nn.sigmoid(gl + b_gate), 2, axis=-1)
    y = (g_lru * y_lru + g_na * y_na) @ w_o
    if not ctx_out:
        return y, None
    y_lru_c = (h_ctx * jax.nn.gelu(gr_c)) @ w_lru_out
    y_na_c = ctx_attention(heads(q_c), heads(k_c), heads(v_c)) @ w_na_out
    gc_lru, gc_na = jnp.split(jax.nn.sigmoid(gl_c + b_gate), 2, axis=-1)
    y_c = (gc_lru * y_lru_c + gc_na * y_na_c) @ w_o
    return y, y_c


def squared_relu_mlp(u, w1, w2):
    return jnp.square(jax.nn.relu(u @ w1)) @ w2


def setup_inputs(seed: int = 0) -> dict:
    key = jax.random.key(seed)
    ks = jax.random.split(key, 24)
    f32 = jnp.float32

    def nrm(k, shape, fan_in, gain=1.0):
        return gain * jax.random.normal(k, shape, f32) * fan_in ** -0.5

    u = jax.random.uniform(ks[13], (DEPTH, 2, D_RNN), f32, 0.9, 0.999)
    s = u ** (1.0 / LRU_C)
    lam = jnp.log(s) - jnp.log1p(-s)
    return {
        'x': jax.random.normal(ks[0], (BATCH, SEQ, D_MODEL), f32),
        'c': jax.random.normal(ks[1], (BATCH, D_MODEL), f32),
        'ctx': jax.random.normal(ks[2], (BATCH, CTX_LEN, D_MODEL), f32),
        'c_ctx': jax.random.normal(ks[3], (D_MODEL,), f32),
        'w_ada': nrm(ks[4], (DEPTH, D_MODEL, N_MOD * D_MODEL), D_MODEL, 0.5),
        'b_ada': 0.02 * jax.random.normal(ks[5], (DEPTH, N_MOD * D_MODEL), f32),
        'g_norm': 1.0 + 0.05 * jax.random.normal(ks[6], (DEPTH, 4, D_MODEL), f32),
        'w_in': nrm(ks[7], (DEPTH, D_MODEL, D_IN), D_MODEL),
        'b_gate': 0.02 * jax.random.normal(ks[8], (DEPTH, 2 * D_MODEL), f32),
        'conv_w': nrm(ks[9], (DEPTH, CONV_W, D_RNN), CONV_W),
        'conv_b': 0.02 * jax.random.normal(ks[10], (DEPTH, D_RNN), f32),
        'w_rg': nrm(ks[11], (DEPTH, 2, 2, N_LRU_BLOCKS, LRU_BLOCK, LRU_BLOCK), LRU_BLOCK),
        'b_rg': 0.02 * jax.random.normal(ks[12], (DEPTH, 2, 2, D_RNN), f32),
        'lam': lam,
        'rpb': 0.1 * jax.random.normal(ks[14], (DEPTH, N_HEADS, 2 * WIN_ROWS - 1, 2 * WIN_COLS - 1), f32),
        'w_lru_out': nrm(ks[15], (DEPTH, D_RNN, D_MODEL), D_RNN),
        'w_na_out': nrm(ks[16], (DEPTH, D_ATT, D_MODEL), D_ATT),
        'w_o': nrm(ks[17], (DEPTH, D_MODEL, D_MODEL), D_MODEL),
        'w_mlp1': nrm(ks[18], (DEPTH, D_MODEL, D_FF), D_MODEL),
        'w_mlp2': nrm(ks[19], (DEPTH, D_FF, D_MODEL), D_FF),
    }


def reference(x, c, ctx, c_ctx, w_ada, b_ada, g_norm, w_in, b_gate, conv_w, conv_b, w_rg, b_rg, lam, rpb,
              w_lru_out, w_na_out, w_o, w_mlp1, w_mlp2):
    for l in range(DEPTH):
        update_ctx = l < DEPTH - 1
        mod = jax.nn.silu(c) @ w_ada[l] + b_ada[l]
        sh1, sc1, gt1, sh2, sc2, gt2 = [m[:, None, :] for m in jnp.split(mod, N_MOD, axis=-1)]
        mod_c = jax.nn.silu(c_ctx) @ w_ada[l] + b_ada[l]
        sh1c, sc1c, gt1c, sh2c, sc2c, gt2c = jnp.split(mod_c, N_MOD, axis=-1)
        u = modulate(x, g_norm[l, 0], sh1, sc1)
        u_c = modulate(ctx, g_norm[l, 0], sh1c, sc1c)
        y, y_c = token_mixer(u, u_c, w_in[l], b_gate[l], conv_w[l], conv_b[l], w_rg[l], b_rg[l], lam[l], rpb[l],
                             w_lru_out[l], w_na_out[l], w_o[l], update_ctx)
        x = x + gt1 * rms_norm(y, g_norm[l, 1])
        u = modulate(x, g_norm[l, 2], sh2, sc2)
        x = x + gt2 * rms_norm(squared_relu_mlp(u, w_mlp1[l], w_mlp2[l]), g_norm[l, 3])
        if update_ctx:
            ctx = ctx + gt1c * rms_norm(y_c, g_norm[l, 1])
            u_c = modulate(ctx, g_norm[l, 2], sh2c, sc2c)
            ctx = ctx + gt2c * rms_norm(squared_relu_mlp(u_c, w_mlp1[l], w_mlp2[l]), g_norm[l, 3])
    return x
```

```python
import functools

import numpy as np
import jax
import jax.numpy as jnp
from jax import lax
from jax.experimental import pallas as pl
from jax.experimental.pallas import tpu as pltpu

F32 = jnp.float32
BF16 = jnp.bfloat16

EPS = 1e-6
NEG = -1e30
LRU_C = 8.0
N_HEADS = 16
HEAD_DIM = 64
GRID_W = 64
WIN_ROWS = 8
WIN_COLS = 16
N_LRU_BLOCKS = 16
CONV_W = 4
CONV_PAD_LEFT = 2

LANES = 128
SUBLANES = 8
MXU_DIM = 256
VMEM_LIMIT = 56 * 1024 * 1024

ATT_ROWS = 4
ATT_Q = ATT_ROWS * GRID_W
ATT_SLOTS = 3 * ATT_ROWS


def _rms(x, g):
    return x * lax.rsqrt(jnp.mean(x * x, axis=-1, keepdims=True) + EPS) * g


def _mod_kernel(c_ref, w_ref, b_ref, o_ref):
    c = c_ref[...]
    s = c * jax.nn.sigmoid(c)
    o_ref[...] = jnp.dot(s, w_ref[...], preferred_element_type=F32) + b_ref[...]


def _mod_call(cs, w_ada, b_ada):
    rows, d = cs.shape
    n_out = w_ada.shape[1]
    return pl.pallas_call(
        _mod_kernel,
        grid=(n_out // d,),
        in_specs=[
            pl.BlockSpec((rows, d), lambda n: (0, 0)),
            pl.BlockSpec((d, d), lambda n: (0, n)),
            pl.BlockSpec((1, d), lambda n: (0, n)),
        ],
        out_specs=pl.BlockSpec((rows, d), lambda n: (0, n)),
        out_shape=jax.ShapeDtypeStruct((rows, n_out), F32),
        compiler_params=pltpu.CompilerParams(
            dimension_semantics=("arbitrary",), vmem_limit_bytes=VMEM_LIMIT),
        name="mod",
    )(cs, w_ada, b_ada)


def _inproj_kernel(x_ref, sh_ref, sc_ref, g_ref, w_ref, bg_ref, xr_ref, pr_ref, u_ref, *, segs):
    n = pl.program_id(1)

    @pl.when(n == 0)
    def _():
        u = _rms(x_ref[...], g_ref[...]) * (1.0 + sc_ref[0]) + sh_ref[0]
        u_ref[...] = u.astype(BF16)

    acc = jnp.dot(u_ref[...], w_ref[...], preferred_element_type=F32)

    for idx, seg in enumerate(segs):
        @pl.when(n == idx)
        def _(seg=seg):
            if seg == 0:
                xr_ref[...] = acc
            elif seg == 1:
                pr_ref[...] = jax.nn.gelu(acc).astype(BF16)
            elif seg == 2:
                pr_ref[...] = (acc * HEAD_DIM ** -0.5).astype(BF16)
            elif seg in (3, 4):
                pr_ref[...] = acc.astype(BF16)
            else:
                pr_ref[...] = jax.nn.sigmoid(acc + bg_ref[0]).astype(BF16)


def _inproj_call(x2, mod3, mod_row_of_tile, g0, w_in_bf, b_gate3, *, seq, tm, segs):
    tokens, d = x2.shape
    n_tb = seq // tm
    n_b = tokens // seq
    segs = tuple(segs)
    seg_arr = segs

    def wcol(t, n):
        col = n
        if seg_arr != tuple(range(len(seg_arr))):
            col = jnp.where(n == 0, 0, n + (seg_arr[1] - 1))
        return (0, col)

    kern = functools.partial(_inproj_kernel, segs=segs)
    return pl.pallas_call(
        kern,
        grid=(tokens // tm, len(segs)),
        in_specs=[
            pl.BlockSpec((tm, d), lambda t, n: (t, 0)),
            pl.BlockSpec((1, 1, d), lambda t, n: (mod_row_of_tile(t), 0, 0)),
            pl.BlockSpec((1, 1, d), lambda t, n: (mod_row_of_tile(t), 0, 1)),
            pl.BlockSpec((1, d), lambda t, n: (0, 0)),
            pl.BlockSpec((d, d), wcol),
            pl.BlockSpec((1, 1, d), lambda t, n: (jnp.clip(n - 5, 0, 1), 0, 0)),
        ],
        out_specs=[
            pl.BlockSpec((tm, d), lambda t, n: (t % n_tb, t // n_tb)),
            pl.BlockSpec((tm, d), lambda t, n: (t, jnp.maximum(n - 1, 0))),
        ],
        out_shape=[
            jax.ShapeDtypeStruct((seq, n_b * d), F32),
            jax.ShapeDtypeStruct((tokens, (len(segs) - 1) * d), BF16),
        ],
        scratch_shapes=[pltpu.VMEM((tm, d), BF16)],
        compiler_params=pltpu.CompilerParams(
            dimension_semantics=("arbitrary", "arbitrary"), vmem_limit_bytes=VMEM_LIMIT),
        name="inproj",
    )(x2, mod3, mod3, g0, w_in_bf, b_gate3)


def _scan_kernel(prev_ref, main_ref, next_ref, cw_ref, cb_ref, wbd_ref, brg_ref, lam_ref, h0_ref,
                 h_ref, a_s, b_s, hst, *, n_chunks, tc):
    d = pl.program_id(0)
    i = pl.program_id(1)
    ci = jnp.where(d == 0, i, n_chunks - 1 - i)

    @pl.when(i == 0)
    def _():
        hst[...] = h0_ref[0]

    has_prev = (ci > 0).astype(F32)
    has_next = (ci < n_chunks - 1).astype(F32)
    n_col = a_s.shape[-1] // MXU_DIM
    for c in range(n_col):
        sl = slice(c * MXU_DIM, (c + 1) * MXU_DIM)
        xe = jnp.concatenate(
            [prev_ref[:, :, sl] * has_prev, main_ref[:, :, sl], next_ref[:, :, sl] * has_next], axis=0)
        xc = cb_ref[:, sl]
        for k in range(CONV_W):
            xc = xc + cw_ref[k:k + 1, sl] * xe[k:k + tc]
        xc2 = xc.reshape(tc * SUBLANES, MXU_DIM)
        g = jnp.dot(xc2.astype(BF16), wbd_ref[0, c], preferred_element_type=F32) + brg_ref[0, c]
        r = jax.nn.sigmoid(g[:, :MXU_DIM])
        gi = jax.nn.sigmoid(g[:, MXU_DIM:])
        z = -lam_ref[0, :, sl]
        softplus = jnp.maximum(z, 0.0) + jnp.log1p(jnp.exp(-jnp.abs(z)))
        log_a = -LRU_C * r * softplus
        a = jnp.exp(log_a)
        one_minus_a2 = -jnp.tanh(log_a) * (a * a + 1.0)
        bb = jnp.sqrt(one_minus_a2) * (gi * xc2)
        a_s[:, :, sl] = a.reshape(tc, SUBLANES, MXU_DIM)
        b_s[:, :, sl] = bb.reshape(tc, SUBLANES, MXU_DIM)

    def step(j, h):
        t = jnp.where(d == 0, j, tc - 1 - j)
        h = a_s[t] * h + b_s[t]
        h_ref[0, t] = h
        return h

    hst[...] = lax.fori_loop(0, tc, step, hst[...], unroll=8)


def _scan_call(xr3, conv_w, conv_b, wbd, brg, lam3, h0, *, tc):
    seq, n_b, d = xr3.shape
    n_chunks = seq // tc
    n_col = d // MXU_DIM

    def chunk(dd, i):
        return jnp.where(dd == 0, i, n_chunks - 1 - i)

    kern = functools.partial(_scan_kernel, n_chunks=n_chunks, tc=tc)
    return pl.pallas_call(
        kern,
        grid=(2, n_chunks),
        in_specs=[
            pl.BlockSpec((2, n_b, d), lambda dd, i: (jnp.maximum(chunk(dd, i) * (tc // 2) - 1, 0), 0, 0)),
            pl.BlockSpec((tc, n_b, d), lambda dd, i: (chunk(dd, i), 0, 0)),
            pl.BlockSpec((1, n_b, d), lambda dd, i: (jnp.minimum((chunk(dd, i) + 1) * tc, seq - 1), 0, 0)),
            pl.BlockSpec((CONV_W, d), lambda dd, i: (0, 0)),
            pl.BlockSpec((1, d), lambda dd, i: (0, 0)),
            pl.BlockSpec((1, n_col, MXU_DIM, 2 * MXU_DIM), lambda dd, i: (dd, 0, 0, 0)),
            pl.BlockSpec((1, n_col, 1, 2 * MXU_DIM), lambda dd, i: (dd, 0, 0, 0)),
            pl.BlockSpec((1, 1, d), lambda dd, i: (dd, 0, 0)),
            pl.BlockSpec((1, n_b, d), lambda dd, i: (dd, 0, 0)),
        ],
        out_specs=pl.BlockSpec((1, tc, n_b, d), lambda dd, i: (dd, chunk(dd, i), 0, 0)),
        out_shape=jax.ShapeDtypeStruct((2, seq, n_b, d), F32),
        scratch_shapes=[
            pltpu.VMEM((tc, n_b, d), F32),
            pltpu.VMEM((tc, n_b, d), F32),
            pltpu.VMEM((n_b, d), F32),
        ],
        compiler_params=pltpu.CompilerParams(
            dimension_semantics=("arbitrary", "arbitrary"), vmem_limit_bytes=VMEM_LIMIT),
        name="scan",
    )(xr3, xr3, xr3, conv_w, conv_b, wbd, brg, lam3, h0)


def _attn_kernel(q_ref, kp_ref, km_ref, kn_ref, vp_ref, vm_ref, vn_ref, kc_ref, vc_ref, bias_ref, o_ref):
    lane = lax.broadcasted_iota(jnp.int32, (1, LANES), 1)
    lo = lane < HEAD_DIM
    dims = (((1,), (1,)), ((), ()))
    for p in range(N_HEADS * HEAD_DIM // LANES):
        ls = slice(p * LANES, (p + 1) * LANES)
        qp = q_ref[:, ls]
        kloc = jnp.concatenate([kp_ref[:, ls], km_ref[:, ls], kn_ref[:, ls]], axis=0)
        vloc = jnp.concatenate([vp_ref[:, ls], vm_ref[:, ls], vn_ref[:, ls]], axis=0)
        kc = kc_ref[:, ls]
        vc = vc_ref[:, ls]
        outs = []
        for hh in range(LANES // HEAD_DIM):
            msk = lo if hh == 0 else jnp.logical_not(lo)
            qm = jnp.where(msk, qp, jnp.zeros_like(qp))
            s_loc = lax.dot_general(qm, kloc, dims, preferred_element_type=F32)
            s_loc = s_loc + bias_ref[0, p * (LANES // HEAD_DIM) + hh]
            s_ctx = lax.dot_general(qm, kc, dims, preferred_element_type=F32)
            m = jnp.maximum(jnp.max(s_loc, axis=-1, keepdims=True), jnp.max(s_ctx, axis=-1, keepdims=True))
            p_loc = jnp.exp(s_loc - m)
            p_ctx = jnp.exp(s_ctx - m)
            den = jnp.sum(p_loc, axis=-1, keepdims=True) + jnp.sum(p_ctx, axis=-1, keepdims=True)
            o = jnp.dot(p_loc.astype(BF16), vloc, preferred_element_type=F32)
            o = o + jnp.dot(p_ctx.astype(BF16), vc, preferred_element_type=F32)
            outs.append(o * (1.0 / den))
        o_ref[:, ls] = jnp.where(lo, outs[0], outs[1]).astype(BF16)


def _attn_call(proj, projc, bias, *, n_b, seq, ctx_len, q_col, k_col, v_col, kc_col, vc_col):
    d = N_HEADS * HEAD_DIM
    n_i = seq // ATT_Q

    def row(i, b):
        return b * n_i + i

    def row_prev(i, b):
        return b * n_i + jnp.maximum(i - 1, 0)

    def row_next(i, b):
        return b * n_i + jnp.minimum(i + 1, n_i - 1)

    def variant(i, b):
        return (i > 0).astype(jnp.int32) + (i == n_i - 1).astype(jnp.int32)

    blk = (ATT_Q, d)
    return pl.pallas_call(
        _attn_kernel,
        grid=(n_i, n_b),
        in_specs=[
            pl.BlockSpec(blk, lambda i, b: (row(i, b), q_col)),
            pl.BlockSpec(blk, lambda i, b: (row_prev(i, b), k_col)),
            pl.BlockSpec(blk, lambda i, b: (row(i, b), k_col)),
            pl.BlockSpec(blk, lambda i, b: (row_next(i, b), k_col)),
            pl.BlockSpec(blk, lambda i, b: (row_prev(i, b), v_col)),
            pl.BlockSpec(blk, lambda i, b: (row(i, b), v_col)),
            pl.BlockSpec(blk, lambda i, b: (row_next(i, b), v_col)),
            pl.BlockSpec((ctx_len, d), lambda i, b: (b, kc_col)),
            pl.BlockSpec((ctx_len, d), lambda i, b: (b, vc_col)),
            pl.BlockSpec((1, N_HEADS, ATT_Q, ATT_SLOTS * GRID_W), lambda i, b: (variant(i, b), 0, 0, 0)),
        ],
        out_specs=pl.BlockSpec(blk, lambda i, b: (row(i, b), 0)),
        out_shape=jax.ShapeDtypeStruct((n_b * seq, d), BF16),
        compiler_params=pltpu.CompilerParams(
            dimension_semantics=("arbitrary", "arbitrary"), vmem_limit_bytes=VMEM_LIMIT),
        name="attn",
    )(proj, proj, proj, proj, proj, proj, proj, projc, projc, bias)


def _attn_bias(rpb, rows):
    qc = np.arange(GRID_W)[:, None]
    kc = np.arange(GRID_W)[None, :]
    col_start = np.clip(qc - WIN_COLS // 2, 0, GRID_W - WIN_COLS)
    col_ok = (kc >= col_start) & (kc < col_start + WIN_COLS)
    dc_idx = np.clip(kc - qc + WIN_COLS - 1, 0, 2 * WIN_COLS - 2)
    onehot = (dc_idx[None] == np.arange(2 * WIN_COLS - 1)[:, None, None]) & col_ok[None]
    col_tab = jnp.einsum('had,dqk->haqk', rpb.astype(F32), jnp.asarray(onehot, F32),
                         precision=lax.Precision.HIGHEST)
    col_tab = jnp.where(jnp.asarray(col_ok)[None, None], col_tab, NEG)
    neg_blk = jnp.full((rpb.shape[0], GRID_W, GRID_W), NEG, F32)
    wr = min(WIN_ROWS, rows)
    n_i = rows // ATT_ROWS
    variants = []
    for i in (0, 1, n_i - 1):
        q_rows = []
        for rq in range(ATT_ROWS):
            r = ATT_ROWS * i + rq
            r0 = int(np.clip(r - wr // 2, 0, rows - wr))
            slots = []
            for s in range(ATT_SLOTS):
                kr = ATT_ROWS * i - ATT_ROWS + s
                if r0 <= kr < r0 + wr:
                    slots.append(col_tab[:, kr - r + WIN_ROWS - 1])
                else:
                    slots.append(neg_blk)
            q_rows.append(jnp.concatenate(slots, axis=-1))
        variants.append(jnp.concatenate(q_rows, axis=1))
    return jnp.stack(variants, axis=0)


def _out_kernel(hf_ref, hr_ref, ggr_ref, gl_ref, gna_ref, att_ref, x_ref, gt1_ref, sh2_ref, sc2_ref,
                gt2_ref, gn_ref, wl_ref, wn_ref, wo_ref, w1_ref, w2_ref, out_ref, x1_s, u2_s, acc_s,
                *, n_ff):
    j = pl.program_id(1)

    @pl.when(j == 0)
    def _():
        hl = (hf_ref[0] + hr_ref[0]) * ggr_ref[...].astype(F32)
        y_lru = jnp.dot(hl.astype(BF16), wl_ref[...], preferred_element_type=F32)
        y_na = jnp.dot(att_ref[...], wn_ref[...], preferred_element_type=F32)
        y = gl_ref[...].astype(F32) * y_lru + gna_ref[...].astype(F32) * y_na
        y = jnp.dot(y.astype(BF16), wo_ref[...], preferred_element_type=F32)
        x1 = x_ref[...] + gt1_ref[0] * _rms(y, gn_ref[1:2, :])
        x1_s[...] = x1
        u2 = _rms(x1, gn_ref[2:3, :]) * (1.0 + sc2_ref[0]) + sh2_ref[0]
        u2_s[...] = u2.astype(BF16)
        acc_s[...] = jnp.zeros_like(acc_s)

    h1 = jnp.dot(u2_s[...], w1_ref[...], preferred_element_type=F32)
    h1 = jnp.maximum(h1, 0.0)
    h1 = h1 * h1
    acc_s[...] += jnp.dot(h1.astype(BF16), w2_ref[...], preferred_element_type=F32)

    @pl.when(j == n_ff - 1)
    def _():
        out_ref[...] = x1_s[...] + gt2_ref[0] * _rms(acc_s[...], gn_ref[3:4, :])


def _out_call(h2, proj, att, x2, mod3, g_norm, wl, wn, wo, w1, w2, *, seq, tm, tf,
              ggr_col, gl_col, gna_col):
    tokens, d = x2.shape
    d_ff = w1.shape[1]
    n_tb = seq // tm
    n_ff = d_ff // tf

    def tb(t):
        return t % n_tb

    def bb(t):
        return t // n_tb

    def modspec(k):
        return pl.BlockSpec((1, 1, d), lambda t, j: (bb(t), 0, k))

    kern = functools.partial(_out_kernel, n_ff=n_ff)
    tile = (tm, d)
    return pl.pallas_call(
        kern,
        grid=(tokens // tm, n_ff),
        in_specs=[
            pl.BlockSpec((1, tm, d), lambda t, j: (0, tb(t), bb(t))),
            pl.BlockSpec((1, tm, d), lambda t, j: (1, tb(t), bb(t))),
            pl.BlockSpec(tile, lambda t, j: (t, ggr_col)),
            pl.BlockSpec(tile, lambda t, j: (t, gl_col)),
            pl.BlockSpec(tile, lambda t, j: (t, gna_col)),
            pl.BlockSpec(tile, lambda t, j: (t, 0)),
            pl.BlockSpec(tile, lambda t, j: (t, 0)),
            modspec(2), modspec(3), modspec(4), modspec(5),
            pl.BlockSpec((4, d), lambda t, j: (0, 0)),
            pl.BlockSpec((d, d), lambda t, j: (0, 0)),
            pl.BlockSpec((d, d), lambda t, j: (0, 0)),
            pl.BlockSpec((d, d), lambda t, j: (0, 0)),
            pl.BlockSpec((d, tf), lambda t, j: (0, j)),
            pl.BlockSpec((tf, d), lambda t, j: (j, 0)),
        ],
        out_specs=pl.BlockSpec(tile, lambda t, j: (t, 0)),
        out_shape=jax.ShapeDtypeStruct((tokens, d), F32),
        scratch_shapes=[
            pltpu.VMEM(tile, F32),
            pltpu.VMEM(tile, BF16),
            pltpu.VMEM(tile, F32),
        ],
        compiler_params=pltpu.CompilerParams(
            dimension_semantics=("arbitrary", "arbitrary"), vmem_limit_bytes=VMEM_LIMIT),
        name="out",
    )(h2, h2, proj, proj, proj, att, x2, mod3, mod3, mod3, mod3, g_norm, wl, wn, wo, w1, w2)


def _gate_weights(w_rg, b_rg):
    n_dir, n_gate, n_blk, bw, _ = w_rg.shape
    per = MXU_DIM // bw
    n_col = n_blk // per
    w = w_rg.reshape(n_dir, n_gate, n_col, per, bw, bw)
    eye = jnp.eye(per, dtype=w.dtype)
    wbd = jnp.einsum('dgcpij,pq->dgcpiqj', w, eye).reshape(n_dir, n_gate, n_col, MXU_DIM, MXU_DIM)
    wbd = jnp.concatenate([wbd[:, 0], wbd[:, 1]], axis=-1).astype(BF16)
    b = b_rg.reshape(n_dir, n_gate, n_col, MXU_DIM)
    brg = jnp.concatenate([b[:, 0], b[:, 1]], axis=-1)[:, :, None, :]
    return wbd, brg


def kernel(x, c, ctx, c_ctx, w_ada, b_ada, g_norm, w_in, b_gate, conv_w, conv_b, w_rg, b_rg, lam, rpb,
           w_lru_out, w_na_out, w_o, w_mlp1, w_mlp2):
    n_b, seq, d = x.shape
    ctx_len = ctx.shape[1]
    rows = seq // GRID_W
    depth = w_ada.shape[0]
    assert depth == 1 and rows % ATT_ROWS == 0 and n_b == SUBLANES

    l = 0
    cs = jnp.concatenate([c, c_ctx[None], jnp.zeros((2 * SUBLANES - n_b - 1, d), F32)], axis=0)
    mod = _mod_call(cs, w_ada[l], b_ada[l][None])
    mod3 = mod[:, None, :]

    w_in_bf = w_in[l].astype(BF16)
    b_gate3 = b_gate[l].reshape(2, 1, d)
    g0 = g_norm[l, 0:1]

    tm_in = 1024
    n_tb = seq // tm_in
    xr, proj = _inproj_call(x.reshape(n_b * seq, d), mod3, lambda t: t // n_tb, g0, w_in_bf, b_gate3,
                            seq=seq, tm=tm_in, segs=(0, 1, 2, 3, 4, 5, 6))
    xr_c, projc = _inproj_call(ctx.reshape(n_b * ctx_len, d), mod3, lambda t: n_b, g0, w_in_bf, b_gate3,
                               seq=ctx_len, tm=ctx_len, segs=(0, 3, 4))

    wbd, brg = _gate_weights(w_rg[l], b_rg[l])
    lam3 = lam[l][:, None, :]
    cb = conv_b[l][None]
    h_c = _scan_call(xr_c.reshape(ctx_len, n_b, d), conv_w[l], cb, wbd, brg, lam3,
                     jnp.zeros((2, n_b, d), F32), tc=128)
    h0 = jnp.stack([h_c[0, ctx_len - 1], h_c[1, 0]], axis=0)
    h = _scan_call(xr.reshape(seq, n_b, d), conv_w[l], cb, wbd, brg, lam3, h0, tc=128)

    bias = _attn_bias(rpb[l], rows)
    att = _attn_call(proj, projc, bias, n_b=n_b, seq=seq, ctx_len=ctx_len,
                     q_col=1, k_col=2, v_col=3, kc_col=0, vc_col=1)

    out = _out_call(h.reshape(2, seq, n_b * d), proj, att, x.reshape(n_b * seq, d), mod3, g_norm[l],
                    w_lru_out[l].astype(BF16), w_na_out[l].astype(BF16), w_o[l].astype(BF16),
                    w_mlp1[l].astype(BF16), w_mlp2[l].astype(BF16),
                    seq=seq, tm=512, tf=1024, ggr_col=0, gl_col=4, gna_col=5)
    return out.reshape(n_b, seq, d)
```

```python
import functools

import numpy as np
import jax
import jax.numpy as jnp
from jax import lax
from jax.experimental import pallas as pl
from jax.experimental.pallas import tpu as pltpu

F32 = jnp.float32
BF16 = jnp.bfloat16

EPS = 1e-6
NEG = -1e30
LRU_C = 8.0
N_HEADS = 16
HEAD_DIM = 64
GRID_W = 64
WIN_ROWS = 8
WIN_COLS = 16
N_LRU_BLOCKS = 16
CONV_W = 4
CONV_PAD_LEFT = 2

LANES = 128
SUBLANES = 8
MXU_DIM = 256
VMEM_LIMIT = 56 * 1024 * 1024

ATT_ROWS = 4
ATT_Q = ATT_ROWS * GRID_W
ATT_SLOTS = 3 * ATT_ROWS


def _rms(x, g):
    return x * lax.rsqrt(jnp.mean(x * x, axis=-1, keepdims=True) + EPS) * g


def _mod_kernel(c_ref, w_ref, b_ref, o_ref):
    c = c_ref[...]
    s = c * jax.nn.sigmoid(c)
    o_ref[...] = jnp.dot(s, w_ref[...], preferred_element_type=F32) + b_ref[...]


def _mod_call(cs, w_ada, b_ada):
    rows, d = cs.shape
    n_out = w_ada.shape[1]
    return pl.pallas_call(
        _mod_kernel,
        grid=(n_out // d,),
        in_specs=[
            pl.BlockSpec((rows, d), lambda n: (0, 0)),
            pl.BlockSpec((d, d), lambda n: (0, n)),
            pl.BlockSpec((1, d), lambda n: (0, n)),
        ],
        out_specs=pl.BlockSpec((rows, d), lambda n: (0, n)),
        out_shape=jax.ShapeDtypeStruct((rows, n_out), F32),
        compiler_params=pltpu.CompilerParams(
            dimension_semantics=("arbitrary",), vmem_limit_bytes=VMEM_LIMIT),
        name="mod",
    )(cs, w_ada, b_ada)


def _inproj_kernel(x_ref, sh_ref, sc_ref, g_ref, w_ref, bg_ref, xr_ref, pr_ref, *, segs):
    n_b, ts, d = x_ref.shape
    u = _rms(x_ref[...], g_ref[...]) * (1.0 + sc_ref[...]) + sh_ref[...]
    u = u.reshape(n_b * ts, d).astype(BF16)
    for idx, seg in enumerate(segs):
        acc = jnp.dot(u, w_ref[:, seg * d:(seg + 1) * d], preferred_element_type=F32)
        if seg == 0:
            for b in range(n_b):
                for k in range(d // LANES):
                    xr_ref[k, pl.ds(b, ts, stride=n_b), :] = acc[b * ts:(b + 1) * ts, k * LANES:(k + 1) * LANES]
            continue
        if seg == 1:
            val = jax.nn.gelu(acc)
        elif seg == 2:
            val = acc * HEAD_DIM ** -0.5
        elif seg in (3, 4):
            val = acc
        else:
            val = jax.nn.sigmoid(acc + bg_ref[:, (seg - 5) * d:(seg - 4) * d])
        pr_ref[:, :, (idx - 1) * d:idx * d] = val.astype(BF16).reshape(n_b, ts, d)


def _inproj_call(x3, mod3, mod_row, g0, w_in_bf, b_gate2, *, ts, segs):
    n_b, seq, d = x3.shape
    segs = tuple(segs)
    n_out = (len(segs) - 1) * d
    if mod_row is None:
        mod_blk, mod_idx = (n_b, 1, d), 0
    else:
        mod_blk, mod_idx = (1, 1, d), mod_row
    kern = functools.partial(_inproj_kernel, segs=segs)
    return pl.pallas_call(
        kern,
        grid=(seq // ts,),
        in_specs=[
            pl.BlockSpec((n_b, ts, d), lambda t: (0, t, 0)),
            pl.BlockSpec(mod_blk, lambda t: (mod_idx, 0, 0)),
            pl.BlockSpec(mod_blk, lambda t: (mod_idx, 0, 1)),
            pl.BlockSpec((1, d), lambda t: (0, 0)),
            pl.BlockSpec(w_in_bf.shape, lambda t: (0, 0), pipeline_mode=pl.Buffered(1)),
            pl.BlockSpec(b_gate2.shape, lambda t: (0, 0)),
        ],
        out_specs=[
            pl.BlockSpec((d // LANES, ts * n_b, LANES), lambda t: (0, t, 0)),
            pl.BlockSpec((n_b, ts, n_out), lambda t: (0, t, 0)),
        ],
        out_shape=[
            jax.ShapeDtypeStruct((d // LANES, seq * n_b, LANES), F32),
            jax.ShapeDtypeStruct((n_b, seq, n_out), BF16),
        ],
        compiler_params=pltpu.CompilerParams(
            dimension_semantics=("arbitrary",), vmem_limit_bytes=VMEM_LIMIT),
        name="inproj",
    )(x3, mod3, mod3, g0, w_in_bf, b_gate2)


def _scan_kernel(prev_ref, main_ref, next_ref, cw_ref, cb_ref, wbd_ref, brg_ref, lam_ref, h0_ref,
                 h_ref, hfin_ref, a_s, b_s, hst, *, n_chunks, tc):
    d = pl.program_id(0)
    i = pl.program_id(1)
    ci = jnp.where(d == 0, i, n_chunks - 1 - i)

    @pl.when(i == 0)
    def _():
        hst[...] = h0_ref[d]

    has_prev = (ci > 0).astype(F32)
    has_next = (ci < n_chunks - 1).astype(F32)
    n_lc = a_s.shape[0]
    rows = tc * SUBLANES
    per = MXU_DIM // LANES
    for c in range(n_lc // per):
        xcs = []
        for k in range(c * per, (c + 1) * per):
            sl = slice(k * LANES, (k + 1) * LANES)
            xe = jnp.concatenate([prev_ref[k] * has_prev, main_ref[k], next_ref[k] * has_next], axis=0)
            xc = cb_ref[:, sl]
            for j in range(CONV_W):
                xc = xc + cw_ref[j:j + 1, sl] * xe[j * SUBLANES:j * SUBLANES + rows]
            xcs.append(xc)
        xc2 = jnp.concatenate(xcs, axis=1)
        sl = slice(c * MXU_DIM, (c + 1) * MXU_DIM)
        g = jnp.dot(xc2.astype(BF16), wbd_ref[0, c], preferred_element_type=F32) + brg_ref[0, c]
        r = jax.nn.sigmoid(g[:, :MXU_DIM])
        gi = jax.nn.sigmoid(g[:, MXU_DIM:])
        z = -lam_ref[0, :, sl]
        softplus = jnp.maximum(z, 0.0) + jnp.log1p(jnp.exp(-jnp.abs(z)))
        log_a = -LRU_C * r * softplus
        a = jnp.exp(log_a)
        one_minus_a2 = -jnp.tanh(log_a) * (a * a + 1.0)
        bb = jnp.sqrt(one_minus_a2) * (gi * xc2)
        for kk in range(per):
            a_s[c * per + kk] = a[:, kk * LANES:(kk + 1) * LANES]
            b_s[c * per + kk] = bb[:, kk * LANES:(kk + 1) * LANES]

    def step(j, h):
        t = jnp.where(d == 0, j, tc - 1 - j)
        r0 = pl.multiple_of(t * SUBLANES, SUBLANES)
        h = a_s[:, pl.ds(r0, SUBLANES), :] * h + b_s[:, pl.ds(r0, SUBLANES), :]
        h_ref[0, :, pl.ds(r0, SUBLANES), :] = h
        return h

    h_last = lax.fori_loop(0, tc, step, hst[...], unroll=8)
    hst[...] = h_last
    hfin_ref[0] = h_last


def _scan_call(xr, conv_w, conv_b, wbd, brg, lam3, h0, *, tc):
    n_lc, n_rows, _ = xr.shape
    n_b = SUBLANES
    seq = n_rows // n_b
    d = n_lc * LANES
    n_chunks = seq // tc
    n_col = d // MXU_DIM
    rows = tc * n_b

    def chunk(dd, i):
        return jnp.where(dd == 0, i, n_chunks - 1 - i)

    kern = functools.partial(_scan_kernel, n_chunks=n_chunks, tc=tc)
    return pl.pallas_call(
        kern,
        grid=(2, n_chunks),
        in_specs=[
            pl.BlockSpec((n_lc, 2 * n_b, LANES),
                         lambda dd, i: (0, jnp.maximum(chunk(dd, i) * (tc // 2) - 1, 0), 0)),
            pl.BlockSpec((n_lc, rows, LANES), lambda dd, i: (0, chunk(dd, i), 0)),
            pl.BlockSpec((n_lc, n_b, LANES),
                         lambda dd, i: (0, jnp.minimum((chunk(dd, i) + 1) * tc, seq - 1), 0)),
            pl.BlockSpec((CONV_W, d), lambda dd, i: (0, 0)),
            pl.BlockSpec((1, d), lambda dd, i: (0, 0)),
            pl.BlockSpec((1, n_col, MXU_DIM, 2 * MXU_DIM), lambda dd, i: (dd, 0, 0, 0)),
            pl.BlockSpec((1, n_col, 1, 2 * MXU_DIM), lambda dd, i: (dd, 0, 0, 0)),
            pl.BlockSpec((1, 1, d), lambda dd, i: (dd, 0, 0)),
            pl.BlockSpec((2, n_lc, n_b, LANES), lambda dd, i: (0, 0, 0, 0)),
        ],
        out_specs=[
            pl.BlockSpec((1, n_lc, rows, LANES), lambda dd, i: (dd, 0, chunk(dd, i), 0)),
            pl.BlockSpec((1, n_lc, n_b, LANES), lambda dd, i: (dd, 0, 0, 0)),
        ],
        out_shape=[
            jax.ShapeDtypeStruct((2, n_lc, n_rows, LANES), F32),
            jax.ShapeDtypeStruct((2, n_lc, n_b, LANES), F32),
        ],
        scratch_shapes=[
            pltpu.VMEM((n_lc, rows, LANES), F32),
            pltpu.VMEM((n_lc, rows, LANES), F32),
            pltpu.VMEM((n_lc, n_b, LANES), F32),
        ],
        compiler_params=pltpu.CompilerParams(
            dimension_semantics=("arbitrary", "arbitrary"), vmem_limit_bytes=VMEM_LIMIT),
        name="scan",
    )(xr, xr, xr, conv_w, conv_b, wbd, brg, lam3, h0)


def _attn_kernel(q_ref, kp_ref, km_ref, kn_ref, vp_ref, vm_ref, vn_ref, kc_ref, vc_ref, bias_ref, o_ref):
    lane = lax.broadcasted_iota(jnp.int32, (1, LANES), 1)
    lo = lane < HEAD_DIM
    dims = (((1,), (1,)), ((), ()))
    for p in range(N_HEADS * HEAD_DIM // LANES):
        ls = slice(p * LANES, (p + 1) * LANES)
        qp = q_ref[:, ls]
        kloc = jnp.concatenate([kp_ref[:, ls], km_ref[:, ls], kn_ref[:, ls]], axis=0)
        vloc = jnp.concatenate([vp_ref[:, ls], vm_ref[:, ls], vn_ref[:, ls]], axis=0)
        kc = kc_ref[:, ls]
        vc = vc_ref[:, ls]
        outs = []
        for hh in range(LANES // HEAD_DIM):
            msk = lo if hh == 0 else jnp.logical_not(lo)
            qm = jnp.where(msk, qp, jnp.zeros_like(qp))
            s_loc = lax.dot_general(qm, kloc, dims, preferred_element_type=F32)
            s_loc = s_loc + bias_ref[0, p * (LANES // HEAD_DIM) + hh]
            s_ctx = lax.dot_general(qm, kc, dims, preferred_element_type=F32)
            m = jnp.maximum(jnp.max(s_loc, axis=-1, keepdims=True), jnp.max(s_ctx, axis=-1, keepdims=True))
            p_loc = jnp.exp(s_loc - m)
            p_ctx = jnp.exp(s_ctx - m)
            den = jnp.sum(p_loc, axis=-1, keepdims=True) + jnp.sum(p_ctx, axis=-1, keepdims=True)
            o = jnp.dot(p_loc.astype(BF16), vloc, preferred_element_type=F32)
            o = o + jnp.dot(p_ctx.astype(BF16), vc, preferred_element_type=F32)
            outs.append(o * (1.0 / den))
        o_ref[:, ls] = jnp.where(lo, outs[0], outs[1]).astype(BF16)


def _attn_call(proj, projc, bias, *, n_b, seq, ctx_len, q_col, k_col, v_col, kc_col, vc_col):
    d = N_HEADS * HEAD_DIM
    n_i = seq // ATT_Q

    def row(i, b):
        return b * n_i + i

    def row_prev(i, b):
        return b * n_i + jnp.maximum(i - 1, 0)

    def row_next(i, b):
        return b * n_i + jnp.minimum(i + 1, n_i - 1)

    def variant(i, b):
        return (i > 0).astype(jnp.int32) + (i == n_i - 1).astype(jnp.int32)

    blk = (ATT_Q, d)
    return pl.pallas_call(
        _attn_kernel,
        grid=(n_i, n_b),
        in_specs=[
            pl.BlockSpec(blk, lambda i, b: (row(i, b), q_col)),
            pl.BlockSpec(blk, lambda i, b: (row_prev(i, b), k_col)),
            pl.BlockSpec(blk, lambda i, b: (row(i, b), k_col)),
            pl.BlockSpec(blk, lambda i, b: (row_next(i, b), k_col)),
            pl.BlockSpec(blk, lambda i, b: (row_prev(i, b), v_col)),
            pl.BlockSpec(blk, lambda i, b: (row(i, b), v_col)),
            pl.BlockSpec(blk, lambda i, b: (row_next(i, b), v_col)),
            pl.BlockSpec((ctx_len, d), lambda i, b: (b, kc_col)),
            pl.BlockSpec((ctx_len, d), lambda i, b: (b, vc_col)),
            pl.BlockSpec((1, N_HEADS, ATT_Q, ATT_SLOTS * GRID_W), lambda i, b: (variant(i, b), 0, 0, 0)),
        ],
        out_specs=pl.BlockSpec(blk, lambda i, b: (row(i, b), 0)),
        out_shape=jax.ShapeDtypeStruct((n_b * seq, d), BF16),
        compiler_params=pltpu.CompilerParams(
            dimension_semantics=("arbitrary", "arbitrary"), vmem_limit_bytes=VMEM_LIMIT),
        name="attn",
    )(proj, proj, proj, proj, proj, proj, proj, projc, projc, bias)


def _attn_bias(rpb, rows):
    qc = np.arange(GRID_W)[:, None]
    kc = np.arange(GRID_W)[None, :]
    col_start = np.clip(qc - WIN_COLS // 2, 0, GRID_W - WIN_COLS)
    col_ok = (kc >= col_start) & (kc < col_start + WIN_COLS)
    dc_idx = np.clip(kc - qc + WIN_COLS - 1, 0, 2 * WIN_COLS - 2)
    onehot = (dc_idx[None] == np.arange(2 * WIN_COLS - 1)[:, None, None]) & col_ok[None]
    col_tab = jnp.einsum('had,dqk->haqk', rpb.astype(F32), jnp.asarray(onehot, F32),
                         precision=lax.Precision.HIGHEST)
    col_tab = jnp.where(jnp.asarray(col_ok)[None, None], col_tab, NEG)
    neg_blk = jnp.full((rpb.shape[0], GRID_W, GRID_W), NEG, F32)
    wr = min(WIN_ROWS, rows)
    n_i = rows // ATT_ROWS
    variants = []
    for i in (0, 1, n_i - 1):
        q_rows = []
        for rq in range(ATT_ROWS):
            r = ATT_ROWS * i + rq
            r0 = int(np.clip(r - wr // 2, 0, rows - wr))
            slots = []
            for s in range(ATT_SLOTS):
                kr = ATT_ROWS * i - ATT_ROWS + s
                if r0 <= kr < r0 + wr:
                    slots.append(col_tab[:, kr - r + WIN_ROWS - 1])
                else:
                    slots.append(neg_blk)
            q_rows.append(jnp.concatenate(slots, axis=-1))
        variants.append(jnp.concatenate(q_rows, axis=1))
    return jnp.stack(variants, axis=0)


def _out_kernel(hf_ref, hr_ref, ggr_ref, gl_ref, gna_ref, att_ref, x_ref, gt1_ref, sh2_ref, sc2_ref,
                gt2_ref, gn_ref, wl_ref, wn_ref, wo_ref, w1_ref, w2_ref, out_ref):
    n_b, ts, d = x_ref.shape
    rows = n_b * ts
    hl = jnp.stack([
        jnp.concatenate([hf_ref[0, k, pl.ds(b, ts, stride=n_b), :] + hr_ref[0, k, pl.ds(b, ts, stride=n_b), :]
                         for k in range(d // LANES)], axis=1)
        for b in range(n_b)], axis=0)
    hl = hl * ggr_ref[...].astype(F32)
    y_lru = jnp.dot(hl.reshape(rows, d).astype(BF16), wl_ref[...], preferred_element_type=F32)
    y_na = jnp.dot(att_ref[...].reshape(rows, d), wn_ref[...], preferred_element_type=F32)
    y = (gl_ref[...].astype(F32).reshape(rows, d) * y_lru
         + gna_ref[...].astype(F32).reshape(rows, d) * y_na)
    y = jnp.dot(y.astype(BF16), wo_ref[...], preferred_element_type=F32).reshape(n_b, ts, d)
    x1 = x_ref[...] + gt1_ref[...] * _rms(y, gn_ref[1:2, :])
    u2 = _rms(x1, gn_ref[2:3, :]) * (1.0 + sc2_ref[...]) + sh2_ref[...]
    h1 = jnp.dot(u2.reshape(rows, d).astype(BF16), w1_ref[...], preferred_element_type=F32)
    h1 = jnp.maximum(h1, 0.0)
    h1 = h1 * h1
    m = jnp.dot(h1.astype(BF16), w2_ref[...], preferred_element_type=F32).reshape(n_b, ts, d)
    out_ref[...] = x1 + gt2_ref[...] * _rms(m, gn_ref[3:4, :])


def _out_call(h2, proj3, att3, x3, mod3, g_norm, wl, wn, wo, w1, w2, *, ts, ggr_col, gl_col, gna_col):
    n_b, seq, d = x3.shape

    def modspec(k):
        return pl.BlockSpec((n_b, 1, d), lambda t: (0, 0, k))

    def resident(w):
        return pl.BlockSpec(w.shape, lambda t: (0, 0), pipeline_mode=pl.Buffered(1))

    tile = (n_b, ts, d)
    return pl.pallas_call(
        _out_kernel,
        grid=(seq // ts,),
        in_specs=[
            pl.BlockSpec((1, d // LANES, ts * n_b, LANES), lambda t: (0, 0, t, 0)),
            pl.BlockSpec((1, d // LANES, ts * n_b, LANES), lambda t: (1, 0, t, 0)),
            pl.BlockSpec(tile, lambda t: (0, t, ggr_col)),
            pl.BlockSpec(tile, lambda t: (0, t, gl_col)),
            pl.BlockSpec(tile, lambda t: (0, t, gna_col)),
            pl.BlockSpec(tile, lambda t: (0, t, 0)),
            pl.BlockSpec(tile, lambda t: (0, t, 0)),
            modspec(2), modspec(3), modspec(4), modspec(5),
            pl.BlockSpec(g_norm.shape, lambda t: (0, 0)),
            resident(wl), resident(wn), resident(wo), resident(w1), resident(w2),
        ],
        out_specs=pl.BlockSpec(tile, lambda t: (0, t, 0)),
        out_shape=jax.ShapeDtypeStruct((n_b, seq, d), F32),
        compiler_params=pltpu.CompilerParams(
            dimension_semantics=("arbitrary",), vmem_limit_bytes=VMEM_LIMIT),
        name="out",
    )(h2, h2, proj3, proj3, proj3, att3, x3, mod3, mod3, mod3, mod3, g_norm, wl, wn, wo, w1, w2)


def _gate_weights(w_rg, b_rg):
    n_dir, n_gate, n_blk, bw, _ = w_rg.shape
    per = MXU_DIM // bw
    n_col = n_blk // per
    w = w_rg.reshape(n_dir, n_gate, n_col, per, bw, bw)
    eye = jnp.eye(per, dtype=w.dtype)
    wbd = jnp.einsum('dgcpij,pq->dgcpiqj', w, eye).reshape(n_dir, n_gate, n_col, MXU_DIM, MXU_DIM)
    wbd = jnp.concatenate([wbd[:, 0], wbd[:, 1]], axis=-1).astype(BF16)
    b = b_rg.reshape(n_dir, n_gate, n_col, MXU_DIM)
    brg = jnp.concatenate([b[:, 0], b[:, 1]], axis=-1)[:, :, None, :]
    return wbd, brg


def kernel(x, c, ctx, c_ctx, w_ada, b_ada, g_norm, w_in, b_gate, conv_w, conv_b, w_rg, b_rg, lam, rpb,
           w_lru_out, w_na_out, w_o, w_mlp1, w_mlp2):
    n_b, seq, d = x.shape
    ctx_len = ctx.shape[1]
    rows = seq // GRID_W
    depth = w_ada.shape[0]
    assert depth == 1 and rows % ATT_ROWS == 0 and n_b == SUBLANES

    l = 0
    cs = jnp.concatenate([c, c_ctx[None], jnp.zeros((2 * SUBLANES - n_b - 1, d), F32)], axis=0)
    mod = _mod_call(cs, w_ada[l], b_ada[l][None])
    mod3 = mod[:, None, :]

    w_in_bf = w_in[l].astype(BF16)
    b_gate2 = b_gate[l][None]
    g0 = g_norm[l, 0:1]

    xr, proj3 = _inproj_call(x, mod3, None, g0, w_in_bf, b_gate2, ts=64, segs=(0, 1, 2, 3, 4, 5, 6))
    xr_c, projc3 = _inproj_call(ctx, mod3, n_b, g0, w_in_bf, b_gate2, ts=64, segs=(0, 3, 4))

    wbd, brg = _gate_weights(w_rg[l], b_rg[l])
    lam3 = lam[l][:, None, :]
    cb = conv_b[l][None]
    _, h0 = _scan_call(xr_c, conv_w[l], cb, wbd, brg, lam3, jnp.zeros((2, d // LANES, n_b, LANES), F32), tc=128)
    h, _ = _scan_call(xr, conv_w[l], cb, wbd, brg, lam3, h0, tc=128)

    bias = _attn_bias(rpb[l], rows)
    att = _attn_call(proj3.reshape(n_b * seq, -1), projc3.reshape(n_b * ctx_len, -1), bias,
                     n_b=n_b, seq=seq, ctx_len=ctx_len, q_col=1, k_col=2, v_col=3, kc_col=0, vc_col=1)

    return _out_call(h, proj3, att.reshape(n_b, seq, d), x, mod3, g_norm[l],
                     w_lru_out[l].astype(BF16), w_na_out[l].astype(BF16), w_o[l].astype(BF16),
                     w_mlp1[l].astype(BF16), w_mlp2[l].astype(BF16),
                     ts=32, ggr_col=0, gl_col=4, gna_col=5)
```

```python
import functools

import numpy as np
import jax
import jax.numpy as jnp
from jax import lax
from jax.experimental import pallas as pl
from jax.experimental.pallas import tpu as pltpu

F32 = jnp.float32
BF16 = jnp.bfloat16

EPS = 1e-6
NEG = -1e30
LRU_C = 8.0
N_HEADS = 16
HEAD_DIM = 64
GRID_W = 64
WIN_ROWS = 8
WIN_COLS = 16
N_LRU_BLOCKS = 16
CONV_W = 4
CONV_PAD_LEFT = 2

LANES = 128
SUBLANES = 8
MXU_DIM = 256
VMEM_LIMIT = 56 * 1024 * 1024

ATT_ROWS = 4
ATT_Q = ATT_ROWS * GRID_W
ATT_SLOTS = 3 * ATT_ROWS
LOG2E = 1.4426950408889634


def _rms(x, g):
    return x * lax.rsqrt(jnp.mean(x * x, axis=-1, keepdims=True) + EPS) * g


def _mod_kernel(c_ref, w_ref, b_ref, o_ref):
    c = c_ref[...]
    s = c * jax.nn.sigmoid(c)
    o_ref[...] = jnp.dot(s, w_ref[...], preferred_element_type=F32) + b_ref[...]


def _mod_call(cs, w_ada, b_ada):
    rows, d = cs.shape
    n_out = w_ada.shape[1]
    return pl.pallas_call(
        _mod_kernel,
        grid=(n_out // d,),
        in_specs=[
            pl.BlockSpec((rows, d), lambda n: (0, 0)),
            pl.BlockSpec((d, d), lambda n: (0, n)),
            pl.BlockSpec((1, d), lambda n: (0, n)),
        ],
        out_specs=pl.BlockSpec((rows, d), lambda n: (0, n)),
        out_shape=jax.ShapeDtypeStruct((rows, n_out), F32),
        compiler_params=pltpu.CompilerParams(
            dimension_semantics=("arbitrary",), vmem_limit_bytes=VMEM_LIMIT),
        name="mod",
    )(cs, w_ada, b_ada)


def _inproj_kernel(x_ref, sh_ref, sc_ref, g_ref, w_ref, bg_ref, xr_ref, pr_ref, *, segs):
    n_b, ts, d = x_ref.shape
    u = _rms(x_ref[...], g_ref[...]) * (1.0 + sc_ref[...]) + sh_ref[...]
    u = u.reshape(n_b * ts, d).astype(BF16)
    for idx, seg in enumerate(segs):
        acc = jnp.dot(u, w_ref[:, seg * d:(seg + 1) * d], preferred_element_type=F32)
        if seg == 0:
            for b in range(n_b):
                for k in range(d // LANES):
                    xr_ref[k, pl.ds(b, ts, stride=n_b), :] = acc[b * ts:(b + 1) * ts, k * LANES:(k + 1) * LANES]
            continue
        if seg == 1:
            val = jax.nn.gelu(acc)
        elif seg == 2:
            val = acc * (HEAD_DIM ** -0.5 * LOG2E)
        elif seg in (3, 4):
            val = acc
        else:
            val = jax.nn.sigmoid(acc + bg_ref[:, (seg - 5) * d:(seg - 4) * d])
        pr_ref[:, :, (idx - 1) * d:idx * d] = val.astype(BF16).reshape(n_b, ts, d)


def _inproj_call(x3, mod3, mod_row, g0, w_in_bf, b_gate2, *, ts, segs):
    n_b, seq, d = x3.shape
    segs = tuple(segs)
    n_out = (len(segs) - 1) * d
    if mod_row is None:
        mod_blk, mod_idx = (n_b, 1, d), 0
    else:
        mod_blk, mod_idx = (1, 1, d), mod_row
    kern = functools.partial(_inproj_kernel, segs=segs)
    return pl.pallas_call(
        kern,
        grid=(seq // ts,),
        in_specs=[
            pl.BlockSpec((n_b, ts, d), lambda t: (0, t, 0)),
            pl.BlockSpec(mod_blk, lambda t: (mod_idx, 0, 0)),
            pl.BlockSpec(mod_blk, lambda t: (mod_idx, 0, 1)),
            pl.BlockSpec((1, d), lambda t: (0, 0)),
            pl.BlockSpec(w_in_bf.shape, lambda t: (0, 0), pipeline_mode=pl.Buffered(1)),
            pl.BlockSpec(b_gate2.shape, lambda t: (0, 0)),
        ],
        out_specs=[
            pl.BlockSpec((d // LANES, ts * n_b, LANES), lambda t: (0, t, 0)),
            pl.BlockSpec((n_b, ts, n_out), lambda t: (0, t, 0)),
        ],
        out_shape=[
            jax.ShapeDtypeStruct((d // LANES, seq * n_b, LANES), F32),
            jax.ShapeDtypeStruct((n_b, seq, n_out), BF16),
        ],
        compiler_params=pltpu.CompilerParams(
            dimension_semantics=("arbitrary",), vmem_limit_bytes=VMEM_LIMIT),
        name="inproj",
    )(x3, mod3, mod3, g0, w_in_bf, b_gate2)


def _scan_kernel(prev_ref, main_ref, next_ref, cw_ref, cb_ref, wbd_ref, brg_ref, lam_ref, h0_ref,
                 h_ref, hfin_ref, a_s, b_s, hst, *, n_chunks, tc):
    d = pl.program_id(0)
    i = pl.program_id(1)
    ci = jnp.where(d == 0, i, n_chunks - 1 - i)

    @pl.when(i == 0)
    def _():
        hst[...] = h0_ref[d]

    has_prev = (ci > 0).astype(F32)
    has_next = (ci < n_chunks - 1).astype(F32)
    n_lc = a_s.shape[0]
    rows = tc * SUBLANES
    per = MXU_DIM // LANES
    for c in range(n_lc // per):
        xcs = []
        for k in range(c * per, (c + 1) * per):
            sl = slice(k * LANES, (k + 1) * LANES)
            xe = jnp.concatenate([prev_ref[k] * has_prev, main_ref[k], next_ref[k] * has_next], axis=0)
            xc = cb_ref[:, sl]
            for j in range(CONV_W):
                xc = xc + cw_ref[j:j + 1, sl] * xe[j * SUBLANES:j * SUBLANES + rows]
            xcs.append(xc)
        xc2 = jnp.concatenate(xcs, axis=1)
        sl = slice(c * MXU_DIM, (c + 1) * MXU_DIM)
        g = jnp.dot(xc2.astype(BF16), wbd_ref[0, c], preferred_element_type=F32) + brg_ref[0, c]
        r = jax.nn.sigmoid(g[:, :MXU_DIM])
        gi = jax.nn.sigmoid(g[:, MXU_DIM:])
        z = -lam_ref[0, :, sl]
        softplus = jnp.maximum(z, 0.0) + jnp.log1p(jnp.exp(-jnp.abs(z)))
        log_a = -LRU_C * r * softplus
        a = jnp.exp(log_a)
        one_minus_a2 = -jnp.tanh(log_a) * (a * a + 1.0)
        bb = jnp.sqrt(one_minus_a2) * (gi * xc2)
        for kk in range(per):
            a_s[c * per + kk] = a[:, kk * LANES:(kk + 1) * LANES]
            b_s[c * per + kk] = bb[:, kk * LANES:(kk + 1) * LANES]

    def step(j, h):
        t = jnp.where(d == 0, j, tc - 1 - j)
        r0 = pl.multiple_of(t * SUBLANES, SUBLANES)
        h = a_s[:, pl.ds(r0, SUBLANES), :] * h + b_s[:, pl.ds(r0, SUBLANES), :]
        h_ref[0, :, pl.ds(r0, SUBLANES), :] = h
        return h

    h_last = lax.fori_loop(0, tc, step, hst[...], unroll=8)
    hst[...] = h_last
    hfin_ref[0] = h_last


def _scan_call(xr, conv_w, conv_b, wbd, brg, lam3, h0, *, tc):
    n_lc, n_rows, _ = xr.shape
    n_b = SUBLANES
    seq = n_rows // n_b
    d = n_lc * LANES
    n_chunks = seq // tc
    n_col = d // MXU_DIM
    rows = tc * n_b

    def chunk(dd, i):
        return jnp.where(dd == 0, i, n_chunks - 1 - i)

    kern = functools.partial(_scan_kernel, n_chunks=n_chunks, tc=tc)
    return pl.pallas_call(
        kern,
        grid=(2, n_chunks),
        in_specs=[
            pl.BlockSpec((n_lc, 2 * n_b, LANES),
                         lambda dd, i: (0, jnp.maximum(chunk(dd, i) * (tc // 2) - 1, 0), 0)),
            pl.BlockSpec((n_lc, rows, LANES), lambda dd, i: (0, chunk(dd, i), 0)),
            pl.BlockSpec((n_lc, n_b, LANES),
                         lambda dd, i: (0, jnp.minimum((chunk(dd, i) + 1) * tc, seq - 1), 0)),
            pl.BlockSpec((CONV_W, d), lambda dd, i: (0, 0)),
            pl.BlockSpec((1, d), lambda dd, i: (0, 0)),
            pl.BlockSpec((1, n_col, MXU_DIM, 2 * MXU_DIM), lambda dd, i: (dd, 0, 0, 0)),
            pl.BlockSpec((1, n_col, 1, 2 * MXU_DIM), lambda dd, i: (dd, 0, 0, 0)),
            pl.BlockSpec((1, 1, d), lambda dd, i: (dd, 0, 0)),
            pl.BlockSpec((2, n_lc, n_b, LANES), lambda dd, i: (0, 0, 0, 0)),
        ],
        out_specs=[
            pl.BlockSpec((1, n_lc, rows, LANES), lambda dd, i: (dd, 0, chunk(dd, i), 0)),
            pl.BlockSpec((1, n_lc, n_b, LANES), lambda dd, i: (dd, 0, 0, 0)),
        ],
        out_shape=[
            jax.ShapeDtypeStruct((2, n_lc, n_rows, LANES), F32),
            jax.ShapeDtypeStruct((2, n_lc, n_b, LANES), F32),
        ],
        scratch_shapes=[
            pltpu.VMEM((n_lc, rows, LANES), F32),
            pltpu.VMEM((n_lc, rows, LANES), F32),
            pltpu.VMEM((n_lc, n_b, LANES), F32),
        ],
        compiler_params=pltpu.CompilerParams(
            dimension_semantics=("arbitrary", "arbitrary"), vmem_limit_bytes=VMEM_LIMIT),
        name="scan",
    )(xr, xr, xr, conv_w, conv_b, wbd, brg, lam3, h0)


def _attn_kernel(q_ref, kp_ref, km_ref, kn_ref, vp_ref, vm_ref, vn_ref, kc_ref, vc_ref, bias_ref, o_ref, s_buf):
    lane = lax.broadcasted_iota(jnp.int32, (1, LANES), 1)
    lo = lane < HEAD_DIM
    n_loc = bias_ref.shape[2]
    heads_per_pair = LANES // HEAD_DIM
    n_pairs = N_HEADS // heads_per_pair
    dims_nt = (((1,), (1,)), ((), ()))
    dims_tn = (((0,), (0,)), ((), ()))

    def scores(p):
        ls = slice(p * LANES, (p + 1) * LANES)
        qp = q_ref[:, ls]
        k_all = jnp.concatenate([kp_ref[:, ls], km_ref[:, ls], kn_ref[:, ls], kc_ref[:, ls]], axis=0)
        for hh in range(heads_per_pair):
            msk = lo if hh == 0 else jnp.logical_not(lo)
            qm = jnp.where(msk, qp, jnp.zeros_like(qp))
            s_buf[p % 2, hh] = lax.dot_general(k_all, qm, dims_nt, preferred_element_type=F32)

    def softmax_pv(p):
        ls = slice(p * LANES, (p + 1) * LANES)
        v_all = jnp.concatenate([vp_ref[:, ls], vm_ref[:, ls], vn_ref[:, ls], vc_ref[:, ls]], axis=0)
        o_t = []
        for hh in range(heads_per_pair):
            own = lo if hh == 0 else jnp.logical_not(lo)
            s_loc = s_buf[p % 2, hh, :n_loc] + bias_ref[0, p * heads_per_pair + hh]
            s_ctx = s_buf[p % 2, hh, n_loc:]
            m = jnp.maximum(jnp.max(s_loc, axis=0, keepdims=True), jnp.max(s_ctx, axis=0, keepdims=True))
            p_all = jnp.concatenate([jnp.exp2(s_loc - m), jnp.exp2(s_ctx - m)], axis=0).astype(BF16)
            v_own = jnp.where(own, v_all, jnp.ones_like(v_all))
            o = lax.dot_general(v_own, p_all, dims_tn, preferred_element_type=F32)
            den = o[(1 - hh) * HEAD_DIM:(1 - hh) * HEAD_DIM + 1]
            o_t.append(o[hh * HEAD_DIM:(hh + 1) * HEAD_DIM] * (1.0 / den))
        o_ref[:, ls] = jnp.concatenate(o_t, axis=0).T.astype(BF16)

    scores(0)
    for p in range(n_pairs):
        if p + 1 < n_pairs:
            scores(p + 1)
        softmax_pv(p)


def _attn_call(proj, projc, bias, *, n_b, seq, ctx_len, q_col, k_col, v_col, kc_col, vc_col):
    d = N_HEADS * HEAD_DIM
    n_i = seq // ATT_Q

    def row(i, b):
        return b * n_i + i

    def row_prev(i, b):
        return b * n_i + jnp.maximum(i - 1, 0)

    def row_next(i, b):
        return b * n_i + jnp.minimum(i + 1, n_i - 1)

    def variant(i, b):
        return (i > 0).astype(jnp.int32) + (i == n_i - 1).astype(jnp.int32)

    blk = (ATT_Q, d)
    return pl.pallas_call(
        _attn_kernel,
        grid=(n_i, n_b),
        in_specs=[
            pl.BlockSpec(blk, lambda i, b: (row(i, b), q_col)),
            pl.BlockSpec(blk, lambda i, b: (row_prev(i, b), k_col)),
            pl.BlockSpec(blk, lambda i, b: (row(i, b), k_col)),
            pl.BlockSpec(blk, lambda i, b: (row_next(i, b), k_col)),
            pl.BlockSpec(blk, lambda i, b: (row_prev(i, b), v_col)),
            pl.BlockSpec(blk, lambda i, b: (row(i, b), v_col)),
            pl.BlockSpec(blk, lambda i, b: (row_next(i, b), v_col)),
            pl.BlockSpec((ctx_len, d), lambda i, b: (b, kc_col)),
            pl.BlockSpec((ctx_len, d), lambda i, b: (b, vc_col)),
            pl.BlockSpec((1, N_HEADS, ATT_SLOTS * GRID_W, ATT_Q), lambda i, b: (variant(i, b), 0, 0, 0),
                         pipeline_mode=pl.Buffered(1)),
        ],
        out_specs=pl.BlockSpec(blk, lambda i, b: (row(i, b), 0)),
        out_shape=jax.ShapeDtypeStruct((n_b * seq, d), BF16),
        scratch_shapes=[pltpu.VMEM((2, LANES // HEAD_DIM, ATT_SLOTS * GRID_W + ctx_len, ATT_Q), F32)],
        compiler_params=pltpu.CompilerParams(
            dimension_semantics=("arbitrary", "arbitrary"), vmem_limit_bytes=VMEM_LIMIT),
        name="attn",
    )(proj, proj, proj, proj, proj, proj, proj, projc, projc, bias)


def _attn_bias(rpb, rows):
    qc = np.arange(GRID_W)[None, :]
    kc = np.arange(GRID_W)[:, None]
    col_start = np.clip(qc - WIN_COLS // 2, 0, GRID_W - WIN_COLS)
    col_ok = (kc >= col_start) & (kc < col_start + WIN_COLS)
    dc_idx = np.clip(kc - qc + WIN_COLS - 1, 0, 2 * WIN_COLS - 2)
    onehot = (dc_idx[None] == np.arange(2 * WIN_COLS - 1)[:, None, None]) & col_ok[None]
    col_tab = jnp.einsum('had,dkq->hakq', rpb.astype(F32), jnp.asarray(onehot, F32),
                         precision=lax.Precision.HIGHEST)
    col_tab = jnp.where(jnp.asarray(col_ok)[None, None], col_tab * LOG2E, NEG)
    neg_blk = jnp.full((rpb.shape[0], GRID_W, GRID_W), NEG, F32)
    wr = min(WIN_ROWS, rows)
    n_i = rows // ATT_ROWS
    variants = []
    for i in (0, 1, n_i - 1):
        slots = []
        for s in range(ATT_SLOTS):
            kr = ATT_ROWS * i - ATT_ROWS + s
            q_rows = []
            for rq in range(ATT_ROWS):
                r = ATT_ROWS * i + rq
                r0 = int(np.clip(r - wr // 2, 0, rows - wr))
                if r0 <= kr < r0 + wr:
                    q_rows.append(col_tab[:, kr - r + WIN_ROWS - 1])
                else:
                    q_rows.append(neg_blk)
            slots.append(jnp.concatenate(q_rows, axis=-1))
        variants.append(jnp.concatenate(slots, axis=1))
    return jnp.stack(variants, axis=0)


def _out_kernel(hf_ref, hr_ref, ggr_ref, gl_ref, gna_ref, att_ref, x_ref, gt1_ref, sh2_ref, sc2_ref,
                gt2_ref, gn_ref, wl_ref, wn_ref, wo_ref, w1_ref, w2_ref, out_ref):
    n_b, ts, d = x_ref.shape
    rows = n_b * ts
    hl = jnp.stack([
        jnp.concatenate([hf_ref[0, k, pl.ds(b, ts, stride=n_b), :] + hr_ref[0, k, pl.ds(b, ts, stride=n_b), :]
                         for k in range(d // LANES)], axis=1)
        for b in range(n_b)], axis=0)
    hl = hl * ggr_ref[...].astype(F32)
    y_lru = jnp.dot(hl.reshape(rows, d).astype(BF16), wl_ref[...], preferred_element_type=F32)
    y_na = jnp.dot(att_ref[...].reshape(rows, d), wn_ref[...], preferred_element_type=F32)
    y = (gl_ref[...].astype(F32).reshape(rows, d) * y_lru
         + gna_ref[...].astype(F32).reshape(rows, d) * y_na)
    y = jnp.dot(y.astype(BF16), wo_ref[...], preferred_element_type=F32).reshape(n_b, ts, d)
    x1 = x_ref[...] + gt1_ref[...] * _rms(y, gn_ref[1:2, :])
    u2 = _rms(x1, gn_ref[2:3, :]) * (1.0 + sc2_ref[...]) + sh2_ref[...]
    h1 = jnp.dot(u2.reshape(rows, d).astype(BF16), w1_ref[...], preferred_element_type=F32)
    h1 = jnp.maximum(h1, 0.0)
    h1 = h1 * h1
    m = jnp.dot(h1.astype(BF16), w2_ref[...], preferred_element_type=F32).reshape(n_b, ts, d)
    out_ref[...] = x1 + gt2_ref[...] * _rms(m, gn_ref[3:4, :])


def _out_call(h2, proj3, att3, x3, mod3, g_norm, wl, wn, wo, w1, w2, *, ts, ggr_col, gl_col, gna_col):
    n_b, seq, d = x3.shape

    def modspec(k):
        return pl.BlockSpec((n_b, 1, d), lambda t: (0, 0, k))

    def resident(w):
        return pl.BlockSpec(w.shape, lambda t: (0, 0), pipeline_mode=pl.Buffered(1))

    tile = (n_b, ts, d)
    return pl.pallas_call(
        _out_kernel,
        grid=(seq // ts,),
        in_specs=[
            pl.BlockSpec((1, d // LANES, ts * n_b, LANES), lambda t: (0, 0, t, 0)),
            pl.BlockSpec((1, d // LANES, ts * n_b, LANES), lambda t: (1, 0, t, 0)),
            pl.BlockSpec(tile, lambda t: (0, t, ggr_col)),
            pl.BlockSpec(tile, lambda t: (0, t, gl_col)),
            pl.BlockSpec(tile, lambda t: (0, t, gna_col)),
            pl.BlockSpec(tile, lambda t: (0, t, 0)),
            pl.BlockSpec(tile, lambda t: (0, t, 0)),
            modspec(2), modspec(3), modspec(4), modspec(5),
            pl.BlockSpec(g_norm.shape, lambda t: (0, 0)),
            resident(wl), resident(wn), resident(wo), resident(w1), resident(w2),
        ],
        out_specs=pl.BlockSpec(tile, lambda t: (0, t, 0)),
        out_shape=jax.ShapeDtypeStruct((n_b, seq, d), F32),
        compiler_params=pltpu.CompilerParams(
            dimension_semantics=("arbitrary",), vmem_limit_bytes=VMEM_LIMIT),
        name="out",
    )(h2, h2, proj3, proj3, proj3, att3, x3, mod3, mod3, mod3, mod3, g_norm, wl, wn, wo, w1, w2)


def _gate_weights(w_rg, b_rg):
    n_dir, n_gate, n_blk, bw, _ = w_rg.shape
    per = MXU_DIM // bw
    n_col = n_blk // per
    w = w_rg.reshape(n_dir, n_gate, n_col, per, bw, bw)
    eye = jnp.eye(per, dtype=w.dtype)
    wbd = jnp.einsum('dgcpij,pq->dgcpiqj', w, eye).reshape(n_dir, n_gate, n_col, MXU_DIM, MXU_DIM)
    wbd = jnp.concatenate([wbd[:, 0], wbd[:, 1]], axis=-1).astype(BF16)
    b = b_rg.reshape(n_dir, n_gate, n_col, MXU_DIM)
    brg = jnp.concatenate([b[:, 0], b[:, 1]], axis=-1)[:, :, None, :]
    return wbd, brg


def kernel(x, c, ctx, c_ctx, w_ada, b_ada, g_norm, w_in, b_gate, conv_w, conv_b, w_rg, b_rg, lam, rpb,
           w_lru_out, w_na_out, w_o, w_mlp1, w_mlp2):
    n_b, seq, d = x.shape
    ctx_len = ctx.shape[1]
    rows = seq // GRID_W
    depth = w_ada.shape[0]
    assert depth == 1 and rows % ATT_ROWS == 0 and n_b == SUBLANES

    l = 0
    cs = jnp.concatenate([c, c_ctx[None], jnp.zeros((2 * SUBLANES - n_b - 1, d), F32)], axis=0)
    mod = _mod_call(cs, w_ada[l], b_ada[l][None])
    mod3 = mod[:, None, :]

    w_in_bf = w_in[l].astype(BF16)
    b_gate2 = b_gate[l][None]
    g0 = g_norm[l, 0:1]

    xr, proj3 = _inproj_call(x, mod3, None, g0, w_in_bf, b_gate2, ts=64, segs=(0, 1, 2, 3, 4, 5, 6))
    xr_c, projc3 = _inproj_call(ctx, mod3, n_b, g0, w_in_bf, b_gate2, ts=64, segs=(0, 3, 4))

    wbd, brg = _gate_weights(w_rg[l], b_rg[l])
    lam3 = lam[l][:, None, :]
    cb = conv_b[l][None]
    _, h0 = _scan_call(xr_c, conv_w[l], cb, wbd, brg, lam3, jnp.zeros((2, d // LANES, n_b, LANES), F32), tc=128)
    h, _ = _scan_call(xr, conv_w[l], cb, wbd, brg, lam3, h0, tc=128)

    bias = _attn_bias(rpb[l], rows)
    att = _attn_call(proj3.reshape(n_b * seq, -1), projc3.reshape(n_b * ctx_len, -1), bias,
                     n_b=n_b, seq=seq, ctx_len=ctx_len, q_col=1, k_col=2, v_col=3, kc_col=0, vc_col=1)

    return _out_call(h, proj3, att.reshape(n_b, seq, d), x, mod3, g_norm[l],
                     w_lru_out[l].astype(BF16), w_na_out[l].astype(BF16), w_o[l].astype(BF16),
                     w_mlp1[l].astype(BF16), w_mlp2[l].astype(BF16),
                     ts=32, ggr_col=0, gl_col=4, gna_col=5)
```

```python
import functools

import numpy as np
import jax
import jax.numpy as jnp
from jax import lax
from jax.experimental import pallas as pl
from jax.experimental.pallas import tpu as pltpu

F32 = jnp.float32
BF16 = jnp.bfloat16

EPS = 1e-6
NEG = -1e30
LRU_C = 8.0
N_HEADS = 16
HEAD_DIM = 64
GRID_W = 64
WIN_ROWS = 8
WIN_COLS = 16
N_LRU_BLOCKS = 16
CONV_W = 4
CONV_PAD_LEFT = 2

LANES = 128
SUBLANES = 8
MXU_DIM = 256
VMEM_LIMIT = 56 * 1024 * 1024

ATT_ROWS = 4
ATT_Q = ATT_ROWS * GRID_W
ATT_SLOTS = 3 * ATT_ROWS
LOG2E = 1.4426950408889634


def _rms(x, g):
    return x * lax.rsqrt(jnp.mean(x * x, axis=-1, keepdims=True) + EPS) * g


def _mod_kernel(c_ref, w_ref, b_ref, o_ref):
    c = c_ref[...]
    s = c * jax.nn.sigmoid(c)
    o_ref[...] = jnp.dot(s, w_ref[...], preferred_element_type=F32) + b_ref[...]


def _mod_call(cs, w_ada, b_ada):
    rows, d = cs.shape
    n_out = w_ada.shape[1]
    return pl.pallas_call(
        _mod_kernel,
        grid=(n_out // d,),
        in_specs=[
            pl.BlockSpec((rows, d), lambda n: (0, 0)),
            pl.BlockSpec((d, d), lambda n: (0, n)),
            pl.BlockSpec((1, d), lambda n: (0, n)),
        ],
        out_specs=pl.BlockSpec((rows, d), lambda n: (0, n)),
        out_shape=jax.ShapeDtypeStruct((rows, n_out), F32),
        compiler_params=pltpu.CompilerParams(
            dimension_semantics=("arbitrary",), vmem_limit_bytes=VMEM_LIMIT),
        name="mod",
    )(cs, w_ada, b_ada)


def _inproj_kernel(x_ref, sh_ref, sc_ref, g_ref, w_ref, bg_ref, xr_ref, pr_ref, *, segs):
    n_b, ts, d = x_ref.shape
    u = _rms(x_ref[...], g_ref[...]) * (1.0 + sc_ref[...]) + sh_ref[...]
    u = u.reshape(n_b * ts, d).astype(BF16)
    for idx, seg in enumerate(segs):
        acc = jnp.dot(u, w_ref[:, seg * d:(seg + 1) * d], preferred_element_type=F32)
        if seg == 0:
            for b in range(n_b):
                for k in range(d // LANES):
                    xr_ref[k, pl.ds(b, ts, stride=n_b), :] = acc[b * ts:(b + 1) * ts, k * LANES:(k + 1) * LANES]
            continue
        if seg == 1:
            val = jax.nn.gelu(acc)
        elif seg == 2:
            val = acc * (HEAD_DIM ** -0.5 * LOG2E)
        elif seg in (3, 4):
            val = acc
        else:
            val = jax.nn.sigmoid(acc + bg_ref[:, (seg - 5) * d:(seg - 4) * d])
        pr_ref[:, :, (idx - 1) * d:idx * d] = val.astype(BF16).reshape(n_b, ts, d)


def _inproj_call(x3, mod3, mod_row, g0, w_in_bf, b_gate2, *, ts, segs):
    n_b, seq, d = x3.shape
    segs = tuple(segs)
    n_out = (len(segs) - 1) * d
    if mod_row is None:
        mod_blk, mod_idx = (n_b, 1, d), 0
    else:
        mod_blk, mod_idx = (1, 1, d), mod_row
    kern = functools.partial(_inproj_kernel, segs=segs)
    return pl.pallas_call(
        kern,
        grid=(seq // ts,),
        in_specs=[
            pl.BlockSpec((n_b, ts, d), lambda t: (0, t, 0)),
            pl.BlockSpec(mod_blk, lambda t: (mod_idx, 0, 0)),
            pl.BlockSpec(mod_blk, lambda t: (mod_idx, 0, 1)),
            pl.BlockSpec((1, d), lambda t: (0, 0)),
            pl.BlockSpec(w_in_bf.shape, lambda t: (0, 0), pipeline_mode=pl.Buffered(1)),
            pl.BlockSpec(b_gate2.shape, lambda t: (0, 0)),
        ],
        out_specs=[
            pl.BlockSpec((d // LANES, ts * n_b, LANES), lambda t: (0, t, 0)),
            pl.BlockSpec((n_b, ts, n_out), lambda t: (0, t, 0)),
        ],
        out_shape=[
            jax.ShapeDtypeStruct((d // LANES, seq * n_b, LANES), F32),
            jax.ShapeDtypeStruct((n_b, seq, n_out), BF16),
        ],
        compiler_params=pltpu.CompilerParams(
            dimension_semantics=("arbitrary",), vmem_limit_bytes=VMEM_LIMIT),
        name="inproj",
    )(x3, mod3, mod3, g0, w_in_bf, b_gate2)


def _scan_kernel(prev_ref, main_ref, next_ref, cw_ref, cb_ref, wbd_ref, brg_ref, lam_ref, h0_ref,
                 h_ref, hfin_ref, a_s, b_s, hst, *, n_chunks, tc):
    d = pl.program_id(0)
    i = pl.program_id(1)
    ci = jnp.where(d == 0, i, n_chunks - 1 - i)

    @pl.when(i == 0)
    def _():
        hst[...] = h0_ref[d]

    has_prev = (ci > 0).astype(F32)
    has_next = (ci < n_chunks - 1).astype(F32)
    n_lc = a_s.shape[0]
    rows = tc * SUBLANES
    per = MXU_DIM // LANES
    for c in range(n_lc // per):
        xcs = []
        for k in range(c * per, (c + 1) * per):
            sl = slice(k * LANES, (k + 1) * LANES)
            xe = jnp.concatenate([prev_ref[k] * has_prev, main_ref[k], next_ref[k] * has_next], axis=0)
            xc = cb_ref[:, sl]
            for j in range(CONV_W):
                xc = xc + cw_ref[j:j + 1, sl] * xe[j * SUBLANES:j * SUBLANES + rows]
            xcs.append(xc)
        xc2 = jnp.concatenate(xcs, axis=1)
        sl = slice(c * MXU_DIM, (c + 1) * MXU_DIM)
        g = jnp.dot(xc2.astype(BF16), wbd_ref[0, c], preferred_element_type=F32) + brg_ref[0, c]
        r2 = jnp.tanh(g[:, :MXU_DIM]) + 1.0
        i2 = jnp.tanh(g[:, MXU_DIM:]) + 1.0
        z = -lam_ref[0, :, sl]
        softplus = jnp.maximum(z, 0.0) + jnp.log1p(jnp.exp(-jnp.abs(z)))
        c_nat = (0.5 * LRU_C) * softplus
        neg_log_a = c_nat * r2
        a = jnp.exp2((-LOG2E * c_nat) * r2)
        one_minus_a2 = jnp.tanh(neg_log_a) * (a * a + 1.0)
        root = jnp.where(one_minus_a2 > 0.0, one_minus_a2 * lax.rsqrt(one_minus_a2), 0.0)
        bb = root * (i2 * (0.5 * xc2))
        for kk in range(per):
            a_s[c * per + kk] = a[:, kk * LANES:(kk + 1) * LANES]
            b_s[c * per + kk] = bb[:, kk * LANES:(kk + 1) * LANES]

    def step(j, h):
        t = jnp.where(d == 0, j, tc - 1 - j)
        r0 = pl.multiple_of(t * SUBLANES, SUBLANES)
        h = a_s[:, pl.ds(r0, SUBLANES), :] * h + b_s[:, pl.ds(r0, SUBLANES), :]
        h_ref[0, :, pl.ds(r0, SUBLANES), :] = h
        return h

    h_last = lax.fori_loop(0, tc, step, hst[...], unroll=8)
    hst[...] = h_last
    hfin_ref[0] = h_last


def _scan_call(xr, conv_w, conv_b, wbd, brg, lam3, h0, *, tc):
    n_lc, n_rows, _ = xr.shape
    n_b = SUBLANES
    seq = n_rows // n_b
    d = n_lc * LANES
    n_chunks = seq // tc
    n_col = d // MXU_DIM
    rows = tc * n_b

    def chunk(dd, i):
        return jnp.where(dd == 0, i, n_chunks - 1 - i)

    kern = functools.partial(_scan_kernel, n_chunks=n_chunks, tc=tc)
    return pl.pallas_call(
        kern,
        grid=(2, n_chunks),
        in_specs=[
            pl.BlockSpec((n_lc, 2 * n_b, LANES),
                         lambda dd, i: (0, jnp.maximum(chunk(dd, i) * (tc // 2) - 1, 0), 0)),
            pl.BlockSpec((n_lc, rows, LANES), lambda dd, i: (0, chunk(dd, i), 0)),
            pl.BlockSpec((n_lc, n_b, LANES),
                         lambda dd, i: (0, jnp.minimum((chunk(dd, i) + 1) * tc, seq - 1), 0)),
            pl.BlockSpec((CONV_W, d), lambda dd, i: (0, 0)),
            pl.BlockSpec((1, d), lambda dd, i: (0, 0)),
            pl.BlockSpec((1, n_col, MXU_DIM, 2 * MXU_DIM), lambda dd, i: (dd, 0, 0, 0)),
            pl.BlockSpec((1, n_col, 1, 2 * MXU_DIM), lambda dd, i: (dd, 0, 0, 0)),
            pl.BlockSpec((1, 1, d), lambda dd, i: (dd, 0, 0)),
            pl.BlockSpec((2, n_lc, n_b, LANES), lambda dd, i: (0, 0, 0, 0)),
        ],
        out_specs=[
            pl.BlockSpec((1, n_lc, rows, LANES), lambda dd, i: (dd, 0, chunk(dd, i), 0)),
            pl.BlockSpec((1, n_lc, n_b, LANES), lambda dd, i: (dd, 0, 0, 0)),
        ],
        out_shape=[
            jax.ShapeDtypeStruct((2, n_lc, n_rows, LANES), F32),
            jax.ShapeDtypeStruct((2, n_lc, n_b, LANES), F32),
        ],
        scratch_shapes=[
            pltpu.VMEM((n_lc, rows, LANES), F32),
            pltpu.VMEM((n_lc, rows, LANES), F32),
            pltpu.VMEM((n_lc, n_b, LANES), F32),
        ],
        compiler_params=pltpu.CompilerParams(
            dimension_semantics=("arbitrary", "arbitrary"), vmem_limit_bytes=VMEM_LIMIT),
        name="scan",
    )(xr, xr, xr, conv_w, conv_b, wbd, brg, lam3, h0)


def _attn_kernel(q_ref, kp_ref, km_ref, kn_ref, vp_ref, vm_ref, vn_ref, kc_ref, vc_ref, bias_ref, o_ref, s_buf,
                 vt_buf):
    lane = lax.broadcasted_iota(jnp.int32, (1, LANES), 1)
    lo = lane < HEAD_DIM
    n_loc = bias_ref.shape[2]
    heads_per_pair = LANES // HEAD_DIM
    n_pairs = N_HEADS // heads_per_pair
    dims_nt = (((1,), (1,)), ((), ()))
    dims_tn = (((0,), (0,)), ((), ()))

    def scores(p):
        ls = slice(p * LANES, (p + 1) * LANES)
        qp = q_ref[:, ls]
        k_all = jnp.concatenate([kp_ref[:, ls], km_ref[:, ls], kn_ref[:, ls], kc_ref[:, ls]], axis=0)
        for hh in range(heads_per_pair):
            msk = lo if hh == 0 else jnp.logical_not(lo)
            qm = jnp.where(msk, qp, jnp.zeros_like(qp))
            s_buf[p % 2, hh] = lax.dot_general(k_all, qm, dims_nt, preferred_element_type=F32)
        v_all = jnp.concatenate([vp_ref[:, ls], vm_ref[:, ls], vn_ref[:, ls], vc_ref[:, ls]], axis=0)
        vt = v_all.T
        ch = lax.broadcasted_iota(jnp.int32, (LANES, 1), 0)
        vt_buf[p % 2, 0] = jnp.where(ch < HEAD_DIM, vt, jnp.ones_like(vt))
        vt_buf[p % 2, 1] = jnp.where(ch >= HEAD_DIM, vt, jnp.ones_like(vt))

    def softmax_pv(p):
        ls = slice(p * LANES, (p + 1) * LANES)
        o_t = []
        for hh in range(heads_per_pair):
            s_loc = s_buf[p % 2, hh, :n_loc] + bias_ref[0, p * heads_per_pair + hh]
            s_ctx = s_buf[p % 2, hh, n_loc:]
            m = jnp.maximum(jnp.max(s_loc, axis=0, keepdims=True), jnp.max(s_ctx, axis=0, keepdims=True))
            p_all = jnp.concatenate([jnp.exp2(s_loc - m), jnp.exp2(s_ctx - m)], axis=0).astype(BF16)
            o = jnp.dot(vt_buf[p % 2, hh], p_all, preferred_element_type=F32)
            den = o[(1 - hh) * HEAD_DIM:(1 - hh) * HEAD_DIM + 1]
            o_t.append(o[hh * HEAD_DIM:(hh + 1) * HEAD_DIM] * (1.0 / den))
        o_ref[:, ls] = jnp.concatenate(o_t, axis=0).T.astype(BF16)

    scores(0)
    for p in range(n_pairs):
        if p + 1 < n_pairs:
            scores(p + 1)
        softmax_pv(p)


def _attn_call(proj, projc, bias, *, n_b, seq, ctx_len, q_col, k_col, v_col, kc_col, vc_col):
    d = N_HEADS * HEAD_DIM
    n_i = seq // ATT_Q

    def row(i, b):
        return b * n_i + i

    def row_prev(i, b):
        return b * n_i + jnp.maximum(i - 1, 0)

    def row_next(i, b):
        return b * n_i + jnp.minimum(i + 1, n_i - 1)

    def variant(i, b):
        return (i > 0).astype(jnp.int32) + (i == n_i - 1).astype(jnp.int32)

    blk = (ATT_Q, d)
    return pl.pallas_call(
        _attn_kernel,
        grid=(n_i, n_b),
        in_specs=[
            pl.BlockSpec(blk, lambda i, b: (row(i, b), q_col)),
            pl.BlockSpec(blk, lambda i, b: (row_prev(i, b), k_col)),
            pl.BlockSpec(blk, lambda i, b: (row(i, b), k_col)),
            pl.BlockSpec(blk, lambda i, b: (row_next(i, b), k_col)),
            pl.BlockSpec(blk, lambda i, b: (row_prev(i, b), v_col)),
            pl.BlockSpec(blk, lambda i, b: (row(i, b), v_col)),
            pl.BlockSpec(blk, lambda i, b: (row_next(i, b), v_col)),
            pl.BlockSpec((ctx_len, d), lambda i, b: (b, kc_col)),
            pl.BlockSpec((ctx_len, d), lambda i, b: (b, vc_col)),
            pl.BlockSpec((1, N_HEADS, ATT_SLOTS * GRID_W, ATT_Q), lambda i, b: (variant(i, b), 0, 0, 0),
                         pipeline_mode=pl.Buffered(1)),
        ],
        out_specs=pl.BlockSpec(blk, lambda i, b: (row(i, b), 0)),
        out_shape=jax.ShapeDtypeStruct((n_b * seq, d), BF16),
        scratch_shapes=[
            pltpu.VMEM((2, LANES // HEAD_DIM, ATT_SLOTS * GRID_W + ctx_len, ATT_Q), F32),
            pltpu.VMEM((2, LANES // HEAD_DIM, LANES, ATT_SLOTS * GRID_W + ctx_len), BF16),
        ],
        compiler_params=pltpu.CompilerParams(
            dimension_semantics=("arbitrary", "arbitrary"), vmem_limit_bytes=VMEM_LIMIT),
        name="attn",
    )(proj, proj, proj, proj, proj, proj, proj, projc, projc, bias)


def _attn_bias(rpb, rows):
    qc = np.arange(GRID_W)[None, :]
    kc = np.arange(GRID_W)[:, None]
    col_start = np.clip(qc - WIN_COLS // 2, 0, GRID_W - WIN_COLS)
    col_ok = (kc >= col_start) & (kc < col_start + WIN_COLS)
    dc_idx = np.clip(kc - qc + WIN_COLS - 1, 0, 2 * WIN_COLS - 2)
    onehot = (dc_idx[None] == np.arange(2 * WIN_COLS - 1)[:, None, None]) & col_ok[None]
    col_tab = jnp.einsum('had,dkq->hakq', rpb.astype(F32), jnp.asarray(onehot, F32),
                         precision=lax.Precision.HIGHEST)
    col_tab = jnp.where(jnp.asarray(col_ok)[None, None], col_tab * LOG2E, NEG)
    neg_blk = jnp.full((rpb.shape[0], GRID_W, GRID_W), NEG, F32)
    wr = min(WIN_ROWS, rows)
    n_i = rows // ATT_ROWS
    variants = []
    for i in (0, 1, n_i - 1):
        slots = []
        for s in range(ATT_SLOTS):
            kr = ATT_ROWS * i - ATT_ROWS + s
            q_rows = []
            for rq in range(ATT_ROWS):
                r = ATT_ROWS * i + rq
                r0 = int(np.clip(r - wr // 2, 0, rows - wr))
                if r0 <= kr < r0 + wr:
                    q_rows.append(col_tab[:, kr - r + WIN_ROWS - 1])
                else:
                    q_rows.append(neg_blk)
            slots.append(jnp.concatenate(q_rows, axis=-1))
        variants.append(jnp.concatenate(slots, axis=1))
    return jnp.stack(variants, axis=0)


def _out_kernel(hf_ref, hr_ref, ggr_ref, gl_ref, gna_ref, att_ref, x_ref, gt1_ref, sh2_ref, sc2_ref,
                gt2_ref, gn_ref, wl_ref, wn_ref, wo_ref, w1_ref, w2_ref, out_ref):
    n_b, ts, d = x_ref.shape
    rows = n_b * ts
    y_na = jnp.dot(att_ref[...].reshape(rows, d), wn_ref[...], preferred_element_type=F32)
    hl = jnp.stack([
        jnp.concatenate([hf_ref[0, k, pl.ds(b, ts, stride=n_b), :] + hr_ref[0, k, pl.ds(b, ts, stride=n_b), :]
                         for k in range(d // LANES)], axis=1)
        for b in range(n_b)], axis=0)
    hl = hl * ggr_ref[...].astype(F32)
    y_lru = jnp.dot(hl.reshape(rows, d).astype(BF16), wl_ref[...], preferred_element_type=F32)
    y = (gl_ref[...].astype(F32).reshape(rows, d) * y_lru
         + gna_ref[...].astype(F32).reshape(rows, d) * y_na)
    y = jnp.dot(y.astype(BF16), wo_ref[...], preferred_element_type=F32).reshape(n_b, ts, d)
    x1 = x_ref[...] + gt1_ref[...] * _rms(y, gn_ref[1:2, :])
    u2 = _rms(x1, gn_ref[2:3, :]) * (1.0 + sc2_ref[...]) + sh2_ref[...]
    h1 = jnp.dot(u2.reshape(rows, d).astype(BF16), w1_ref[...], preferred_element_type=F32)
    h1 = jnp.maximum(h1, 0.0)
    h1 = h1 * h1
    m = jnp.dot(h1.astype(BF16), w2_ref[...], preferred_element_type=F32).reshape(n_b, ts, d)
    out_ref[...] = x1 + gt2_ref[...] * _rms(m, gn_ref[3:4, :])


def _out_call(h2, proj3, att3, x3, mod3, g_norm, wl, wn, wo, w1, w2, *, ts, ggr_col, gl_col, gna_col):
    n_b, seq, d = x3.shape

    def modspec(k):
        return pl.BlockSpec((n_b, 1, d), lambda t: (0, 0, k))

    def resident(w):
        return pl.BlockSpec(w.shape, lambda t: (0, 0), pipeline_mode=pl.Buffered(1))

    tile = (n_b, ts, d)
    return pl.pallas_call(
        _out_kernel,
        grid=(seq // ts,),
        in_specs=[
            pl.BlockSpec((1, d // LANES, ts * n_b, LANES), lambda t: (0, 0, t, 0)),
            pl.BlockSpec((1, d // LANES, ts * n_b, LANES), lambda t: (1, 0, t, 0)),
            pl.BlockSpec(tile, lambda t: (0, t, ggr_col)),
            pl.BlockSpec(tile, lambda t: (0, t, gl_col)),
            pl.BlockSpec(tile, lambda t: (0, t, gna_col)),
            pl.BlockSpec(tile, lambda t: (0, t, 0)),
            pl.BlockSpec(tile, lambda t: (0, t, 0)),
            modspec(2), modspec(3), modspec(4), modspec(5),
            pl.BlockSpec(g_norm.shape, lambda t: (0, 0)),
            resident(wl), resident(wn), resident(wo), resident(w1), resident(w2),
        ],
        out_specs=pl.BlockSpec(tile, lambda t: (0, t, 0)),
        out_shape=jax.ShapeDtypeStruct((n_b, seq, d), F32),
        compiler_params=pltpu.CompilerParams(
            dimension_semantics=("arbitrary",), vmem_limit_bytes=VMEM_LIMIT),
        name="out",
    )(h2, h2, proj3, proj3, proj3, att3, x3, mod3, mod3, mod3, mod3, g_norm, wl, wn, wo, w1, w2)


def _gate_weights(w_rg, b_rg):
    n_dir, n_gate, n_blk, bw, _ = w_rg.shape
    per = MXU_DIM // bw
    n_col = n_blk // per
    w_rg = 0.5 * w_rg
    b_rg = 0.5 * b_rg
    w = w_rg.reshape(n_dir, n_gate, n_col, per, bw, bw)
    eye = jnp.eye(per, dtype=w.dtype)
    wbd = jnp.einsum('dgcpij,pq->dgcpiqj', w, eye).reshape(n_dir, n_gate, n_col, MXU_DIM, MXU_DIM)
    wbd = jnp.concatenate([wbd[:, 0], wbd[:, 1]], axis=-1).astype(BF16)
    b = b_rg.reshape(n_dir, n_gate, n_col, MXU_DIM)
    brg = jnp.concatenate([b[:, 0], b[:, 1]], axis=-1)[:, :, None, :]
    return wbd, brg


def kernel(x, c, ctx, c_ctx, w_ada, b_ada, g_norm, w_in, b_gate, conv_w, conv_b, w_rg, b_rg, lam, rpb,
           w_lru_out, w_na_out, w_o, w_mlp1, w_mlp2):
    n_b, seq, d = x.shape
    ctx_len = ctx.shape[1]
    rows = seq // GRID_W
    depth = w_ada.shape[0]
    assert depth == 1 and rows % ATT_ROWS == 0 and n_b == SUBLANES

    l = 0
    cs = jnp.concatenate([c, c_ctx[None], jnp.zeros((2 * SUBLANES - n_b - 1, d), F32)], axis=0)
    mod = _mod_call(cs, w_ada[l], b_ada[l][None])
    mod3 = mod[:, None, :]

    w_in_bf = w_in[l].astype(BF16)
    b_gate2 = b_gate[l][None]
    g0 = g_norm[l, 0:1]

    xr, proj3 = _inproj_call(x, mod3, None, g0, w_in_bf, b_gate2, ts=64, segs=(0, 1, 2, 3, 4, 5, 6))
    xr_c, projc3 = _inproj_call(ctx, mod3, n_b, g0, w_in_bf, b_gate2, ts=64, segs=(0, 3, 4))

    wbd, brg = _gate_weights(w_rg[l], b_rg[l])
    lam3 = lam[l][:, None, :]
    cb = conv_b[l][None]
    _, h0 = _scan_call(xr_c, conv_w[l], cb, wbd, brg, lam3, jnp.zeros((2, d // LANES, n_b, LANES), F32), tc=128)
    h, _ = _scan_call(xr, conv_w[l], cb, wbd, brg, lam3, h0, tc=128)

    bias = _attn_bias(rpb[l], rows)
    att = _attn_call(proj3.reshape(n_b * seq, -1), projc3.reshape(n_b * ctx_len, -1), bias,
                     n_b=n_b, seq=seq, ctx_len=ctx_len, q_col=1, k_col=2, v_col=3, kc_col=0, vc_col=1)

    return _out_call(h, proj3, att.reshape(n_b, seq, d), x, mod3, g_norm[l],
                     w_lru_out[l].astype(BF16), w_na_out[l].astype(BF16), w_o[l].astype(BF16),
                     w_mlp1[l].astype(BF16), w_mlp2[l].astype(BF16),
                     ts=32, ggr_col=0, gl_col=4, gna_col=5)
```

```python
import functools

import numpy as np
import jax
import jax.numpy as jnp
from jax import lax
from jax.experimental import pallas as pl
from jax.experimental.pallas import tpu as pltpu

F32 = jnp.float32
BF16 = jnp.bfloat16

EPS = 1e-6
NEG = -1e30
LRU_C = 8.0
N_HEADS = 16
HEAD_DIM = 64
GRID_W = 64
WIN_ROWS = 8
WIN_COLS = 16
N_LRU_BLOCKS = 16
CONV_W = 4
CONV_PAD_LEFT = 2

LANES = 128
SUBLANES = 8
MXU_DIM = 256
VMEM_LIMIT = 56 * 1024 * 1024

ATT_ROWS = 4
ATT_Q = ATT_ROWS * GRID_W
ATT_SLOTS = 3 * ATT_ROWS
LOG2E = 1.4426950408889634


def _rms(x, g):
    return x * lax.rsqrt(jnp.mean(x * x, axis=-1, keepdims=True) + EPS) * g


def _mod_kernel(c_ref, w_ref, b_ref, o_ref):
    c = c_ref[...]
    s = c * jax.nn.sigmoid(c)
    o_ref[...] = jnp.dot(s, w_ref[...], preferred_element_type=F32) + b_ref[...]


def _mod_call(cs, w_ada, b_ada):
    rows, d = cs.shape
    n_out = w_ada.shape[1]
    return pl.pallas_call(
        _mod_kernel,
        grid=(n_out // d,),
        in_specs=[
            pl.BlockSpec((rows, d), lambda n: (0, 0)),
            pl.BlockSpec((d, d), lambda n: (0, n)),
            pl.BlockSpec((1, d), lambda n: (0, n)),
        ],
        out_specs=pl.BlockSpec((rows, d), lambda n: (0, n)),
        out_shape=jax.ShapeDtypeStruct((rows, n_out), F32),
        compiler_params=pltpu.CompilerParams(
            dimension_semantics=("arbitrary",), vmem_limit_bytes=VMEM_LIMIT),
        name="mod",
    )(cs, w_ada, b_ada)


def _inproj_kernel(x_ref, sh_ref, sc_ref, g_ref, w_ref, bg_ref, xr_ref, pr_ref, *, segs):
    n_b, ts, d = x_ref.shape
    u = _rms(x_ref[...], g_ref[...]) * (1.0 + sc_ref[...]) + sh_ref[...]
    u = u.reshape(n_b * ts, d).astype(BF16)
    for idx, seg in enumerate(segs):
        acc = jnp.dot(u, w_ref[:, seg * d:(seg + 1) * d], preferred_element_type=F32)
        if seg == 0:
            for b in range(n_b):
                for k in range(d // LANES):
                    xr_ref[k, pl.ds(b, ts, stride=n_b), :] = acc[b * ts:(b + 1) * ts, k * LANES:(k + 1) * LANES]
            continue
        if seg == 1:
            val = jax.nn.gelu(acc)
        elif seg == 2:
            val = acc * (HEAD_DIM ** -0.5 * LOG2E)
        elif seg in (3, 4):
            val = acc
        else:
            val = jax.nn.sigmoid(acc + bg_ref[:, (seg - 5) * d:(seg - 4) * d])
        pr_ref[:, :, (idx - 1) * d:idx * d] = val.astype(BF16).reshape(n_b, ts, d)


def _inproj_call(x3, mod3, mod_row, g0, w_in_bf, b_gate2, *, ts, segs):
    n_b, seq, d = x3.shape
    segs = tuple(segs)
    n_out = (len(segs) - 1) * d
    if mod_row is None:
        mod_blk, mod_idx = (n_b, 1, d), 0
    else:
        mod_blk, mod_idx = (1, 1, d), mod_row
    kern = functools.partial(_inproj_kernel, segs=segs)
    return pl.pallas_call(
        kern,
        grid=(seq // ts,),
        in_specs=[
            pl.BlockSpec((n_b, ts, d), lambda t: (0, t, 0)),
            pl.BlockSpec(mod_blk, lambda t: (mod_idx, 0, 0)),
            pl.BlockSpec(mod_blk, lambda t: (mod_idx, 0, 1)),
            pl.BlockSpec((1, d), lambda t: (0, 0)),
            pl.BlockSpec(w_in_bf.shape, lambda t: (0, 0), pipeline_mode=pl.Buffered(1)),
            pl.BlockSpec(b_gate2.shape, lambda t: (0, 0)),
        ],
        out_specs=[
            pl.BlockSpec((d // LANES, ts * n_b, LANES), lambda t: (0, t, 0)),
            pl.BlockSpec((n_b, ts, n_out), lambda t: (0, t, 0)),
        ],
        out_shape=[
            jax.ShapeDtypeStruct((d // LANES, seq * n_b, LANES), F32),
            jax.ShapeDtypeStruct((n_b, seq, n_out), BF16),
        ],
        compiler_params=pltpu.CompilerParams(
            dimension_semantics=("arbitrary",), vmem_limit_bytes=VMEM_LIMIT),
        name="inproj",
    )(x3, mod3, mod3, g0, w_in_bf, b_gate2)


def _scan_kernel(prev_ref, main_ref, next_ref, cw_ref, cb_ref, wbd_ref, brg_ref, lam_ref, h0_ref,
                 h_ref, hfin_ref, a_s, b_s, hst, *, n_chunks, tc):
    d = pl.program_id(0)
    i = pl.program_id(1)
    ci = jnp.where(d == 0, i, n_chunks - 1 - i)

    @pl.when(i == 0)
    def _():
        hst[...] = h0_ref[d]

    has_prev = (ci > 0).astype(F32)
    has_next = (ci < n_chunks - 1).astype(F32)
    n_lc = a_s.shape[0]
    rows = tc * SUBLANES
    per = MXU_DIM // LANES
    for c in range(n_lc // per):
        xcs = []
        for k in range(c * per, (c + 1) * per):
            sl = slice(k * LANES, (k + 1) * LANES)
            xe = jnp.concatenate([prev_ref[k] * has_prev, main_ref[k], next_ref[k] * has_next], axis=0)
            xc = cb_ref[:, sl]
            for j in range(CONV_W):
                xc = xc + cw_ref[j:j + 1, sl] * xe[j * SUBLANES:j * SUBLANES + rows]
            xcs.append(xc)
        xc2 = jnp.concatenate(xcs, axis=1)
        sl = slice(c * MXU_DIM, (c + 1) * MXU_DIM)
        g = jnp.dot(xc2.astype(BF16), wbd_ref[0, c], preferred_element_type=F32) + brg_ref[0, c]
        r2 = jnp.tanh(g[:, :MXU_DIM]) + 1.0
        i2 = jnp.tanh(g[:, MXU_DIM:]) + 1.0
        z = -lam_ref[0, :, sl]
        softplus = jnp.maximum(z, 0.0) + jnp.log1p(jnp.exp(-jnp.abs(z)))
        c_nat = (0.5 * LRU_C) * softplus
        neg_log_a = c_nat * r2
        a = jnp.exp2((-LOG2E * c_nat) * r2)
        one_minus_a2 = jnp.tanh(neg_log_a) * (a * a + 1.0)
        root = jnp.where(one_minus_a2 > 0.0, one_minus_a2 * lax.rsqrt(one_minus_a2), 0.0)
        bb = root * (i2 * (0.5 * xc2))
        for kk in range(per):
            a_s[c * per + kk] = a[:, kk * LANES:(kk + 1) * LANES]
            b_s[c * per + kk] = bb[:, kk * LANES:(kk + 1) * LANES]

    def step(j, h):
        t = jnp.where(d == 0, j, tc - 1 - j)
        r0 = pl.multiple_of(t * SUBLANES, SUBLANES)
        h = a_s[:, pl.ds(r0, SUBLANES), :] * h + b_s[:, pl.ds(r0, SUBLANES), :]
        h_ref[0, :, pl.ds(r0, SUBLANES), :] = h
        return h

    h_last = lax.fori_loop(0, tc, step, hst[...], unroll=8)
    hst[...] = h_last
    hfin_ref[0] = h_last


def _scan_call(xr, conv_w, conv_b, wbd, brg, lam3, h0, *, tc):
    n_lc, n_rows, _ = xr.shape
    n_b = SUBLANES
    seq = n_rows // n_b
    d = n_lc * LANES
    n_chunks = seq // tc
    n_col = d // MXU_DIM
    rows = tc * n_b

    def chunk(dd, i):
        return jnp.where(dd == 0, i, n_chunks - 1 - i)

    kern = functools.partial(_scan_kernel, n_chunks=n_chunks, tc=tc)
    return pl.pallas_call(
        kern,
        grid=(2, n_chunks),
        in_specs=[
            pl.BlockSpec((n_lc, 2 * n_b, LANES),
                         lambda dd, i: (0, jnp.maximum(chunk(dd, i) * (tc // 2) - 1, 0), 0)),
            pl.BlockSpec((n_lc, rows, LANES), lambda dd, i: (0, chunk(dd, i), 0)),
            pl.BlockSpec((n_lc, n_b, LANES),
                         lambda dd, i: (0, jnp.minimum((chunk(dd, i) + 1) * tc, seq - 1), 0)),
            pl.BlockSpec((CONV_W, d), lambda dd, i: (0, 0)),
            pl.BlockSpec((1, d), lambda dd, i: (0, 0)),
            pl.BlockSpec((1, n_col, MXU_DIM, 2 * MXU_DIM), lambda dd, i: (dd, 0, 0, 0)),
            pl.BlockSpec((1, n_col, 1, 2 * MXU_DIM), lambda dd, i: (dd, 0, 0, 0)),
            pl.BlockSpec((1, 1, d), lambda dd, i: (dd, 0, 0)),
            pl.BlockSpec((2, n_lc, n_b, LANES), lambda dd, i: (0, 0, 0, 0)),
        ],
        out_specs=[
            pl.BlockSpec((1, n_lc, rows, LANES), lambda dd, i: (dd, 0, chunk(dd, i), 0)),
            pl.BlockSpec((1, n_lc, n_b, LANES), lambda dd, i: (dd, 0, 0, 0)),
        ],
        out_shape=[
            jax.ShapeDtypeStruct((2, n_lc, n_rows, LANES), F32),
            jax.ShapeDtypeStruct((2, n_lc, n_b, LANES), F32),
        ],
        scratch_shapes=[
            pltpu.VMEM((n_lc, rows, LANES), F32),
            pltpu.VMEM((n_lc, rows, LANES), F32),
            pltpu.VMEM((n_lc, n_b, LANES), F32),
        ],
        compiler_params=pltpu.CompilerParams(
            dimension_semantics=("arbitrary", "arbitrary"), vmem_limit_bytes=VMEM_LIMIT),
        name="scan",
    )(xr, xr, xr, conv_w, conv_b, wbd, brg, lam3, h0)


def _window_plan(i, n_i, rows):
    wr = min(WIN_ROWS, rows)
    rng = []
    for rq in range(ATT_ROWS):
        r = ATT_ROWS * i + rq
        r0 = min(max(r - wr // 2, 0), rows - wr)
        lo = r0 - (ATT_ROWS * i - ATT_ROWS)
        rng.append((lo, lo + wr))
    s_lo = min(lo for lo, _ in rng)
    s_hi = max(hi for _, hi in rng)
    assert 0 <= s_lo and s_hi <= ATT_SLOTS
    rows_per_tile = LANES // GRID_W
    tiles = []
    for j in range(ATT_ROWS // rows_per_tile):
        rqs = range(j * rows_per_tile, (j + 1) * rows_per_tile)
        a = min(rng[rq][0] for rq in rqs)
        b = max(rng[rq][1] for rq in rqs)
        half = {}
        for s in range(a, b):
            ok = tuple(rng[rq][0] <= s < rng[rq][1] for rq in rqs)
            if not all(ok):
                half[s - s_lo] = ok
        tiles.append((a - s_lo, b - s_lo, half))
    return s_lo, s_hi, tiles


def _attn_block(q_ref, k_refs, v_refs, kc_ref, vc_ref, bias_ref, o_ref, s_buf, plan):
    s_lo, s_hi, tiles = plan
    n_slots = s_hi - s_lo
    n_loc = n_slots * GRID_W
    n_ctx = kc_ref.shape[0]
    lane = lax.broadcasted_iota(jnp.int32, (1, LANES), 1)
    lo = lane < HEAD_DIM
    heads_per_pair = LANES // HEAD_DIM
    n_pairs = N_HEADS // heads_per_pair
    dims_nt = (((1,), (1,)), ((), ()))
    dims_tn = (((0,), (0,)), ((), ()))

    def keys_of(refs, c_ref, ls):
        parts = []
        for blk, ref in enumerate(refs):
            a = max(s_lo, blk * ATT_ROWS) - blk * ATT_ROWS
            b = min(s_hi, (blk + 1) * ATT_ROWS) - blk * ATT_ROWS
            if b > a:
                parts.append(ref[a * GRID_W:b * GRID_W, ls])
        return jnp.concatenate(parts + [c_ref[:, ls]], axis=0)

    def scores(p):
        ls = slice(p * LANES, (p + 1) * LANES)
        qp = q_ref[:, ls]
        k_all = keys_of(k_refs, kc_ref, ls)
        for hh in range(heads_per_pair):
            msk = lo if hh == 0 else jnp.logical_not(lo)
            qm = jnp.where(msk, qp, jnp.zeros_like(qp))
            s_buf[p % 2, hh, :n_loc + n_ctx] = lax.dot_general(k_all, qm, dims_nt, preferred_element_type=F32)

    def softmax_pv(p):
        ls = slice(p * LANES, (p + 1) * LANES)
        v_all = keys_of(v_refs, vc_ref, ls)
        o_t = []
        for hh in range(heads_per_pair):
            own = lo if hh == 0 else jnp.logical_not(lo)
            head = p * heads_per_pair + hh
            cols = []
            for j, (a, b, half) in enumerate(tiles):
                qs = slice(j * LANES, (j + 1) * LANES)
                blocks = []
                for s in range(a, b):
                    blk = (s_buf[p % 2, hh, s * GRID_W:(s + 1) * GRID_W, qs]
                           + bias_ref[head, (s_lo + s) * GRID_W:(s_lo + s + 1) * GRID_W, qs])
                    if s in half:
                        blk = jnp.where(lo if half[s][0] else jnp.logical_not(lo), blk, NEG)
                    blocks.append(blk)
                s_loc = jnp.concatenate(blocks, axis=0)
                s_ctx = s_buf[p % 2, hh, n_loc:n_loc + n_ctx, qs]
                m = jnp.maximum(jnp.max(s_loc, axis=0, keepdims=True), jnp.max(s_ctx, axis=0, keepdims=True))
                col = [jnp.exp2(s_loc - m), jnp.exp2(s_ctx - m)]
                if a > 0:
                    col.insert(0, jnp.zeros((a * GRID_W, LANES), F32))
                if b < n_slots:
                    col.insert(-1, jnp.zeros(((n_slots - b) * GRID_W, LANES), F32))
                cols.append(jnp.concatenate(col, axis=0))
            p_all = jnp.concatenate(cols, axis=1).astype(BF16)
            v_own = jnp.where(own, v_all, jnp.ones_like(v_all))
            o = lax.dot_general(v_own, p_all, dims_tn, preferred_element_type=F32)
            den = o[(1 - hh) * HEAD_DIM:(1 - hh) * HEAD_DIM + 1]
            o_t.append(o[hh * HEAD_DIM:(hh + 1) * HEAD_DIM] * (1.0 / den))
        o_ref[:, ls] = jnp.concatenate(o_t, axis=0).T.astype(BF16)

    scores(0)
    for p in range(n_pairs):
        if p + 1 < n_pairs:
            scores(p + 1)
        softmax_pv(p)


def _attn_kernel(q_ref, kp_ref, km_ref, kn_ref, vp_ref, vm_ref, vn_ref, kc_ref, vc_ref, bias_ref, o_ref, s_buf,
                 *, plans):
    i = pl.program_id(0)
    n_i = pl.num_programs(0)
    variant = (i > 0).astype(jnp.int32) + (i == n_i - 1).astype(jnp.int32)
    for v, plan in enumerate(plans):
        @pl.when(variant == v)
        def _(plan=plan):
            _attn_block(q_ref, (kp_ref, km_ref, kn_ref), (vp_ref, vm_ref, vn_ref), kc_ref, vc_ref,
                        bias_ref, o_ref, s_buf, plan)


def _attn_call(proj, projc, bias, *, n_b, seq, ctx_len, q_col, k_col, v_col, kc_col, vc_col):
    d = N_HEADS * HEAD_DIM
    n_i = seq // ATT_Q
    rows = seq // GRID_W
    plans = tuple(_window_plan(i, n_i, rows) for i in (0, 1, n_i - 1))

    def row(i, b):
        return b * n_i + i

    def row_prev(i, b):
        return b * n_i + jnp.maximum(i - 1, 0)

    def row_next(i, b):
        return b * n_i + jnp.minimum(i + 1, n_i - 1)

    blk = (ATT_Q, d)
    return pl.pallas_call(
        functools.partial(_attn_kernel, plans=plans),
        grid=(n_i, n_b),
        in_specs=[
            pl.BlockSpec(blk, lambda i, b: (row(i, b), q_col)),
            pl.BlockSpec(blk, lambda i, b: (row_prev(i, b), k_col)),
            pl.BlockSpec(blk, lambda i, b: (row(i, b), k_col)),
            pl.BlockSpec(blk, lambda i, b: (row_next(i, b), k_col)),
            pl.BlockSpec(blk, lambda i, b: (row_prev(i, b), v_col)),
            pl.BlockSpec(blk, lambda i, b: (row(i, b), v_col)),
            pl.BlockSpec(blk, lambda i, b: (row_next(i, b), v_col)),
            pl.BlockSpec((ctx_len, d), lambda i, b: (b, kc_col)),
            pl.BlockSpec((ctx_len, d), lambda i, b: (b, vc_col)),
            pl.BlockSpec(bias.shape, lambda i, b: (0, 0, 0), pipeline_mode=pl.Buffered(1)),
        ],
        out_specs=pl.BlockSpec(blk, lambda i, b: (row(i, b), 0)),
        out_shape=jax.ShapeDtypeStruct((n_b * seq, d), BF16),
        scratch_shapes=[pltpu.VMEM((2, LANES // HEAD_DIM, ATT_SLOTS * GRID_W + ctx_len, ATT_Q), F32)],
        compiler_params=pltpu.CompilerParams(
            dimension_semantics=("arbitrary", "arbitrary"), vmem_limit_bytes=VMEM_LIMIT),
        name="attn",
    )(proj, proj, proj, proj, proj, proj, proj, projc, projc, bias)


def _attn_bias(rpb):
    qc = np.arange(GRID_W)[None, :]
    kc = np.arange(GRID_W)[:, None]
    col_start = np.clip(qc - WIN_COLS // 2, 0, GRID_W - WIN_COLS)
    col_ok = (kc >= col_start) & (kc < col_start + WIN_COLS)
    dc_idx = np.clip(kc - qc + WIN_COLS - 1, 0, 2 * WIN_COLS - 2)
    onehot = (dc_idx[None] == np.arange(2 * WIN_COLS - 1)[:, None, None]) & col_ok[None]
    col_tab = jnp.einsum('had,dkq->hakq', rpb.astype(F32), jnp.asarray(onehot, F32),
                         precision=lax.Precision.HIGHEST)
    col_tab = jnp.where(jnp.asarray(col_ok)[None, None], col_tab * LOG2E, NEG)
    slots = []
    for s in range(ATT_SLOTS):
        q_rows = [col_tab[:, s - ATT_ROWS - rq + WIN_ROWS - 1] for rq in range(ATT_ROWS)]
        slots.append(jnp.concatenate(q_rows, axis=-1))
    return jnp.concatenate(slots, axis=1)


def _out_kernel(hf_ref, hr_ref, ggr_ref, gl_ref, gna_ref, att_ref, x_ref, gt1_ref, sh2_ref, sc2_ref,
                gt2_ref, gn_ref, wl_ref, wn_ref, wo_ref, w1_ref, w2_ref, out_ref):
    n_b, ts, d = x_ref.shape
    rows = n_b * ts
    y_na = jnp.dot(att_ref[...].reshape(rows, d), wn_ref[...], preferred_element_type=F32)
    hl = jnp.stack([
        jnp.concatenate([hf_ref[0, k, pl.ds(b, ts, stride=n_b), :] + hr_ref[0, k, pl.ds(b, ts, stride=n_b), :]
                         for k in range(d // LANES)], axis=1)
        for b in range(n_b)], axis=0)
    hl = hl * ggr_ref[...].astype(F32)
    y_lru = jnp.dot(hl.reshape(rows, d).astype(BF16), wl_ref[...], preferred_element_type=F32)
    y = (gl_ref[...].astype(F32).reshape(rows, d) * y_lru
         + gna_ref[...].astype(F32).reshape(rows, d) * y_na)
    y = jnp.dot(y.astype(BF16), wo_ref[...], preferred_element_type=F32).reshape(n_b, ts, d)
    x1 = x_ref[...] + gt1_ref[...] * _rms(y, gn_ref[1:2, :])
    u2 = _rms(x1, gn_ref[2:3, :]) * (1.0 + sc2_ref[...]) + sh2_ref[...]
    h1 = jnp.dot(u2.reshape(rows, d).astype(BF16), w1_ref[...], preferred_element_type=F32)
    h1 = jnp.maximum(h1, 0.0)
    h1 = h1 * h1
    m = jnp.dot(h1.astype(BF16), w2_ref[...], preferred_element_type=F32).reshape(n_b, ts, d)
    out_ref[...] = x1 + gt2_ref[...] * _rms(m, gn_ref[3:4, :])


def _out_call(h2, proj3, att3, x3, mod3, g_norm, wl, wn, wo, w1, w2, *, ts, ggr_col, gl_col, gna_col):
    n_b, seq, d = x3.shape

    def modspec(k):
        return pl.BlockSpec((n_b, 1, d), lambda t: (0, 0, k))

    def resident(w):
        return pl.BlockSpec(w.shape, lambda t: (0, 0), pipeline_mode=pl.Buffered(1))

    tile = (n_b, ts, d)
    return pl.pallas_call(
        _out_kernel,
        grid=(seq // ts,),
        in_specs=[
            pl.BlockSpec((1, d // LANES, ts * n_b, LANES), lambda t: (0, 0, t, 0)),
            pl.BlockSpec((1, d // LANES, ts * n_b, LANES), lambda t: (1, 0, t, 0)),
            pl.BlockSpec(tile, lambda t: (0, t, ggr_col)),
            pl.BlockSpec(tile, lambda t: (0, t, gl_col)),
            pl.BlockSpec(tile, lambda t: (0, t, gna_col)),
            pl.BlockSpec(tile, lambda t: (0, t, 0)),
            pl.BlockSpec(tile, lambda t: (0, t, 0)),
            modspec(2), modspec(3), modspec(4), modspec(5),
            pl.BlockSpec(g_norm.shape, lambda t: (0, 0)),
            resident(wl), resident(wn), resident(wo), resident(w1), resident(w2),
        ],
        out_specs=pl.BlockSpec(tile, lambda t: (0, t, 0)),
        out_shape=jax.ShapeDtypeStruct((n_b, seq, d), F32),
        compiler_params=pltpu.CompilerParams(
            dimension_semantics=("arbitrary",), vmem_limit_bytes=VMEM_LIMIT),
        name="out",
    )(h2, h2, proj3, proj3, proj3, att3, x3, mod3, mod3, mod3, mod3, g_norm, wl, wn, wo, w1, w2)


def _gate_weights(w_rg, b_rg):
    n_dir, n_gate, n_blk, bw, _ = w_rg.shape
    per = MXU_DIM // bw
    n_col = n_blk // per
    w_rg = 0.5 * w_rg
    b_rg = 0.5 * b_rg
    w = w_rg.reshape(n_dir, n_gate, n_col, per, bw, bw)
    eye = jnp.eye(per, dtype=w.dtype)
    wbd = jnp.einsum('dgcpij,pq->dgcpiqj', w, eye).reshape(n_dir, n_gate, n_col, MXU_DIM, MXU_DIM)
    wbd = jnp.concatenate([wbd[:, 0], wbd[:, 1]], axis=-1).astype(BF16)
    b = b_rg.reshape(n_dir, n_gate, n_col, MXU_DIM)
    brg = jnp.concatenate([b[:, 0], b[:, 1]], axis=-1)[:, :, None, :]
    return wbd, brg


def kernel(x, c, ctx, c_ctx, w_ada, b_ada, g_norm, w_in, b_gate, conv_w, conv_b, w_rg, b_rg, lam, rpb,
           w_lru_out, w_na_out, w_o, w_mlp1, w_mlp2):
    n_b, seq, d = x.shape
    ctx_len = ctx.shape[1]
    rows = seq // GRID_W
    depth = w_ada.shape[0]
    assert depth == 1 and rows % ATT_ROWS == 0 and n_b == SUBLANES

    l = 0
    cs = jnp.concatenate([c, c_ctx[None], jnp.zeros((2 * SUBLANES - n_b - 1, d), F32)], axis=0)
    mod = _mod_call(cs, w_ada[l], b_ada[l][None])
    mod3 = mod[:, None, :]

    w_in_bf = w_in[l].astype(BF16)
    b_gate2 = b_gate[l][None]
    g0 = g_norm[l, 0:1]

    xr, proj3 = _inproj_call(x, mod3, None, g0, w_in_bf, b_gate2, ts=64, segs=(0, 1, 2, 3, 4, 5, 6))
    xr_c, projc3 = _inproj_call(ctx, mod3, n_b, g0, w_in_bf, b_gate2, ts=64, segs=(0, 3, 4))

    wbd, brg = _gate_weights(w_rg[l], b_rg[l])
    lam3 = lam[l][:, None, :]
    cb = conv_b[l][None]
    _, h0 = _scan_call(xr_c, conv_w[l], cb, wbd, brg, lam3, jnp.zeros((2, d // LANES, n_b, LANES), F32), tc=128)
    h, _ = _scan_call(xr, conv_w[l], cb, wbd, brg, lam3, h0, tc=128)

    bias = _attn_bias(rpb[l])
    att = _attn_call(proj3.reshape(n_b * seq, -1), projc3.reshape(n_b * ctx_len, -1), bias,
                     n_b=n_b, seq=seq, ctx_len=ctx_len, q_col=1, k_col=2, v_col=3, kc_col=0, vc_col=1)

    return _out_call(h, proj3, att.reshape(n_b, seq, d), x, mod3, g_norm[l],
                     w_lru_out[l].astype(BF16), w_na_out[l].astype(BF16), w_o[l].astype(BF16),
                     w_mlp1[l].astype(BF16), w_mlp2[l].astype(BF16),
                     ts=32, ggr_col=0, gl_col=4, gna_col=5)
```

```python
import functools

import numpy as np
import jax
import jax.numpy as jnp
from jax import lax
from jax.experimental import pallas as pl
from jax.experimental.pallas import tpu as pltpu

F32 = jnp.float32
BF16 = jnp.bfloat16

EPS = 1e-6
NEG = -1e30
LRU_C = 8.0
N_HEADS = 16
HEAD_DIM = 64
GRID_W = 64
WIN_ROWS = 8
WIN_COLS = 16
N_LRU_BLOCKS = 16
CONV_W = 4
CONV_PAD_LEFT = 2

LANES = 128
SUBLANES = 8
MXU_DIM = 256
VMEM_LIMIT = 56 * 1024 * 1024

ATT_ROWS = 4
ATT_Q = ATT_ROWS * GRID_W
ATT_SLOTS = 3 * ATT_ROWS
LOG2E = 1.4426950408889634


def _rms(x, g):
    return x * lax.rsqrt(jnp.mean(x * x, axis=-1, keepdims=True) + EPS) * g


def _mod_kernel(c_ref, w_ref, b_ref, o_ref):
    c = c_ref[...]
    s = c * jax.nn.sigmoid(c)
    o_ref[...] = jnp.dot(s, w_ref[...], preferred_element_type=F32) + b_ref[...]


def _mod_call(cs, w_ada, b_ada):
    rows, d = cs.shape
    n_out = w_ada.shape[1]
    return pl.pallas_call(
        _mod_kernel,
        grid=(n_out // d,),
        in_specs=[
            pl.BlockSpec((rows, d), lambda n: (0, 0)),
            pl.BlockSpec((d, d), lambda n: (0, n)),
            pl.BlockSpec((1, d), lambda n: (0, n)),
        ],
        out_specs=pl.BlockSpec((rows, d), lambda n: (0, n)),
        out_shape=jax.ShapeDtypeStruct((rows, n_out), F32),
        compiler_params=pltpu.CompilerParams(
            dimension_semantics=("arbitrary",), vmem_limit_bytes=VMEM_LIMIT),
        name="mod",
    )(cs, w_ada, b_ada)


def _inproj_kernel(x0_ref, xn_ref, sh_ref, sc_ref, g_ref, w_ref, bg_ref, xr_ref, pr_ref, u_s, *, segs):
    n_b, ts, d = xn_ref.shape

    def modulated(x_ref):
        u = _rms(x_ref[...], g_ref[...]) * (1.0 + sc_ref[...]) + sh_ref[...]
        return u.reshape(n_b * ts, d).astype(BF16)

    @pl.when(pl.program_id(0) == 0)
    def _():
        u_s[...] = modulated(x0_ref)

    u = u_s[...]
    u_next = None
    order = sorted(range(len(segs)), key=lambda n: segs[n] in (3, 4))
    for idx in order:
        seg = segs[idx]
        acc = jnp.dot(u, w_ref[:, seg * d:(seg + 1) * d], preferred_element_type=F32)
        if u_next is None:
            u_next = modulated(xn_ref)
        if seg == 0:
            for b in range(n_b):
                for k in range(d // LANES):
                    xr_ref[k, pl.ds(b, ts, stride=n_b), :] = acc[b * ts:(b + 1) * ts, k * LANES:(k + 1) * LANES]
            continue
        if seg == 1:
            val = jax.nn.gelu(acc)
        elif seg == 2:
            val = acc * (HEAD_DIM ** -0.5 * LOG2E)
        elif seg in (3, 4):
            val = acc
        else:
            val = jax.nn.sigmoid(acc + bg_ref[:, (seg - 5) * d:(seg - 4) * d])
        pr_ref[:, :, (idx - 1) * d:idx * d] = val.astype(BF16).reshape(n_b, ts, d)
    u_s[...] = u_next


def _inproj_call(x3, mod3, mod_row, g0, w_in_bf, b_gate2, *, ts, segs):
    n_b, seq, d = x3.shape
    segs = tuple(segs)
    n_out = (len(segs) - 1) * d
    if mod_row is None:
        mod_blk, mod_idx = (n_b, 1, d), 0
    else:
        mod_blk, mod_idx = (1, 1, d), mod_row
    kern = functools.partial(_inproj_kernel, segs=segs)
    return pl.pallas_call(
        kern,
        grid=(seq // ts,),
        in_specs=[
            pl.BlockSpec((n_b, ts, d), lambda t: (0, 0, 0), pipeline_mode=pl.Buffered(1)),
            pl.BlockSpec((n_b, ts, d), lambda t: (0, jnp.minimum(t + 1, seq // ts - 1), 0)),
            pl.BlockSpec(mod_blk, lambda t: (mod_idx, 0, 0)),
            pl.BlockSpec(mod_blk, lambda t: (mod_idx, 0, 1)),
            pl.BlockSpec((1, d), lambda t: (0, 0)),
            pl.BlockSpec(w_in_bf.shape, lambda t: (0, 0), pipeline_mode=pl.Buffered(1)),
            pl.BlockSpec(b_gate2.shape, lambda t: (0, 0)),
        ],
        out_specs=[
            pl.BlockSpec((d // LANES, ts * n_b, LANES), lambda t: (0, t, 0)),
            pl.BlockSpec((n_b, ts, n_out), lambda t: (0, t, 0)),
        ],
        out_shape=[
            jax.ShapeDtypeStruct((d // LANES, seq * n_b, LANES), F32),
            jax.ShapeDtypeStruct((n_b, seq, n_out), BF16),
        ],
        scratch_shapes=[pltpu.VMEM((n_b * ts, d), BF16)],
        compiler_params=pltpu.CompilerParams(
            dimension_semantics=("arbitrary",), vmem_limit_bytes=VMEM_LIMIT),
        name="inproj",
    )(x3, x3, mod3, mod3, g0, w_in_bf, b_gate2)


def _scan_kernel(prev_ref, main_ref, next_ref, cw_ref, cb_ref, wbd_ref, brg_ref, lam_ref, h0_ref,
                 h_ref, hfin_ref, a_s, b_s, hst, *, n_chunks, tc):
    d = pl.program_id(0)
    i = pl.program_id(1)
    ci = jnp.where(d == 0, i, n_chunks - 1 - i)

    @pl.when(i == 0)
    def _():
        hst[...] = h0_ref[d]

    has_prev = (ci > 0).astype(F32)
    has_next = (ci < n_chunks - 1).astype(F32)
    n_lc = a_s.shape[0]
    rows = tc * SUBLANES
    per = MXU_DIM // LANES
    for c in range(n_lc // per):
        xcs = []
        for k in range(c * per, (c + 1) * per):
            sl = slice(k * LANES, (k + 1) * LANES)
            xe = jnp.concatenate([prev_ref[k] * has_prev, main_ref[k], next_ref[k] * has_next], axis=0)
            xc = cb_ref[:, sl]
            for j in range(CONV_W):
                xc = xc + cw_ref[j:j + 1, sl] * xe[j * SUBLANES:j * SUBLANES + rows]
            xcs.append(xc)
        xc2 = jnp.concatenate(xcs, axis=1)
        sl = slice(c * MXU_DIM, (c + 1) * MXU_DIM)
        g = jnp.dot(xc2.astype(BF16), wbd_ref[0, c], preferred_element_type=F32) + brg_ref[0, c]
        r2 = jnp.tanh(g[:, :MXU_DIM]) + 1.0
        i2 = jnp.tanh(g[:, MXU_DIM:]) + 1.0
        z = -lam_ref[0, :, sl]
        softplus = jnp.maximum(z, 0.0) + jnp.log1p(jnp.exp(-jnp.abs(z)))
        c_nat = (0.5 * LRU_C) * softplus
        neg_log_a = c_nat * r2
        a = jnp.exp2((-LOG2E * c_nat) * r2)
        one_minus_a2 = jnp.tanh(neg_log_a) * (a * a + 1.0)
        root = jnp.where(one_minus_a2 > 0.0, one_minus_a2 * lax.rsqrt(one_minus_a2), 0.0)
        bb = root * (i2 * (0.5 * xc2))
        for kk in range(per):
            a_s[c * per + kk] = a[:, kk * LANES:(kk + 1) * LANES]
            b_s[c * per + kk] = bb[:, kk * LANES:(kk + 1) * LANES]

    def step(j, h):
        t = jnp.where(d == 0, j, tc - 1 - j)
        r0 = pl.multiple_of(t * SUBLANES, SUBLANES)
        h = a_s[:, pl.ds(r0, SUBLANES), :] * h + b_s[:, pl.ds(r0, SUBLANES), :]
        h_ref[0, :, pl.ds(r0, SUBLANES), :] = h
        return h

    h_last = lax.fori_loop(0, tc, step, hst[...], unroll=8)
    hst[...] = h_last
    hfin_ref[0] = h_last


def _scan_call(xr, conv_w, conv_b, wbd, brg, lam3, h0, *, tc):
    n_lc, n_rows, _ = xr.shape
    n_b = SUBLANES
    seq = n_rows // n_b
    d = n_lc * LANES
    n_chunks = seq // tc
    n_col = d // MXU_DIM
    rows = tc * n_b

    def chunk(dd, i):
        return jnp.where(dd == 0, i, n_chunks - 1 - i)

    kern = functools.partial(_scan_kernel, n_chunks=n_chunks, tc=tc)
    return pl.pallas_call(
        kern,
        grid=(2, n_chunks),
        in_specs=[
            pl.BlockSpec((n_lc, 2 * n_b, LANES),
                         lambda dd, i: (0, jnp.maximum(chunk(dd, i) * (tc // 2) - 1, 0), 0)),
            pl.BlockSpec((n_lc, rows, LANES), lambda dd, i: (0, chunk(dd, i), 0)),
            pl.BlockSpec((n_lc, n_b, LANES),
                         lambda dd, i: (0, jnp.minimum((chunk(dd, i) + 1) * tc, seq - 1), 0)),
            pl.BlockSpec((CONV_W, d), lambda dd, i: (0, 0)),
            pl.BlockSpec((1, d), lambda dd, i: (0, 0)),
            pl.BlockSpec((1, n_col, MXU_DIM, 2 * MXU_DIM), lambda dd, i: (dd, 0, 0, 0)),
            pl.BlockSpec((1, n_col, 1, 2 * MXU_DIM), lambda dd, i: (dd, 0, 0, 0)),
            pl.BlockSpec((1, 1, d), lambda dd, i: (dd, 0, 0)),
            pl.BlockSpec((2, n_lc, n_b, LANES), lambda dd, i: (0, 0, 0, 0)),
        ],
        out_specs=[
            pl.BlockSpec((1, n_lc, rows, LANES), lambda dd, i: (dd, 0, chunk(dd, i), 0)),
            pl.BlockSpec((1, n_lc, n_b, LANES), lambda dd, i: (dd, 0, 0, 0)),
        ],
        out_shape=[
            jax.ShapeDtypeStruct((2, n_lc, n_rows, LANES), F32),
            jax.ShapeDtypeStruct((2, n_lc, n_b, LANES), F32),
        ],
        scratch_shapes=[
            pltpu.VMEM((n_lc, rows, LANES), F32),
            pltpu.VMEM((n_lc, rows, LANES), F32),
            pltpu.VMEM((n_lc, n_b, LANES), F32),
        ],
        compiler_params=pltpu.CompilerParams(
            dimension_semantics=("arbitrary", "arbitrary"), vmem_limit_bytes=VMEM_LIMIT),
        name="scan",
    )(xr, xr, xr, conv_w, conv_b, wbd, brg, lam3, h0)


def _window_plan(i, n_i, rows):
    wr = min(WIN_ROWS, rows)
    rng = []
    for rq in range(ATT_ROWS):
        r = ATT_ROWS * i + rq
        r0 = min(max(r - wr // 2, 0), rows - wr)
        lo = r0 - (ATT_ROWS * i - ATT_ROWS)
        rng.append((lo, lo + wr))
    s_lo = min(lo for lo, _ in rng)
    s_hi = max(hi for _, hi in rng)
    assert 0 <= s_lo and s_hi <= ATT_SLOTS
    rows_per_tile = LANES // GRID_W
    tiles = []
    for j in range(ATT_ROWS // rows_per_tile):
        rqs = range(j * rows_per_tile, (j + 1) * rows_per_tile)
        a = min(rng[rq][0] for rq in rqs)
        b = max(rng[rq][1] for rq in rqs)
        half = {}
        for s in range(a, b):
            ok = tuple(rng[rq][0] <= s < rng[rq][1] for rq in rqs)
            if not all(ok):
                half[s - s_lo] = ok
        tiles.append((a - s_lo, b - s_lo, half))
    return s_lo, s_hi, tiles


def _attn_block(q_ref, k_refs, v_refs, kc_ref, vc_ref, bias_ref, o_ref, s_buf, plan):
    s_lo, s_hi, tiles = plan
    n_slots = s_hi - s_lo
    n_loc = n_slots * GRID_W
    n_ctx = kc_ref.shape[0]
    lane = lax.broadcasted_iota(jnp.int32, (1, LANES), 1)
    lo = lane < HEAD_DIM
    heads_per_pair = LANES // HEAD_DIM
    n_pairs = N_HEADS // heads_per_pair
    dims_nt = (((1,), (1,)), ((), ()))
    dims_tn = (((0,), (0,)), ((), ()))

    def keys_of(refs, c_ref, ls):
        parts = []
        for blk, ref in enumerate(refs):
            a = max(s_lo, blk * ATT_ROWS) - blk * ATT_ROWS
            b = min(s_hi, (blk + 1) * ATT_ROWS) - blk * ATT_ROWS
            if b > a:
                parts.append(ref[a * GRID_W:b * GRID_W, ls])
        return jnp.concatenate(parts + [c_ref[:, ls]], axis=0)

    def scores(p):
        ls = slice(p * LANES, (p + 1) * LANES)
        qp = q_ref[:, ls]
        k_all = keys_of(k_refs, kc_ref, ls)
        for hh in range(heads_per_pair):
            msk = lo if hh == 0 else jnp.logical_not(lo)
            qm = jnp.where(msk, qp, jnp.zeros_like(qp))
            s_buf[p % 2, hh, :n_loc + n_ctx] = lax.dot_general(k_all, qm, dims_nt, preferred_element_type=F32)

    def softmax_pv(p):
        ls = slice(p * LANES, (p + 1) * LANES)
        v_all = keys_of(v_refs, vc_ref, ls)
        o_t = []
        for hh in range(heads_per_pair):
            own = lo if hh == 0 else jnp.logical_not(lo)
            head = p * heads_per_pair + hh
            cols = []
            for j, (a, b, half) in enumerate(tiles):
                qs = slice(j * LANES, (j + 1) * LANES)
                blocks = []
                for s in range(a, b):
                    blk = (s_buf[p % 2, hh, s * GRID_W:(s + 1) * GRID_W, qs]
                           + bias_ref[head, (s_lo + s) * GRID_W:(s_lo + s + 1) * GRID_W, qs])
                    if s in half:
                        blk = jnp.where(lo if half[s][0] else jnp.logical_not(lo), blk, NEG)
                    blocks.append(blk)
                s_loc = jnp.concatenate(blocks, axis=0)
                s_ctx = s_buf[p % 2, hh, n_loc:n_loc + n_ctx, qs]
                m = jnp.maximum(jnp.max(s_loc, axis=0, keepdims=True), jnp.max(s_ctx, axis=0, keepdims=True))
                col = [jnp.exp2(s_loc - m), jnp.exp2(s_ctx - m)]
                if a > 0:
                    col.insert(0, jnp.zeros((a * GRID_W, LANES), F32))
                if b < n_slots:
                    col.insert(-1, jnp.zeros(((n_slots - b) * GRID_W, LANES), F32))
                cols.append(jnp.concatenate(col, axis=0))
            p_all = jnp.concatenate(cols, axis=1).astype(BF16)
            v_own = jnp.where(own, v_all, jnp.ones_like(v_all))
            o = lax.dot_general(v_own, p_all, dims_tn, preferred_element_type=F32)
            den = o[(1 - hh) * HEAD_DIM:(1 - hh) * HEAD_DIM + 1]
            o_t.append(o[hh * HEAD_DIM:(hh + 1) * HEAD_DIM] * (1.0 / den))
        o_ref[:, ls] = jnp.concatenate(o_t, axis=0).T.astype(BF16)

    scores(0)
    for p in range(n_pairs):
        if p + 1 < n_pairs:
            scores(p + 1)
        softmax_pv(p)


def _attn_kernel(q_ref, kp_ref, km_ref, kn_ref, vp_ref, vm_ref, vn_ref, kc_ref, vc_ref, bias_ref, o_ref, s_buf,
                 *, plans):
    i = pl.program_id(0)
    n_i = pl.num_programs(0)
    variant = (i > 0).astype(jnp.int32) + (i == n_i - 1).astype(jnp.int32)
    for v, plan in enumerate(plans):
        @pl.when(variant == v)
        def _(plan=plan):
            _attn_block(q_ref, (kp_ref, km_ref, kn_ref), (vp_ref, vm_ref, vn_ref), kc_ref, vc_ref,
                        bias_ref, o_ref, s_buf, plan)


def _attn_call(proj, projc, bias, *, n_b, seq, ctx_len, q_col, k_col, v_col, kc_col, vc_col):
    d = N_HEADS * HEAD_DIM
    n_i = seq // ATT_Q
    rows = seq // GRID_W
    plans = tuple(_window_plan(i, n_i, rows) for i in (0, 1, n_i - 1))

    def row(i, b):
        return b * n_i + i

    def row_prev(i, b):
        return b * n_i + jnp.maximum(i - 1, 0)

    def row_next(i, b):
        return b * n_i + jnp.minimum(i + 1, n_i - 1)

    blk = (ATT_Q, d)
    return pl.pallas_call(
        functools.partial(_attn_kernel, plans=plans),
        grid=(n_i, n_b),
        in_specs=[
            pl.BlockSpec(blk, lambda i, b: (row(i, b), q_col)),
            pl.BlockSpec(blk, lambda i, b: (row_prev(i, b), k_col)),
            pl.BlockSpec(blk, lambda i, b: (row(i, b), k_col)),
            pl.BlockSpec(blk, lambda i, b: (row_next(i, b), k_col)),
            pl.BlockSpec(blk, lambda i, b: (row_prev(i, b), v_col)),
            pl.BlockSpec(blk, lambda i, b: (row(i, b), v_col)),
            pl.BlockSpec(blk, lambda i, b: (row_next(i, b), v_col)),
            pl.BlockSpec((ctx_len, d), lambda i, b: (b, kc_col)),
            pl.BlockSpec((ctx_len, d), lambda i, b: (b, vc_col)),
            pl.BlockSpec(bias.shape, lambda i, b: (0, 0, 0), pipeline_mode=pl.Buffered(1)),
        ],
        out_specs=pl.BlockSpec(blk, lambda i, b: (row(i, b), 0)),
        out_shape=jax.ShapeDtypeStruct((n_b * seq, d), BF16),
        scratch_shapes=[pltpu.VMEM((2, LANES // HEAD_DIM, ATT_SLOTS * GRID_W + ctx_len, ATT_Q), F32)],
        compiler_params=pltpu.CompilerParams(
            dimension_semantics=("arbitrary", "arbitrary"), vmem_limit_bytes=VMEM_LIMIT),
        name="attn",
    )(proj, proj, proj, proj, proj, proj, proj, projc, projc, bias)


def _attn_bias(rpb):
    qc = np.arange(GRID_W)[None, :]
    kc = np.arange(GRID_W)[:, None]
    col_start = np.clip(qc - WIN_COLS // 2, 0, GRID_W - WIN_COLS)
    col_ok = (kc >= col_start) & (kc < col_start + WIN_COLS)
    dc_idx = np.clip(kc - qc + WIN_COLS - 1, 0, 2 * WIN_COLS - 2)
    onehot = (dc_idx[None] == np.arange(2 * WIN_COLS - 1)[:, None, None]) & col_ok[None]
    col_tab = jnp.einsum('had,dkq->hakq', rpb.astype(F32), jnp.asarray(onehot, F32),
                         precision=lax.Precision.HIGHEST)
    col_tab = jnp.where(jnp.asarray(col_ok)[None, None], col_tab * LOG2E, NEG)
    slots = []
    for s in range(ATT_SLOTS):
        q_rows = [col_tab[:, s - ATT_ROWS - rq + WIN_ROWS - 1] for rq in range(ATT_ROWS)]
        slots.append(jnp.concatenate(q_rows, axis=-1))
    return jnp.concatenate(slots, axis=1)


def _out_kernel(hf_ref, hr_ref, ggr_ref, gl_ref, gna_ref, att_ref, x_ref, gt1_ref, sh2_ref, sc2_ref,
                gt2_ref, gn_ref, wl_ref, wn_ref, wo_ref, w1_ref, w2_ref, out_ref, y_s):
    n_b, ts, d = x_ref.shape
    rows = n_b * ts

    @pl.when(pl.program_id(0) == 0)
    def _():
        y_s[...] = jnp.zeros_like(y_s)

    y_na = jnp.dot(att_ref[...].reshape(rows, d), wn_ref[...], preferred_element_type=F32)
    hl = jnp.stack([
        jnp.concatenate([hf_ref[0, k, pl.ds(b, ts, stride=n_b), :] + hr_ref[0, k, pl.ds(b, ts, stride=n_b), :]
                         for k in range(d // LANES)], axis=1)
        for b in range(n_b)], axis=0)
    hl = hl * ggr_ref[...].astype(F32)
    y_lru = jnp.dot(hl.reshape(rows, d).astype(BF16), wl_ref[...], preferred_element_type=F32)

    x1 = x_ref[...] + gt1_ref[...] * _rms(y_s[...], gn_ref[1:2, :])
    u2 = _rms(x1, gn_ref[2:3, :]) * (1.0 + sc2_ref[...]) + sh2_ref[...]
    h1 = jnp.dot(u2.reshape(rows, d).astype(BF16), w1_ref[...], preferred_element_type=F32)
    h1 = jnp.maximum(h1, 0.0)
    h1 = h1 * h1

    m = jnp.dot(h1.astype(BF16), w2_ref[...], preferred_element_type=F32).reshape(n_b, ts, d)
    out_ref[...] = x1 + gt2_ref[...] * _rms(m, gn_ref[3:4, :])

    y = (gl_ref[...].astype(F32).reshape(rows, d) * y_lru
         + gna_ref[...].astype(F32).reshape(rows, d) * y_na)
    y_s[...] = jnp.dot(y.astype(BF16), wo_ref[...], preferred_element_type=F32).reshape(n_b, ts, d)


def _out_call(h2, proj3, att3, x3, mod3, g_norm, wl, wn, wo, w1, w2, *, ts, ggr_col, gl_col, gna_col):
    n_b, seq, d = x3.shape

    def modspec(k):
        return pl.BlockSpec((n_b, 1, d), lambda t: (0, 0, k))

    def resident(w):
        return pl.BlockSpec(w.shape, lambda t: (0, 0), pipeline_mode=pl.Buffered(1))

    tile = (n_b, ts, d)
    n_t = seq // ts

    def cur(t):
        return jnp.minimum(t, n_t - 1)

    def prev(t):
        return jnp.maximum(t - 1, 0)

    return pl.pallas_call(
        _out_kernel,
        grid=(n_t + 1,),
        in_specs=[
            pl.BlockSpec((1, d // LANES, ts * n_b, LANES), lambda t: (0, 0, cur(t), 0)),
            pl.BlockSpec((1, d // LANES, ts * n_b, LANES), lambda t: (1, 0, cur(t), 0)),
            pl.BlockSpec(tile, lambda t: (0, cur(t), ggr_col)),
            pl.BlockSpec(tile, lambda t: (0, cur(t), gl_col)),
            pl.BlockSpec(tile, lambda t: (0, cur(t), gna_col)),
            pl.BlockSpec(tile, lambda t: (0, cur(t), 0)),
            pl.BlockSpec(tile, lambda t: (0, prev(t), 0)),
            modspec(2), modspec(3), modspec(4), modspec(5),
            pl.BlockSpec(g_norm.shape, lambda t: (0, 0)),
            resident(wl), resident(wn), resident(wo), resident(w1), resident(w2),
        ],
        out_specs=pl.BlockSpec(tile, lambda t: (0, prev(t), 0)),
        out_shape=jax.ShapeDtypeStruct((n_b, seq, d), F32),
        scratch_shapes=[pltpu.VMEM(tile, F32)],
        compiler_params=pltpu.CompilerParams(
            dimension_semantics=("arbitrary",), vmem_limit_bytes=VMEM_LIMIT),
        name="out",
    )(h2, h2, proj3, proj3, proj3, att3, x3, mod3, mod3, mod3, mod3, g_norm, wl, wn, wo, w1, w2)


def _gate_weights(w_rg, b_rg):
    n_dir, n_gate, n_blk, bw, _ = w_rg.shape
    per = MXU_DIM // bw
    n_col = n_blk // per
    w_rg = 0.5 * w_rg
    b_rg = 0.5 * b_rg
    w = w_rg.reshape(n_dir, n_gate, n_col, per, bw, bw)
    eye = jnp.eye(per, dtype=w.dtype)
    wbd = jnp.einsum('dgcpij,pq->dgcpiqj', w, eye).reshape(n_dir, n_gate, n_col, MXU_DIM, MXU_DIM)
    wbd = jnp.concatenate([wbd[:, 0], wbd[:, 1]], axis=-1).astype(BF16)
    b = b_rg.reshape(n_dir, n_gate, n_col, MXU_DIM)
    brg = jnp.concatenate([b[:, 0], b[:, 1]], axis=-1)[:, :, None, :]
    return wbd, brg


def kernel(x, c, ctx, c_ctx, w_ada, b_ada, g_norm, w_in, b_gate, conv_w, conv_b, w_rg, b_rg, lam, rpb,
           w_lru_out, w_na_out, w_o, w_mlp1, w_mlp2):
    n_b, seq, d = x.shape
    ctx_len = ctx.shape[1]
    rows = seq // GRID_W
    depth = w_ada.shape[0]
    assert depth == 1 and rows % ATT_ROWS == 0 and n_b == SUBLANES

    l = 0
    cs = jnp.concatenate([c, c_ctx[None], jnp.zeros((2 * SUBLANES - n_b - 1, d), F32)], axis=0)
    mod = _mod_call(cs, w_ada[l], b_ada[l][None])
    mod3 = mod[:, None, :]

    w_in_bf = w_in[l].astype(BF16)
    b_gate2 = b_gate[l][None]
    g0 = g_norm[l, 0:1]

    xr, proj3 = _inproj_call(x, mod3, None, g0, w_in_bf, b_gate2, ts=64, segs=(0, 1, 2, 3, 4, 5, 6))
    xr_c, projc3 = _inproj_call(ctx, mod3, n_b, g0, w_in_bf, b_gate2, ts=64, segs=(0, 3, 4))

    wbd, brg = _gate_weights(w_rg[l], b_rg[l])
    lam3 = lam[l][:, None, :]
    cb = conv_b[l][None]
    _, h0 = _scan_call(xr_c, conv_w[l], cb, wbd, brg, lam3, jnp.zeros((2, d // LANES, n_b, LANES), F32), tc=128)
    h, _ = _scan_call(xr, conv_w[l], cb, wbd, brg, lam3, h0, tc=128)

    bias = _attn_bias(rpb[l])
    att = _attn_call(proj3.reshape(n_b * seq, -1), projc3.reshape(n_b * ctx_len, -1), bias,
                     n_b=n_b, seq=seq, ctx_len=ctx_len, q_col=1, k_col=2, v_col=3, kc_col=0, vc_col=1)

    return _out_call(h, proj3, att.reshape(n_b, seq, d), x, mod3, g_norm[l],
                     w_lru_out[l].astype(BF16), w_na_out[l].astype(BF16), w_o[l].astype(BF16),
                     w_mlp1[l].astype(BF16), w_mlp2[l].astype(BF16),
                     ts=32, ggr_col=0, gl_col=4, gna_col=5)
```

```python
import functools

import numpy as np
import jax
import jax.numpy as jnp
from jax import lax
from jax.experimental import pallas as pl
from jax.experimental.pallas import tpu as pltpu

F32 = jnp.float32
BF16 = jnp.bfloat16

EPS = 1e-6
NEG = -1e30
LRU_C = 8.0
N_HEADS = 16
HEAD_DIM = 64
GRID_W = 64
WIN_ROWS = 8
WIN_COLS = 16
N_LRU_BLOCKS = 16
CONV_W = 4
CONV_PAD_LEFT = 2

LANES = 128
SUBLANES = 8
MXU_DIM = 256
VMEM_LIMIT = 56 * 1024 * 1024

ATT_ROWS = 4
ATT_Q = ATT_ROWS * GRID_W
ATT_SLOTS = 3 * ATT_ROWS
LOG2E = 1.4426950408889634


def _rms(x, g):
    return x * lax.rsqrt(jnp.mean(x * x, axis=-1, keepdims=True) + EPS) * g


def _mod_kernel(c_ref, w_ref, b_ref, o_ref):
    c = c_ref[...]
    s = c * jax.nn.sigmoid(c)
    o_ref[...] = jnp.dot(s, w_ref[...], preferred_element_type=F32) + b_ref[...]


def _mod_call(cs, w_ada, b_ada):
    rows, d = cs.shape
    n_out = w_ada.shape[1]
    return pl.pallas_call(
        _mod_kernel,
        grid=(n_out // d,),
        in_specs=[
            pl.BlockSpec((rows, d), lambda n: (0, 0)),
            pl.BlockSpec((d, d), lambda n: (0, n)),
            pl.BlockSpec((1, d), lambda n: (0, n)),
        ],
        out_specs=pl.BlockSpec((rows, d), lambda n: (0, n)),
        out_shape=jax.ShapeDtypeStruct((rows, n_out), F32),
        compiler_params=pltpu.CompilerParams(
            dimension_semantics=("arbitrary",), vmem_limit_bytes=VMEM_LIMIT),
        name="mod",
    )(cs, w_ada, b_ada)


def _inproj_kernel(x_ref, sh_ref, sc_ref, g_ref, w_ref, bg_ref, xr_ref, pr_ref, *, segs):
    n_b, ts, d = x_ref.shape
    u = _rms(x_ref[...], g_ref[...]) * (1.0 + sc_ref[...]) + sh_ref[...]
    u = u.reshape(n_b * ts, d).astype(BF16)
    for idx, seg in enumerate(segs):
        acc = jnp.dot(u, w_ref[:, seg * d:(seg + 1) * d], preferred_element_type=F32)
        if seg == 0:
            for b in range(n_b):
                for k in range(d // LANES):
                    xr_ref[k, pl.ds(b, ts, stride=n_b), :] = acc[b * ts:(b + 1) * ts, k * LANES:(k + 1) * LANES]
            continue
        if seg == 1:
            val = jax.nn.gelu(acc)
        elif seg == 2:
            val = acc * (HEAD_DIM ** -0.5 * LOG2E)
        elif seg in (3, 4):
            val = acc
        else:
            val = jax.nn.sigmoid(acc + bg_ref[:, (seg - 5) * d:(seg - 4) * d])
        pr_ref[:, :, (idx - 1) * d:idx * d] = val.astype(BF16).reshape(n_b, ts, d)


def _inproj_call(x3, mod3, mod_row, g0, w_in_bf, b_gate2, *, ts, segs):
    n_b, seq, d = x3.shape
    segs = tuple(segs)
    n_out = (len(segs) - 1) * d
    if mod_row is None:
        mod_blk, mod_idx = (n_b, 1, d), 0
    else:
        mod_blk, mod_idx = (1, 1, d), mod_row
    kern = functools.partial(_inproj_kernel, segs=segs)
    return pl.pallas_call(
        kern,
        grid=(seq // ts,),
        in_specs=[
            pl.BlockSpec((n_b, ts, d), lambda t: (0, t, 0)),
            pl.BlockSpec(mod_blk, lambda t: (mod_idx, 0, 0)),
            pl.BlockSpec(mod_blk, lambda t: (mod_idx, 0, 1)),
            pl.BlockSpec((1, d), lambda t: (0, 0)),
            pl.BlockSpec(w_in_bf.shape, lambda t: (0, 0), pipeline_mode=pl.Buffered(1)),
            pl.BlockSpec(b_gate2.shape, lambda t: (0, 0)),
        ],
        out_specs=[
            pl.BlockSpec((d // LANES, ts * n_b, LANES), lambda t: (0, t, 0)),
            pl.BlockSpec((n_b, ts, n_out), lambda t: (0, t, 0)),
        ],
        out_shape=[
            jax.ShapeDtypeStruct((d // LANES, seq * n_b, LANES), F32),
            jax.ShapeDtypeStruct((n_b, seq, n_out), BF16),
        ],
        compiler_params=pltpu.CompilerParams(
            dimension_semantics=("arbitrary",), vmem_limit_bytes=VMEM_LIMIT),
        name="inproj",
    )(x3, mod3, mod3, g0, w_in_bf, b_gate2)


def _scan_kernel(prev_ref, main_ref, next_ref, cw_ref, cb_ref, wbd_ref, brg_ref, lam_ref, h0_ref,
                 h_ref, hfin_ref, a_s, b_s, hst, *, n_chunks, tc):
    d = pl.program_id(0)
    i = pl.program_id(1)
    ci = jnp.where(d == 0, i, n_chunks - 1 - i)

    @pl.when(i == 0)
    def _():
        hst[...] = h0_ref[d]

    has_prev = (ci > 0).astype(F32)
    has_next = (ci < n_chunks - 1).astype(F32)
    n_lc = a_s.shape[0]
    rows = tc * SUBLANES
    per = MXU_DIM // LANES
    for c in range(n_lc // per):
        xcs = []
        for k in range(c * per, (c + 1) * per):
            sl = slice(k * LANES, (k + 1) * LANES)
            xe = jnp.concatenate([prev_ref[k] * has_prev, main_ref[k], next_ref[k] * has_next], axis=0)
            xc = cb_ref[:, sl]
            for j in range(CONV_W):
                xc = xc + cw_ref[j:j + 1, sl] * xe[j * SUBLANES:j * SUBLANES + rows]
            xcs.append(xc)
        xc2 = jnp.concatenate(xcs, axis=1)
        sl = slice(c * MXU_DIM, (c + 1) * MXU_DIM)
        g = jnp.dot(xc2.astype(BF16), wbd_ref[0, c], preferred_element_type=F32) + brg_ref[0, c]
        r2 = jnp.tanh(g[:, :MXU_DIM]) + 1.0
        i2 = jnp.tanh(g[:, MXU_DIM:]) + 1.0
        z = -lam_ref[0, :, sl]
        softplus = jnp.maximum(z, 0.0) + jnp.log1p(jnp.exp(-jnp.abs(z)))
        c_nat = (0.5 * LRU_C) * softplus
        neg_log_a = c_nat * r2
        a = jnp.exp2((-LOG2E * c_nat) * r2)
        one_minus_a2 = jnp.tanh(neg_log_a) * (a * a + 1.0)
        root = jnp.where(one_minus_a2 > 0.0, one_minus_a2 * lax.rsqrt(one_minus_a2), 0.0)
        bb = root * (i2 * (0.5 * xc2))
        for kk in range(per):
            a_s[c * per + kk, :rows] = a[:, kk * LANES:(kk + 1) * LANES]
            b_s[c * per + kk, :rows] = bb[:, kk * LANES:(kk + 1) * LANES]

    def step(j, h):
        t = jnp.where(d == 0, j, tc - 1 - j)
        r0 = pl.multiple_of(t * SUBLANES, SUBLANES)
        h = a_s[:, pl.ds(r0, SUBLANES), :] * h + b_s[:, pl.ds(r0, SUBLANES), :]
        h_ref[0, :, pl.ds(r0, SUBLANES), :] = h
        return h

    h_last = lax.fori_loop(0, tc, step, hst[...], unroll=8)
    hst[...] = h_last
    hfin_ref[0] = h_last


def _scan_call(xr, conv_w, conv_b, wbd, brg, lam3, h0, *, tc):
    n_lc, n_rows, _ = xr.shape
    n_b = SUBLANES
    seq = n_rows // n_b
    d = n_lc * LANES
    n_chunks = seq // tc
    n_col = d // MXU_DIM
    rows = tc * n_b

    def chunk(dd, i):
        return jnp.where(dd == 0, i, n_chunks - 1 - i)

    kern = functools.partial(_scan_kernel, n_chunks=n_chunks, tc=tc)
    return pl.pallas_call(
        kern,
        grid=(2, n_chunks),
        in_specs=[
            pl.BlockSpec((n_lc, 2 * n_b, LANES),
                         lambda dd, i: (0, jnp.maximum(chunk(dd, i) * (tc // 2) - 1, 0), 0)),
            pl.BlockSpec((n_lc, rows, LANES), lambda dd, i: (0, chunk(dd, i), 0)),
            pl.BlockSpec((n_lc, n_b, LANES),
                         lambda dd, i: (0, jnp.minimum((chunk(dd, i) + 1) * tc, seq - 1), 0)),
            pl.BlockSpec((CONV_W, d), lambda dd, i: (0, 0)),
            pl.BlockSpec((1, d), lambda dd, i: (0, 0)),
            pl.BlockSpec((1, n_col, MXU_DIM, 2 * MXU_DIM), lambda dd, i: (dd, 0, 0, 0)),
            pl.BlockSpec((1, n_col, 1, 2 * MXU_DIM), lambda dd, i: (dd, 0, 0, 0)),
            pl.BlockSpec((1, 1, d), lambda dd, i: (dd, 0, 0)),
            pl.BlockSpec((2, n_lc, n_b, LANES), lambda dd, i: (0, 0, 0, 0)),
        ],
        out_specs=[
            pl.BlockSpec((1, n_lc, rows, LANES), lambda dd, i: (dd, 0, chunk(dd, i), 0)),
            pl.BlockSpec((1, n_lc, n_b, LANES), lambda dd, i: (dd, 0, 0, 0)),
        ],
        out_shape=[
            jax.ShapeDtypeStruct((2, n_lc, n_rows, LANES), F32),
            jax.ShapeDtypeStruct((2, n_lc, n_b, LANES), F32),
        ],
        scratch_shapes=[
            pltpu.VMEM((n_lc, rows + n_b, LANES), F32),
            pltpu.VMEM((n_lc, rows + n_b, LANES), F32),
            pltpu.VMEM((n_lc, n_b, LANES), F32),
        ],
        compiler_params=pltpu.CompilerParams(
            dimension_semantics=("arbitrary", "arbitrary"), vmem_limit_bytes=VMEM_LIMIT),
        name="scan",
    )(xr, xr, xr, conv_w, conv_b, wbd, brg, lam3, h0)


def _window_plan(i, n_i, rows):
    wr = min(WIN_ROWS, rows)
    rng = []
    for rq in range(ATT_ROWS):
        r = ATT_ROWS * i + rq
        r0 = min(max(r - wr // 2, 0), rows - wr)
        lo = r0 - (ATT_ROWS * i - ATT_ROWS)
        rng.append((lo, lo + wr))
    s_lo = min(lo for lo, _ in rng)
    s_hi = max(hi for _, hi in rng)
    assert 0 <= s_lo and s_hi <= ATT_SLOTS
    rows_per_tile = LANES // GRID_W
    tiles = []
    for j in range(ATT_ROWS // rows_per_tile):
        rqs = range(j * rows_per_tile, (j + 1) * rows_per_tile)
        a = min(rng[rq][0] for rq in rqs)
        b = max(rng[rq][1] for rq in rqs)
        half = {}
        for s in range(a, b):
            ok = tuple(rng[rq][0] <= s < rng[rq][1] for rq in rqs)
            if not all(ok):
                half[s - s_lo] = ok
        tiles.append((a - s_lo, b - s_lo, half))
    return s_lo, s_hi, tiles


def _attn_block(q_ref, k_refs, v_refs, kc_ref, vc_ref, bias_ref, o_ref, s_buf, plan):
    s_lo, s_hi, tiles = plan
    n_slots = s_hi - s_lo
    n_loc = n_slots * GRID_W
    n_ctx = kc_ref.shape[0]
    lane = lax.broadcasted_iota(jnp.int32, (1, LANES), 1)
    lo = lane < HEAD_DIM
    heads_per_pair = LANES // HEAD_DIM
    n_pairs = N_HEADS // heads_per_pair
    dims_nt = (((1,), (1,)), ((), ()))
    dims_tn = (((0,), (0,)), ((), ()))

    def keys_of(refs, c_ref, ls):
        parts = []
        for blk, ref in enumerate(refs):
            a = max(s_lo, blk * ATT_ROWS) - blk * ATT_ROWS
            b = min(s_hi, (blk + 1) * ATT_ROWS) - blk * ATT_ROWS
            if b > a:
                parts.append(ref[a * GRID_W:b * GRID_W, ls])
        return jnp.concatenate(parts + [c_ref[:, ls]], axis=0)

    def scores(p):
        ls = slice(p * LANES, (p + 1) * LANES)
        qp = q_ref[:, ls]
        k_all = keys_of(k_refs, kc_ref, ls)
        for hh in range(heads_per_pair):
            msk = lo if hh == 0 else jnp.logical_not(lo)
            qm = jnp.where(msk, qp, jnp.zeros_like(qp))
            s_buf[p % 2, hh, :n_loc + n_ctx] = lax.dot_general(k_all, qm, dims_nt, preferred_element_type=F32)

    def softmax_pv(p):
        ls = slice(p * LANES, (p + 1) * LANES)
        v_all = keys_of(v_refs, vc_ref, ls)
        o_t = []
        for hh in range(heads_per_pair):
            own = lo if hh == 0 else jnp.logical_not(lo)
            head = p * heads_per_pair + hh
            cols = []
            for j, (a, b, half) in enumerate(tiles):
                qs = slice(j * LANES, (j + 1) * LANES)
                blocks = []
                for s in range(a, b):
                    blk = (s_buf[p % 2, hh, s * GRID_W:(s + 1) * GRID_W, qs]
                           + bias_ref[head, (s_lo + s) * GRID_W:(s_lo + s + 1) * GRID_W, qs])
                    if s in half:
                        blk = jnp.where(lo if half[s][0] else jnp.logical_not(lo), blk, NEG)
                    blocks.append(blk)
                s_loc = jnp.concatenate(blocks, axis=0)
                s_ctx = s_buf[p % 2, hh, n_loc:n_loc + n_ctx, qs]
                m = jnp.maximum(jnp.max(s_loc, axis=0, keepdims=True), jnp.max(s_ctx, axis=0, keepdims=True))
                col = [jnp.exp2(s_loc - m), jnp.exp2(s_ctx - m)]
                if a > 0:
                    col.insert(0, jnp.zeros((a * GRID_W, LANES), F32))
                if b < n_slots:
                    col.insert(-1, jnp.zeros(((n_slots - b) * GRID_W, LANES), F32))
                cols.append(jnp.concatenate(col, axis=0))
            p_all = jnp.concatenate(cols, axis=1).astype(BF16)
            v_own = jnp.where(own, v_all, jnp.ones_like(v_all))
            o = lax.dot_general(v_own, p_all, dims_tn, preferred_element_type=F32)
            den = o[(1 - hh) * HEAD_DIM:(1 - hh) * HEAD_DIM + 1]
            o_t.append(o[hh * HEAD_DIM:(hh + 1) * HEAD_DIM] * (1.0 / den))
        o_ref[:, ls] = jnp.concatenate(o_t, axis=0).T.astype(BF16)

    scores(0)
    for p in range(n_pairs):
        if p + 1 < n_pairs:
            scores(p + 1)
        softmax_pv(p)


def _attn_kernel(q_ref, kp_ref, km_ref, kn_ref, vp_ref, vm_ref, vn_ref, kc_ref, vc_ref, bias_ref, o_ref, s_buf,
                 *, plans):
    i = pl.program_id(0)
    n_i = pl.num_programs(0)
    variant = (i > 0).astype(jnp.int32) + (i == n_i - 1).astype(jnp.int32)
    for v, plan in enumerate(plans):
        @pl.when(variant == v)
        def _(plan=plan):
            _attn_block(q_ref, (kp_ref, km_ref, kn_ref), (vp_ref, vm_ref, vn_ref), kc_ref, vc_ref,
                        bias_ref, o_ref, s_buf, plan)


def _attn_call(proj, projc, bias, *, n_b, seq, ctx_len, q_col, k_col, v_col, kc_col, vc_col):
    d = N_HEADS * HEAD_DIM
    n_i = seq // ATT_Q
    rows = seq // GRID_W
    plans = tuple(_window_plan(i, n_i, rows) for i in (0, 1, n_i - 1))

    def row(i, b):
        return b * n_i + i

    def row_prev(i, b):
        return b * n_i + jnp.maximum(i - 1, 0)

    def row_next(i, b):
        return b * n_i + jnp.minimum(i + 1, n_i - 1)

    blk = (ATT_Q, d)
    return pl.pallas_call(
        functools.partial(_attn_kernel, plans=plans),
        grid=(n_i, n_b),
        in_specs=[
            pl.BlockSpec(blk, lambda i, b: (row(i, b), q_col)),
            pl.BlockSpec(blk, lambda i, b: (row_prev(i, b), k_col)),
            pl.BlockSpec(blk, lambda i, b: (row(i, b), k_col)),
            pl.BlockSpec(blk, lambda i, b: (row_next(i, b), k_col)),
            pl.BlockSpec(blk, lambda i, b: (row_prev(i, b), v_col)),
            pl.BlockSpec(blk, lambda i, b: (row(i, b), v_col)),
            pl.BlockSpec(blk, lambda i, b: (row_next(i, b), v_col)),
            pl.BlockSpec((ctx_len, d), lambda i, b: (b, kc_col)),
            pl.BlockSpec((ctx_len, d), lambda i, b: (b, vc_col)),
            pl.BlockSpec(bias.shape, lambda i, b: (0, 0, 0), pipeline_mode=pl.Buffered(1)),
        ],
        out_specs=pl.BlockSpec(blk, lambda i, b: (row(i, b), 0)),
        out_shape=jax.ShapeDtypeStruct((n_b * seq, d), BF16),
        scratch_shapes=[pltpu.VMEM((2, LANES // HEAD_DIM, ATT_SLOTS * GRID_W + ctx_len, ATT_Q), F32)],
        compiler_params=pltpu.CompilerParams(
            dimension_semantics=("arbitrary", "arbitrary"), vmem_limit_bytes=VMEM_LIMIT),
        name="attn",
    )(proj, proj, proj, proj, proj, proj, proj, projc, projc, bias)


def _attn_bias(rpb):
    qc = np.arange(GRID_W)[None, :]
    kc = np.arange(GRID_W)[:, None]
    col_start = np.clip(qc - WIN_COLS // 2, 0, GRID_W - WIN_COLS)
    col_ok = (kc >= col_start) & (kc < col_start + WIN_COLS)
    dc_idx = np.clip(kc - qc + WIN_COLS - 1, 0, 2 * WIN_COLS - 2)
    onehot = (dc_idx[None] == np.arange(2 * WIN_COLS - 1)[:, None, None]) & col_ok[None]
    col_tab = jnp.einsum('had,dkq->hakq', rpb.astype(F32), jnp.asarray(onehot, F32),
                         precision=lax.Precision.HIGHEST)
    col_tab = jnp.where(jnp.asarray(col_ok)[None, None], col_tab * LOG2E, NEG)
    slots = []
    for s in range(ATT_SLOTS):
        q_rows = [col_tab[:, s - ATT_ROWS - rq + WIN_ROWS - 1] for rq in range(ATT_ROWS)]
        slots.append(jnp.concatenate(q_rows, axis=-1))
    return jnp.concatenate(slots, axis=1)


def _out_kernel(hf_ref, hr_ref, ggr_ref, gl_ref, gna_ref, att_ref, x_ref, gt1_ref, sh2_ref, sc2_ref,
                gt2_ref, gn_ref, wl_ref, wn_ref, wo_ref, w1_ref, w2_ref, out_ref):
    n_b, ts, d = x_ref.shape
    rows = n_b * ts
    y_na = jnp.dot(att_ref[...].reshape(rows, d), wn_ref[...], preferred_element_type=F32)
    hl = jnp.stack([
        jnp.concatenate([hf_ref[0, k, pl.ds(b, ts, stride=n_b), :] + hr_ref[0, k, pl.ds(b, ts, stride=n_b), :]
                         for k in range(d // LANES)], axis=1)
        for b in range(n_b)], axis=0)
    hl = hl * ggr_ref[...].astype(F32)
    y_lru = jnp.dot(hl.reshape(rows, d).astype(BF16), wl_ref[...], preferred_element_type=F32)
    y = (gl_ref[...].astype(F32).reshape(rows, d) * y_lru
         + gna_ref[...].astype(F32).reshape(rows, d) * y_na)
    y = jnp.dot(y.astype(BF16), wo_ref[...], preferred_element_type=F32).reshape(n_b, ts, d)
    x1 = x_ref[...] + gt1_ref[...] * _rms(y, gn_ref[1:2, :])
    u2 = _rms(x1, gn_ref[2:3, :]) * (1.0 + sc2_ref[...]) + sh2_ref[...]
    h1 = jnp.dot(u2.reshape(rows, d).astype(BF16), w1_ref[...], preferred_element_type=F32)
    h1 = jnp.maximum(h1, 0.0)
    h1 = h1 * h1
    m = jnp.dot(h1.astype(BF16), w2_ref[...], preferred_element_type=F32).reshape(n_b, ts, d)
    out_ref[...] = x1 + gt2_ref[...] * _rms(m, gn_ref[3:4, :])


def _out_call(h2, proj3, att3, x3, mod3, g_norm, wl, wn, wo, w1, w2, *, ts, ggr_col, gl_col, gna_col):
    n_b, seq, d = x3.shape

    def modspec(k):
        return pl.BlockSpec((n_b, 1, d), lambda t: (0, 0, k))

    def resident(w):
        return pl.BlockSpec(w.shape, lambda t: (0, 0), pipeline_mode=pl.Buffered(1))

    tile = (n_b, ts, d)
    return pl.pallas_call(
        _out_kernel,
        grid=(seq // ts,),
        in_specs=[
            pl.BlockSpec((1, d // LANES, ts * n_b, LANES), lambda t: (0, 0, t, 0)),
            pl.BlockSpec((1, d // LANES, ts * n_b, LANES), lambda t: (1, 0, t, 0)),
            pl.BlockSpec(tile, lambda t: (0, t, ggr_col)),
            pl.BlockSpec(tile, lambda t: (0, t, gl_col)),
            pl.BlockSpec(tile, lambda t: (0, t, gna_col)),
            pl.BlockSpec(tile, lambda t: (0, t, 0)),
            pl.BlockSpec(tile, lambda t: (0, t, 0)),
            modspec(2), modspec(3), modspec(4), modspec(5),
            pl.BlockSpec(g_norm.shape, lambda t: (0, 0)),
            resident(wl), resident(wn), resident(wo), resident(w1), resident(w2),
        ],
        out_specs=pl.BlockSpec(tile, lambda t: (0, t, 0)),
        out_shape=jax.ShapeDtypeStruct((n_b, seq, d), F32),
        compiler_params=pltpu.CompilerParams(
            dimension_semantics=("arbitrary",), vmem_limit_bytes=VMEM_LIMIT),
        name="out",
    )(h2, h2, proj3, proj3, proj3, att3, x3, mod3, mod3, mod3, mod3, g_norm, wl, wn, wo, w1, w2)


def _gate_weights(w_rg, b_rg):
    n_dir, n_gate, n_blk, bw, _ = w_rg.shape
    per = MXU_DIM // bw
    n_col = n_blk // per
    w_rg = 0.5 * w_rg
    b_rg = 0.5 * b_rg
    w = w_rg.reshape(n_dir, n_gate, n_col, per, bw, bw)
    eye = jnp.eye(per, dtype=w.dtype)
    wbd = jnp.einsum('dgcpij,pq->dgcpiqj', w, eye).reshape(n_dir, n_gate, n_col, MXU_DIM, MXU_DIM)
    wbd = jnp.concatenate([wbd[:, 0], wbd[:, 1]], axis=-1).astype(BF16)
    b = b_rg.reshape(n_dir, n_gate, n_col, MXU_DIM)
    brg = jnp.concatenate([b[:, 0], b[:, 1]], axis=-1)[:, :, None, :]
    return wbd, brg


def kernel(x, c, ctx, c_ctx, w_ada, b_ada, g_norm, w_in, b_gate, conv_w, conv_b, w_rg, b_rg, lam, rpb,
           w_lru_out, w_na_out, w_o, w_mlp1, w_mlp2):
    n_b, seq, d = x.shape
    ctx_len = ctx.shape[1]
    rows = seq // GRID_W
    depth = w_ada.shape[0]
    assert depth == 1 and rows % ATT_ROWS == 0 and n_b == SUBLANES

    l = 0
    cs = jnp.concatenate([c, c_ctx[None], jnp.zeros((2 * SUBLANES - n_b - 1, d), F32)], axis=0)
    mod = _mod_call(cs, w_ada[l], b_ada[l][None])
    mod3 = mod[:, None, :]

    w_in_bf = w_in[l].astype(BF16)
    b_gate2 = b_gate[l][None]
    g0 = g_norm[l, 0:1]

    xr, proj3 = _inproj_call(x, mod3, None, g0, w_in_bf, b_gate2, ts=64, segs=(0, 1, 2, 3, 4, 5, 6))
    xr_c, projc3 = _inproj_call(ctx, mod3, n_b, g0, w_in_bf, b_gate2, ts=64, segs=(0, 3, 4))

    wbd, brg = _gate_weights(w_rg[l], b_rg[l])
    lam3 = lam[l][:, None, :]
    cb = conv_b[l][None]
    _, h0 = _scan_call(xr_c, conv_w[l], cb, wbd, brg, lam3, jnp.zeros((2, d // LANES, n_b, LANES), F32), tc=128)
    h, _ = _scan_call(xr, conv_w[l], cb, wbd, brg, lam3, h0, tc=128)

    bias = _attn_bias(rpb[l])
    att = _attn_call(proj3.reshape(n_b * seq, -1), projc3.reshape(n_b * ctx_len, -1), bias,
                     n_b=n_b, seq=seq, ctx_len=ctx_len, q_col=1, k_col=2, v_col=3, kc_col=0, vc_col=1)

    return _out_call(h, proj3, att.reshape(n_b, seq, d), x, mod3, g_norm[l],
                     w_lru_out[l].astype(BF16), w_na_out[l].astype(BF16), w_o[l].astype(BF16),
                     w_mlp1[l].astype(BF16), w_mlp2[l].astype(BF16),
                     ts=32, ggr_col=0, gl_col=4, gna_col=5)
```

```python
import functools

import numpy as np
import jax
import jax.numpy as jnp
from jax import lax
from jax.experimental import pallas as pl
from jax.experimental.pallas import tpu as pltpu

F32 = jnp.float32
BF16 = jnp.bfloat16

EPS = 1e-6
NEG = -1e30
LRU_C = 8.0
N_HEADS = 16
HEAD_DIM = 64
GRID_W = 64
WIN_ROWS = 8
WIN_COLS = 16
N_LRU_BLOCKS = 16
CONV_W = 4
CONV_PAD_LEFT = 2

LANES = 128
SUBLANES = 8
MXU_DIM = 256
VMEM_LIMIT = 56 * 1024 * 1024

ATT_ROWS = 4
ATT_Q = ATT_ROWS * GRID_W
ATT_SLOTS = 3 * ATT_ROWS
FF_CHUNK = 1024
LOG2E = 1.4426950408889634


def _rms(x, g):
    return x * lax.rsqrt(jnp.mean(x * x, axis=-1, keepdims=True) + EPS) * g


def _mod_kernel(c_ref, w_ref, b_ref, o_ref):
    c = c_ref[...]
    s = c * jax.nn.sigmoid(c)
    o_ref[...] = jnp.dot(s, w_ref[...], preferred_element_type=F32) + b_ref[...]


def _mod_call(cs, w_ada, b_ada):
    rows, d = cs.shape
    n_out = w_ada.shape[1]
    return pl.pallas_call(
        _mod_kernel,
        grid=(n_out // d,),
        in_specs=[
            pl.BlockSpec((rows, d), lambda n: (0, 0)),
            pl.BlockSpec((d, d), lambda n: (0, n)),
            pl.BlockSpec((1, d), lambda n: (0, n)),
        ],
        out_specs=pl.BlockSpec((rows, d), lambda n: (0, n)),
        out_shape=jax.ShapeDtypeStruct((rows, n_out), F32),
        compiler_params=pltpu.CompilerParams(
            dimension_semantics=("arbitrary",), vmem_limit_bytes=VMEM_LIMIT),
        name="mod",
    )(cs, w_ada, b_ada)


def _inproj_kernel(x_ref, sh_ref, sc_ref, g_ref, w_ref, bg_ref, xr_ref, pr_ref, *, segs):
    n_b, ts, d = x_ref.shape
    u = _rms(x_ref[...], g_ref[...]) * (1.0 + sc_ref[...]) + sh_ref[...]
    u = u.reshape(n_b * ts, d).astype(BF16)
    for idx, seg in enumerate(segs):
        acc = jnp.dot(u, w_ref[:, seg * d:(seg + 1) * d], preferred_element_type=F32)
        if seg == 0:
            for b in range(n_b):
                for k in range(d // LANES):
                    xr_ref[k, pl.ds(b, ts, stride=n_b), :] = acc[b * ts:(b + 1) * ts, k * LANES:(k + 1) * LANES]
            continue
        if seg == 1:
            val = jax.nn.gelu(acc)
        elif seg == 2:
            val = acc * (HEAD_DIM ** -0.5 * LOG2E)
        elif seg in (3, 4):
            val = acc
        else:
            val = jax.nn.sigmoid(acc + bg_ref[:, (seg - 5) * d:(seg - 4) * d])
        pr_ref[:, :, (idx - 1) * d:idx * d] = val.astype(BF16).reshape(n_b, ts, d)


def _inproj_call(x3, mod3, mod_row, g0, w_in_bf, b_gate2, *, ts, segs):
    n_b, seq, d = x3.shape
    segs = tuple(segs)
    n_out = (len(segs) - 1) * d
    if mod_row is None:
        mod_blk, mod_idx = (n_b, 1, d), 0
    else:
        mod_blk, mod_idx = (1, 1, d), mod_row
    kern = functools.partial(_inproj_kernel, segs=segs)
    return pl.pallas_call(
        kern,
        grid=(seq // ts,),
        in_specs=[
            pl.BlockSpec((n_b, ts, d), lambda t: (0, t, 0)),
            pl.BlockSpec(mod_blk, lambda t: (mod_idx, 0, 0)),
            pl.BlockSpec(mod_blk, lambda t: (mod_idx, 0, 1)),
            pl.BlockSpec((1, d), lambda t: (0, 0)),
            pl.BlockSpec(w_in_bf.shape, lambda t: (0, 0), pipeline_mode=pl.Buffered(1)),
            pl.BlockSpec(b_gate2.shape, lambda t: (0, 0)),
        ],
        out_specs=[
            pl.BlockSpec((d // LANES, ts * n_b, LANES), lambda t: (0, t, 0)),
            pl.BlockSpec((n_b, ts, n_out), lambda t: (0, t, 0)),
        ],
        out_shape=[
            jax.ShapeDtypeStruct((d // LANES, seq * n_b, LANES), F32),
            jax.ShapeDtypeStruct((n_b, seq, n_out), BF16),
        ],
        compiler_params=pltpu.CompilerParams(
            dimension_semantics=("arbitrary",), vmem_limit_bytes=VMEM_LIMIT),
        name="inproj",
    )(x3, mod3, mod3, g0, w_in_bf, b_gate2)


def _scan_kernel(prev_ref, main_ref, next_ref, cw_ref, cb_ref, wbd_ref, brg_ref, lam_ref, h0_ref,
                 h_ref, hfin_ref, a_s, b_s, hst, *, n_chunks, tc):
    d = pl.program_id(0)
    i = pl.program_id(1)
    ci = jnp.where(d == 0, i, n_chunks - 1 - i)

    @pl.when(i == 0)
    def _():
        hst[...] = h0_ref[d]

    has_prev = (ci > 0).astype(F32)
    has_next = (ci < n_chunks - 1).astype(F32)
    n_lc = a_s.shape[0]
    rows = tc * SUBLANES
    per = MXU_DIM // LANES
    for c in range(n_lc // per):
        xcs = []
        for k in range(c * per, (c + 1) * per):
            sl = slice(k * LANES, (k + 1) * LANES)
            xe = jnp.concatenate([prev_ref[k] * has_prev, main_ref[k], next_ref[k] * has_next], axis=0)
            xc = cb_ref[:, sl]
            for j in range(CONV_W):
                xc = xc + cw_ref[j:j + 1, sl] * xe[j * SUBLANES:j * SUBLANES + rows]
            xcs.append(xc)
        xc2 = jnp.concatenate(xcs, axis=1)
        sl = slice(c * MXU_DIM, (c + 1) * MXU_DIM)
        g = jnp.dot(xc2.astype(BF16), wbd_ref[0, c], preferred_element_type=F32) + brg_ref[0, c]
        r2 = jnp.tanh(g[:, :MXU_DIM]) + 1.0
        i2 = jnp.tanh(g[:, MXU_DIM:]) + 1.0
        z = -lam_ref[0, :, sl]
        softplus = jnp.maximum(z, 0.0) + jnp.log1p(jnp.exp(-jnp.abs(z)))
        c_nat = (0.5 * LRU_C) * softplus
        neg_log_a = c_nat * r2
        a = jnp.exp2((-LOG2E * c_nat) * r2)
        one_minus_a2 = jnp.tanh(neg_log_a) * (a * a + 1.0)
        root = jnp.where(one_minus_a2 > 0.0, one_minus_a2 * lax.rsqrt(one_minus_a2), 0.0)
        bb = root * (i2 * (0.5 * xc2))
        for kk in range(per):
            a_s[c * per + kk, :rows] = a[:, kk * LANES:(kk + 1) * LANES]
            b_s[c * per + kk, :rows] = bb[:, kk * LANES:(kk + 1) * LANES]

    def step(j, h):
        t = jnp.where(d == 0, j, tc - 1 - j)
        r0 = pl.multiple_of(t * SUBLANES, SUBLANES)
        h = a_s[:, pl.ds(r0, SUBLANES), :] * h + b_s[:, pl.ds(r0, SUBLANES), :]
        h_ref[0, :, pl.ds(r0, SUBLANES), :] = h
        return h

    h_last = lax.fori_loop(0, tc, step, hst[...], unroll=8)
    hst[...] = h_last
    hfin_ref[0] = h_last


def _scan_call(xr, conv_w, conv_b, wbd, brg, lam3, h0, *, tc):
    n_lc, n_rows, _ = xr.shape
    n_b = SUBLANES
    seq = n_rows // n_b
    d = n_lc * LANES
    n_chunks = seq // tc
    n_col = d // MXU_DIM
    rows = tc * n_b

    def chunk(dd, i):
        return jnp.where(dd == 0, i, n_chunks - 1 - i)

    kern = functools.partial(_scan_kernel, n_chunks=n_chunks, tc=tc)
    return pl.pallas_call(
        kern,
        grid=(2, n_chunks),
        in_specs=[
            pl.BlockSpec((n_lc, 2 * n_b, LANES),
                         lambda dd, i: (0, jnp.maximum(chunk(dd, i) * (tc // 2) - 1, 0), 0)),
            pl.BlockSpec((n_lc, rows, LANES), lambda dd, i: (0, chunk(dd, i), 0)),
            pl.BlockSpec((n_lc, n_b, LANES),
                         lambda dd, i: (0, jnp.minimum((chunk(dd, i) + 1) * tc, seq - 1), 0)),
            pl.BlockSpec((CONV_W, d), lambda dd, i: (0, 0)),
            pl.BlockSpec((1, d), lambda dd, i: (0, 0)),
            pl.BlockSpec((1, n_col, MXU_DIM, 2 * MXU_DIM), lambda dd, i: (dd, 0, 0, 0)),
            pl.BlockSpec((1, n_col, 1, 2 * MXU_DIM), lambda dd, i: (dd, 0, 0, 0)),
            pl.BlockSpec((1, 1, d), lambda dd, i: (dd, 0, 0)),
            pl.BlockSpec((2, n_lc, n_b, LANES), lambda dd, i: (0, 0, 0, 0)),
        ],
        out_specs=[
            pl.BlockSpec((1, n_lc, rows, LANES), lambda dd, i: (dd, 0, chunk(dd, i), 0)),
            pl.BlockSpec((1, n_lc, n_b, LANES), lambda dd, i: (dd, 0, 0, 0)),
        ],
        out_shape=[
            jax.ShapeDtypeStruct((2, n_lc, n_rows, LANES), F32),
            jax.ShapeDtypeStruct((2, n_lc, n_b, LANES), F32),
        ],
        scratch_shapes=[
            pltpu.VMEM((n_lc, rows + n_b, LANES), F32),
            pltpu.VMEM((n_lc, rows + n_b, LANES), F32),
            pltpu.VMEM((n_lc, n_b, LANES), F32),
        ],
        compiler_params=pltpu.CompilerParams(
            dimension_semantics=("arbitrary", "arbitrary"), vmem_limit_bytes=VMEM_LIMIT),
        name="scan",
    )(xr, xr, xr, conv_w, conv_b, wbd, brg, lam3, h0)


def _window_plan(i, n_i, rows):
    wr = min(WIN_ROWS, rows)
    rng = []
    for rq in range(ATT_ROWS):
        r = ATT_ROWS * i + rq
        r0 = min(max(r - wr // 2, 0), rows - wr)
        lo = r0 - (ATT_ROWS * i - ATT_ROWS)
        rng.append((lo, lo + wr))
    s_lo = min(lo for lo, _ in rng)
    s_hi = max(hi for _, hi in rng)
    assert 0 <= s_lo and s_hi <= ATT_SLOTS
    rows_per_tile = LANES // GRID_W
    tiles = []
    for j in range(ATT_ROWS // rows_per_tile):
        rqs = range(j * rows_per_tile, (j + 1) * rows_per_tile)
        a = min(rng[rq][0] for rq in rqs)
        b = max(rng[rq][1] for rq in rqs)
        half = {}
        for s in range(a, b):
            ok = tuple(rng[rq][0] <= s < rng[rq][1] for rq in rqs)
            if not all(ok):
                half[s - s_lo] = ok
        tiles.append((a - s_lo, b - s_lo, half))
    return s_lo, s_hi, tiles


def _attn_block(q_ref, k_refs, v_refs, kc_ref, vc_ref, bias_ref, o_ref, s_buf, plan):
    s_lo, s_hi, tiles = plan
    n_slots = s_hi - s_lo
    n_loc = n_slots * GRID_W
    n_ctx = kc_ref.shape[0]
    lane = lax.broadcasted_iota(jnp.int32, (1, LANES), 1)
    lo = lane < HEAD_DIM
    heads_per_pair = LANES // HEAD_DIM
    n_pairs = N_HEADS // heads_per_pair
    dims_nt = (((1,), (1,)), ((), ()))
    dims_tn = (((0,), (0,)), ((), ()))

    def keys_of(refs, c_ref, ls):
        parts = []
        for blk, ref in enumerate(refs):
            a = max(s_lo, blk * ATT_ROWS) - blk * ATT_ROWS
            b = min(s_hi, (blk + 1) * ATT_ROWS) - blk * ATT_ROWS
            if b > a:
                parts.append(ref[a * GRID_W:b * GRID_W, ls])
        return jnp.concatenate(parts + [c_ref[:, ls]], axis=0)

    def scores(p):
        ls = slice(p * LANES, (p + 1) * LANES)
        qp = q_ref[:, ls]
        k_all = keys_of(k_refs, kc_ref, ls)
        for hh in range(heads_per_pair):
            msk = lo if hh == 0 else jnp.logical_not(lo)
            qm = jnp.where(msk, qp, jnp.zeros_like(qp))
            s_buf[p % 2, hh, :n_loc + n_ctx] = lax.dot_general(k_all, qm, dims_nt, preferred_element_type=F32)

    def softmax_pv(p):
        ls = slice(p * LANES, (p + 1) * LANES)
        v_all = keys_of(v_refs, vc_ref, ls)
        o_t = []
        for hh in range(heads_per_pair):
            own = lo if hh == 0 else jnp.logical_not(lo)
            head = p * heads_per_pair + hh
            cols = []
            for j, (a, b, half) in enumerate(tiles):
                qs = slice(j * LANES, (j + 1) * LANES)
                blocks = []
                for s in range(a, b):
                    blk = (s_buf[p % 2, hh, s * GRID_W:(s + 1) * GRID_W, qs]
                           + bias_ref[head, (s_lo + s) * GRID_W:(s_lo + s + 1) * GRID_W, qs])
                    if s in half:
                        blk = jnp.where(lo if half[s][0] else jnp.logical_not(lo), blk, NEG)
                    blocks.append(blk)
                s_loc = jnp.concatenate(blocks, axis=0)
                s_ctx = s_buf[p % 2, hh, n_loc:n_loc + n_ctx, qs]
                m = jnp.maximum(jnp.max(s_loc, axis=0, keepdims=True), jnp.max(s_ctx, axis=0, keepdims=True))
                col = [jnp.exp2(s_loc - m), jnp.exp2(s_ctx - m)]
                if a > 0:
                    col.insert(0, jnp.zeros((a * GRID_W, LANES), F32))
                if b < n_slots:
                    col.insert(-1, jnp.zeros(((n_slots - b) * GRID_W, LANES), F32))
                cols.append(jnp.concatenate(col, axis=0))
            p_all = jnp.concatenate(cols, axis=1).astype(BF16)
            v_own = jnp.where(own, v_all, jnp.ones_like(v_all))
            o = lax.dot_general(v_own, p_all, dims_tn, preferred_element_type=F32)
            den = o[(1 - hh) * HEAD_DIM:(1 - hh) * HEAD_DIM + 1]
            o_t.append(o[hh * HEAD_DIM:(hh + 1) * HEAD_DIM] * (1.0 / den))
        o_ref[:, ls] = jnp.concatenate(o_t, axis=0).T.astype(BF16)

    scores(0)
    for p in range(n_pairs):
        if p + 1 < n_pairs:
            scores(p + 1)
        softmax_pv(p)


def _attn_kernel(q_ref, kp_ref, km_ref, kn_ref, vp_ref, vm_ref, vn_ref, kc_ref, vc_ref, bias_ref, o_ref, s_buf,
                 *, plans):
    i = pl.program_id(0)
    n_i = pl.num_programs(0)
    variant = (i > 0).astype(jnp.int32) + (i == n_i - 1).astype(jnp.int32)
    for v, plan in enumerate(plans):
        @pl.when(variant == v)
        def _(plan=plan):
            _attn_block(q_ref, (kp_ref, km_ref, kn_ref), (vp_ref, vm_ref, vn_ref), kc_ref, vc_ref,
                        bias_ref, o_ref, s_buf, plan)


def _attn_call(proj, projc, bias, *, n_b, seq, ctx_len, q_col, k_col, v_col, kc_col, vc_col):
    d = N_HEADS * HEAD_DIM
    n_i = seq // ATT_Q
    rows = seq // GRID_W
    plans = tuple(_window_plan(i, n_i, rows) for i in (0, 1, n_i - 1))

    def row(i, b):
        return b * n_i + i

    def row_prev(i, b):
        return b * n_i + jnp.maximum(i - 1, 0)

    def row_next(i, b):
        return b * n_i + jnp.minimum(i + 1, n_i - 1)

    blk = (ATT_Q, d)
    return pl.pallas_call(
        functools.partial(_attn_kernel, plans=plans),
        grid=(n_i, n_b),
        in_specs=[
            pl.BlockSpec(blk, lambda i, b: (row(i, b), q_col)),
            pl.BlockSpec(blk, lambda i, b: (row_prev(i, b), k_col)),
            pl.BlockSpec(blk, lambda i, b: (row(i, b), k_col)),
            pl.BlockSpec(blk, lambda i, b: (row_next(i, b), k_col)),
            pl.BlockSpec(blk, lambda i, b: (row_prev(i, b), v_col)),
            pl.BlockSpec(blk, lambda i, b: (row(i, b), v_col)),
            pl.BlockSpec(blk, lambda i, b: (row_next(i, b), v_col)),
            pl.BlockSpec((ctx_len, d), lambda i, b: (b, kc_col)),
            pl.BlockSpec((ctx_len, d), lambda i, b: (b, vc_col)),
            pl.BlockSpec(bias.shape, lambda i, b: (0, 0, 0), pipeline_mode=pl.Buffered(1)),
        ],
        out_specs=pl.BlockSpec(blk, lambda i, b: (row(i, b), 0)),
        out_shape=jax.ShapeDtypeStruct((n_b * seq, d), BF16),
        scratch_shapes=[pltpu.VMEM((2, LANES // HEAD_DIM, ATT_SLOTS * GRID_W + ctx_len, ATT_Q), F32)],
        compiler_params=pltpu.CompilerParams(
            dimension_semantics=("arbitrary", "arbitrary"), vmem_limit_bytes=VMEM_LIMIT),
        name="attn",
    )(proj, proj, proj, proj, proj, proj, proj, projc, projc, bias)


def _attn_bias(rpb):
    qc = np.arange(GRID_W)[None, :]
    kc = np.arange(GRID_W)[:, None]
    col_start = np.clip(qc - WIN_COLS // 2, 0, GRID_W - WIN_COLS)
    col_ok = (kc >= col_start) & (kc < col_start + WIN_COLS)
    dc_idx = np.clip(kc - qc + WIN_COLS - 1, 0, 2 * WIN_COLS - 2)
    onehot = (dc_idx[None] == np.arange(2 * WIN_COLS - 1)[:, None, None]) & col_ok[None]
    col_tab = jnp.einsum('had,dkq->hakq', rpb.astype(F32), jnp.asarray(onehot, F32),
                         precision=lax.Precision.HIGHEST)
    col_tab = jnp.where(jnp.asarray(col_ok)[None, None], col_tab * LOG2E, NEG)
    slots = []
    for s in range(ATT_SLOTS):
        q_rows = [col_tab[:, s - ATT_ROWS - rq + WIN_ROWS - 1] for rq in range(ATT_ROWS)]
        slots.append(jnp.concatenate(q_rows, axis=-1))
    return jnp.concatenate(slots, axis=1)


def _out_kernel(hf_ref, hr_ref, ggr_ref, gl_ref, gna_ref, att_ref, x_ref, gt1_ref, sh2_ref, sc2_ref,
                gt2_ref, gn_ref, wl_ref, wn_ref, wo_ref, w1_ref, w2_ref, out_ref):
    n_b, ts, d = x_ref.shape
    rows = n_b * ts
    y_na = jnp.dot(att_ref[...].reshape(rows, d), wn_ref[...], preferred_element_type=F32)
    hl = jnp.stack([
        jnp.concatenate([hf_ref[0, k, pl.ds(b, ts, stride=n_b), :] + hr_ref[0, k, pl.ds(b, ts, stride=n_b), :]
                         for k in range(d // LANES)], axis=1)
        for b in range(n_b)], axis=0)
    hl = hl * ggr_ref[...].astype(F32)
    y_lru = jnp.dot(hl.reshape(rows, d).astype(BF16), wl_ref[...], preferred_element_type=F32)
    y = (gl_ref[...].astype(F32).reshape(rows, d) * y_lru
         + gna_ref[...].astype(F32).reshape(rows, d) * y_na)
    y = jnp.dot(y.astype(BF16), wo_ref[...], preferred_element_type=F32).reshape(n_b, ts, d)
    x1 = x_ref[...] + gt1_ref[...] * _rms(y, gn_ref[1:2, :])
    u2 = _rms(x1, gn_ref[2:3, :]) * (1.0 + sc2_ref[...]) + sh2_ref[...]
    u2 = u2.reshape(rows, d).astype(BF16)
    m = None
    for f in range(0, w1_ref.shape[1], FF_CHUNK):
        h1 = jnp.dot(u2, w1_ref[:, f:f + FF_CHUNK], preferred_element_type=F32)
        h1 = jnp.maximum(h1, 0.0)
        h1 = (h1 * h1).astype(BF16)
        part = jnp.dot(h1, w2_ref[f:f + FF_CHUNK, :], preferred_element_type=F32)
        m = part if m is None else m + part
    out_ref[...] = x1 + gt2_ref[...] * _rms(m.reshape(n_b, ts, d), gn_ref[3:4, :])


def _out_call(h2, proj3, att3, x3, mod3, g_norm, wl, wn, wo, w1, w2, *, ts, ggr_col, gl_col, gna_col):
    n_b, seq, d = x3.shape

    def modspec(k):
        return pl.BlockSpec((n_b, 1, d), lambda t: (0, 0, k))

    def resident(w):
        return pl.BlockSpec(w.shape, lambda t: (0, 0), pipeline_mode=pl.Buffered(1))

    tile = (n_b, ts, d)
    return pl.pallas_call(
        _out_kernel,
        grid=(seq // ts,),
        in_specs=[
            pl.BlockSpec((1, d // LANES, ts * n_b, LANES), lambda t: (0, 0, t, 0)),
            pl.BlockSpec((1, d // LANES, ts * n_b, LANES), lambda t: (1, 0, t, 0)),
            pl.BlockSpec(tile, lambda t: (0, t, ggr_col)),
            pl.BlockSpec(tile, lambda t: (0, t, gl_col)),
            pl.BlockSpec(tile, lambda t: (0, t, gna_col)),
            pl.BlockSpec(tile, lambda t: (0, t, 0)),
            pl.BlockSpec(tile, lambda t: (0, t, 0)),
            modspec(2), modspec(3), modspec(4), modspec(5),
            pl.BlockSpec(g_norm.shape, lambda t: (0, 0)),
            resident(wl), resident(wn), resident(wo), resident(w1), resident(w2),
        ],
        out_specs=pl.BlockSpec(tile, lambda t: (0, t, 0)),
        out_shape=jax.ShapeDtypeStruct((n_b, seq, d), F32),
        compiler_params=pltpu.CompilerParams(
            dimension_semantics=("arbitrary",), vmem_limit_bytes=VMEM_LIMIT),
        name="out",
    )(h2, h2, proj3, proj3, proj3, att3, x3, mod3, mod3, mod3, mod3, g_norm, wl, wn, wo, w1, w2)


def _gate_weights(w_rg, b_rg):
    n_dir, n_gate, n_blk, bw, _ = w_rg.shape
    per = MXU_DIM // bw
    n_col = n_blk // per
    w_rg = 0.5 * w_rg
    b_rg = 0.5 * b_rg
    w = w_rg.reshape(n_dir, n_gate, n_col, per, bw, bw)
    eye = jnp.eye(per, dtype=w.dtype)
    wbd = jnp.einsum('dgcpij,pq->dgcpiqj', w, eye).reshape(n_dir, n_gate, n_col, MXU_DIM, MXU_DIM)
    wbd = jnp.concatenate([wbd[:, 0], wbd[:, 1]], axis=-1).astype(BF16)
    b = b_rg.reshape(n_dir, n_gate, n_col, MXU_DIM)
    brg = jnp.concatenate([b[:, 0], b[:, 1]], axis=-1)[:, :, None, :]
    return wbd, brg


def kernel(x, c, ctx, c_ctx, w_ada, b_ada, g_norm, w_in, b_gate, conv_w, conv_b, w_rg, b_rg, lam, rpb,
           w_lru_out, w_na_out, w_o, w_mlp1, w_mlp2):
    n_b, seq, d = x.shape
    ctx_len = ctx.shape[1]
    rows = seq // GRID_W
    depth = w_ada.shape[0]
    assert depth == 1 and rows % ATT_ROWS == 0 and n_b == SUBLANES

    l = 0
    cs = jnp.concatenate([c, c_ctx[None], jnp.zeros((2 * SUBLANES - n_b - 1, d), F32)], axis=0)
    mod = _mod_call(cs, w_ada[l], b_ada[l][None])
    mod3 = mod[:, None, :]

    w_in_bf = w_in[l].astype(BF16)
    b_gate2 = b_gate[l][None]
    g0 = g_norm[l, 0:1]

    xr, proj3 = _inproj_call(x, mod3, None, g0, w_in_bf, b_gate2, ts=64, segs=(0, 1, 2, 3, 4, 5, 6))
    xr_c, projc3 = _inproj_call(ctx, mod3, n_b, g0, w_in_bf, b_gate2, ts=64, segs=(0, 3, 4))

    wbd, brg = _gate_weights(w_rg[l], b_rg[l])
    lam3 = lam[l][:, None, :]
    cb = conv_b[l][None]
    _, h0 = _scan_call(xr_c, conv_w[l], cb, wbd, brg, lam3, jnp.zeros((2, d // LANES, n_b, LANES), F32), tc=128)
    h, _ = _scan_call(xr, conv_w[l], cb, wbd, brg, lam3, h0, tc=128)

    bias = _attn_bias(rpb[l])
    att = _attn_call(proj3.reshape(n_b * seq, -1), projc3.reshape(n_b * ctx_len, -1), bias,
                     n_b=n_b, seq=seq, ctx_len=ctx_len, q_col=1, k_col=2, v_col=3, kc_col=0, vc_col=1)

    return _out_call(h, proj3, att.reshape(n_b, seq, d), x, mod3, g_norm[l],
                     w_lru_out[l].astype(BF16), w_na_out[l].astype(BF16), w_o[l].astype(BF16),
                     w_mlp1[l].astype(BF16), w_mlp2[l].astype(BF16),
                     ts=64, ggr_col=0, gl_col=4, gna_col=5)
```

```python
import functools

import numpy as np
import jax
import jax.numpy as jnp
from jax import lax
from jax.experimental import pallas as pl
from jax.experimental.pallas import tpu as pltpu

F32 = jnp.float32
BF16 = jnp.bfloat16

EPS = 1e-6
NEG = -1e30
LRU_C = 8.0
N_HEADS = 16
HEAD_DIM = 64
GRID_W = 64
WIN_ROWS = 8
WIN_COLS = 16
N_LRU_BLOCKS = 16
CONV_W = 4
CONV_PAD_LEFT = 2

LANES = 128
SUBLANES = 8
MXU_DIM = 256
VMEM_LIMIT = 56 * 1024 * 1024

ATT_ROWS = 4
ATT_Q = ATT_ROWS * GRID_W
ATT_SLOTS = 3 * ATT_ROWS
FF_CHUNK = 1024
LOG2E = 1.4426950408889634


def _rms(x, g):
    return x * lax.rsqrt(jnp.mean(x * x, axis=-1, keepdims=True) + EPS) * g


def _mod_kernel(c_ref, w_ref, b_ref, o_ref):
    c = c_ref[...]
    s = c * jax.nn.sigmoid(c)
    o_ref[...] = jnp.dot(s, w_ref[...], preferred_element_type=F32) + b_ref[...]


def _mod_call(cs, w_ada, b_ada):
    rows, d = cs.shape
    n_out = w_ada.shape[1]
    return pl.pallas_call(
        _mod_kernel,
        grid=(n_out // d,),
        in_specs=[
            pl.BlockSpec((rows, d), lambda n: (0, 0)),
            pl.BlockSpec((d, d), lambda n: (0, n)),
            pl.BlockSpec((1, d), lambda n: (0, n)),
        ],
        out_specs=pl.BlockSpec((rows, d), lambda n: (0, n)),
        out_shape=jax.ShapeDtypeStruct((rows, n_out), F32),
        compiler_params=pltpu.CompilerParams(
            dimension_semantics=("arbitrary",), vmem_limit_bytes=VMEM_LIMIT),
        name="mod",
    )(cs, w_ada, b_ada)


def _inproj_kernel(x_ref, sh_ref, sc_ref, g_ref, w_ref, bg_ref, xr_ref, pr_ref, *, segs):
    n_b, ts, d = x_ref.shape
    u = _rms(x_ref[...], g_ref[...]) * (1.0 + sc_ref[...]) + sh_ref[...]
    u = u.reshape(n_b * ts, d).astype(BF16)
    for idx, seg in enumerate(segs):
        acc = jnp.dot(u, w_ref[:, seg * d:(seg + 1) * d], preferred_element_type=F32)
        if seg == 0:
            for b in range(n_b):
                for k in range(d // LANES):
                    xr_ref[k, pl.ds(b, ts, stride=n_b), :] = acc[b * ts:(b + 1) * ts, k * LANES:(k + 1) * LANES]
            continue
        if seg == 1:
            val = jax.nn.gelu(acc)
        elif seg == 2:
            val = acc * (HEAD_DIM ** -0.5 * LOG2E)
        elif seg in (3, 4):
            val = acc
        else:
            val = jax.nn.sigmoid(acc + bg_ref[:, (seg - 5) * d:(seg - 4) * d])
        pr_ref[:, :, (idx - 1) * d:idx * d] = val.astype(BF16).reshape(n_b, ts, d)


def _inproj_call(x3, mod3, mod_row, g0, w_in_bf, b_gate2, *, ts, segs):
    n_b, seq, d = x3.shape
    segs = tuple(segs)
    n_out = (len(segs) - 1) * d
    if mod_row is None:
        mod_blk, mod_idx = (n_b, 1, d), 0
    else:
        mod_blk, mod_idx = (1, 1, d), mod_row
    kern = functools.partial(_inproj_kernel, segs=segs)
    return pl.pallas_call(
        kern,
        grid=(seq // ts,),
        in_specs=[
            pl.BlockSpec((n_b, ts, d), lambda t: (0, t, 0)),
            pl.BlockSpec(mod_blk, lambda t: (mod_idx, 0, 0)),
            pl.BlockSpec(mod_blk, lambda t: (mod_idx, 0, 1)),
            pl.BlockSpec((1, d), lambda t: (0, 0)),
            pl.BlockSpec(w_in_bf.shape, lambda t: (0, 0), pipeline_mode=pl.Buffered(1)),
            pl.BlockSpec(b_gate2.shape, lambda t: (0, 0)),
        ],
        out_specs=[
            pl.BlockSpec((d // LANES, ts * n_b, LANES), lambda t: (0, t, 0)),
            pl.BlockSpec((n_b, ts, n_out), lambda t: (0, t, 0)),
        ],
        out_shape=[
            jax.ShapeDtypeStruct((d // LANES, seq * n_b, LANES), F32),
            jax.ShapeDtypeStruct((n_b, seq, n_out), BF16),
        ],
        compiler_params=pltpu.CompilerParams(
            dimension_semantics=("arbitrary",), vmem_limit_bytes=VMEM_LIMIT),
        name="inproj",
    )(x3, mod3, mod3, g0, w_in_bf, b_gate2)


def _scan_kernel(prev_ref, main_ref, next_ref, cw_ref, cb_ref, wbd_ref, brg_ref, lam_ref, h0_ref,
                 h_ref, hfin_ref, a_s, b_s, hst, *, n_chunks, tc):
    d = pl.program_id(0)
    i = pl.program_id(1)
    ci = jnp.where(d == 0, i, n_chunks - 1 - i)

    @pl.when(i == 0)
    def _():
        hst[...] = h0_ref[d]

    has_prev = (ci > 0).astype(F32)
    has_next = (ci < n_chunks - 1).astype(F32)
    n_lc = a_s.shape[0]
    rows = tc * SUBLANES
    per = MXU_DIM // LANES
    for c in range(n_lc // per):
        xcs = []
        for k in range(c * per, (c + 1) * per):
            sl = slice(k * LANES, (k + 1) * LANES)
            xe = jnp.concatenate([prev_ref[k] * has_prev, main_ref[k], next_ref[k] * has_next], axis=0)
            xc = cb_ref[:, sl]
            for j in range(CONV_W):
                xc = xc + cw_ref[j:j + 1, sl] * xe[j * SUBLANES:j * SUBLANES + rows]
            xcs.append(xc)
        xc2 = jnp.concatenate(xcs, axis=1)
        sl = slice(c * MXU_DIM, (c + 1) * MXU_DIM)
        g = jnp.dot(xc2.astype(BF16), wbd_ref[0, c], preferred_element_type=F32) + brg_ref[0, c]
        r2 = jnp.tanh(g[:, :MXU_DIM]) + 1.0
        i2 = jnp.tanh(g[:, MXU_DIM:]) + 1.0
        z = -lam_ref[0, :, sl]
        softplus = jnp.maximum(z, 0.0) + jnp.log1p(jnp.exp(-jnp.abs(z)))
        c_nat = (0.5 * LRU_C) * softplus
        neg_log_a = c_nat * r2
        a = jnp.exp2((-LOG2E * c_nat) * r2)
        one_minus_a2 = jnp.tanh(neg_log_a) * (a * a + 1.0)
        root = jnp.where(one_minus_a2 > 0.0, one_minus_a2 * lax.rsqrt(one_minus_a2), 0.0)
        bb = root * (i2 * (0.5 * xc2))
        for kk in range(per):
            a_s[c * per + kk, :rows] = a[:, kk * LANES:(kk + 1) * LANES]
            b_s[c * per + kk, :rows] = bb[:, kk * LANES:(kk + 1) * LANES]

    def step(j, h):
        t = jnp.where(d == 0, j, tc - 1 - j)
        r0 = pl.multiple_of(t * SUBLANES, SUBLANES)
        h = a_s[:, pl.ds(r0, SUBLANES), :] * h + b_s[:, pl.ds(r0, SUBLANES), :]
        h_ref[0, :, pl.ds(r0, SUBLANES), :] = h
        return h

    h_last = lax.fori_loop(0, tc, step, hst[...], unroll=8)
    hst[...] = h_last
    hfin_ref[0] = h_last


def _scan_call(xr, conv_w, conv_b, wbd, brg, lam3, h0, *, tc):
    n_lc, n_rows, _ = xr.shape
    n_b = SUBLANES
    seq = n_rows // n_b
    d = n_lc * LANES
    n_chunks = seq // tc
    n_col = d // MXU_DIM
    rows = tc * n_b

    def chunk(dd, i):
        return jnp.where(dd == 0, i, n_chunks - 1 - i)

    kern = functools.partial(_scan_kernel, n_chunks=n_chunks, tc=tc)
    return pl.pallas_call(
        kern,
        grid=(2, n_chunks),
        in_specs=[
            pl.BlockSpec((n_lc, 2 * n_b, LANES),
                         lambda dd, i: (0, jnp.maximum(chunk(dd, i) * (tc // 2) - 1, 0), 0)),
            pl.BlockSpec((n_lc, rows, LANES), lambda dd, i: (0, chunk(dd, i), 0)),
            pl.BlockSpec((n_lc, n_b, LANES),
                         lambda dd, i: (0, jnp.minimum((chunk(dd, i) + 1) * tc, seq - 1), 0)),
            pl.BlockSpec((CONV_W, d), lambda dd, i: (0, 0)),
            pl.BlockSpec((1, d), lambda dd, i: (0, 0)),
            pl.BlockSpec((1, n_col, MXU_DIM, 2 * MXU_DIM), lambda dd, i: (dd, 0, 0, 0)),
            pl.BlockSpec((1, n_col, 1, 2 * MXU_DIM), lambda dd, i: (dd, 0, 0, 0)),
            pl.BlockSpec((1, 1, d), lambda dd, i: (dd, 0, 0)),
            pl.BlockSpec((2, n_lc, n_b, LANES), lambda dd, i: (0, 0, 0, 0)),
        ],
        out_specs=[
            pl.BlockSpec((1, n_lc, rows, LANES), lambda dd, i: (dd, 0, chunk(dd, i), 0)),
            pl.BlockSpec((1, n_lc, n_b, LANES), lambda dd, i: (dd, 0, 0, 0)),
        ],
        out_shape=[
            jax.ShapeDtypeStruct((2, n_lc, n_rows, LANES), F32),
            jax.ShapeDtypeStruct((2, n_lc, n_b, LANES), F32),
        ],
        scratch_shapes=[
            pltpu.VMEM((n_lc, rows + n_b, LANES), F32),
            pltpu.VMEM((n_lc, rows + n_b, LANES), F32),
            pltpu.VMEM((n_lc, n_b, LANES), F32),
        ],
        compiler_params=pltpu.CompilerParams(
            dimension_semantics=("arbitrary", "arbitrary"), vmem_limit_bytes=VMEM_LIMIT),
        name="scan",
    )(xr, xr, xr, conv_w, conv_b, wbd, brg, lam3, h0)


def _window_plan(i, n_i, rows):
    wr = min(WIN_ROWS, rows)
    rng = []
    for rq in range(ATT_ROWS):
        r = ATT_ROWS * i + rq
        r0 = min(max(r - wr // 2, 0), rows - wr)
        lo = r0 - (ATT_ROWS * i - ATT_ROWS)
        rng.append((lo, lo + wr))
    s_lo = min(lo for lo, _ in rng)
    s_hi = max(hi for _, hi in rng)
    assert 0 <= s_lo and s_hi <= ATT_SLOTS
    rows_per_tile = LANES // GRID_W
    tiles = []
    for j in range(ATT_ROWS // rows_per_tile):
        rqs = range(j * rows_per_tile, (j + 1) * rows_per_tile)
        a = min(rng[rq][0] for rq in rqs)
        b = max(rng[rq][1] for rq in rqs)
        half = {}
        for s in range(a, b):
            ok = tuple(rng[rq][0] <= s < rng[rq][1] for rq in rqs)
            if not all(ok):
                half[s - s_lo] = ok
        tiles.append((a - s_lo, b - s_lo, half))
    return s_lo, s_hi, tiles


def _attn_block(q_ref, k_refs, v_refs, kc_ref, vc_ref, bias_ref, o_ref, s_buf, plan):
    s_lo, s_hi, tiles = plan
    n_slots = s_hi - s_lo
    n_loc = n_slots * GRID_W
    n_ctx = kc_ref.shape[0]
    lane = lax.broadcasted_iota(jnp.int32, (1, LANES), 1)
    lo = lane < HEAD_DIM
    heads_per_pair = LANES // HEAD_DIM
    n_pairs = N_HEADS // heads_per_pair
    dims_nt = (((1,), (1,)), ((), ()))
    dims_tn = (((0,), (0,)), ((), ()))

    def keys_of(refs, c_ref, ls):
        parts = []
        for blk, ref in enumerate(refs):
            a = max(s_lo, blk * ATT_ROWS) - blk * ATT_ROWS
            b = min(s_hi, (blk + 1) * ATT_ROWS) - blk * ATT_ROWS
            if b > a:
                parts.append(ref[a * GRID_W:b * GRID_W, ls])
        return jnp.concatenate(parts + [c_ref[:, ls]], axis=0)

    def scores(p):
        ls = slice(p * LANES, (p + 1) * LANES)
        qp = q_ref[:, ls]
        k_all = keys_of(k_refs, kc_ref, ls)
        for hh in range(heads_per_pair):
            msk = lo if hh == 0 else jnp.logical_not(lo)
            qm = jnp.where(msk, qp, jnp.zeros_like(qp))
            s_buf[p % 2, hh, :n_loc + n_ctx] = lax.dot_general(k_all, qm, dims_nt, preferred_element_type=F32)

    def softmax_pv(p):
        ls = slice(p * LANES, (p + 1) * LANES)
        v_all = keys_of(v_refs, vc_ref, ls)
        o_t = []
        for hh in range(heads_per_pair):
            own = lo if hh == 0 else jnp.logical_not(lo)
            head = p * heads_per_pair + hh
            cols = []
            for j, (a, b, half) in enumerate(tiles):
                qs = slice(j * LANES, (j + 1) * LANES)
                blocks = []
                for s in range(a, b):
                    blk = (s_buf[p % 2, hh, s * GRID_W:(s + 1) * GRID_W, qs]
                           + bias_ref[head, (s_lo + s) * GRID_W:(s_lo + s + 1) * GRID_W, qs])
                    if s in half:
                        blk = jnp.where(lo if half[s][0] else jnp.logical_not(lo), blk, NEG)
                    blocks.append(blk)
                s_loc = jnp.concatenate(blocks, axis=0)
                s_ctx = s_buf[p % 2, hh, n_loc:n_loc + n_ctx, qs]
                m = jnp.maximum(jnp.max(s_loc, axis=0, keepdims=True), jnp.max(s_ctx, axis=0, keepdims=True))
                col = [jnp.exp2(s_loc - m), jnp.exp2(s_ctx - m)]
                if a > 0:
                    col.insert(0, jnp.zeros((a * GRID_W, LANES), F32))
                if b < n_slots:
                    col.insert(-1, jnp.zeros(((n_slots - b) * GRID_W, LANES), F32))
                cols.append(jnp.concatenate(col, axis=0))
            p_all = jnp.concatenate(cols, axis=1).astype(BF16)
            v_own = jnp.where(own, v_all, jnp.ones_like(v_all))
            o = lax.dot_general(v_own, p_all, dims_tn, preferred_element_type=F32)
            den = o[(1 - hh) * HEAD_DIM:(1 - hh) * HEAD_DIM + 1]
            o_t.append(o[hh * HEAD_DIM:(hh + 1) * HEAD_DIM] * (1.0 / den))
        o_ref[:, ls] = jnp.concatenate(o_t, axis=0).T.astype(BF16)

    scores(0)
    for p in range(n_pairs):
        if p + 1 < n_pairs:
            scores(p + 1)
        softmax_pv(p)


def _attn_kernel(q_ref, k0_ref, kn_ref, v0_ref, vn_ref, kc_ref, vc_ref, bias_ref, o_ref, s_buf, k_win, v_win,
                 *, plans):
    i = pl.program_id(1)
    n_i = pl.num_programs(1)

    @pl.when(i == 0)
    def _():
        k_win[0] = k0_ref[...]
        v_win[0] = v0_ref[...]

    nxt = lax.rem(i + 1, 3)
    k_win[nxt] = kn_ref[...]
    v_win[nxt] = vn_ref[...]
    slots = (lax.rem(i + 2, 3), lax.rem(i, 3), nxt)
    k_refs = tuple(k_win.at[s] for s in slots)
    v_refs = tuple(v_win.at[s] for s in slots)

    variant = (i > 0).astype(jnp.int32) + (i == n_i - 1).astype(jnp.int32)
    for v, plan in enumerate(plans):
        @pl.when(variant == v)
        def _(plan=plan):
            _attn_block(q_ref, k_refs, v_refs, kc_ref, vc_ref, bias_ref, o_ref, s_buf, plan)


def _attn_call(proj, projc, bias, *, n_b, seq, ctx_len, q_col, k_col, v_col, kc_col, vc_col):
    d = N_HEADS * HEAD_DIM
    n_i = seq // ATT_Q
    rows = seq // GRID_W
    plans = tuple(_window_plan(i, n_i, rows) for i in (0, 1, n_i - 1))

    def row(b, i):
        return b * n_i + i

    def row_first(b, i):
        return b * n_i

    def row_next(b, i):
        return b * n_i + jnp.minimum(i + 1, n_i - 1)

    blk = (ATT_Q, d)
    return pl.pallas_call(
        functools.partial(_attn_kernel, plans=plans),
        grid=(n_b, n_i),
        in_specs=[
            pl.BlockSpec(blk, lambda b, i: (row(b, i), q_col)),
            pl.BlockSpec(blk, lambda b, i: (row_first(b, i), k_col)),
            pl.BlockSpec(blk, lambda b, i: (row_next(b, i), k_col)),
            pl.BlockSpec(blk, lambda b, i: (row_first(b, i), v_col)),
            pl.BlockSpec(blk, lambda b, i: (row_next(b, i), v_col)),
            pl.BlockSpec((ctx_len, d), lambda b, i: (b, kc_col)),
            pl.BlockSpec((ctx_len, d), lambda b, i: (b, vc_col)),
            pl.BlockSpec(bias.shape, lambda b, i: (0, 0, 0), pipeline_mode=pl.Buffered(1)),
        ],
        out_specs=pl.BlockSpec(blk, lambda b, i: (row(b, i), 0)),
        out_shape=jax.ShapeDtypeStruct((n_b * seq, d), BF16),
        scratch_shapes=[
            pltpu.VMEM((2, LANES // HEAD_DIM, ATT_SLOTS * GRID_W + ctx_len, ATT_Q), F32),
            pltpu.VMEM((3,) + blk, BF16),
            pltpu.VMEM((3,) + blk, BF16),
        ],
        compiler_params=pltpu.CompilerParams(
            dimension_semantics=("arbitrary", "arbitrary"), vmem_limit_bytes=VMEM_LIMIT),
        name="attn",
    )(proj, proj, proj, proj, proj, projc, projc, bias)


def _attn_bias(rpb):
    qc = np.arange(GRID_W)[None, :]
    kc = np.arange(GRID_W)[:, None]
    col_start = np.clip(qc - WIN_COLS // 2, 0, GRID_W - WIN_COLS)
    col_ok = (kc >= col_start) & (kc < col_start + WIN_COLS)
    dc_idx = np.clip(kc - qc + WIN_COLS - 1, 0, 2 * WIN_COLS - 2)
    onehot = (dc_idx[None] == np.arange(2 * WIN_COLS - 1)[:, None, None]) & col_ok[None]
    col_tab = jnp.einsum('had,dkq->hakq', rpb.astype(F32), jnp.asarray(onehot, F32),
                         precision=lax.Precision.HIGHEST)
    col_tab = jnp.where(jnp.asarray(col_ok)[None, None], col_tab * LOG2E, NEG)
    slots = []
    for s in range(ATT_SLOTS):
        q_rows = [col_tab[:, s - ATT_ROWS - rq + WIN_ROWS - 1] for rq in range(ATT_ROWS)]
        slots.append(jnp.concatenate(q_rows, axis=-1))
    return jnp.concatenate(slots, axis=1)


def _out_kernel(hf_ref, hr_ref, ggr_ref, gl_ref, gna_ref, att_ref, x_ref, gt1_ref, sh2_ref, sc2_ref,
                gt2_ref, gn_ref, wl_ref, wn_ref, wo_ref, w1_ref, w2_ref, out_ref):
    n_b, ts, d = x_ref.shape
    rows = n_b * ts
    y_na = jnp.dot(att_ref[...].reshape(rows, d), wn_ref[...], preferred_element_type=F32)
    hl = jnp.stack([
        jnp.concatenate([hf_ref[0, k, pl.ds(b, ts, stride=n_b), :] + hr_ref[0, k, pl.ds(b, ts, stride=n_b), :]
                         for k in range(d // LANES)], axis=1)
        for b in range(n_b)], axis=0)
    hl = hl * ggr_ref[...].astype(F32)
    y_lru = jnp.dot(hl.reshape(rows, d).astype(BF16), wl_ref[...], preferred_element_type=F32)
    y = (gl_ref[...].astype(F32).reshape(rows, d) * y_lru
         + gna_ref[...].astype(F32).reshape(rows, d) * y_na)
    y = jnp.dot(y.astype(BF16), wo_ref[...], preferred_element_type=F32).reshape(n_b, ts, d)
    x1 = x_ref[...] + gt1_ref[...] * _rms(y, gn_ref[1:2, :])
    u2 = _rms(x1, gn_ref[2:3, :]) * (1.0 + sc2_ref[...]) + sh2_ref[...]
    u2 = u2.reshape(rows, d).astype(BF16)
    m = None
    for f in range(0, w1_ref.shape[1], FF_CHUNK):
        h1 = jnp.dot(u2, w1_ref[:, f:f + FF_CHUNK], preferred_element_type=F32)
        h1 = jnp.maximum(h1, 0.0)
        h1 = (h1 * h1).astype(BF16)
        part = jnp.dot(h1, w2_ref[f:f + FF_CHUNK, :], preferred_element_type=F32)
        m = part if m is None else m + part
    out_ref[...] = x1 + gt2_ref[...] * _rms(m.reshape(n_b, ts, d), gn_ref[3:4, :])


def _out_call(h2, proj3, att3, x3, mod3, g_norm, wl, wn, wo, w1, w2, *, ts, ggr_col, gl_col, gna_col):
    n_b, seq, d = x3.shape

    def modspec(k):
        return pl.BlockSpec((n_b, 1, d), lambda t: (0, 0, k))

    def resident(w):
        return pl.BlockSpec(w.shape, lambda t: (0, 0), pipeline_mode=pl.Buffered(1))

    tile = (n_b, ts, d)
    return pl.pallas_call(
        _out_kernel,
        grid=(seq // ts,),
        in_specs=[
            pl.BlockSpec((1, d // LANES, ts * n_b, LANES), lambda t: (0, 0, t, 0)),
            pl.BlockSpec((1, d // LANES, ts * n_b, LANES), lambda t: (1, 0, t, 0)),
            pl.BlockSpec(tile, lambda t: (0, t, ggr_col)),
            pl.BlockSpec(tile, lambda t: (0, t, gl_col)),
            pl.BlockSpec(tile, lambda t: (0, t, gna_col)),
            pl.BlockSpec(tile, lambda t: (0, t, 0)),
            pl.BlockSpec(tile, lambda t: (0, t, 0)),
            modspec(2), modspec(3), modspec(4), modspec(5),
            pl.BlockSpec(g_norm.shape, lambda t: (0, 0)),
            resident(wl), resident(wn), resident(wo), resident(w1), resident(w2),
        ],
        out_specs=pl.BlockSpec(tile, lambda t: (0, t, 0)),
        out_shape=jax.ShapeDtypeStruct((n_b, seq, d), F32),
        compiler_params=pltpu.CompilerParams(
            dimension_semantics=("arbitrary",), vmem_limit_bytes=VMEM_LIMIT),
        name="out",
    )(h2, h2, proj3, proj3, proj3, att3, x3, mod3, mod3, mod3, mod3, g_norm, wl, wn, wo, w1, w2)


def _gate_weights(w_rg, b_rg):
    n_dir, n_gate, n_blk, bw, _ = w_rg.shape
    per = MXU_DIM // bw
    n_col = n_blk // per
    w_rg = 0.5 * w_rg
    b_rg = 0.5 * b_rg
    w = w_rg.reshape(n_dir, n_gate, n_col, per, bw, bw)
    eye = jnp.eye(per, dtype=w.dtype)
    wbd = jnp.einsum('dgcpij,pq->dgcpiqj', w, eye).reshape(n_dir, n_gate, n_col, MXU_DIM, MXU_DIM)
    wbd = jnp.concatenate([wbd[:, 0], wbd[:, 1]], axis=-1).astype(BF16)
    b = b_rg.reshape(n_dir, n_gate, n_col, MXU_DIM)
    brg = jnp.concatenate([b[:, 0], b[:, 1]], axis=-1)[:, :, None, :]
    return wbd, brg


def kernel(x, c, ctx, c_ctx, w_ada, b_ada, g_norm, w_in, b_gate, conv_w, conv_b, w_rg, b_rg, lam, rpb,
           w_lru_out, w_na_out, w_o, w_mlp1, w_mlp2):
    n_b, seq, d = x.shape
    ctx_len = ctx.shape[1]
    rows = seq // GRID_W
    depth = w_ada.shape[0]
    assert depth == 1 and rows % ATT_ROWS == 0 and n_b == SUBLANES

    l = 0
    cs = jnp.concatenate([c, c_ctx[None], jnp.zeros((2 * SUBLANES - n_b - 1, d), F32)], axis=0)
    mod = _mod_call(cs, w_ada[l], b_ada[l][None])
    mod3 = mod[:, None, :]

    w_in_bf = w_in[l].astype(BF16)
    b_gate2 = b_gate[l][None]
    g0 = g_norm[l, 0:1]

    xr, proj3 = _inproj_call(x, mod3, None, g0, w_in_bf, b_gate2, ts=64, segs=(0, 1, 2, 3, 4, 5, 6))
    xr_c, projc3 = _inproj_call(ctx, mod3, n_b, g0, w_in_bf, b_gate2, ts=64, segs=(0, 3, 4))

    wbd, brg = _gate_weights(w_rg[l], b_rg[l])
    lam3 = lam[l][:, None, :]
    cb = conv_b[l][None]
    _, h0 = _scan_call(xr_c, conv_w[l], cb, wbd, brg, lam3, jnp.zeros((2, d // LANES, n_b, LANES), F32), tc=128)
    h, _ = _scan_call(xr, conv_w[l], cb, wbd, brg, lam3, h0, tc=128)

    bias = _attn_bias(rpb[l])
    att = _attn_call(proj3.reshape(n_b * seq, -1), projc3.reshape(n_b * ctx_len, -1), bias,
                     n_b=n_b, seq=seq, ctx_len=ctx_len, q_col=1, k_col=2, v_col=3, kc_col=0, vc_col=1)

    return _out_call(h, proj3, att.reshape(n_b, seq, d), x, mod3, g_norm[l],
                     w_lru_out[l].astype(BF16), w_na_out[l].astype(BF16), w_o[l].astype(BF16),
                     w_mlp1[l].astype(BF16), w_mlp2[l].astype(BF16),
                     ts=64, ggr_col=0, gl_col=4, gna_col=5)
```

```python
import functools

import numpy as np
import jax
import jax.numpy as jnp
from jax import lax
from jax.experimental import pallas as pl
from jax.experimental.pallas import tpu as pltpu

F32 = jnp.float32
BF16 = jnp.bfloat16

EPS = 1e-6
NEG = -1e30
LRU_C = 8.0
N_HEADS = 16
HEAD_DIM = 64
GRID_W = 64
WIN_ROWS = 8
WIN_COLS = 16
N_LRU_BLOCKS = 16
CONV_W = 4
CONV_PAD_LEFT = 2

LANES = 128
SUBLANES = 8
MXU_DIM = 256
VMEM_LIMIT = 56 * 1024 * 1024

ATT_ROWS = 4
ATT_Q = ATT_ROWS * GRID_W
ATT_SLOTS = 3 * ATT_ROWS
FF_CHUNK = 1024
LOG2E = 1.4426950408889634


def _rms(x, g):
    return x * lax.rsqrt(jnp.mean(x * x, axis=-1, keepdims=True) + EPS) * g


def _mod_kernel(c_ref, w_ref, b_ref, o_ref):
    c = c_ref[...]
    s = c * jax.nn.sigmoid(c)
    o_ref[...] = jnp.dot(s, w_ref[...], preferred_element_type=F32) + b_ref[...]


def _mod_call(cs, w_ada, b_ada):
    rows, d = cs.shape
    n_out = w_ada.shape[1]
    return pl.pallas_call(
        _mod_kernel,
        grid=(n_out // d,),
        in_specs=[
            pl.BlockSpec((rows, d), lambda n: (0, 0)),
            pl.BlockSpec((d, d), lambda n: (0, n)),
            pl.BlockSpec((1, d), lambda n: (0, n)),
        ],
        out_specs=pl.BlockSpec((rows, d), lambda n: (0, n)),
        out_shape=jax.ShapeDtypeStruct((rows, n_out), F32),
        compiler_params=pltpu.CompilerParams(
            dimension_semantics=("arbitrary",), vmem_limit_bytes=VMEM_LIMIT),
        name="mod",
    )(cs, w_ada, b_ada)


def _inproj_kernel(x_ref, sh_ref, sc_ref, g_ref, w_ref, bg_ref, xr_ref, pr_ref, *, segs):
    n_b, ts, d = x_ref.shape
    u = _rms(x_ref[...], g_ref[...]) * (1.0 + sc_ref[...]) + sh_ref[...]
    u = u.reshape(n_b * ts, d).astype(BF16)
    for idx, seg in enumerate(segs):
        acc = jnp.dot(u, w_ref[:, seg * d:(seg + 1) * d], preferred_element_type=F32)
        if seg == 0:
            for b in range(n_b):
                for k in range(d // LANES):
                    xr_ref[k, pl.ds(b, ts, stride=n_b), :] = acc[b * ts:(b + 1) * ts, k * LANES:(k + 1) * LANES]
            continue
        if seg == 1:
            val = jax.nn.gelu(acc)
        elif seg == 2:
            val = acc * (HEAD_DIM ** -0.5 * LOG2E)
        elif seg in (3, 4):
            val = acc
        else:
            val = jax.nn.sigmoid(acc + bg_ref[:, (seg - 5) * d:(seg - 4) * d])
        pr_ref[:, :, (idx - 1) * d:idx * d] = val.astype(BF16).reshape(n_b, ts, d)


def _inproj_call(x3, mod3, mod_row, g0, w_in_bf, b_gate2, *, ts, segs):
    n_b, seq, d = x3.shape
    segs = tuple(segs)
    n_out = (len(segs) - 1) * d
    if mod_row is None:
        mod_blk, mod_idx = (n_b, 1, d), 0
    else:
        mod_blk, mod_idx = (1, 1, d), mod_row
    kern = functools.partial(_inproj_kernel, segs=segs)
    return pl.pallas_call(
        kern,
        grid=(seq // ts,),
        in_specs=[
            pl.BlockSpec((n_b, ts, d), lambda t: (0, t, 0)),
            pl.BlockSpec(mod_blk, lambda t: (mod_idx, 0, 0)),
            pl.BlockSpec(mod_blk, lambda t: (mod_idx, 0, 1)),
            pl.BlockSpec((1, d), lambda t: (0, 0)),
            pl.BlockSpec(w_in_bf.shape, lambda t: (0, 0), pipeline_mode=pl.Buffered(1)),
            pl.BlockSpec(b_gate2.shape, lambda t: (0, 0)),
        ],
        out_specs=[
            pl.BlockSpec((d // LANES, ts * n_b, LANES), lambda t: (0, t, 0)),
            pl.BlockSpec((n_b, ts, n_out), lambda t: (0, t, 0)),
        ],
        out_shape=[
            jax.ShapeDtypeStruct((d // LANES, seq * n_b, LANES), F32),
            jax.ShapeDtypeStruct((n_b, seq, n_out), BF16),
        ],
        compiler_params=pltpu.CompilerParams(
            dimension_semantics=("arbitrary",), vmem_limit_bytes=VMEM_LIMIT),
        name="inproj",
    )(x3, mod3, mod3, g0, w_in_bf, b_gate2)


def _scan_kernel(*refs, n_chunks, tc, reverse):
    if reverse:
        xc_ref, wbd_ref, brg_ref, lam_ref, h0_ref, h_ref, hfin_ref, a_s, b_s, hst = refs
    else:
        (prev_ref, main_ref, next_ref, cw_ref, cb_ref, wbd_ref, brg_ref, lam_ref, h0_ref,
         h_ref, xc_ref, hfin_ref, a_s, b_s, hst) = refs
    i = pl.program_id(0)

    @pl.when(i == 0)
    def _():
        hst[...] = h0_ref[...]

    n_lc = a_s.shape[0]
    rows = tc * SUBLANES
    per = MXU_DIM // LANES
    if not reverse:
        has_prev = (i > 0).astype(F32)
        has_next = (i < n_chunks - 1).astype(F32)
    for c in range(n_lc // per):
        xcs = []
        for k in range(c * per, (c + 1) * per):
            if reverse:
                xc = xc_ref[k]
            else:
                sl = slice(k * LANES, (k + 1) * LANES)
                xe = jnp.concatenate([prev_ref[k] * has_prev, main_ref[k], next_ref[k] * has_next], axis=0)
                xc = cb_ref[:, sl]
                for j in range(CONV_W):
                    xc = xc + cw_ref[j:j + 1, sl] * xe[j * SUBLANES:j * SUBLANES + rows]
                xc_ref[k] = xc
            xcs.append(xc)
        xc2 = jnp.concatenate(xcs, axis=1)
        sl = slice(c * MXU_DIM, (c + 1) * MXU_DIM)
        g = jnp.dot(xc2.astype(BF16), wbd_ref[c], preferred_element_type=F32) + brg_ref[c]
        r2 = jnp.tanh(g[:, :MXU_DIM]) + 1.0
        i2 = jnp.tanh(g[:, MXU_DIM:]) + 1.0
        z = -lam_ref[:, sl]
        softplus = jnp.maximum(z, 0.0) + jnp.log1p(jnp.exp(-jnp.abs(z)))
        c_nat = (0.5 * LRU_C) * softplus
        neg_log_a = c_nat * r2
        a = jnp.exp2((-LOG2E * c_nat) * r2)
        one_minus_a2 = jnp.tanh(neg_log_a) * (a * a + 1.0)
        root = jnp.where(one_minus_a2 > 0.0, one_minus_a2 * lax.rsqrt(one_minus_a2), 0.0)
        bb = root * (i2 * xc2)
        for kk in range(per):
            a_s[c * per + kk, :rows] = a[:, kk * LANES:(kk + 1) * LANES]
            b_s[c * per + kk, :rows] = bb[:, kk * LANES:(kk + 1) * LANES]

    def step(j, h):
        t = tc - 1 - j if reverse else j
        r0 = pl.multiple_of(t * SUBLANES, SUBLANES)
        h = a_s[:, pl.ds(r0, SUBLANES), :] * h + b_s[:, pl.ds(r0, SUBLANES), :]
        h_ref[:, pl.ds(r0, SUBLANES), :] = h
        return h

    h_last = lax.fori_loop(0, tc, step, hst[...], unroll=8)
    hst[...] = h_last
    hfin_ref[...] = h_last


def _scan_call(x, conv, wbd, brg, lam2, h0, *, tc, reverse):
    n_lc, n_rows, _ = x.shape
    n_b = SUBLANES
    seq = n_rows // n_b
    d = n_lc * LANES
    n_chunks = seq // tc
    n_col = d // MXU_DIM
    rows = tc * n_b

    def chunk(i):
        return n_chunks - 1 - i if reverse else i

    const2 = lambda i: (0, 0)
    const3 = lambda i: (0, 0, 0)
    main_spec = pl.BlockSpec((n_lc, rows, LANES), lambda i: (0, chunk(i), 0))
    param_specs = [
        pl.BlockSpec((n_col, MXU_DIM, 2 * MXU_DIM), const3),
        pl.BlockSpec((n_col, 1, 2 * MXU_DIM), const3),
        pl.BlockSpec((1, d), const2),
        pl.BlockSpec((n_lc, n_b, LANES), const3),
    ]
    h_shape = jax.ShapeDtypeStruct((n_lc, n_rows, LANES), F32)
    fin_spec = pl.BlockSpec((n_lc, n_b, LANES), const3)
    fin_shape = jax.ShapeDtypeStruct((n_lc, n_b, LANES), F32)
    if reverse:
        in_specs = [main_spec] + param_specs
        operands = (x, wbd, brg, lam2, h0)
        out_specs = [main_spec, fin_spec]
        out_shape = [h_shape, fin_shape]
    else:
        in_specs = [
            pl.BlockSpec((n_lc, 2 * n_b, LANES), lambda i: (0, jnp.maximum(i * (tc // 2) - 1, 0), 0)),
            main_spec,
            pl.BlockSpec((n_lc, n_b, LANES), lambda i: (0, jnp.minimum((i + 1) * tc, seq - 1), 0)),
            pl.BlockSpec((CONV_W, d), const2),
            pl.BlockSpec((1, d), const2),
        ] + param_specs
        operands = (x, x, x) + tuple(conv) + (wbd, brg, lam2, h0)
        out_specs = [main_spec, main_spec, fin_spec]
        out_shape = [h_shape, h_shape, fin_shape]
    return pl.pallas_call(
        functools.partial(_scan_kernel, n_chunks=n_chunks, tc=tc, reverse=reverse),
        grid=(n_chunks,),
        in_specs=in_specs,
        out_specs=out_specs,
        out_shape=out_shape,
        scratch_shapes=[
            pltpu.VMEM((n_lc, rows + n_b, LANES), F32),
            pltpu.VMEM((n_lc, rows + n_b, LANES), F32),
            pltpu.VMEM((n_lc, n_b, LANES), F32),
        ],
        compiler_params=pltpu.CompilerParams(
            dimension_semantics=("arbitrary",), vmem_limit_bytes=VMEM_LIMIT),
        name="scan_rev" if reverse else "scan_fwd",
    )(*operands)


def _window_plan(i, n_i, rows):
    wr = min(WIN_ROWS, rows)
    rng = []
    for rq in range(ATT_ROWS):
        r = ATT_ROWS * i + rq
        r0 = min(max(r - wr // 2, 0), rows - wr)
        lo = r0 - (ATT_ROWS * i - ATT_ROWS)
        rng.append((lo, lo + wr))
    s_lo = min(lo for lo, _ in rng)
    s_hi = max(hi for _, hi in rng)
    assert 0 <= s_lo and s_hi <= ATT_SLOTS
    rows_per_tile = LANES // GRID_W
    tiles = []
    for j in range(ATT_ROWS // rows_per_tile):
        rqs = range(j * rows_per_tile, (j + 1) * rows_per_tile)
        a = min(rng[rq][0] for rq in rqs)
        b = max(rng[rq][1] for rq in rqs)
        half = {}
        for s in range(a, b):
            ok = tuple(rng[rq][0] <= s < rng[rq][1] for rq in rqs)
            if not all(ok):
                half[s - s_lo] = ok
        tiles.append((a - s_lo, b - s_lo, half))
    return s_lo, s_hi, tiles


def _attn_block(q_ref, k_refs, v_refs, kc_ref, vc_ref, bias_ref, o_ref, s_buf, plan):
    s_lo, s_hi, tiles = plan
    n_slots = s_hi - s_lo
    n_loc = n_slots * GRID_W
    n_ctx = kc_ref.shape[0]
    lane = lax.broadcasted_iota(jnp.int32, (1, LANES), 1)
    lo = lane < HEAD_DIM
    heads_per_pair = LANES // HEAD_DIM
    n_pairs = N_HEADS // heads_per_pair
    dims_nt = (((1,), (1,)), ((), ()))
    dims_tn = (((0,), (0,)), ((), ()))

    def keys_of(refs, c_ref, ls):
        parts = []
        for blk, ref in enumerate(refs):
            a = max(s_lo, blk * ATT_ROWS) - blk * ATT_ROWS
            b = min(s_hi, (blk + 1) * ATT_ROWS) - blk * ATT_ROWS
            if b > a:
                parts.append(ref[a * GRID_W:b * GRID_W, ls])
        return jnp.concatenate(parts + [c_ref[:, ls]], axis=0)

    def scores(p):
        ls = slice(p * LANES, (p + 1) * LANES)
        qp = q_ref[:, ls]
        k_all = keys_of(k_refs, kc_ref, ls)
        for hh in range(heads_per_pair):
            msk = lo if hh == 0 else jnp.logical_not(lo)
            qm = jnp.where(msk, qp, jnp.zeros_like(qp))
            s_buf[p % 2, hh, :n_loc + n_ctx] = lax.dot_general(k_all, qm, dims_nt, preferred_element_type=F32)

    def softmax_pv(p):
        ls = slice(p * LANES, (p + 1) * LANES)
        v_all = keys_of(v_refs, vc_ref, ls)
        o_t = []
        for hh in range(heads_per_pair):
            own = lo if hh == 0 else jnp.logical_not(lo)
            head = p * heads_per_pair + hh
            cols = []
            for j, (a, b, half) in enumerate(tiles):
                qs = slice(j * LANES, (j + 1) * LANES)
                blocks = []
                for s in range(a, b):
                    blk = (s_buf[p % 2, hh, s * GRID_W:(s + 1) * GRID_W, qs]
                           + bias_ref[head, (s_lo + s) * GRID_W:(s_lo + s + 1) * GRID_W, qs])
                    if s in half:
                        blk = jnp.where(lo if half[s][0] else jnp.logical_not(lo), blk, NEG)
                    blocks.append(blk)
                s_loc = jnp.concatenate(blocks, axis=0)
                s_ctx = s_buf[p % 2, hh, n_loc:n_loc + n_ctx, qs]
                m = jnp.maximum(jnp.max(s_loc, axis=0, keepdims=True), jnp.max(s_ctx, axis=0, keepdims=True))
                col = [jnp.exp2(s_loc - m), jnp.exp2(s_ctx - m)]
                if a > 0:
                    col.insert(0, jnp.zeros((a * GRID_W, LANES), F32))
                if b < n_slots:
                    col.insert(-1, jnp.zeros(((n_slots - b) * GRID_W, LANES), F32))
                cols.append(jnp.concatenate(col, axis=0))
            p_all = jnp.concatenate(cols, axis=1).astype(BF16)
            v_own = jnp.where(own, v_all, jnp.ones_like(v_all))
            o = lax.dot_general(v_own, p_all, dims_tn, preferred_element_type=F32)
            den = o[(1 - hh) * HEAD_DIM:(1 - hh) * HEAD_DIM + 1]
            o_t.append(o[hh * HEAD_DIM:(hh + 1) * HEAD_DIM] * (1.0 / den))
        o_ref[:, ls] = jnp.concatenate(o_t, axis=0).T.astype(BF16)

    scores(0)
    for p in range(n_pairs):
        if p + 1 < n_pairs:
            scores(p + 1)
        softmax_pv(p)


def _attn_kernel(q_ref, kp_ref, km_ref, kn_ref, vp_ref, vm_ref, vn_ref, kc_ref, vc_ref, bias_ref, o_ref, s_buf,
                 *, plans):
    i = pl.program_id(0)
    n_i = pl.num_programs(0)
    variant = (i > 0).astype(jnp.int32) + (i == n_i - 1).astype(jnp.int32)
    for v, plan in enumerate(plans):
        @pl.when(variant == v)
        def _(plan=plan):
            _attn_block(q_ref, (kp_ref, km_ref, kn_ref), (vp_ref, vm_ref, vn_ref), kc_ref, vc_ref,
                        bias_ref, o_ref, s_buf, plan)


def _attn_call(proj, projc, bias, *, n_b, seq, ctx_len, q_col, k_col, v_col, kc_col, vc_col):
    d = N_HEADS * HEAD_DIM
    n_i = seq // ATT_Q
    rows = seq // GRID_W
    plans = tuple(_window_plan(i, n_i, rows) for i in (0, 1, n_i - 1))

    def row(i, b):
        return b * n_i + i

    def row_prev(i, b):
        return b * n_i + jnp.maximum(i - 1, 0)

    def row_next(i, b):
        return b * n_i + jnp.minimum(i + 1, n_i - 1)

    blk = (ATT_Q, d)
    return pl.pallas_call(
        functools.partial(_attn_kernel, plans=plans),
        grid=(n_i, n_b),
        in_specs=[
            pl.BlockSpec(blk, lambda i, b: (row(i, b), q_col)),
            pl.BlockSpec(blk, lambda i, b: (row_prev(i, b), k_col)),
            pl.BlockSpec(blk, lambda i, b: (row(i, b), k_col)),
            pl.BlockSpec(blk, lambda i, b: (row_next(i, b), k_col)),
            pl.BlockSpec(blk, lambda i, b: (row_prev(i, b), v_col)),
            pl.BlockSpec(blk, lambda i, b: (row(i, b), v_col)),
            pl.BlockSpec(blk, lambda i, b: (row_next(i, b), v_col)),
            pl.BlockSpec((ctx_len, d), lambda i, b: (b, kc_col)),
            pl.BlockSpec((ctx_len, d), lambda i, b: (b, vc_col)),
            pl.BlockSpec(bias.shape, lambda i, b: (0, 0, 0), pipeline_mode=pl.Buffered(1)),
        ],
        out_specs=pl.BlockSpec(blk, lambda i, b: (row(i, b), 0)),
        out_shape=jax.ShapeDtypeStruct((n_b * seq, d), BF16),
        scratch_shapes=[pltpu.VMEM((2, LANES // HEAD_DIM, ATT_SLOTS * GRID_W + ctx_len, ATT_Q), F32)],
        compiler_params=pltpu.CompilerParams(
            dimension_semantics=("arbitrary", "arbitrary"), vmem_limit_bytes=VMEM_LIMIT),
        name="attn",
    )(proj, proj, proj, proj, proj, proj, proj, projc, projc, bias)


def _attn_bias(rpb):
    qc = np.arange(GRID_W)[None, :]
    kc = np.arange(GRID_W)[:, None]
    col_start = np.clip(qc - WIN_COLS // 2, 0, GRID_W - WIN_COLS)
    col_ok = (kc >= col_start) & (kc < col_start + WIN_COLS)
    dc_idx = np.clip(kc - qc + WIN_COLS - 1, 0, 2 * WIN_COLS - 2)
    onehot = (dc_idx[None] == np.arange(2 * WIN_COLS - 1)[:, None, None]) & col_ok[None]
    col_tab = jnp.einsum('had,dkq->hakq', rpb.astype(F32), jnp.asarray(onehot, F32),
                         precision=lax.Precision.HIGHEST)
    col_tab = jnp.where(jnp.asarray(col_ok)[None, None], col_tab * LOG2E, NEG)
    row_idx = (np.arange(ATT_SLOTS)[:, None] - ATT_ROWS - np.arange(ATT_ROWS)[None, :] + WIN_ROWS - 1)
    tab = jnp.take(col_tab, jnp.asarray(row_idx.reshape(-1), jnp.int32), axis=1)
    tab = tab.reshape(rpb.shape[0], ATT_SLOTS, ATT_ROWS, GRID_W, GRID_W).transpose(0, 1, 3, 2, 4)
    return tab.reshape(rpb.shape[0], ATT_SLOTS * GRID_W, ATT_Q)


def _out_kernel(hf_ref, hr_ref, ggr_ref, gl_ref, gna_ref, att_ref, x_ref, gt1_ref, sh2_ref, sc2_ref,
                gt2_ref, gn_ref, wl_ref, wn_ref, wo_ref, w1_ref, w2_ref, out_ref):
    n_b, ts, d = x_ref.shape
    rows = n_b * ts
    y_na = jnp.dot(att_ref[...].reshape(rows, d), wn_ref[...], preferred_element_type=F32)
    hl = jnp.stack([
        jnp.concatenate([hf_ref[k, pl.ds(b, ts, stride=n_b), :] + hr_ref[k, pl.ds(b, ts, stride=n_b), :]
                         for k in range(d // LANES)], axis=1)
        for b in range(n_b)], axis=0)
    hl = hl * ggr_ref[...].astype(F32)
    y_lru = jnp.dot(hl.reshape(rows, d).astype(BF16), wl_ref[...], preferred_element_type=F32)
    y = (gl_ref[...].astype(F32).reshape(rows, d) * y_lru
         + gna_ref[...].astype(F32).reshape(rows, d) * y_na)
    y = jnp.dot(y.astype(BF16), wo_ref[...], preferred_element_type=F32).reshape(n_b, ts, d)
    x1 = x_ref[...] + gt1_ref[...] * _rms(y, gn_ref[1:2, :])
    u2 = _rms(x1, gn_ref[2:3, :]) * (1.0 + sc2_ref[...]) + sh2_ref[...]
    u2 = u2.reshape(rows, d).astype(BF16)
    m = None
    for f in range(0, w1_ref.shape[1], FF_CHUNK):
        h1 = jnp.dot(u2, w1_ref[:, f:f + FF_CHUNK], preferred_element_type=F32)
        h1 = jnp.maximum(h1, 0.0)
        h1 = (h1 * h1).astype(BF16)
        part = jnp.dot(h1, w2_ref[f:f + FF_CHUNK, :], preferred_element_type=F32)
        m = part if m is None else m + part
    out_ref[...] = x1 + gt2_ref[...] * _rms(m.reshape(n_b, ts, d), gn_ref[3:4, :])


def _out_call(h_f, h_r, proj3, att3, x3, mod3, g_norm, wl, wn, wo, w1, w2, *, ts, ggr_col, gl_col, gna_col):
    n_b, seq, d = x3.shape

    def modspec(k):
        return pl.BlockSpec((n_b, 1, d), lambda t: (0, 0, k))

    def resident(w):
        return pl.BlockSpec(w.shape, lambda t: (0, 0), pipeline_mode=pl.Buffered(1))

    tile = (n_b, ts, d)
    return pl.pallas_call(
        _out_kernel,
        grid=(seq // ts,),
        in_specs=[
            pl.BlockSpec((d // LANES, ts * n_b, LANES), lambda t: (0, t, 0)),
            pl.BlockSpec((d // LANES, ts * n_b, LANES), lambda t: (0, t, 0)),
            pl.BlockSpec(tile, lambda t: (0, t, ggr_col)),
            pl.BlockSpec(tile, lambda t: (0, t, gl_col)),
            pl.BlockSpec(tile, lambda t: (0, t, gna_col)),
            pl.BlockSpec(tile, lambda t: (0, t, 0)),
            pl.BlockSpec(tile, lambda t: (0, t, 0)),
            modspec(2), modspec(3), modspec(4), modspec(5),
            pl.BlockSpec(g_norm.shape, lambda t: (0, 0)),
            resident(wl), resident(wn), resident(wo), resident(w1), resident(w2),
        ],
        out_specs=pl.BlockSpec(tile, lambda t: (0, t, 0)),
        out_shape=jax.ShapeDtypeStruct((n_b, seq, d), F32),
        compiler_params=pltpu.CompilerParams(
            dimension_semantics=("arbitrary",), vmem_limit_bytes=VMEM_LIMIT),
        name="out",
    )(h_f, h_r, proj3, proj3, proj3, att3, x3, mod3, mod3, mod3, mod3, g_norm, wl, wn, wo, w1, w2)


def _gate_weights(w_rg, b_rg):
    n_dir, n_gate, n_blk, bw, _ = w_rg.shape
    per = MXU_DIM // bw
    n_col = n_blk // per
    b_rg = 0.5 * b_rg
    w = w_rg.reshape(n_dir, n_gate, n_col, per, bw, bw)
    eye = jnp.eye(per, dtype=w.dtype)
    wbd = jnp.einsum('dgcpij,pq->dgcpiqj', w, eye).reshape(n_dir, n_gate, n_col, MXU_DIM, MXU_DIM)
    wbd = jnp.concatenate([wbd[:, 0], wbd[:, 1]], axis=-1).astype(BF16)
    b = b_rg.reshape(n_dir, n_gate, n_col, MXU_DIM)
    brg = jnp.concatenate([b[:, 0], b[:, 1]], axis=-1)[:, :, None, :]
    return wbd, brg


def kernel(x, c, ctx, c_ctx, w_ada, b_ada, g_norm, w_in, b_gate, conv_w, conv_b, w_rg, b_rg, lam, rpb,
           w_lru_out, w_na_out, w_o, w_mlp1, w_mlp2):
    n_b, seq, d = x.shape
    ctx_len = ctx.shape[1]
    rows = seq // GRID_W
    depth = w_ada.shape[0]
    assert depth == 1 and rows % ATT_ROWS == 0 and n_b == SUBLANES

    l = 0
    cs = jnp.concatenate([c, c_ctx[None], jnp.zeros((2 * SUBLANES - n_b - 1, d), F32)], axis=0)
    mod = _mod_call(cs, w_ada[l], b_ada[l][None])
    mod3 = mod[:, None, :]

    w_in_bf = w_in[l].astype(BF16)
    b_gate2 = b_gate[l][None]
    g0 = g_norm[l, 0:1]

    xr, proj3 = _inproj_call(x, mod3, None, g0, w_in_bf, b_gate2, ts=64, segs=(0, 1, 2, 3, 4, 5, 6))
    xr_c, projc3 = _inproj_call(ctx, mod3, n_b, g0, w_in_bf, b_gate2, ts=64, segs=(0, 3, 4))

    wbd, brg = _gate_weights(w_rg[l], b_rg[l])
    conv_half = (0.5 * conv_w[l], 0.5 * conv_b[l][None])
    zero_state = jnp.zeros((d // LANES, n_b, LANES), F32)
    _, xc_c, h0_f = _scan_call(xr_c, conv_half, wbd[0], brg[0], lam[l][0:1], zero_state, tc=128, reverse=False)
    _, h0_r = _scan_call(xc_c, None, wbd[1], brg[1], lam[l][1:2], zero_state, tc=128, reverse=True)
    h_f, xc, _ = _scan_call(xr, conv_half, wbd[0], brg[0], lam[l][0:1], h0_f, tc=128, reverse=False)
    h_r, _ = _scan_call(xc, None, wbd[1], brg[1], lam[l][1:2], h0_r, tc=128, reverse=True)

    bias = _attn_bias(rpb[l])
    att = _attn_call(proj3.reshape(n_b * seq, -1), projc3.reshape(n_b * ctx_len, -1), bias,
                     n_b=n_b, seq=seq, ctx_len=ctx_len, q_col=1, k_col=2, v_col=3, kc_col=0, vc_col=1)

    return _out_call(h_f, h_r, proj3, att.reshape(n_b, seq, d), x, mod3, g_norm[l],
                     w_lru_out[l].astype(BF16), w_na_out[l].astype(BF16), w_o[l].astype(BF16),
                     w_mlp1[l].astype(BF16), w_mlp2[l].astype(BF16),
                     ts=64, ggr_col=0, gl_col=4, gna_col=5)
```

```python
import functools

import numpy as np
import jax
import jax.numpy as jnp
from jax import lax
from jax.experimental import pallas as pl
from jax.experimental.pallas import tpu as pltpu

F32 = jnp.float32
BF16 = jnp.bfloat16

EPS = 1e-6
NEG = -1e30
LRU_C = 8.0
N_HEADS = 16
HEAD_DIM = 64
GRID_W = 64
WIN_ROWS = 8
WIN_COLS = 16
N_LRU_BLOCKS = 16
CONV_W = 4
CONV_PAD_LEFT = 2

LANES = 128
SUBLANES = 8
MXU_DIM = 256
VMEM_LIMIT = 56 * 1024 * 1024

ATT_ROWS = 4
ATT_Q = ATT_ROWS * GRID_W
ATT_SLOTS = 3 * ATT_ROWS
FF_CHUNK = 1024
LOG2E = 1.4426950408889634


def _rms(x, g):
    return x * lax.rsqrt(jnp.mean(x * x, axis=-1, keepdims=True) + EPS) * g


def _mod_kernel(c_ref, w_ref, b_ref, o_ref):
    c = c_ref[...]
    s = c * jax.nn.sigmoid(c)
    o_ref[...] = jnp.dot(s, w_ref[...], preferred_element_type=F32) + b_ref[...]


def _mod_call(cs, w_ada, b_ada):
    rows, d = cs.shape
    n_out = w_ada.shape[1]
    return pl.pallas_call(
        _mod_kernel,
        grid=(n_out // d,),
        in_specs=[
            pl.BlockSpec((rows, d), lambda n: (0, 0)),
            pl.BlockSpec((d, d), lambda n: (0, n)),
            pl.BlockSpec((1, d), lambda n: (0, n)),
        ],
        out_specs=pl.BlockSpec((rows, d), lambda n: (0, n)),
        out_shape=jax.ShapeDtypeStruct((rows, n_out), F32),
        compiler_params=pltpu.CompilerParams(
            dimension_semantics=("arbitrary",), vmem_limit_bytes=VMEM_LIMIT),
        name="mod",
    )(cs, w_ada, b_ada)


def _inproj_kernel(x_ref, sh_ref, sc_ref, g_ref, w_ref, bg_ref, xr_ref, pr_ref, *, segs):
    n_b, ts, d = x_ref.shape
    u = _rms(x_ref[...], g_ref[...]) * (1.0 + sc_ref[...]) + sh_ref[...]
    u = u.reshape(n_b * ts, d).astype(BF16)
    for idx, seg in enumerate(segs):
        acc = jnp.dot(u, w_ref[:, seg * d:(seg + 1) * d], preferred_element_type=F32)
        if seg == 0:
            for b in range(n_b):
                for k in range(d // LANES):
                    xr_ref[k, pl.ds(b, ts, stride=n_b), :] = acc[b * ts:(b + 1) * ts, k * LANES:(k + 1) * LANES]
            continue
        if seg == 1:
            val = jax.nn.gelu(acc)
        elif seg == 2:
            val = acc * (HEAD_DIM ** -0.5 * LOG2E)
        elif seg in (3, 4):
            val = acc
        else:
            val = jax.nn.sigmoid(acc + bg_ref[:, (seg - 5) * d:(seg - 4) * d])
        pr_ref[:, :, (idx - 1) * d:idx * d] = val.astype(BF16).reshape(n_b, ts, d)


def _inproj_call(x3, mod3, mod_row, g0, w_in_bf, b_gate2, *, ts, segs):
    n_b, seq, d = x3.shape
    segs = tuple(segs)
    n_out = (len(segs) - 1) * d
    if mod_row is None:
        mod_blk, mod_idx = (n_b, 1, d), 0
    else:
        mod_blk, mod_idx = (1, 1, d), mod_row
    kern = functools.partial(_inproj_kernel, segs=segs)
    return pl.pallas_call(
        kern,
        grid=(seq // ts,),
        in_specs=[
            pl.BlockSpec((n_b, ts, d), lambda t: (0, t, 0)),
            pl.BlockSpec(mod_blk, lambda t: (mod_idx, 0, 0)),
            pl.BlockSpec(mod_blk, lambda t: (mod_idx, 0, 1)),
            pl.BlockSpec((1, d), lambda t: (0, 0)),
            pl.BlockSpec(w_in_bf.shape, lambda t: (0, 0), pipeline_mode=pl.Buffered(1)),
            pl.BlockSpec(b_gate2.shape, lambda t: (0, 0)),
        ],
        out_specs=[
            pl.BlockSpec((d // LANES, ts * n_b, LANES), lambda t: (0, t, 0)),
            pl.BlockSpec((n_b, ts, n_out), lambda t: (0, t, 0)),
        ],
        out_shape=[
            jax.ShapeDtypeStruct((d // LANES, seq * n_b, LANES), F32),
            jax.ShapeDtypeStruct((n_b, seq, n_out), BF16),
        ],
        compiler_params=pltpu.CompilerParams(
            dimension_semantics=("arbitrary",), vmem_limit_bytes=VMEM_LIMIT),
        name="inproj",
    )(x3, mod3, mod3, g0, w_in_bf, b_gate2)


def _scan_kernel(*refs, n_chunks, tc, reverse):
    if reverse:
        xc_ref, wbd_ref, brg_ref, lam_ref, h0_ref, h_ref, hfin_ref, a_s, b_s, hst = refs
    else:
        (prev_ref, main_ref, next_ref, cw_ref, cb_ref, wbd_ref, brg_ref, lam_ref, h0_ref,
         h_ref, xc_ref, hfin_ref, a_s, b_s, hst) = refs
    i = pl.program_id(0)

    @pl.when(i == 0)
    def _():
        hst[...] = h0_ref[...]

    n_lc = a_s.shape[0]
    rows = tc * SUBLANES
    per = MXU_DIM // LANES
    if not reverse:
        has_prev = (i > 0).astype(F32)
        has_next = (i < n_chunks - 1).astype(F32)
    for c in range(n_lc // per):
        xcs = []
        for k in range(c * per, (c + 1) * per):
            if reverse:
                xc = xc_ref[k]
            else:
                sl = slice(k * LANES, (k + 1) * LANES)
                xe = jnp.concatenate([prev_ref[k] * has_prev, main_ref[k], next_ref[k] * has_next], axis=0)
                xc = cb_ref[:, sl]
                for j in range(CONV_W):
                    xc = xc + cw_ref[j:j + 1, sl] * xe[j * SUBLANES:j * SUBLANES + rows]
                xc_ref[k] = xc
            xcs.append(xc)
        xc2 = jnp.concatenate(xcs, axis=1)
        sl = slice(c * MXU_DIM, (c + 1) * MXU_DIM)
        g = jnp.dot(xc2.astype(BF16), wbd_ref[c], preferred_element_type=F32) + brg_ref[c]
        r2 = jnp.tanh(g[:, :MXU_DIM]) + 1.0
        i2 = jnp.tanh(g[:, MXU_DIM:]) + 1.0
        z = -lam_ref[:, sl]
        softplus = jnp.maximum(z, 0.0) + jnp.log1p(jnp.exp(-jnp.abs(z)))
        c_nat = (0.5 * LRU_C) * softplus
        neg_log_a = c_nat * r2
        a = jnp.exp2((-LOG2E * c_nat) * r2)
        one_minus_a2 = jnp.tanh(neg_log_a) * (a * a + 1.0)
        root = jnp.where(one_minus_a2 > 0.0, one_minus_a2 * lax.rsqrt(one_minus_a2), 0.0)
        bb = root * (i2 * xc2)
        for kk in range(per):
            a_s[c * per + kk, :rows] = a[:, kk * LANES:(kk + 1) * LANES]
            b_s[c * per + kk, :rows] = bb[:, kk * LANES:(kk + 1) * LANES]

    def step(j, h):
        t = tc - 1 - j if reverse else j
        r0 = pl.multiple_of(t * SUBLANES, SUBLANES)
        h = a_s[:, pl.ds(r0, SUBLANES), :] * h + b_s[:, pl.ds(r0, SUBLANES), :]
        h_ref[:, pl.ds(r0, SUBLANES), :] = h
        return h

    h_last = lax.fori_loop(0, tc, step, hst[...], unroll=8)
    hst[...] = h_last
    hfin_ref[...] = h_last


def _scan_call(x, conv, wbd, brg, lam2, h0, *, tc, reverse):
    n_lc, n_rows, _ = x.shape
    n_b = SUBLANES
    seq = n_rows // n_b
    d = n_lc * LANES
    n_chunks = seq // tc
    n_col = d // MXU_DIM
    rows = tc * n_b

    def chunk(i):
        return n_chunks - 1 - i if reverse else i

    const2 = lambda i: (0, 0)
    const3 = lambda i: (0, 0, 0)
    main_spec = pl.BlockSpec((n_lc, rows, LANES), lambda i: (0, chunk(i), 0))
    param_specs = [
        pl.BlockSpec((n_col, MXU_DIM, 2 * MXU_DIM), const3),
        pl.BlockSpec((n_col, 1, 2 * MXU_DIM), const3),
        pl.BlockSpec((1, d), const2),
        pl.BlockSpec((n_lc, n_b, LANES), const3),
    ]
    h_shape = jax.ShapeDtypeStruct((n_lc, n_rows, LANES), F32)
    fin_spec = pl.BlockSpec((n_lc, n_b, LANES), const3)
    fin_shape = jax.ShapeDtypeStruct((n_lc, n_b, LANES), F32)
    if reverse:
        in_specs = [main_spec] + param_specs
        operands = (x, wbd, brg, lam2, h0)
        out_specs = [main_spec, fin_spec]
        out_shape = [h_shape, fin_shape]
    else:
        in_specs = [
            pl.BlockSpec((n_lc, 2 * n_b, LANES), lambda i: (0, jnp.maximum(i * (tc // 2) - 1, 0), 0)),
            main_spec,
            pl.BlockSpec((n_lc, n_b, LANES), lambda i: (0, jnp.minimum((i + 1) * tc, seq - 1), 0)),
            pl.BlockSpec((CONV_W, d), const2),
            pl.BlockSpec((1, d), const2),
        ] + param_specs
        operands = (x, x, x) + tuple(conv) + (wbd, brg, lam2, h0)
        out_specs = [main_spec, main_spec, fin_spec]
        out_shape = [h_shape, h_shape, fin_shape]
    return pl.pallas_call(
        functools.partial(_scan_kernel, n_chunks=n_chunks, tc=tc, reverse=reverse),
        grid=(n_chunks,),
        in_specs=in_specs,
        out_specs=out_specs,
        out_shape=out_shape,
        scratch_shapes=[
            pltpu.VMEM((n_lc, rows + n_b, LANES), F32),
            pltpu.VMEM((n_lc, rows + n_b, LANES), F32),
            pltpu.VMEM((n_lc, n_b, LANES), F32),
        ],
        compiler_params=pltpu.CompilerParams(
            dimension_semantics=("arbitrary",), vmem_limit_bytes=VMEM_LIMIT),
        name="scan_rev" if reverse else "scan_fwd",
    )(*operands)


def _window_plan(i, n_i, rows):
    wr = min(WIN_ROWS, rows)
    rng = []
    for rq in range(ATT_ROWS):
        r = ATT_ROWS * i + rq
        r0 = min(max(r - wr // 2, 0), rows - wr)
        lo = r0 - (ATT_ROWS * i - ATT_ROWS)
        rng.append((lo, lo + wr))
    s_lo = min(lo for lo, _ in rng)
    s_hi = max(hi for _, hi in rng)
    assert 0 <= s_lo and s_hi <= ATT_SLOTS
    rows_per_tile = LANES // GRID_W
    tiles = []
    for j in range(ATT_ROWS // rows_per_tile):
        rqs = range(j * rows_per_tile, (j + 1) * rows_per_tile)
        a = min(rng[rq][0] for rq in rqs)
        b = max(rng[rq][1] for rq in rqs)
        half = {}
        for s in range(a, b):
            ok = tuple(rng[rq][0] <= s < rng[rq][1] for rq in rqs)
            if not all(ok):
                half[s - s_lo] = ok
        tiles.append((a - s_lo, b - s_lo, half))
    return s_lo, s_hi, tiles


def _attn_block(q_ref, k_refs, v_refs, kc_ref, vc_ref, bias_ref, o_ref, s_buf, plan):
    s_lo, s_hi, tiles = plan
    n_slots = s_hi - s_lo
    n_loc = n_slots * GRID_W
    n_ctx = kc_ref.shape[0]
    lane = lax.broadcasted_iota(jnp.int32, (1, LANES), 1)
    lo = lane < HEAD_DIM
    heads_per_pair = LANES // HEAD_DIM
    n_pairs = N_HEADS // heads_per_pair
    dims_nt = (((1,), (1,)), ((), ()))
    dims_tn = (((0,), (0,)), ((), ()))

    def keys_of(refs, c_ref, ls):
        parts = []
        for blk, ref in enumerate(refs):
            a = max(s_lo, blk * ATT_ROWS) - blk * ATT_ROWS
            b = min(s_hi, (blk + 1) * ATT_ROWS) - blk * ATT_ROWS
            if b > a:
                parts.append(ref[a * GRID_W:b * GRID_W, ls])
        return jnp.concatenate(parts + [c_ref[:, ls]], axis=0)

    def scores(p):
        ls = slice(p * LANES, (p + 1) * LANES)
        qp = q_ref[:, ls]
        k_all = keys_of(k_refs, kc_ref, ls)
        for hh in range(heads_per_pair):
            msk = lo if hh == 0 else jnp.logical_not(lo)
            qm = jnp.where(msk, qp, jnp.zeros_like(qp))
            s_buf[p % 2, hh, :n_loc + n_ctx] = lax.dot_general(k_all, qm, dims_nt, preferred_element_type=F32)

    def softmax_pv(p):
        ls = slice(p * LANES, (p + 1) * LANES)
        v_all = keys_of(v_refs, vc_ref, ls)
        o_t = []
        for hh in range(heads_per_pair):
            own = lo if hh == 0 else jnp.logical_not(lo)
            head = p * heads_per_pair + hh
            cols = []
            for j, (a, b, half) in enumerate(tiles):
                qs = slice(j * LANES, (j + 1) * LANES)
                blocks = []
                for s in range(a, b):
                    blk = (s_buf[p % 2, hh, s * GRID_W:(s + 1) * GRID_W, qs]
                           + bias_ref[head, (s_lo + s) * GRID_W:(s_lo + s + 1) * GRID_W, qs])
                    if s in half:
                        blk = jnp.where(lo if half[s][0] else jnp.logical_not(lo), blk, NEG)
                    blocks.append(blk)
                s_loc = jnp.concatenate(blocks, axis=0)
                s_ctx = s_buf[p % 2, hh, n_loc:n_loc + n_ctx, qs]
                m = jnp.maximum(jnp.max(s_loc, axis=0, keepdims=True), jnp.max(s_ctx, axis=0, keepdims=True))
                col = [jnp.exp2(s_loc - m), jnp.exp2(s_ctx - m)]
                if a > 0:
                    col.insert(0, jnp.zeros((a * GRID_W, LANES), F32))
                if b < n_slots:
                    col.insert(-1, jnp.zeros(((n_slots - b) * GRID_W, LANES), F32))
                cols.append(jnp.concatenate(col, axis=0))
            p_all = jnp.concatenate(cols, axis=1).astype(BF16)
            v_own = jnp.where(own, v_all, jnp.ones_like(v_all))
            o = lax.dot_general(v_own, p_all, dims_tn, preferred_element_type=F32)
            den = o[(1 - hh) * HEAD_DIM:(1 - hh) * HEAD_DIM + 1]
            o_t.append(o[hh * HEAD_DIM:(hh + 1) * HEAD_DIM] * (1.0 / den))
        o_ref[:, ls] = jnp.concatenate(o_t, axis=0).T.astype(BF16)

    scores(0)
    for p in range(n_pairs):
        if p + 1 < n_pairs:
            scores(p + 1)
        softmax_pv(p)


def _attn_kernel(q_ref, kp_ref, km_ref, kn_ref, vp_ref, vm_ref, vn_ref, kc_ref, vc_ref, bias_ref, o_ref, s_buf,
                 *, plans):
    i = pl.program_id(0)
    n_i = pl.num_programs(0)
    variant = (i > 0).astype(jnp.int32) + (i == n_i - 1).astype(jnp.int32)
    for v, plan in enumerate(plans):
        @pl.when(variant == v)
        def _(plan=plan):
            _attn_block(q_ref, (kp_ref, km_ref, kn_ref), (vp_ref, vm_ref, vn_ref), kc_ref, vc_ref,
                        bias_ref, o_ref, s_buf, plan)


def _attn_call(proj, projc, bias, *, n_b, seq, ctx_len, q_col, k_col, v_col, kc_col, vc_col):
    d = N_HEADS * HEAD_DIM
    n_i = seq // ATT_Q
    rows = seq // GRID_W
    plans = tuple(_window_plan(i, n_i, rows) for i in (0, 1, n_i - 1))

    def row(i, b):
        return b * n_i + i

    def row_prev(i, b):
        return b * n_i + jnp.maximum(i - 1, 0)

    def row_next(i, b):
        return b * n_i + jnp.minimum(i + 1, n_i - 1)

    blk = (ATT_Q, d)
    return pl.pallas_call(
        functools.partial(_attn_kernel, plans=plans),
        grid=(n_i, n_b),
        in_specs=[
            pl.BlockSpec(blk, lambda i, b: (row(i, b), q_col)),
            pl.BlockSpec(blk, lambda i, b: (row_prev(i, b), k_col)),
            pl.BlockSpec(blk, lambda i, b: (row(i, b), k_col)),
            pl.BlockSpec(blk, lambda i, b: (row_next(i, b), k_col)),
            pl.BlockSpec(blk, lambda i, b: (row_prev(i, b), v_col)),
            pl.BlockSpec(blk, lambda i, b: (row(i, b), v_col)),
            pl.BlockSpec(blk, lambda i, b: (row_next(i, b), v_col)),
            pl.BlockSpec((ctx_len, d), lambda i, b: (b, kc_col)),
            pl.BlockSpec((ctx_len, d), lambda i, b: (b, vc_col)),
            pl.BlockSpec(bias.shape, lambda i, b: (0, 0, 0), pipeline_mode=pl.Buffered(1)),
        ],
        out_specs=pl.BlockSpec(blk, lambda i, b: (row(i, b), 0)),
        out_shape=jax.ShapeDtypeStruct((n_b * seq, d), BF16),
        scratch_shapes=[pltpu.VMEM((2, LANES // HEAD_DIM, ATT_SLOTS * GRID_W + ctx_len, ATT_Q), F32)],
        compiler_params=pltpu.CompilerParams(
            dimension_semantics=("arbitrary", "arbitrary"), vmem_limit_bytes=VMEM_LIMIT),
        name="attn",
    )(proj, proj, proj, proj, proj, proj, proj, projc, projc, bias)


def _attn_bias(rpb):
    qc = np.arange(GRID_W)[None, :]
    kc = np.arange(GRID_W)[:, None]
    col_start = np.clip(qc - WIN_COLS // 2, 0, GRID_W - WIN_COLS)
    col_ok = (kc >= col_start) & (kc < col_start + WIN_COLS)
    dc_idx = np.clip(kc - qc + WIN_COLS - 1, 0, 2 * WIN_COLS - 2)
    onehot = (dc_idx[None] == np.arange(2 * WIN_COLS - 1)[:, None, None]) & col_ok[None]
    col_tab = jnp.einsum('had,dkq->hakq', rpb.astype(F32), jnp.asarray(onehot, F32),
                         precision=lax.Precision.HIGHEST)
    col_tab = jnp.where(jnp.asarray(col_ok)[None, None], col_tab * LOG2E, NEG)
    slots = []
    for s in range(ATT_SLOTS):
        q_rows = [col_tab[:, s - ATT_ROWS - rq + WIN_ROWS - 1] for rq in range(ATT_ROWS)]
        slots.append(jnp.concatenate(q_rows, axis=-1))
    return jnp.concatenate(slots, axis=1)


def _out_kernel(hf_ref, hr_ref, ggr_ref, gl_ref, gna_ref, att_ref, x_ref, gt1_ref, sh2_ref, sc2_ref,
                gt2_ref, gn_ref, wl_ref, wn_ref, wo_ref, w1_ref, w2_ref, out_ref):
    n_b, ts, d = x_ref.shape
    rows = n_b * ts
    y_na = jnp.dot(att_ref[...].reshape(rows, d), wn_ref[...], preferred_element_type=F32)
    hl = jnp.stack([
        jnp.concatenate([hf_ref[k, pl.ds(b, ts, stride=n_b), :] + hr_ref[k, pl.ds(b, ts, stride=n_b), :]
                         for k in range(d // LANES)], axis=1)
        for b in range(n_b)], axis=0)
    hl = hl * ggr_ref[...].astype(F32)
    y_lru = jnp.dot(hl.reshape(rows, d).astype(BF16), wl_ref[...], preferred_element_type=F32)
    y = (gl_ref[...].astype(F32).reshape(rows, d) * y_lru
         + gna_ref[...].astype(F32).reshape(rows, d) * y_na)
    y = jnp.dot(y.astype(BF16), wo_ref[...], preferred_element_type=F32).reshape(n_b, ts, d)
    x1 = x_ref[...] + gt1_ref[...] * _rms(y, gn_ref[1:2, :])
    u2 = _rms(x1, gn_ref[2:3, :]) * (1.0 + sc2_ref[...]) + sh2_ref[...]
    u2 = u2.reshape(rows, d).astype(BF16)
    m = None
    for f in range(0, w1_ref.shape[1], FF_CHUNK):
        h1 = jnp.dot(u2, w1_ref[:, f:f + FF_CHUNK], preferred_element_type=F32)
        h1 = jnp.maximum(h1, 0.0)
        h1 = (h1 * h1).astype(BF16)
        part = jnp.dot(h1, w2_ref[f:f + FF_CHUNK, :], preferred_element_type=F32)
        m = part if m is None else m + part
    out_ref[...] = x1 + gt2_ref[...] * _rms(m.reshape(n_b, ts, d), gn_ref[3:4, :])


def _out_call(h_f, h_r, proj3, att3, x3, mod3, g_norm, wl, wn, wo, w1, w2, *, ts, ggr_col, gl_col, gna_col):
    n_b, seq, d = x3.shape

    def modspec(k):
        return pl.BlockSpec((n_b, 1, d), lambda t: (0, 0, k))

    def resident(w):
        return pl.BlockSpec(w.shape, lambda t: (0, 0), pipeline_mode=pl.Buffered(1))

    tile = (n_b, ts, d)
    return pl.pallas_call(
        _out_kernel,
        grid=(seq // ts,),
        in_specs=[
            pl.BlockSpec((d // LANES, ts * n_b, LANES), lambda t: (0, t, 0)),
            pl.BlockSpec((d // LANES, ts * n_b, LANES), lambda t: (0, t, 0)),
            pl.BlockSpec(tile, lambda t: (0, t, ggr_col)),
            pl.BlockSpec(tile, lambda t: (0, t, gl_col)),
            pl.BlockSpec(tile, lambda t: (0, t, gna_col)),
            pl.BlockSpec(tile, lambda t: (0, t, 0)),
            pl.BlockSpec(tile, lambda t: (0, t, 0)),
            modspec(2), modspec(3), modspec(4), modspec(5),
            pl.BlockSpec(g_norm.shape, lambda t: (0, 0)),
            resident(wl), resident(wn), resident(wo), resident(w1), resident(w2),
        ],
        out_specs=pl.BlockSpec(tile, lambda t: (0, t, 0)),
        out_shape=jax.ShapeDtypeStruct((n_b, seq, d), F32),
        compiler_params=pltpu.CompilerParams(
            dimension_semantics=("arbitrary",), vmem_limit_bytes=VMEM_LIMIT),
        name="out",
    )(h_f, h_r, proj3, proj3, proj3, att3, x3, mod3, mod3, mod3, mod3, g_norm, wl, wn, wo, w1, w2)


def _gate_weights(w_rg, b_rg):
    n_dir, n_gate, n_blk, bw, _ = w_rg.shape
    per = MXU_DIM // bw
    n_col = n_blk // per
    b_rg = 0.5 * b_rg
    w = w_rg.reshape(n_dir, n_gate, n_col, per, bw, bw)
    eye = jnp.eye(per, dtype=w.dtype)
    wbd = jnp.einsum('dgcpij,pq->dgcpiqj', w, eye).reshape(n_dir, n_gate, n_col, MXU_DIM, MXU_DIM)
    wbd = jnp.concatenate([wbd[:, 0], wbd[:, 1]], axis=-1).astype(BF16)
    b = b_rg.reshape(n_dir, n_gate, n_col, MXU_DIM)
    brg = jnp.concatenate([b[:, 0], b[:, 1]], axis=-1)[:, :, None, :]
    return wbd, brg


def kernel(x, c, ctx, c_ctx, w_ada, b_ada, g_norm, w_in, b_gate, conv_w, conv_b, w_rg, b_rg, lam, rpb,
           w_lru_out, w_na_out, w_o, w_mlp1, w_mlp2):
    n_b, seq, d = x.shape
    ctx_len = ctx.shape[1]
    rows = seq // GRID_W
    depth = w_ada.shape[0]
    assert depth == 1 and rows % ATT_ROWS == 0 and n_b == SUBLANES

    l = 0
    cs = jnp.concatenate([c, c_ctx[None], jnp.zeros((2 * SUBLANES - n_b - 1, d), F32)], axis=0)
    mod = _mod_call(cs, w_ada[l], b_ada[l][None])
    mod3 = mod[:, None, :]

    w_in_bf = w_in[l].astype(BF16)
    b_gate2 = b_gate[l][None]
    g0 = g_norm[l, 0:1]

    xr, proj3 = _inproj_call(x, mod3, None, g0, w_in_bf, b_gate2, ts=64, segs=(0, 1, 2, 3, 4, 5, 6))
    xr_c, projc3 = _inproj_call(ctx, mod3, n_b, g0, w_in_bf, b_gate2, ts=64, segs=(0, 3, 4))

    wbd, brg = _gate_weights(w_rg[l], b_rg[l])
    conv_half = (0.5 * conv_w[l], 0.5 * conv_b[l][None])
    zero_state = jnp.zeros((d // LANES, n_b, LANES), F32)
    _, xc_c, h0_f = _scan_call(xr_c, conv_half, wbd[0], brg[0], lam[l][0:1], zero_state, tc=128, reverse=False)
    _, h0_r = _scan_call(xc_c, None, wbd[1], brg[1], lam[l][1:2], zero_state, tc=128, reverse=True)
    h_f, xc, _ = _scan_call(xr, conv_half, wbd[0], brg[0], lam[l][0:1], h0_f, tc=128, reverse=False)
    h_r, _ = _scan_call(xc, None, wbd[1], brg[1], lam[l][1:2], h0_r, tc=128, reverse=True)

    bias = _attn_bias(rpb[l])
    att = _attn_call(proj3.reshape(n_b * seq, -1), projc3.reshape(n_b * ctx_len, -1), bias,
                     n_b=n_b, seq=seq, ctx_len=ctx_len, q_col=1, k_col=2, v_col=3, kc_col=0, vc_col=1)

    return _out_call(h_f, h_r, proj3, att.reshape(n_b, seq, d), x, mod3, g_norm[l],
                     w_lru_out[l].astype(BF16), w_na_out[l].astype(BF16), w_o[l].astype(BF16),
                     w_mlp1[l].astype(BF16), w_mlp2[l].astype(BF16),
                     ts=64, ggr_col=0, gl_col=4, gna_col=5)
```

```python
import functools

import numpy as np
import jax
import jax.numpy as jnp
from jax import lax
from jax.experimental import pallas as pl
from jax.experimental.pallas import tpu as pltpu

F32 = jnp.float32
BF16 = jnp.bfloat16

EPS = 1e-6
NEG = -1e30
LRU_C = 8.0
N_HEADS = 16
HEAD_DIM = 64
GRID_W = 64
WIN_ROWS = 8
WIN_COLS = 16
CONV_W = 4

LANES = 128
SUBLANES = 8
MXU_DIM = 256
VMEM_LIMIT = 56 * 1024 * 1024

INPROJ_TS = 64
SCAN_TC = 128
OUT_TS = 64

ATT_ROWS = 4
ATT_Q = ATT_ROWS * GRID_W
ATT_SLOTS = 3 * ATT_ROWS
FF_CHUNK = 1024
LOG2E = 1.4426950408889634


def _rms(x, g):
    return x * lax.rsqrt(jnp.mean(x * x, axis=-1, keepdims=True) + EPS) * g


def _mod_kernel(c_ref, w_ref, b_ref, o_ref):
    c = c_ref[...]
    s = c * jax.nn.sigmoid(c)
    o_ref[...] = jnp.dot(s, w_ref[...], preferred_element_type=F32) + b_ref[...]


def _mod_call(cs, w_ada, b_ada):
    rows, d = cs.shape
    n_out = w_ada.shape[1]
    return pl.pallas_call(
        _mod_kernel,
        grid=(n_out // d,),
        in_specs=[
            pl.BlockSpec((rows, d), lambda n: (0, 0)),
            pl.BlockSpec((d, d), lambda n: (0, n)),
            pl.BlockSpec((1, d), lambda n: (0, n)),
        ],
        out_specs=pl.BlockSpec((rows, d), lambda n: (0, n)),
        out_shape=jax.ShapeDtypeStruct((rows, n_out), F32),
        compiler_params=pltpu.CompilerParams(
            dimension_semantics=("arbitrary",), vmem_limit_bytes=VMEM_LIMIT),
        name="mod",
    )(cs, w_ada, b_ada)


def _inproj_kernel(x_ref, sh_ref, sc_ref, g_ref, w_ref, bg_ref, xr_ref, pr_ref, *, segs):
    n_b, ts, d = x_ref.shape
    u = _rms(x_ref[...], g_ref[...]) * (1.0 + sc_ref[...]) + sh_ref[...]
    u = u.reshape(n_b * ts, d).astype(BF16)
    for idx, seg in enumerate(segs):
        acc = jnp.dot(u, w_ref[:, seg * d:(seg + 1) * d], preferred_element_type=F32)
        if seg == 0:
            for b in range(n_b):
                for k in range(d // LANES):
                    xr_ref[k, pl.ds(b, ts, stride=n_b), :] = acc[b * ts:(b + 1) * ts, k * LANES:(k + 1) * LANES]
            continue
        if seg == 1:
            val = jax.nn.gelu(acc)
        elif seg == 2:
            val = acc * (HEAD_DIM ** -0.5 * LOG2E)
        elif seg in (3, 4):
            val = acc
        else:
            val = jax.nn.sigmoid(acc + bg_ref[:, (seg - 5) * d:(seg - 4) * d])
        pr_ref[:, :, (idx - 1) * d:idx * d] = val.astype(BF16).reshape(n_b, ts, d)


def _inproj_call(x3, mod3, mod_row, g0, w_in_bf, b_gate2, *, ts, segs):
    n_b, seq, d = x3.shape
    segs = tuple(segs)
    n_out = (len(segs) - 1) * d
    if mod_row is None:
        mod_blk, mod_idx = (n_b, 1, d), 0
    else:
        mod_blk, mod_idx = (1, 1, d), mod_row
    kern = functools.partial(_inproj_kernel, segs=segs)
    return pl.pallas_call(
        kern,
        grid=(seq // ts,),
        in_specs=[
            pl.BlockSpec((n_b, ts, d), lambda t: (0, t, 0)),
            pl.BlockSpec(mod_blk, lambda t: (mod_idx, 0, 0)),
            pl.BlockSpec(mod_blk, lambda t: (mod_idx, 0, 1)),
            pl.BlockSpec((1, d), lambda t: (0, 0)),
            pl.BlockSpec(w_in_bf.shape, lambda t: (0, 0), pipeline_mode=pl.Buffered(1)),
            pl.BlockSpec(b_gate2.shape, lambda t: (0, 0)),
        ],
        out_specs=[
            pl.BlockSpec((d // LANES, ts * n_b, LANES), lambda t: (0, t, 0)),
            pl.BlockSpec((n_b, ts, n_out), lambda t: (0, t, 0)),
        ],
        out_shape=[
            jax.ShapeDtypeStruct((d // LANES, seq * n_b, LANES), F32),
            jax.ShapeDtypeStruct((n_b, seq, n_out), BF16),
        ],
        compiler_params=pltpu.CompilerParams(
            dimension_semantics=("arbitrary",), vmem_limit_bytes=VMEM_LIMIT),
        name="inproj",
    )(x3, mod3, mod3, g0, w_in_bf, b_gate2)


def _scan_kernel(prev_ref, main_ref, next_ref, cw_ref, cb_ref, wbd_ref, brg_ref, lam_ref, h0_ref,
                 h_ref, hfin_ref, a_s, b_s, hst, *, n_chunks, tc):
    d = pl.program_id(0)
    i = pl.program_id(1)
    ci = jnp.where(d == 0, i, n_chunks - 1 - i)

    @pl.when(i == 0)
    def _():
        hst[...] = h0_ref[d]

    has_prev = (ci > 0).astype(F32)
    has_next = (ci < n_chunks - 1).astype(F32)
    n_lc = a_s.shape[0]
    rows = tc * SUBLANES
    per = MXU_DIM // LANES
    for c in range(n_lc // per):
        xcs = []
        for k in range(c * per, (c + 1) * per):
            sl = slice(k * LANES, (k + 1) * LANES)
            xe = jnp.concatenate([prev_ref[k] * has_prev, main_ref[k], next_ref[k] * has_next], axis=0)
            xc = cb_ref[:, sl]
            for j in range(CONV_W):
                xc = xc + cw_ref[j:j + 1, sl] * xe[j * SUBLANES:j * SUBLANES + rows]
            xcs.append(xc)
        xc2 = jnp.concatenate(xcs, axis=1)
        sl = slice(c * MXU_DIM, (c + 1) * MXU_DIM)
        g = jnp.dot(xc2.astype(BF16), wbd_ref[0, c], preferred_element_type=F32) + brg_ref[0, c]
        r2 = jnp.tanh(g[:, :MXU_DIM]) + 1.0
        i2 = jnp.tanh(g[:, MXU_DIM:]) + 1.0
        z = -lam_ref[0, :, sl]
        softplus = jnp.maximum(z, 0.0) + jnp.log1p(jnp.exp(-jnp.abs(z)))
        c_nat = (0.5 * LRU_C) * softplus
        neg_log_a = c_nat * r2
        a = jnp.exp2((-LOG2E * c_nat) * r2)
        one_minus_a2 = jnp.tanh(neg_log_a) * (a * a + 1.0)
        root = jnp.where(one_minus_a2 > 0.0, one_minus_a2 * lax.rsqrt(one_minus_a2), 0.0)
        bb = root * (i2 * (0.5 * xc2))
        for kk in range(per):
            a_s[c * per + kk, :rows] = a[:, kk * LANES:(kk + 1) * LANES]
            b_s[c * per + kk, :rows] = bb[:, kk * LANES:(kk + 1) * LANES]

    def step(j, h):
        t = jnp.where(d == 0, j, tc - 1 - j)
        r0 = pl.multiple_of(t * SUBLANES, SUBLANES)
        h = a_s[:, pl.ds(r0, SUBLANES), :] * h + b_s[:, pl.ds(r0, SUBLANES), :]
        h_ref[0, :, pl.ds(r0, SUBLANES), :] = h
        return h

    h_last = lax.fori_loop(0, tc, step, hst[...], unroll=8)
    hst[...] = h_last
    hfin_ref[0] = h_last


def _scan_call(xr, conv_w, conv_b, wbd, brg, lam3, h0, *, tc):
    n_lc, n_rows, _ = xr.shape
    n_b = SUBLANES
    seq = n_rows // n_b
    d = n_lc * LANES
    n_chunks = seq // tc
    n_col = d // MXU_DIM
    rows = tc * n_b

    def chunk(dd, i):
        return jnp.where(dd == 0, i, n_chunks - 1 - i)

    kern = functools.partial(_scan_kernel, n_chunks=n_chunks, tc=tc)
    return pl.pallas_call(
        kern,
        grid=(2, n_chunks),
        in_specs=[
            pl.BlockSpec((n_lc, 2 * n_b, LANES),
                         lambda dd, i: (0, jnp.maximum(chunk(dd, i) * (tc // 2) - 1, 0), 0)),
            pl.BlockSpec((n_lc, rows, LANES), lambda dd, i: (0, chunk(dd, i), 0)),
            pl.BlockSpec((n_lc, n_b, LANES),
                         lambda dd, i: (0, jnp.minimum((chunk(dd, i) + 1) * tc, seq - 1), 0)),
            pl.BlockSpec((CONV_W, d), lambda dd, i: (0, 0)),
            pl.BlockSpec((1, d), lambda dd, i: (0, 0)),
            pl.BlockSpec((1, n_col, MXU_DIM, 2 * MXU_DIM), lambda dd, i: (dd, 0, 0, 0)),
            pl.BlockSpec((1, n_col, 1, 2 * MXU_DIM), lambda dd, i: (dd, 0, 0, 0)),
            pl.BlockSpec((1, 1, d), lambda dd, i: (dd, 0, 0)),
            pl.BlockSpec((2, n_lc, n_b, LANES), lambda dd, i: (0, 0, 0, 0)),
        ],
        out_specs=[
            pl.BlockSpec((1, n_lc, rows, LANES), lambda dd, i: (dd, 0, chunk(dd, i), 0)),
            pl.BlockSpec((1, n_lc, n_b, LANES), lambda dd, i: (dd, 0, 0, 0)),
        ],
        out_shape=[
            jax.ShapeDtypeStruct((2, n_lc, n_rows, LANES), F32),
            jax.ShapeDtypeStruct((2, n_lc, n_b, LANES), F32),
        ],
        scratch_shapes=[
            pltpu.VMEM((n_lc, rows + n_b, LANES), F32),
            pltpu.VMEM((n_lc, rows + n_b, LANES), F32),
            pltpu.VMEM((n_lc, n_b, LANES), F32),
        ],
        compiler_params=pltpu.CompilerParams(
            dimension_semantics=("arbitrary", "arbitrary"), vmem_limit_bytes=VMEM_LIMIT),
        name="scan",
    )(xr, xr, xr, conv_w, conv_b, wbd, brg, lam3, h0)


def _window_plan(i, n_i, rows):
    wr = min(WIN_ROWS, rows)
    rng = []
    for rq in range(ATT_ROWS):
        r = ATT_ROWS * i + rq
        r0 = min(max(r - wr // 2, 0), rows - wr)
        lo = r0 - (ATT_ROWS * i - ATT_ROWS)
        rng.append((lo, lo + wr))
    s_lo = min(lo for lo, _ in rng)
    s_hi = max(hi for _, hi in rng)
    assert 0 <= s_lo and s_hi <= ATT_SLOTS
    rows_per_tile = LANES // GRID_W
    tiles = []
    for j in range(ATT_ROWS // rows_per_tile):
        rqs = range(j * rows_per_tile, (j + 1) * rows_per_tile)
        a = min(rng[rq][0] for rq in rqs)
        b = max(rng[rq][1] for rq in rqs)
        half = {}
        for s in range(a, b):
            ok = tuple(rng[rq][0] <= s < rng[rq][1] for rq in rqs)
            if not all(ok):
                half[s - s_lo] = ok
        tiles.append((a - s_lo, b - s_lo, half))
    return s_lo, s_hi, tiles


def _attn_block(q_ref, k_refs, v_refs, kc_ref, vc_ref, bias_ref, o_ref, s_buf, plan):
    s_lo, s_hi, tiles = plan
    n_slots = s_hi - s_lo
    n_loc = n_slots * GRID_W
    n_ctx = kc_ref.shape[0]
    lane = lax.broadcasted_iota(jnp.int32, (1, LANES), 1)
    lo = lane < HEAD_DIM
    heads_per_pair = LANES // HEAD_DIM
    n_pairs = N_HEADS // heads_per_pair
    dims_nt = (((1,), (1,)), ((), ()))
    dims_tn = (((0,), (0,)), ((), ()))

    def keys_of(refs, c_ref, ls):
        parts = []
        for blk, ref in enumerate(refs):
            a = max(s_lo, blk * ATT_ROWS) - blk * ATT_ROWS
            b = min(s_hi, (blk + 1) * ATT_ROWS) - blk * ATT_ROWS
            if b > a:
                parts.append(ref[a * GRID_W:b * GRID_W, ls])
        return jnp.concatenate(parts + [c_ref[:, ls]], axis=0)

    def scores(p):
        ls = slice(p * LANES, (p + 1) * LANES)
        qp = q_ref[:, ls]
        k_all = keys_of(k_refs, kc_ref, ls)
        for hh in range(heads_per_pair):
            msk = lo if hh == 0 else jnp.logical_not(lo)
            qm = jnp.where(msk, qp, jnp.zeros_like(qp))
            s_buf[p % 2, hh, :n_loc + n_ctx] = lax.dot_general(k_all, qm, dims_nt, preferred_element_type=F32)

    def softmax_pv(p):
        ls = slice(p * LANES, (p + 1) * LANES)
        v_all = keys_of(v_refs, vc_ref, ls)
        o_t = []
        for hh in range(heads_per_pair):
            own = lo if hh == 0 else jnp.logical_not(lo)
            head = p * heads_per_pair + hh
            cols = []
            for j, (a, b, half) in enumerate(tiles):
                qs = slice(j * LANES, (j + 1) * LANES)
                blocks = []
                for s in range(a, b):
                    blk = (s_buf[p % 2, hh, s * GRID_W:(s + 1) * GRID_W, qs]
                           + bias_ref[head, (s_lo + s) * GRID_W:(s_lo + s + 1) * GRID_W, qs])
                    if s in half:
                        blk = jnp.where(lo if half[s][0] else jnp.logical_not(lo), blk, NEG)
                    blocks.append(blk)
                s_loc = jnp.concatenate(blocks, axis=0)
                s_ctx = s_buf[p % 2, hh, n_loc:n_loc + n_ctx, qs]
                m = jnp.maximum(jnp.max(s_loc, axis=0, keepdims=True), jnp.max(s_ctx, axis=0, keepdims=True))
                col = [jnp.exp2(s_loc - m), jnp.exp2(s_ctx - m)]
                if a > 0:
                    col.insert(0, jnp.zeros((a * GRID_W, LANES), F32))
                if b < n_slots:
                    col.insert(-1, jnp.zeros(((n_slots - b) * GRID_W, LANES), F32))
                cols.append(jnp.concatenate(col, axis=0))
            p_all = jnp.concatenate(cols, axis=1).astype(BF16)
            v_own = jnp.where(own, v_all, jnp.ones_like(v_all))
            o = lax.dot_general(v_own, p_all, dims_tn, preferred_element_type=F32)
            den = o[(1 - hh) * HEAD_DIM:(1 - hh) * HEAD_DIM + 1]
            o_t.append(o[hh * HEAD_DIM:(hh + 1) * HEAD_DIM] * (1.0 / den))
        o_ref[:, ls] = jnp.concatenate(o_t, axis=0).T.astype(BF16)

    scores(0)
    for p in range(n_pairs):
        if p + 1 < n_pairs:
            scores(p + 1)
        softmax_pv(p)


def _attn_kernel(q_ref, kp_ref, km_ref, kn_ref, vp_ref, vm_ref, vn_ref, kc_ref, vc_ref, bias_ref, o_ref, s_buf,
                 *, plans):
    i = pl.program_id(0)
    n_i = pl.num_programs(0)
    variant = (i > 0).astype(jnp.int32) + (i == n_i - 1).astype(jnp.int32)
    for v, plan in enumerate(plans):
        @pl.when(variant == v)
        def _(plan=plan):
            _attn_block(q_ref, (kp_ref, km_ref, kn_ref), (vp_ref, vm_ref, vn_ref), kc_ref, vc_ref,
                        bias_ref, o_ref, s_buf, plan)


def _attn_call(proj, projc, bias, *, n_b, seq, ctx_len, q_col, k_col, v_col, kc_col, vc_col):
    d = N_HEADS * HEAD_DIM
    n_i = seq // ATT_Q
    rows = seq // GRID_W
    plans = tuple(_window_plan(i, n_i, rows) for i in (0, 1, n_i - 1))

    def row(i, b):
        return b * n_i + i

    def row_prev(i, b):
        return b * n_i + jnp.maximum(i - 1, 0)

    def row_next(i, b):
        return b * n_i + jnp.minimum(i + 1, n_i - 1)

    blk = (ATT_Q, d)
    return pl.pallas_call(
        functools.partial(_attn_kernel, plans=plans),
        grid=(n_i, n_b),
        in_specs=[
            pl.BlockSpec(blk, lambda i, b: (row(i, b), q_col)),
            pl.BlockSpec(blk, lambda i, b: (row_prev(i, b), k_col)),
            pl.BlockSpec(blk, lambda i, b: (row(i, b), k_col)),
            pl.BlockSpec(blk, lambda i, b: (row_next(i, b), k_col)),
            pl.BlockSpec(blk, lambda i, b: (row_prev(i, b), v_col)),
            pl.BlockSpec(blk, lambda i, b: (row(i, b), v_col)),
            pl.BlockSpec(blk, lambda i, b: (row_next(i, b), v_col)),
            pl.BlockSpec((ctx_len, d), lambda i, b: (b, kc_col)),
            pl.BlockSpec((ctx_len, d), lambda i, b: (b, vc_col)),
            pl.BlockSpec(bias.shape, lambda i, b: (0, 0, 0), pipeline_mode=pl.Buffered(1)),
        ],
        out_specs=pl.BlockSpec(blk, lambda i, b: (row(i, b), 0)),
        out_shape=jax.ShapeDtypeStruct((n_b * seq, d), BF16),
        scratch_shapes=[pltpu.VMEM((2, LANES // HEAD_DIM, ATT_SLOTS * GRID_W + ctx_len, ATT_Q), F32)],
        compiler_params=pltpu.CompilerParams(
            dimension_semantics=("arbitrary", "arbitrary"), vmem_limit_bytes=VMEM_LIMIT),
        name="attn",
    )(proj, proj, proj, proj, proj, proj, proj, projc, projc, bias)


def _attn_bias(rpb):
    qc = np.arange(GRID_W)[None, :]
    kc = np.arange(GRID_W)[:, None]
    col_start = np.clip(qc - WIN_COLS // 2, 0, GRID_W - WIN_COLS)
    col_ok = (kc >= col_start) & (kc < col_start + WIN_COLS)
    dc_idx = np.clip(kc - qc + WIN_COLS - 1, 0, 2 * WIN_COLS - 2)
    onehot = (dc_idx[None] == np.arange(2 * WIN_COLS - 1)[:, None, None]) & col_ok[None]
    col_tab = jnp.einsum('had,dkq->hakq', rpb.astype(F32), jnp.asarray(onehot, F32),
                         precision=lax.Precision.HIGHEST)
    col_tab = jnp.where(jnp.asarray(col_ok)[None, None], col_tab * LOG2E, NEG)
    slots = []
    for s in range(ATT_SLOTS):
        q_rows = [col_tab[:, s - ATT_ROWS - rq + WIN_ROWS - 1] for rq in range(ATT_ROWS)]
        slots.append(jnp.concatenate(q_rows, axis=-1))
    return jnp.concatenate(slots, axis=1)


def _out_kernel(hf_ref, hr_ref, ggr_ref, gl_ref, gna_ref, att_ref, x_ref, gt1_ref, sh2_ref, sc2_ref,
                gt2_ref, gn_ref, wl_ref, wn_ref, wo_ref, w1_ref, w2_ref, out_ref):
    n_b, ts, d = x_ref.shape
    rows = n_b * ts
    y_na = jnp.dot(att_ref[...].reshape(rows, d), wn_ref[...], preferred_element_type=F32)
    hl = jnp.stack([
        jnp.concatenate([hf_ref[0, k, pl.ds(b, ts, stride=n_b), :] + hr_ref[0, k, pl.ds(b, ts, stride=n_b), :]
                         for k in range(d // LANES)], axis=1)
        for b in range(n_b)], axis=0)
    hl = hl * ggr_ref[...].astype(F32)
    y_lru = jnp.dot(hl.reshape(rows, d).astype(BF16), wl_ref[...], preferred_element_type=F32)
    y = (gl_ref[...].astype(F32).reshape(rows, d) * y_lru
         + gna_ref[...].astype(F32).reshape(rows, d) * y_na)
    y = jnp.dot(y.astype(BF16), wo_ref[...], preferred_element_type=F32).reshape(n_b, ts, d)
    x1 = x_ref[...] + gt1_ref[...] * _rms(y, gn_ref[1:2, :])
    u2 = _rms(x1, gn_ref[2:3, :]) * (1.0 + sc2_ref[...]) + sh2_ref[...]
    u2 = u2.reshape(rows, d).astype(BF16)
    m = None
    for f in range(0, w1_ref.shape[1], FF_CHUNK):
        h1 = jnp.dot(u2, w1_ref[:, f:f + FF_CHUNK], preferred_element_type=F32)
        h1 = jnp.maximum(h1, 0.0)
        h1 = (h1 * h1).astype(BF16)
        part = jnp.dot(h1, w2_ref[f:f + FF_CHUNK, :], preferred_element_type=F32)
        m = part if m is None else m + part
    out_ref[...] = x1 + gt2_ref[...] * _rms(m.reshape(n_b, ts, d), gn_ref[3:4, :])


def _out_call(h2, proj3, att3, x3, mod3, g_norm, wl, wn, wo, w1, w2, *, ts, ggr_col, gl_col, gna_col):
    n_b, seq, d = x3.shape

    def modspec(k):
        return pl.BlockSpec((n_b, 1, d), lambda t: (0, 0, k))

    def resident(w):
        return pl.BlockSpec(w.shape, lambda t: (0, 0), pipeline_mode=pl.Buffered(1))

    tile = (n_b, ts, d)
    return pl.pallas_call(
        _out_kernel,
        grid=(seq // ts,),
        in_specs=[
            pl.BlockSpec((1, d // LANES, ts * n_b, LANES), lambda t: (0, 0, t, 0)),
            pl.BlockSpec((1, d // LANES, ts * n_b, LANES), lambda t: (1, 0, t, 0)),
            pl.BlockSpec(tile, lambda t: (0, t, ggr_col)),
            pl.BlockSpec(tile, lambda t: (0, t, gl_col)),
            pl.BlockSpec(tile, lambda t: (0, t, gna_col)),
            pl.BlockSpec(tile, lambda t: (0, t, 0)),
            pl.BlockSpec(tile, lambda t: (0, t, 0)),
            modspec(2), modspec(3), modspec(4), modspec(5),
            pl.BlockSpec(g_norm.shape, lambda t: (0, 0)),
            resident(wl), resident(wn), resident(wo), resident(w1), resident(w2),
        ],
        out_specs=pl.BlockSpec(tile, lambda t: (0, t, 0)),
        out_shape=jax.ShapeDtypeStruct((n_b, seq, d), F32),
        compiler_params=pltpu.CompilerParams(
            dimension_semantics=("arbitrary",), vmem_limit_bytes=VMEM_LIMIT),
        name="out",
    )(h2, h2, proj3, proj3, proj3, att3, x3, mod3, mod3, mod3, mod3, g_norm, wl, wn, wo, w1, w2)


def _gate_weights(w_rg, b_rg):
    n_dir, n_gate, n_blk, bw, _ = w_rg.shape
    per = MXU_DIM // bw
    n_col = n_blk // per
    w_rg = 0.5 * w_rg
    b_rg = 0.5 * b_rg
    w = w_rg.reshape(n_dir, n_gate, n_col, per, bw, bw)
    eye = jnp.eye(per, dtype=w.dtype)
    wbd = jnp.einsum('dgcpij,pq->dgcpiqj', w, eye).reshape(n_dir, n_gate, n_col, MXU_DIM, MXU_DIM)
    wbd = jnp.concatenate([wbd[:, 0], wbd[:, 1]], axis=-1).astype(BF16)
    b = b_rg.reshape(n_dir, n_gate, n_col, MXU_DIM)
    brg = jnp.concatenate([b[:, 0], b[:, 1]], axis=-1)[:, :, None, :]
    return wbd, brg


def kernel(x, c, ctx, c_ctx, w_ada, b_ada, g_norm, w_in, b_gate, conv_w, conv_b, w_rg, b_rg, lam, rpb,
           w_lru_out, w_na_out, w_o, w_mlp1, w_mlp2):
    n_b, seq, d = x.shape
    ctx_len = ctx.shape[1]
    rows = seq // GRID_W
    depth = w_ada.shape[0]
    assert depth == 1 and rows % ATT_ROWS == 0 and n_b == SUBLANES

    l = 0
    cs = jnp.concatenate([c, c_ctx[None], jnp.zeros((2 * SUBLANES - n_b - 1, d), F32)], axis=0)
    mod = _mod_call(cs, w_ada[l], b_ada[l][None])
    mod3 = mod[:, None, :]

    w_in_bf = w_in[l].astype(BF16)
    b_gate2 = b_gate[l][None]
    g0 = g_norm[l, 0:1]

    xr, proj3 = _inproj_call(x, mod3, None, g0, w_in_bf, b_gate2, ts=INPROJ_TS, segs=(0, 1, 2, 3, 4, 5, 6))
    xr_c, projc3 = _inproj_call(ctx, mod3, n_b, g0, w_in_bf, b_gate2, ts=INPROJ_TS, segs=(0, 3, 4))

    wbd, brg = _gate_weights(w_rg[l], b_rg[l])
    lam3 = lam[l][:, None, :]
    cb = conv_b[l][None]
    _, h0 = _scan_call(xr_c, conv_w[l], cb, wbd, brg, lam3, jnp.zeros((2, d // LANES, n_b, LANES), F32),
                       tc=SCAN_TC)
    h, _ = _scan_call(xr, conv_w[l], cb, wbd, brg, lam3, h0, tc=SCAN_TC)

    bias = _attn_bias(rpb[l])
    att = _attn_call(proj3.reshape(n_b * seq, -1), projc3.reshape(n_b * ctx_len, -1), bias,
                     n_b=n_b, seq=seq, ctx_len=ctx_len, q_col=1, k_col=2, v_col=3, kc_col=0, vc_col=1)

    return _out_call(h, proj3, att.reshape(n_b, seq, d), x, mod3, g_norm[l],
                     w_lru_out[l].astype(BF16), w_na_out[l].astype(BF16), w_o[l].astype(BF16),
                     w_mlp1[l].astype(BF16), w_mlp2[l].astype(BF16),
                     ts=OUT_TS, ggr_col=0, gl_col=4, gna_col=5)
```

```python
import functools

import numpy as np
import jax
import jax.numpy as jnp
from jax import lax
from jax.experimental import pallas as pl
from jax.experimental.pallas import tpu as pltpu

F32 = jnp.float32
BF16 = jnp.bfloat16

EPS = 1e-6
NEG = -1e30
LRU_C = 8.0
N_HEADS = 16
HEAD_DIM = 64
GRID_W = 64
WIN_ROWS = 8
WIN_COLS = 16
CONV_W = 4

LANES = 128
SUBLANES = 8
MXU_DIM = 256
VMEM_LIMIT = 56 * 1024 * 1024

INPROJ_TS = 64
SCAN_TC = 128
OUT_TS = 64

ATT_ROWS = 4
ATT_Q = ATT_ROWS * GRID_W
ATT_SLOTS = 3 * ATT_ROWS
OUT_GROUPS = 2
FF_CHUNK = 1024
LOG2E = 1.4426950408889634


def _rms(x, g):
    return x * lax.rsqrt(jnp.mean(x * x, axis=-1, keepdims=True) + EPS) * g


def _mod_kernel(c_ref, w_ref, b_ref, o_ref):
    c = c_ref[...]
    s = c * jax.nn.sigmoid(c)
    o_ref[...] = jnp.dot(s, w_ref[...], preferred_element_type=F32) + b_ref[...]


def _mod_call(cs, w_ada, b_ada):
    rows, d = cs.shape
    n_out = w_ada.shape[1]
    return pl.pallas_call(
        _mod_kernel,
        grid=(n_out // d,),
        in_specs=[
            pl.BlockSpec((rows, d), lambda n: (0, 0)),
            pl.BlockSpec((d, d), lambda n: (0, n)),
            pl.BlockSpec((1, d), lambda n: (0, n)),
        ],
        out_specs=pl.BlockSpec((rows, d), lambda n: (0, n)),
        out_shape=jax.ShapeDtypeStruct((rows, n_out), F32),
        compiler_params=pltpu.CompilerParams(
            dimension_semantics=("arbitrary",), vmem_limit_bytes=VMEM_LIMIT),
        name="mod",
    )(cs, w_ada, b_ada)


def _inproj_kernel(x_ref, sh_ref, sc_ref, g_ref, w_ref, bg_ref, xr_ref, pr_ref, *, segs):
    n_b, ts, d = x_ref.shape
    u = _rms(x_ref[...], g_ref[...]) * (1.0 + sc_ref[...]) + sh_ref[...]
    u = u.reshape(n_b * ts, d).astype(BF16)
    for idx, seg in enumerate(segs):
        acc = jnp.dot(u, w_ref[:, seg * d:(seg + 1) * d], preferred_element_type=F32)
        if seg == 0:
            for b in range(n_b):
                for k in range(d // LANES):
                    xr_ref[k, pl.ds(b, ts, stride=n_b), :] = acc[b * ts:(b + 1) * ts, k * LANES:(k + 1) * LANES]
            continue
        if seg == 1:
            val = jax.nn.gelu(acc)
        elif seg == 2:
            val = acc * (HEAD_DIM ** -0.5 * LOG2E)
        elif seg in (3, 4):
            val = acc
        else:
            val = jax.nn.sigmoid(acc + bg_ref[:, (seg - 5) * d:(seg - 4) * d])
        pr_ref[:, :, (idx - 1) * d:idx * d] = val.astype(BF16).reshape(n_b, ts, d)


def _inproj_call(x3, mod3, mod_row, g0, w_in_bf, b_gate2, *, ts, segs):
    n_b, seq, d = x3.shape
    segs = tuple(segs)
    n_out = (len(segs) - 1) * d
    if mod_row is None:
        mod_blk, mod_idx = (n_b, 1, d), 0
    else:
        mod_blk, mod_idx = (1, 1, d), mod_row
    kern = functools.partial(_inproj_kernel, segs=segs)
    return pl.pallas_call(
        kern,
        grid=(seq // ts,),
        in_specs=[
            pl.BlockSpec((n_b, ts, d), lambda t: (0, t, 0)),
            pl.BlockSpec(mod_blk, lambda t: (mod_idx, 0, 0)),
            pl.BlockSpec(mod_blk, lambda t: (mod_idx, 0, 1)),
            pl.BlockSpec((1, d), lambda t: (0, 0)),
            pl.BlockSpec(w_in_bf.shape, lambda t: (0, 0), pipeline_mode=pl.Buffered(1)),
            pl.BlockSpec(b_gate2.shape, lambda t: (0, 0)),
        ],
        out_specs=[
            pl.BlockSpec((d // LANES, ts * n_b, LANES), lambda t: (0, t, 0)),
            pl.BlockSpec((n_b, ts, n_out), lambda t: (0, t, 0)),
        ],
        out_shape=[
            jax.ShapeDtypeStruct((d // LANES, seq * n_b, LANES), F32),
            jax.ShapeDtypeStruct((n_b, seq, n_out), BF16),
        ],
        compiler_params=pltpu.CompilerParams(
            dimension_semantics=("arbitrary",), vmem_limit_bytes=VMEM_LIMIT),
        name="inproj",
    )(x3, mod3, mod3, g0, w_in_bf, b_gate2)


def _scan_kernel(prev_ref, main_ref, next_ref, cw_ref, cb_ref, wbd_ref, brg_ref, lam_ref, h0_ref,
                 h_ref, hfin_ref, a_s, b_s, hst, *, n_chunks, tc):
    d = pl.program_id(0)
    i = pl.program_id(1)
    ci = jnp.where(d == 0, i, n_chunks - 1 - i)

    @pl.when(i == 0)
    def _():
        hst[...] = h0_ref[d]

    has_prev = (ci > 0).astype(F32)
    has_next = (ci < n_chunks - 1).astype(F32)
    n_lc = a_s.shape[0]
    rows = tc * SUBLANES
    per = MXU_DIM // LANES
    for c in range(n_lc // per):
        xcs = []
        for k in range(c * per, (c + 1) * per):
            sl = slice(k * LANES, (k + 1) * LANES)
            xe = jnp.concatenate([prev_ref[k] * has_prev, main_ref[k], next_ref[k] * has_next], axis=0)
            xc = cb_ref[:, sl]
            for j in range(CONV_W):
                xc = xc + cw_ref[j:j + 1, sl] * xe[j * SUBLANES:j * SUBLANES + rows]
            xcs.append(xc)
        xc2 = jnp.concatenate(xcs, axis=1)
        sl = slice(c * MXU_DIM, (c + 1) * MXU_DIM)
        g = jnp.dot(xc2.astype(BF16), wbd_ref[0, c], preferred_element_type=F32) + brg_ref[0, c]
        r2 = jnp.tanh(g[:, :MXU_DIM]) + 1.0
        i2 = jnp.tanh(g[:, MXU_DIM:]) + 1.0
        z = -lam_ref[0, :, sl]
        softplus = jnp.maximum(z, 0.0) + jnp.log1p(jnp.exp(-jnp.abs(z)))
        c_nat = (0.5 * LRU_C) * softplus
        neg_log_a = c_nat * r2
        a = jnp.exp2((-LOG2E * c_nat) * r2)
        one_minus_a2 = jnp.tanh(neg_log_a) * (a * a + 1.0)
        root = jnp.where(one_minus_a2 > 0.0, one_minus_a2 * lax.rsqrt(one_minus_a2), 0.0)
        bb = root * (i2 * (0.5 * xc2))
        for kk in range(per):
            a_s[c * per + kk, :rows] = a[:, kk * LANES:(kk + 1) * LANES]
            b_s[c * per + kk, :rows] = bb[:, kk * LANES:(kk + 1) * LANES]

    def step(j, h):
        t = jnp.where(d == 0, j, tc - 1 - j)
        r0 = pl.multiple_of(t * SUBLANES, SUBLANES)
        h = a_s[:, pl.ds(r0, SUBLANES), :] * h + b_s[:, pl.ds(r0, SUBLANES), :]
        h_ref[0, :, pl.ds(r0, SUBLANES), :] = h
        return h

    h_last = lax.fori_loop(0, tc, step, hst[...], unroll=8)
    hst[...] = h_last
    hfin_ref[0] = h_last


def _scan_call(xr, conv_w, conv_b, wbd, brg, lam3, h0, *, tc):
    n_lc, n_rows, _ = xr.shape
    n_b = SUBLANES
    seq = n_rows // n_b
    d = n_lc * LANES
    n_chunks = seq // tc
    n_col = d // MXU_DIM
    rows = tc * n_b

    def chunk(dd, i):
        return jnp.where(dd == 0, i, n_chunks - 1 - i)

    kern = functools.partial(_scan_kernel, n_chunks=n_chunks, tc=tc)
    return pl.pallas_call(
        kern,
        grid=(2, n_chunks),
        in_specs=[
            pl.BlockSpec((n_lc, 2 * n_b, LANES),
                         lambda dd, i: (0, jnp.maximum(chunk(dd, i) * (tc // 2) - 1, 0), 0)),
            pl.BlockSpec((n_lc, rows, LANES), lambda dd, i: (0, chunk(dd, i), 0)),
            pl.BlockSpec((n_lc, n_b, LANES),
                         lambda dd, i: (0, jnp.minimum((chunk(dd, i) + 1) * tc, seq - 1), 0)),
            pl.BlockSpec((CONV_W, d), lambda dd, i: (0, 0)),
            pl.BlockSpec((1, d), lambda dd, i: (0, 0)),
            pl.BlockSpec((1, n_col, MXU_DIM, 2 * MXU_DIM), lambda dd, i: (dd, 0, 0, 0)),
            pl.BlockSpec((1, n_col, 1, 2 * MXU_DIM), lambda dd, i: (dd, 0, 0, 0)),
            pl.BlockSpec((1, 1, d), lambda dd, i: (dd, 0, 0)),
            pl.BlockSpec((2, n_lc, n_b, LANES), lambda dd, i: (0, 0, 0, 0)),
        ],
        out_specs=[
            pl.BlockSpec((1, n_lc, rows, LANES), lambda dd, i: (dd, 0, chunk(dd, i), 0)),
            pl.BlockSpec((1, n_lc, n_b, LANES), lambda dd, i: (dd, 0, 0, 0)),
        ],
        out_shape=[
            jax.ShapeDtypeStruct((2, n_lc, n_rows, LANES), F32),
            jax.ShapeDtypeStruct((2, n_lc, n_b, LANES), F32),
        ],
        scratch_shapes=[
            pltpu.VMEM((n_lc, rows + n_b, LANES), F32),
            pltpu.VMEM((n_lc, rows + n_b, LANES), F32),
            pltpu.VMEM((n_lc, n_b, LANES), F32),
        ],
        compiler_params=pltpu.CompilerParams(
            dimension_semantics=("arbitrary", "arbitrary"), vmem_limit_bytes=VMEM_LIMIT),
        name="scan",
    )(xr, xr, xr, conv_w, conv_b, wbd, brg, lam3, h0)


def _window_plan(i, n_i, rows):
    wr = min(WIN_ROWS, rows)
    rng = []
    for rq in range(ATT_ROWS):
        r = ATT_ROWS * i + rq
        r0 = min(max(r - wr // 2, 0), rows - wr)
        lo = r0 - (ATT_ROWS * i - ATT_ROWS)
        rng.append((lo, lo + wr))
    s_lo = min(lo for lo, _ in rng)
    s_hi = max(hi for _, hi in rng)
    assert 0 <= s_lo and s_hi <= ATT_SLOTS
    rows_per_tile = LANES // GRID_W
    tiles = []
    for j in range(ATT_ROWS // rows_per_tile):
        rqs = range(j * rows_per_tile, (j + 1) * rows_per_tile)
        a = min(rng[rq][0] for rq in rqs)
        b = max(rng[rq][1] for rq in rqs)
        half = {}
        for s in range(a, b):
            ok = tuple(rng[rq][0] <= s < rng[rq][1] for rq in rqs)
            if not all(ok):
                half[s - s_lo] = ok
        tiles.append((a - s_lo, b - s_lo, half))
    return s_lo, s_hi, tiles


def _attn_block(q_ref, k_refs, v_refs, kc_ref, vc_ref, bias_ref, o_ref, s_buf, plan):
    s_lo, s_hi, tiles = plan
    n_slots = s_hi - s_lo
    n_loc = n_slots * GRID_W
    n_ctx = kc_ref.shape[0]
    lane = lax.broadcasted_iota(jnp.int32, (1, LANES), 1)
    lo = lane < HEAD_DIM
    heads_per_pair = LANES // HEAD_DIM
    n_pairs = N_HEADS // heads_per_pair
    dims_nt = (((1,), (1,)), ((), ()))
    dims_tn = (((0,), (0,)), ((), ()))

    def keys_of(refs, c_ref, ls):
        parts = []
        for blk, ref in enumerate(refs):
            a = max(s_lo, blk * ATT_ROWS) - blk * ATT_ROWS
            b = min(s_hi, (blk + 1) * ATT_ROWS) - blk * ATT_ROWS
            if b > a:
                parts.append(ref[a * GRID_W:b * GRID_W, ls])
        return jnp.concatenate(parts + [c_ref[:, ls]], axis=0)

    def scores(p):
        ls = slice(p * LANES, (p + 1) * LANES)
        qp = q_ref[:, ls]
        k_all = keys_of(k_refs, kc_ref, ls)
        for hh in range(heads_per_pair):
            msk = lo if hh == 0 else jnp.logical_not(lo)
            qm = jnp.where(msk, qp, jnp.zeros_like(qp))
            s_buf[p % 2, hh, :n_loc + n_ctx] = lax.dot_general(k_all, qm, dims_nt, preferred_element_type=F32)

    def softmax_pv(p):
        ls = slice(p * LANES, (p + 1) * LANES)
        v_all = keys_of(v_refs, vc_ref, ls)
        o_t = []
        for hh in range(heads_per_pair):
            own = lo if hh == 0 else jnp.logical_not(lo)
            head = p * heads_per_pair + hh
            cols = []
            for j, (a, b, half) in enumerate(tiles):
                qs = slice(j * LANES, (j + 1) * LANES)
                blocks = []
                for s in range(a, b):
                    blk = (s_buf[p % 2, hh, s * GRID_W:(s + 1) * GRID_W, qs]
                           + bias_ref[head, (s_lo + s) * GRID_W:(s_lo + s + 1) * GRID_W, qs])
                    if s in half:
                        blk = jnp.where(lo if half[s][0] else jnp.logical_not(lo), blk, NEG)
                    blocks.append(blk)
                s_loc = jnp.concatenate(blocks, axis=0)
                s_ctx = s_buf[p % 2, hh, n_loc:n_loc + n_ctx, qs]
                m = jnp.maximum(jnp.max(s_loc, axis=0, keepdims=True), jnp.max(s_ctx, axis=0, keepdims=True))
                col = [jnp.exp2(s_loc - m), jnp.exp2(s_ctx - m)]
                if a > 0:
                    col.insert(0, jnp.zeros((a * GRID_W, LANES), F32))
                if b < n_slots:
                    col.insert(-1, jnp.zeros(((n_slots - b) * GRID_W, LANES), F32))
                cols.append(jnp.concatenate(col, axis=0))
            p_all = jnp.concatenate(cols, axis=1).astype(BF16)
            v_own = jnp.where(own, v_all, jnp.ones_like(v_all))
            o = lax.dot_general(v_own, p_all, dims_tn, preferred_element_type=F32)
            den = o[(1 - hh) * HEAD_DIM:(1 - hh) * HEAD_DIM + 1]
            o_t.append(o[hh * HEAD_DIM:(hh + 1) * HEAD_DIM] * (1.0 / den))
        o_ref[:, ls] = jnp.concatenate(o_t, axis=0).T.astype(BF16)

    scores(0)
    for p in range(n_pairs):
        if p + 1 < n_pairs:
            scores(p + 1)
        softmax_pv(p)


def _attn_kernel(q_ref, kp_ref, km_ref, kn_ref, vp_ref, vm_ref, vn_ref, kc_ref, vc_ref, bias_ref, o_ref, s_buf,
                 *, plans):
    i = pl.program_id(0)
    n_i = pl.num_programs(0)
    variant = (i > 0).astype(jnp.int32) + (i == n_i - 1).astype(jnp.int32)
    for v, plan in enumerate(plans):
        @pl.when(variant == v)
        def _(plan=plan):
            _attn_block(q_ref, (kp_ref, km_ref, kn_ref), (vp_ref, vm_ref, vn_ref), kc_ref, vc_ref,
                        bias_ref, o_ref, s_buf, plan)


def _attn_call(proj, projc, bias, *, n_b, seq, ctx_len, q_col, k_col, v_col, kc_col, vc_col):
    d = N_HEADS * HEAD_DIM
    n_i = seq // ATT_Q
    rows = seq // GRID_W
    plans = tuple(_window_plan(i, n_i, rows) for i in (0, 1, n_i - 1))

    def row(i, b):
        return b * n_i + i

    def row_prev(i, b):
        return b * n_i + jnp.maximum(i - 1, 0)

    def row_next(i, b):
        return b * n_i + jnp.minimum(i + 1, n_i - 1)

    blk = (ATT_Q, d)
    return pl.pallas_call(
        functools.partial(_attn_kernel, plans=plans),
        grid=(n_i, n_b),
        in_specs=[
            pl.BlockSpec(blk, lambda i, b: (row(i, b), q_col)),
            pl.BlockSpec(blk, lambda i, b: (row_prev(i, b), k_col)),
            pl.BlockSpec(blk, lambda i, b: (row(i, b), k_col)),
            pl.BlockSpec(blk, lambda i, b: (row_next(i, b), k_col)),
            pl.BlockSpec(blk, lambda i, b: (row_prev(i, b), v_col)),
            pl.BlockSpec(blk, lambda i, b: (row(i, b), v_col)),
            pl.BlockSpec(blk, lambda i, b: (row_next(i, b), v_col)),
            pl.BlockSpec((ctx_len, d), lambda i, b: (b, kc_col)),
            pl.BlockSpec((ctx_len, d), lambda i, b: (b, vc_col)),
            pl.BlockSpec(bias.shape, lambda i, b: (0, 0, 0), pipeline_mode=pl.Buffered(1)),
        ],
        out_specs=pl.BlockSpec(blk, lambda i, b: (row(i, b), 0)),
        out_shape=jax.ShapeDtypeStruct((n_b * seq, d), BF16),
        scratch_shapes=[pltpu.VMEM((2, LANES // HEAD_DIM, ATT_SLOTS * GRID_W + ctx_len, ATT_Q), F32)],
        compiler_params=pltpu.CompilerParams(
            dimension_semantics=("arbitrary", "arbitrary"), vmem_limit_bytes=VMEM_LIMIT),
        name="attn",
    )(proj, proj, proj, proj, proj, proj, proj, projc, projc, bias)


def _attn_bias(rpb):
    qc = np.arange(GRID_W)[None, :]
    kc = np.arange(GRID_W)[:, None]
    col_start = np.clip(qc - WIN_COLS // 2, 0, GRID_W - WIN_COLS)
    col_ok = (kc >= col_start) & (kc < col_start + WIN_COLS)
    dc_idx = np.clip(kc - qc + WIN_COLS - 1, 0, 2 * WIN_COLS - 2)
    onehot = (dc_idx[None] == np.arange(2 * WIN_COLS - 1)[:, None, None]) & col_ok[None]
    col_tab = jnp.einsum('had,dkq->hakq', rpb.astype(F32), jnp.asarray(onehot, F32),
                         precision=lax.Precision.HIGHEST)
    col_tab = jnp.where(jnp.asarray(col_ok)[None, None], col_tab * LOG2E, NEG)
    slots = []
    for s in range(ATT_SLOTS):
        q_rows = [col_tab[:, s - ATT_ROWS - rq + WIN_ROWS - 1] for rq in range(ATT_ROWS)]
        slots.append(jnp.concatenate(q_rows, axis=-1))
    return jnp.concatenate(slots, axis=1)


def _out_kernel(hf_ref, hr_ref, ggr_ref, gl_ref, gna_ref, att_ref, x_ref, gt1_ref, sh2_ref, sc2_ref,
                gt2_ref, gn_ref, wl_ref, wn_ref, wo_ref, w1_ref, w2_ref, out_ref):
    n_b, ts, d = x_ref.shape
    group = n_b // OUT_GROUPS
    rows = group * ts
    groups = [slice(g * group, (g + 1) * group) for g in range(OUT_GROUPS)]

    y_na, y_lru = [], []
    for bs in groups:
        y_na.append(jnp.dot(att_ref[bs].reshape(rows, d), wn_ref[...], preferred_element_type=F32))
        hl = jnp.stack([
            jnp.concatenate([hf_ref[0, k, pl.ds(b, ts, stride=n_b), :] + hr_ref[0, k, pl.ds(b, ts, stride=n_b), :]
                             for k in range(d // LANES)], axis=1)
            for b in range(bs.start, bs.stop)], axis=0)
        hl = hl * ggr_ref[bs].astype(F32)
        y_lru.append(jnp.dot(hl.reshape(rows, d).astype(BF16), wl_ref[...], preferred_element_type=F32))

    ys = []
    for g, bs in enumerate(groups):
        y = (gl_ref[bs].astype(F32).reshape(rows, d) * y_lru[g]
             + gna_ref[bs].astype(F32).reshape(rows, d) * y_na[g])
        ys.append(jnp.dot(y.astype(BF16), wo_ref[...], preferred_element_type=F32).reshape(group, ts, d))

    x1s, u2s = [], []
    for g, bs in enumerate(groups):
        x1 = x_ref[bs] + gt1_ref[bs] * _rms(ys[g], gn_ref[1:2, :])
        u2 = _rms(x1, gn_ref[2:3, :]) * (1.0 + sc2_ref[bs]) + sh2_ref[bs]
        x1s.append(x1)
        u2s.append(u2.reshape(rows, d).astype(BF16))

    ms = [None] * OUT_GROUPS
    for f in range(0, w1_ref.shape[1], FF_CHUNK):
        h1s = []
        for g in range(OUT_GROUPS):
            h1 = jnp.dot(u2s[g], w1_ref[:, f:f + FF_CHUNK], preferred_element_type=F32)
            h1 = jnp.maximum(h1, 0.0)
            h1s.append((h1 * h1).astype(BF16))
        for g in range(OUT_GROUPS):
            part = jnp.dot(h1s[g], w2_ref[f:f + FF_CHUNK, :], preferred_element_type=F32)
            ms[g] = part if ms[g] is None else ms[g] + part

    for g, bs in enumerate(groups):
        out_ref[bs] = x1s[g] + gt2_ref[bs] * _rms(ms[g].reshape(group, ts, d), gn_ref[3:4, :])


def _out_call(h2, proj3, att3, x3, mod3, g_norm, wl, wn, wo, w1, w2, *, ts, ggr_col, gl_col, gna_col):
    n_b, seq, d = x3.shape

    def modspec(k):
        return pl.BlockSpec((n_b, 1, d), lambda t: (0, 0, k))

    def resident(w):
        return pl.BlockSpec(w.shape, lambda t: (0, 0), pipeline_mode=pl.Buffered(1))

    tile = (n_b, ts, d)
    return pl.pallas_call(
        _out_kernel,
        grid=(seq // ts,),
        in_specs=[
            pl.BlockSpec((1, d // LANES, ts * n_b, LANES), lambda t: (0, 0, t, 0)),
            pl.BlockSpec((1, d // LANES, ts * n_b, LANES), lambda t: (1, 0, t, 0)),
            pl.BlockSpec(tile, lambda t: (0, t, ggr_col)),
            pl.BlockSpec(tile, lambda t: (0, t, gl_col)),
            pl.BlockSpec(tile, lambda t: (0, t, gna_col)),
            pl.BlockSpec(tile, lambda t: (0, t, 0)),
            pl.BlockSpec(tile, lambda t: (0, t, 0)),
            modspec(2), modspec(3), modspec(4), modspec(5),
            pl.BlockSpec(g_norm.shape, lambda t: (0, 0)),
            resident(wl), resident(wn), resident(wo), resident(w1), resident(w2),
        ],
        out_specs=pl.BlockSpec(tile, lambda t: (0, t, 0)),
        out_shape=jax.ShapeDtypeStruct((n_b, seq, d), F32),
        compiler_params=pltpu.CompilerParams(
            dimension_semantics=("arbitrary",), vmem_limit_bytes=VMEM_LIMIT),
        name="out",
    )(h2, h2, proj3, proj3, proj3, att3, x3, mod3, mod3, mod3, mod3, g_norm, wl, wn, wo, w1, w2)


def _gate_weights(w_rg, b_rg):
    n_dir, n_gate, n_blk, bw, _ = w_rg.shape
    per = MXU_DIM // bw
    n_col = n_blk // per
    w_rg = 0.5 * w_rg
    b_rg = 0.5 * b_rg
    w = w_rg.reshape(n_dir, n_gate, n_col, per, bw, bw)
    eye = jnp.eye(per, dtype=w.dtype)
    wbd = jnp.einsum('dgcpij,pq->dgcpiqj', w, eye).reshape(n_dir, n_gate, n_col, MXU_DIM, MXU_DIM)
    wbd = jnp.concatenate([wbd[:, 0], wbd[:, 1]], axis=-1).astype(BF16)
    b = b_rg.reshape(n_dir, n_gate, n_col, MXU_DIM)
    brg = jnp.concatenate([b[:, 0], b[:, 1]], axis=-1)[:, :, None, :]
    return wbd, brg


def kernel(x, c, ctx, c_ctx, w_ada, b_ada, g_norm, w_in, b_gate, conv_w, conv_b, w_rg, b_rg, lam, rpb,
           w_lru_out, w_na_out, w_o, w_mlp1, w_mlp2):
    n_b, seq, d = x.shape
    ctx_len = ctx.shape[1]
    rows = seq // GRID_W
    depth = w_ada.shape[0]
    assert depth == 1 and rows % ATT_ROWS == 0 and n_b == SUBLANES

    l = 0
    cs = jnp.concatenate([c, c_ctx[None], jnp.zeros((2 * SUBLANES - n_b - 1, d), F32)], axis=0)
    mod = _mod_call(cs, w_ada[l], b_ada[l][None])
    mod3 = mod[:, None, :]

    w_in_bf = w_in[l].astype(BF16)
    b_gate2 = b_gate[l][None]
    g0 = g_norm[l, 0:1]

    xr, proj3 = _inproj_call(x, mod3, None, g0, w_in_bf, b_gate2, ts=INPROJ_TS, segs=(0, 1, 2, 3, 4, 5, 6))
    xr_c, projc3 = _inproj_call(ctx, mod3, n_b, g0, w_in_bf, b_gate2, ts=INPROJ_TS, segs=(0, 3, 4))

    wbd, brg = _gate_weights(w_rg[l], b_rg[l])
    lam3 = lam[l][:, None, :]
    cb = conv_b[l][None]
    _, h0 = _scan_call(xr_c, conv_w[l], cb, wbd, brg, lam3, jnp.zeros((2, d // LANES, n_b, LANES), F32),
                       tc=SCAN_TC)
    h, _ = _scan_call(xr, conv_w[l], cb, wbd, brg, lam3, h0, tc=SCAN_TC)

    bias = _attn_bias(rpb[l])
    att = _attn_call(proj3.reshape(n_b * seq, -1), projc3.reshape(n_b * ctx_len, -1), bias,
                     n_b=n_b, seq=seq, ctx_len=ctx_len, q_col=1, k_col=2, v_col=3, kc_col=0, vc_col=1)

    return _out_call(h, proj3, att.reshape(n_b, seq, d), x, mod3, g_norm[l],
                     w_lru_out[l].astype(BF16), w_na_out[l].astype(BF16), w_o[l].astype(BF16),
                     w_mlp1[l].astype(BF16), w_mlp2[l].astype(BF16),
                     ts=OUT_TS, ggr_col=0, gl_col=4, gna_col=5)
```

```python
import functools

import numpy as np
import jax
import jax.numpy as jnp
from jax import lax
from jax.experimental import pallas as pl
from jax.experimental.pallas import tpu as pltpu

F32 = jnp.float32
BF16 = jnp.bfloat16

EPS = 1e-6
NEG = -1e30
LRU_C = 8.0
N_HEADS = 16
HEAD_DIM = 64
GRID_W = 64
WIN_ROWS = 8
WIN_COLS = 16
CONV_W = 4

LANES = 128
SUBLANES = 8
MXU_DIM = 256
VMEM_LIMIT = 56 * 1024 * 1024

INPROJ_TS = 64
SCAN_TC = 128
OUT_TS = 64

ATT_ROWS = 4
ATT_Q = ATT_ROWS * GRID_W
ATT_SLOTS = 3 * ATT_ROWS
INPROJ_GROUPS = 2
OUT_GROUPS = 2
FF_CHUNK = 1024
LOG2E = 1.4426950408889634


def _rms(x, g):
    return x * lax.rsqrt(jnp.mean(x * x, axis=-1, keepdims=True) + EPS) * g


def _mod_kernel(c_ref, w_ref, b_ref, o_ref):
    c = c_ref[...]
    s = c * jax.nn.sigmoid(c)
    o_ref[...] = jnp.dot(s, w_ref[...], preferred_element_type=F32) + b_ref[...]


def _mod_call(cs, w_ada, b_ada):
    rows, d = cs.shape
    n_out = w_ada.shape[1]
    return pl.pallas_call(
        _mod_kernel,
        grid=(n_out // d,),
        in_specs=[
            pl.BlockSpec((rows, d), lambda n: (0, 0)),
            pl.BlockSpec((d, d), lambda n: (0, n)),
            pl.BlockSpec((1, d), lambda n: (0, n)),
        ],
        out_specs=pl.BlockSpec((rows, d), lambda n: (0, n)),
        out_shape=jax.ShapeDtypeStruct((rows, n_out), F32),
        compiler_params=pltpu.CompilerParams(
            dimension_semantics=("arbitrary",), vmem_limit_bytes=VMEM_LIMIT),
        name="mod",
    )(cs, w_ada, b_ada)


def _inproj_kernel(x_ref, sh_ref, sc_ref, g_ref, w_ref, bg_ref, xr_ref, pr_ref, *, segs):
    n_b, ts, d = x_ref.shape
    group = n_b // INPROJ_GROUPS
    groups = [slice(g * group, (g + 1) * group) for g in range(INPROJ_GROUPS)]

    def per_sample(ref, bs):
        return ref[bs] if ref.shape[0] == n_b else ref[...]

    us = []
    for bs in groups:
        u = _rms(x_ref[bs], g_ref[...]) * (1.0 + per_sample(sc_ref, bs)) + per_sample(sh_ref, bs)
        us.append(u.reshape(group * ts, d).astype(BF16))
    for idx, seg in enumerate(segs):
        for bs, u in zip(groups, us):
            acc = jnp.dot(u, w_ref[:, seg * d:(seg + 1) * d], preferred_element_type=F32)
            if seg == 0:
                for b in range(group):
                    for k in range(d // LANES):
                        xr_ref[k, pl.ds(bs.start + b, ts, stride=n_b), :] = (
                            acc[b * ts:(b + 1) * ts, k * LANES:(k + 1) * LANES])
                continue
            if seg == 1:
                val = jax.nn.gelu(acc)
            elif seg == 2:
                val = acc * (HEAD_DIM ** -0.5 * LOG2E)
            elif seg in (3, 4):
                val = acc
            else:
                val = jax.nn.sigmoid(acc + bg_ref[:, (seg - 5) * d:(seg - 4) * d])
            pr_ref[bs, :, (idx - 1) * d:idx * d] = val.astype(BF16).reshape(group, ts, d)


def _inproj_call(x3, mod3, mod_row, g0, w_in_bf, b_gate2, *, ts, segs):
    n_b, seq, d = x3.shape
    segs = tuple(segs)
    n_out = (len(segs) - 1) * d
    if mod_row is None:
        mod_blk, mod_idx = (n_b, 1, d), 0
    else:
        mod_blk, mod_idx = (1, 1, d), mod_row
    kern = functools.partial(_inproj_kernel, segs=segs)
    return pl.pallas_call(
        kern,
        grid=(seq // ts,),
        in_specs=[
            pl.BlockSpec((n_b, ts, d), lambda t: (0, t, 0)),
            pl.BlockSpec(mod_blk, lambda t: (mod_idx, 0, 0)),
            pl.BlockSpec(mod_blk, lambda t: (mod_idx, 0, 1)),
            pl.BlockSpec((1, d), lambda t: (0, 0)),
            pl.BlockSpec(w_in_bf.shape, lambda t: (0, 0), pipeline_mode=pl.Buffered(1)),
            pl.BlockSpec(b_gate2.shape, lambda t: (0, 0)),
        ],
        out_specs=[
            pl.BlockSpec((d // LANES, ts * n_b, LANES), lambda t: (0, t, 0)),
            pl.BlockSpec((n_b, ts, n_out), lambda t: (0, t, 0)),
        ],
        out_shape=[
            jax.ShapeDtypeStruct((d // LANES, seq * n_b, LANES), F32),
            jax.ShapeDtypeStruct((n_b, seq, n_out), BF16),
        ],
        compiler_params=pltpu.CompilerParams(
            dimension_semantics=("arbitrary",), vmem_limit_bytes=VMEM_LIMIT),
        name="inproj",
    )(x3, mod3, mod3, g0, w_in_bf, b_gate2)


def _scan_kernel(prev_ref, main_ref, next_ref, cw_ref, cb_ref, wbd_ref, brg_ref, lam_ref, h0_ref,
                 h_ref, hfin_ref, a_s, b_s, hst, *, n_chunks, tc):
    d = pl.program_id(0)
    i = pl.program_id(1)
    ci = jnp.where(d == 0, i, n_chunks - 1 - i)

    @pl.when(i == 0)
    def _():
        hst[...] = h0_ref[d]

    has_prev = (ci > 0).astype(F32)
    has_next = (ci < n_chunks - 1).astype(F32)
    n_lc = a_s.shape[0]
    rows = tc * SUBLANES
    per = MXU_DIM // LANES
    for c in range(n_lc // per):
        xcs = []
        for k in range(c * per, (c + 1) * per):
            sl = slice(k * LANES, (k + 1) * LANES)
            xe = jnp.concatenate([prev_ref[k] * has_prev, main_ref[k], next_ref[k] * has_next], axis=0)
            xc = cb_ref[:, sl]
            for j in range(CONV_W):
                xc = xc + cw_ref[j:j + 1, sl] * xe[j * SUBLANES:j * SUBLANES + rows]
            xcs.append(xc)
        xc2 = jnp.concatenate(xcs, axis=1)
        sl = slice(c * MXU_DIM, (c + 1) * MXU_DIM)
        g = jnp.dot(xc2.astype(BF16), wbd_ref[0, c], preferred_element_type=F32) + brg_ref[0, c]
        r2 = jnp.tanh(g[:, :MXU_DIM]) + 1.0
        i2 = jnp.tanh(g[:, MXU_DIM:]) + 1.0
        z = -lam_ref[0, :, sl]
        softplus = jnp.maximum(z, 0.0) + jnp.log1p(jnp.exp(-jnp.abs(z)))
        c_nat = (0.5 * LRU_C) * softplus
        neg_log_a = c_nat * r2
        a = jnp.exp2((-LOG2E * c_nat) * r2)
        one_minus_a2 = jnp.tanh(neg_log_a) * (a * a + 1.0)
        root = jnp.where(one_minus_a2 > 0.0, one_minus_a2 * lax.rsqrt(one_minus_a2), 0.0)
        bb = root * (i2 * (0.5 * xc2))
        for kk in range(per):
            a_s[c * per + kk, :rows] = a[:, kk * LANES:(kk + 1) * LANES]
            b_s[c * per + kk, :rows] = bb[:, kk * LANES:(kk + 1) * LANES]

    def step(j, h):
        t = jnp.where(d == 0, j, tc - 1 - j)
        r0 = pl.multiple_of(t * SUBLANES, SUBLANES)
        h = a_s[:, pl.ds(r0, SUBLANES), :] * h + b_s[:, pl.ds(r0, SUBLANES), :]
        h_ref[0, :, pl.ds(r0, SUBLANES), :] = h
        return h

    h_last = lax.fori_loop(0, tc, step, hst[...], unroll=8)
    hst[...] = h_last
    hfin_ref[0] = h_last


def _scan_call(xr, conv_w, conv_b, wbd, brg, lam3, h0, *, tc):
    n_lc, n_rows, _ = xr.shape
    n_b = SUBLANES
    seq = n_rows // n_b
    d = n_lc * LANES
    n_chunks = seq // tc
    n_col = d // MXU_DIM
    rows = tc * n_b

    def chunk(dd, i):
        return jnp.where(dd == 0, i, n_chunks - 1 - i)

    kern = functools.partial(_scan_kernel, n_chunks=n_chunks, tc=tc)
    return pl.pallas_call(
        kern,
        grid=(2, n_chunks),
        in_specs=[
            pl.BlockSpec((n_lc, 2 * n_b, LANES),
                         lambda dd, i: (0, jnp.maximum(chunk(dd, i) * (tc // 2) - 1, 0), 0)),
            pl.BlockSpec((n_lc, rows, LANES), lambda dd, i: (0, chunk(dd, i), 0)),
            pl.BlockSpec((n_lc, n_b, LANES),
                         lambda dd, i: (0, jnp.minimum((chunk(dd, i) + 1) * tc, seq - 1), 0)),
            pl.BlockSpec((CONV_W, d), lambda dd, i: (0, 0)),
            pl.BlockSpec((1, d), lambda dd, i: (0, 0)),
            pl.BlockSpec((1, n_col, MXU_DIM, 2 * MXU_DIM), lambda dd, i: (dd, 0, 0, 0)),
            pl.BlockSpec((1, n_col, 1, 2 * MXU_DIM), lambda dd, i: (dd, 0, 0, 0)),
            pl.BlockSpec((1, 1, d), lambda dd, i: (dd, 0, 0)),
            pl.BlockSpec((2, n_lc, n_b, LANES), lambda dd, i: (0, 0, 0, 0)),
        ],
        out_specs=[
            pl.BlockSpec((1, n_lc, rows, LANES), lambda dd, i: (dd, 0, chunk(dd, i), 0)),
            pl.BlockSpec((1, n_lc, n_b, LANES), lambda dd, i: (dd, 0, 0, 0)),
        ],
        out_shape=[
            jax.ShapeDtypeStruct((2, n_lc, n_rows, LANES), F32),
            jax.ShapeDtypeStruct((2, n_lc, n_b, LANES), F32),
        ],
        scratch_shapes=[
            pltpu.VMEM((n_lc, rows + n_b, LANES), F32),
            pltpu.VMEM((n_lc, rows + n_b, LANES), F32),
            pltpu.VMEM((n_lc, n_b, LANES), F32),
        ],
        compiler_params=pltpu.CompilerParams(
            dimension_semantics=("arbitrary", "arbitrary"), vmem_limit_bytes=VMEM_LIMIT),
        name="scan",
    )(xr, xr, xr, conv_w, conv_b, wbd, brg, lam3, h0)


def _window_plan(i, n_i, rows):
    wr = min(WIN_ROWS, rows)
    rng = []
    for rq in range(ATT_ROWS):
        r = ATT_ROWS * i + rq
        r0 = min(max(r - wr // 2, 0), rows - wr)
        lo = r0 - (ATT_ROWS * i - ATT_ROWS)
        rng.append((lo, lo + wr))
    s_lo = min(lo for lo, _ in rng)
    s_hi = max(hi for _, hi in rng)
    assert 0 <= s_lo and s_hi <= ATT_SLOTS
    rows_per_tile = LANES // GRID_W
    tiles = []
    for j in range(ATT_ROWS // rows_per_tile):
        rqs = range(j * rows_per_tile, (j + 1) * rows_per_tile)
        a = min(rng[rq][0] for rq in rqs)
        b = max(rng[rq][1] for rq in rqs)
        half = {}
        for s in range(a, b):
            ok = tuple(rng[rq][0] <= s < rng[rq][1] for rq in rqs)
            if not all(ok):
                half[s - s_lo] = ok
        tiles.append((a - s_lo, b - s_lo, half))
    return s_lo, s_hi, tiles


def _attn_block(q_ref, k_refs, v_refs, kc_ref, vc_ref, bias_ref, o_ref, s_buf, plan):
    s_lo, s_hi, tiles = plan
    n_slots = s_hi - s_lo
    n_loc = n_slots * GRID_W
    n_ctx = kc_ref.shape[0]
    lane = lax.broadcasted_iota(jnp.int32, (1, LANES), 1)
    lo = lane < HEAD_DIM
    heads_per_pair = LANES // HEAD_DIM
    n_pairs = N_HEADS // heads_per_pair
    dims_nt = (((1,), (1,)), ((), ()))
    dims_tn = (((0,), (0,)), ((), ()))

    def keys_of(refs, c_ref, ls):
        parts = []
        for blk, ref in enumerate(refs):
            a = max(s_lo, blk * ATT_ROWS) - blk * ATT_ROWS
            b = min(s_hi, (blk + 1) * ATT_ROWS) - blk * ATT_ROWS
            if b > a:
                parts.append(ref[a * GRID_W:b * GRID_W, ls])
        return jnp.concatenate(parts + [c_ref[:, ls]], axis=0)

    def scores(p):
        ls = slice(p * LANES, (p + 1) * LANES)
        qp = q_ref[:, ls]
        k_all = keys_of(k_refs, kc_ref, ls)
        for hh in range(heads_per_pair):
            msk = lo if hh == 0 else jnp.logical_not(lo)
            qm = jnp.where(msk, qp, jnp.zeros_like(qp))
            s_buf[p % 2, hh, :n_loc + n_ctx] = lax.dot_general(k_all, qm, dims_nt, preferred_element_type=F32)

    def softmax_pv(p):
        ls = slice(p * LANES, (p + 1) * LANES)
        v_all = keys_of(v_refs, vc_ref, ls)
        o_t = []
        for hh in range(heads_per_pair):
            own = lo if hh == 0 else jnp.logical_not(lo)
            head = p * heads_per_pair + hh
            cols = []
            for j, (a, b, half) in enumerate(tiles):
                qs = slice(j * LANES, (j + 1) * LANES)
                blocks = []
                for s in range(a, b):
                    blk = (s_buf[p % 2, hh, s * GRID_W:(s + 1) * GRID_W, qs]
                           + bias_ref[head, (s_lo + s) * GRID_W:(s_lo + s + 1) * GRID_W, qs])
                    if s in half:
                        blk = jnp.where(lo if half[s][0] else jnp.logical_not(lo), blk, NEG)
                    blocks.append(blk)
                s_loc = jnp.concatenate(blocks, axis=0)
                s_ctx = s_buf[p % 2, hh, n_loc:n_loc + n_ctx, qs]
                m = jnp.maximum(jnp.max(s_loc, axis=0, keepdims=True), jnp.max(s_ctx, axis=0, keepdims=True))
                col = [jnp.exp2(s_loc - m), jnp.exp2(s_ctx - m)]
                if a > 0:
                    col.insert(0, jnp.zeros((a * GRID_W, LANES), F32))
                if b < n_slots:
                    col.insert(-1, jnp.zeros(((n_slots - b) * GRID_W, LANES), F32))
                cols.append(jnp.concatenate(col, axis=0))
            p_all = jnp.concatenate(cols, axis=1).astype(BF16)
            v_own = jnp.where(own, v_all, jnp.ones_like(v_all))
            o = lax.dot_general(v_own, p_all, dims_tn, preferred_element_type=F32)
            den = o[(1 - hh) * HEAD_DIM:(1 - hh) * HEAD_DIM + 1]
            o_t.append(o[hh * HEAD_DIM:(hh + 1) * HEAD_DIM] * (1.0 / den))
        o_ref[:, ls] = jnp.concatenate(o_t, axis=0).T.astype(BF16)

    scores(0)
    for p in range(n_pairs):
        if p + 1 < n_pairs:
            scores(p + 1)
        softmax_pv(p)


def _attn_kernel(q_ref, kp_ref, km_ref, kn_ref, vp_ref, vm_ref, vn_ref, kc_ref, vc_ref, bias_ref, o_ref, s_buf,
                 *, plans):
    i = pl.program_id(0)
    n_i = pl.num_programs(0)
    variant = (i > 0).astype(jnp.int32) + (i == n_i - 1).astype(jnp.int32)
    for v, plan in enumerate(plans):
        @pl.when(variant == v)
        def _(plan=plan):
            _attn_block(q_ref, (kp_ref, km_ref, kn_ref), (vp_ref, vm_ref, vn_ref), kc_ref, vc_ref,
                        bias_ref, o_ref, s_buf, plan)


def _attn_call(proj, projc, bias, *, n_b, seq, ctx_len, q_col, k_col, v_col, kc_col, vc_col):
    d = N_HEADS * HEAD_DIM
    n_i = seq // ATT_Q
    rows = seq // GRID_W
    plans = tuple(_window_plan(i, n_i, rows) for i in (0, 1, n_i - 1))

    def row(i, b):
        return b * n_i + i

    def row_prev(i, b):
        return b * n_i + jnp.maximum(i - 1, 0)

    def row_next(i, b):
        return b * n_i + jnp.minimum(i + 1, n_i - 1)

    blk = (ATT_Q, d)
    return pl.pallas_call(
        functools.partial(_attn_kernel, plans=plans),
        grid=(n_i, n_b),
        in_specs=[
            pl.BlockSpec(blk, lambda i, b: (row(i, b), q_col)),
            pl.BlockSpec(blk, lambda i, b: (row_prev(i, b), k_col)),
            pl.BlockSpec(blk, lambda i, b: (row(i, b), k_col)),
            pl.BlockSpec(blk, lambda i, b: (row_next(i, b), k_col)),
            pl.BlockSpec(blk, lambda i, b: (row_prev(i, b), v_col)),
            pl.BlockSpec(blk, lambda i, b: (row(i, b), v_col)),
            pl.BlockSpec(blk, lambda i, b: (row_next(i, b), v_col)),
            pl.BlockSpec((ctx_len, d), lambda i, b: (b, kc_col)),
            pl.BlockSpec((ctx_len, d), lambda i, b: (b, vc_col)),
            pl.BlockSpec(bias.shape, lambda i, b: (0, 0, 0), pipeline_mode=pl.Buffered(1)),
        ],
        out_specs=pl.BlockSpec(blk, lambda i, b: (row(i, b), 0)),
        out_shape=jax.ShapeDtypeStruct((n_b * seq, d), BF16),
        scratch_shapes=[pltpu.VMEM((2, LANES // HEAD_DIM, ATT_SLOTS * GRID_W + ctx_len, ATT_Q), F32)],
        compiler_params=pltpu.CompilerParams(
            dimension_semantics=("arbitrary", "arbitrary"), vmem_limit_bytes=VMEM_LIMIT),
        name="attn",
    )(proj, proj, proj, proj, proj, proj, proj, projc, projc, bias)


def _attn_bias(rpb):
    qc = np.arange(GRID_W)[None, :]
    kc = np.arange(GRID_W)[:, None]
    col_start = np.clip(qc - WIN_COLS // 2, 0, GRID_W - WIN_COLS)
    col_ok = (kc >= col_start) & (kc < col_start + WIN_COLS)
    dc_idx = np.clip(kc - qc + WIN_COLS - 1, 0, 2 * WIN_COLS - 2)
    onehot = (dc_idx[None] == np.arange(2 * WIN_COLS - 1)[:, None, None]) & col_ok[None]
    col_tab = jnp.einsum('had,dkq->hakq', rpb.astype(F32), jnp.asarray(onehot, F32),
                         precision=lax.Precision.HIGHEST)
    col_tab = jnp.where(jnp.asarray(col_ok)[None, None], col_tab * LOG2E, NEG)
    slots = []
    for s in range(ATT_SLOTS):
        q_rows = [col_tab[:, s - ATT_ROWS - rq + WIN_ROWS - 1] for rq in range(ATT_ROWS)]
        slots.append(jnp.concatenate(q_rows, axis=-1))
    return jnp.concatenate(slots, axis=1)


def _out_kernel(hf_ref, hr_ref, ggr_ref, gl_ref, gna_ref, att_ref, x_ref, gt1_ref, sh2_ref, sc2_ref,
                gt2_ref, gn_ref, wl_ref, wn_ref, wo_ref, w1_ref, w2_ref, out_ref):
    n_b, ts, d = x_ref.shape
    group = n_b // OUT_GROUPS
    rows = group * ts
    groups = [slice(g * group, (g + 1) * group) for g in range(OUT_GROUPS)]

    y_na, y_lru = [], []
    for bs in groups:
        y_na.append(jnp.dot(att_ref[bs].reshape(rows, d), wn_ref[...], preferred_element_type=F32))
        hl = jnp.stack([
            jnp.concatenate([hf_ref[0, k, pl.ds(b, ts, stride=n_b), :] + hr_ref[0, k, pl.ds(b, ts, stride=n_b), :]
                             for k in range(d // LANES)], axis=1)
            for b in range(bs.start, bs.stop)], axis=0)
        hl = hl * ggr_ref[bs].astype(F32)
        y_lru.append(jnp.dot(hl.reshape(rows, d).astype(BF16), wl_ref[...], preferred_element_type=F32))

    ys = []
    for g, bs in enumerate(groups):
        y = (gl_ref[bs].astype(F32).reshape(rows, d) * y_lru[g]
             + gna_ref[bs].astype(F32).reshape(rows, d) * y_na[g])
        ys.append(jnp.dot(y.astype(BF16), wo_ref[...], preferred_element_type=F32).reshape(group, ts, d))

    x1s, u2s = [], []
    for g, bs in enumerate(groups):
        x1 = x_ref[bs] + gt1_ref[bs] * _rms(ys[g], gn_ref[1:2, :])
        u2 = _rms(x1, gn_ref[2:3, :]) * (1.0 + sc2_ref[bs]) + sh2_ref[bs]
        x1s.append(x1)
        u2s.append(u2.reshape(rows, d).astype(BF16))

    ms = [None] * OUT_GROUPS
    for f in range(0, w1_ref.shape[1], FF_CHUNK):
        h1s = []
        for g in range(OUT_GROUPS):
            h1 = jnp.dot(u2s[g], w1_ref[:, f:f + FF_CHUNK], preferred_element_type=F32)
            h1 = jnp.maximum(h1, 0.0)
            h1s.append((h1 * h1).astype(BF16))
        for g in range(OUT_GROUPS):
            part = jnp.dot(h1s[g], w2_ref[f:f + FF_CHUNK, :], preferred_element_type=F32)
            ms[g] = part if ms[g] is None else ms[g] + part

    for g, bs in enumerate(groups):
        out_ref[bs] = x1s[g] + gt2_ref[bs] * _rms(ms[g].reshape(group, ts, d), gn_ref[3:4, :])


def _out_call(h2, proj3, att3, x3, mod3, g_norm, wl, wn, wo, w1, w2, *, ts, ggr_col, gl_col, gna_col):
    n_b, seq, d = x3.shape

    def modspec(k):
        return pl.BlockSpec((n_b, 1, d), lambda t: (0, 0, k))

    def resident(w):
        return pl.BlockSpec(w.shape, lambda t: (0, 0), pipeline_mode=pl.Buffered(1))

    tile = (n_b, ts, d)
    return pl.pallas_call(
        _out_kernel,
        grid=(seq // ts,),
        in_specs=[
            pl.BlockSpec((1, d // LANES, ts * n_b, LANES), lambda t: (0, 0, t, 0)),
            pl.BlockSpec((1, d // LANES, ts * n_b, LANES), lambda t: (1, 0, t, 0)),
            pl.BlockSpec(tile, lambda t: (0, t, ggr_col)),
            pl.BlockSpec(tile, lambda t: (0, t, gl_col)),
            pl.BlockSpec(tile, lambda t: (0, t, gna_col)),
            pl.BlockSpec(tile, lambda t: (0, t, 0)),
            pl.BlockSpec(tile, lambda t: (0, t, 0)),
            modspec(2), modspec(3), modspec(4), modspec(5),
            pl.BlockSpec(g_norm.shape, lambda t: (0, 0)),
            resident(wl), resident(wn), resident(wo), resident(w1), resident(w2),
        ],
        out_specs=pl.BlockSpec(tile, lambda t: (0, t, 0)),
        out_shape=jax.ShapeDtypeStruct((n_b, seq, d), F32),
        compiler_params=pltpu.CompilerParams(
            dimension_semantics=("arbitrary",), vmem_limit_bytes=VMEM_LIMIT),
        name="out",
    )(h2, h2, proj3, proj3, proj3, att3, x3, mod3, mod3, mod3, mod3, g_norm, wl, wn, wo, w1, w2)


def _gate_weights(w_rg, b_rg):
    n_dir, n_gate, n_blk, bw, _ = w_rg.shape
    per = MXU_DIM // bw
    n_col = n_blk // per
    w_rg = 0.5 * w_rg
    b_rg = 0.5 * b_rg
    w = w_rg.reshape(n_dir, n_gate, n_col, per, bw, bw)
    eye = jnp.eye(per, dtype=w.dtype)
    wbd = jnp.einsum('dgcpij,pq->dgcpiqj', w, eye).reshape(n_dir, n_gate, n_col, MXU_DIM, MXU_DIM)
    wbd = jnp.concatenate([wbd[:, 0], wbd[:, 1]], axis=-1).astype(BF16)
    b = b_rg.reshape(n_dir, n_gate, n_col, MXU_DIM)
    brg = jnp.concatenate([b[:, 0], b[:, 1]], axis=-1)[:, :, None, :]
    return wbd, brg


def kernel(x, c, ctx, c_ctx, w_ada, b_ada, g_norm, w_in, b_gate, conv_w, conv_b, w_rg, b_rg, lam, rpb,
           w_lru_out, w_na_out, w_o, w_mlp1, w_mlp2):
    n_b, seq, d = x.shape
    ctx_len = ctx.shape[1]
    rows = seq // GRID_W
    depth = w_ada.shape[0]
    assert depth == 1 and rows % ATT_ROWS == 0 and n_b == SUBLANES

    l = 0
    cs = jnp.concatenate([c, c_ctx[None], jnp.zeros((2 * SUBLANES - n_b - 1, d), F32)], axis=0)
    mod = _mod_call(cs, w_ada[l], b_ada[l][None])
    mod3 = mod[:, None, :]

    w_in_bf = w_in[l].astype(BF16)
    b_gate2 = b_gate[l][None]
    g0 = g_norm[l, 0:1]

    xr, proj3 = _inproj_call(x, mod3, None, g0, w_in_bf, b_gate2, ts=INPROJ_TS, segs=(0, 1, 2, 3, 4, 5, 6))
    xr_c, projc3 = _inproj_call(ctx, mod3, n_b, g0, w_in_bf, b_gate2, ts=INPROJ_TS, segs=(0, 3, 4))

    wbd, brg = _gate_weights(w_rg[l], b_rg[l])
    lam3 = lam[l][:, None, :]
    cb = conv_b[l][None]
    _, h0 = _scan_call(xr_c, conv_w[l], cb, wbd, brg, lam3, jnp.zeros((2, d // LANES, n_b, LANES), F32),
                       tc=SCAN_TC)
    h, _ = _scan_call(xr, conv_w[l], cb, wbd, brg, lam3, h0, tc=SCAN_TC)

    bias = _attn_bias(rpb[l])
    att = _attn_call(proj3.reshape(n_b * seq, -1), projc3.reshape(n_b * ctx_len, -1), bias,
                     n_b=n_b, seq=seq, ctx_len=ctx_len, q_col=1, k_col=2, v_col=3, kc_col=0, vc_col=1)

    return _out_call(h, proj3, att.reshape(n_b, seq, d), x, mod3, g_norm[l],
                     w_lru_out[l].astype(BF16), w_na_out[l].astype(BF16), w_o[l].astype(BF16),
                     w_mlp1[l].astype(BF16), w_mlp2[l].astype(BF16),
                     ts=OUT_TS, ggr_col=0, gl_col=4, gna_col=5)
```

```python
import functools

import numpy as np
import jax
import jax.numpy as jnp
from jax import lax
from jax.experimental import pallas as pl
from jax.experimental.pallas import tpu as pltpu

F32 = jnp.float32
BF16 = jnp.bfloat16

EPS = 1e-6
NEG = -1e30
LRU_C = 8.0
N_HEADS = 16
HEAD_DIM = 64
GRID_W = 64
WIN_ROWS = 8
WIN_COLS = 16
CONV_W = 4
CONV_PAD_LEFT = 2

LANES = 128
SUBLANES = 8
MXU_DIM = 256
VMEM_LIMIT = 56 * 1024 * 1024

INPROJ_TS = 64
SCAN_TC = 128
OUT_TS = 64

ATT_ROWS = 4
ATT_Q = ATT_ROWS * GRID_W
ATT_SLOTS = 3 * ATT_ROWS
INPROJ_GROUPS = 2
OUT_GROUPS = 2
FF_CHUNK = 1024
LOG2E = 1.4426950408889634


def _rms(x, g):
    return x * lax.rsqrt(jnp.mean(x * x, axis=-1, keepdims=True) + EPS) * g


def _mod_kernel(c_ref, w_ref, b_ref, o_ref):
    c = c_ref[...]
    s = c * jax.nn.sigmoid(c)
    o_ref[...] = jnp.dot(s, w_ref[...], preferred_element_type=F32) + b_ref[...]


def _mod_call(cs, w_ada, b_ada):
    rows, d = cs.shape
    n_out = w_ada.shape[1]
    return pl.pallas_call(
        _mod_kernel,
        grid=(n_out // d,),
        in_specs=[
            pl.BlockSpec((rows, d), lambda n: (0, 0)),
            pl.BlockSpec((d, d), lambda n: (0, n)),
            pl.BlockSpec((1, d), lambda n: (0, n)),
        ],
        out_specs=pl.BlockSpec((rows, d), lambda n: (0, n)),
        out_shape=jax.ShapeDtypeStruct((rows, n_out), F32),
        compiler_params=pltpu.CompilerParams(
            dimension_semantics=("arbitrary",), vmem_limit_bytes=VMEM_LIMIT),
        name="mod",
    )(cs, w_ada, b_ada)


def _inproj_kernel(x_ref, xp_ref, xn_ref, sh_ref, sc_ref, g_ref, w_ref, bg_ref, cw_ref, cb_ref, xc_ref, pr_ref,
                   xr_s, *, segs):
    n_b, ts, d = x_ref.shape
    halo = xp_ref.shape[1]
    group = n_b // INPROJ_GROUPS
    groups = [slice(g * group, (g + 1) * group) for g in range(INPROJ_GROUPS)]
    t_idx = pl.program_id(0)
    has_prev = (t_idx > 0).astype(F32)
    has_next = (t_idx < pl.num_programs(0) - 1).astype(F32)
    hrow = lax.broadcasted_iota(jnp.int32, (1, halo, 1), 1)
    from_prev = hrow >= halo - CONV_PAD_LEFT

    def per_sample(ref, bs):
        return ref[bs] if ref.shape[0] == n_b else ref[...]

    def modulated(xv, bs):
        return _rms(xv, g_ref[...]) * (1.0 + per_sample(sc_ref, bs)) + per_sample(sh_ref, bs)

    us = []
    for bs in groups:
        u = modulated(x_ref[bs], bs).reshape(group * ts, d)
        u_halo = modulated(jnp.where(from_prev, xp_ref[bs], xn_ref[bs]), bs).reshape(group * halo, d)
        us.append(jnp.concatenate([u, u_halo], axis=0).astype(BF16))
    for idx, seg in enumerate(segs):
        for bs, u in zip(groups, us):
            if seg == 0:
                acc = jnp.dot(u, w_ref[:, :d], preferred_element_type=F32)
                hal = acc[group * ts:].reshape(group, halo, d)
                hal = hal * jnp.where(from_prev, has_prev, has_next)
                for b in range(group):
                    row = bs.start + b
                    for k in range(d // LANES):
                        ks = slice(k * LANES, (k + 1) * LANES)
                        xr_s[k, pl.ds(CONV_PAD_LEFT * n_b + row, ts, stride=n_b), :] = acc[b * ts:(b + 1) * ts, ks]
                        xr_s[k, row:row + 1, :] = hal[b, halo - 2:halo - 1, ks]
                        xr_s[k, n_b + row:n_b + row + 1, :] = hal[b, halo - 1:halo, ks]
                        xr_s[k, (CONV_PAD_LEFT + ts) * n_b + row:(CONV_PAD_LEFT + ts) * n_b + row + 1, :] = (
                            hal[b, 0:1, ks])
                if bs is groups[-1]:
                    for k in range(d // LANES):
                        ks = slice(k * LANES, (k + 1) * LANES)
                        xe = xr_s[k]
                        xc = cb_ref[:, ks]
                        for j in range(CONV_W):
                            xc = xc + cw_ref[j:j + 1, ks] * xe[j * n_b:(j + ts) * n_b]
                        xc_ref[k] = xc
                continue
            acc = jnp.dot(u[:group * ts], w_ref[:, seg * d:(seg + 1) * d], preferred_element_type=F32)
            if seg == 1:
                val = jax.nn.gelu(acc)
            elif seg == 2:
                val = acc * (HEAD_DIM ** -0.5 * LOG2E)
            elif seg in (3, 4):
                val = acc
            else:
                val = jax.nn.sigmoid(acc + bg_ref[:, (seg - 5) * d:(seg - 4) * d])
            pr_ref[bs, :, (idx - 1) * d:idx * d] = val.astype(BF16).reshape(group, ts, d)


def _inproj_call(x3, mod3, mod_row, g0, w_in_bf, b_gate2, conv_w, conv_b, *, ts, segs):
    n_b, seq, d = x3.shape
    assert CONV_W == 4 and CONV_PAD_LEFT == 2 and ts % SUBLANES == 0
    halo = SUBLANES
    n_halo = seq // halo
    segs = tuple(segs)
    n_out = (len(segs) - 1) * d
    if mod_row is None:
        mod_blk, mod_idx = (n_b, 1, d), 0
    else:
        mod_blk, mod_idx = (1, 1, d), mod_row
    kern = functools.partial(_inproj_kernel, segs=segs)
    return pl.pallas_call(
        kern,
        grid=(seq // ts,),
        in_specs=[
            pl.BlockSpec((n_b, ts, d), lambda t: (0, t, 0)),
            pl.BlockSpec((n_b, halo, d), lambda t: (0, jnp.maximum(t * (ts // halo) - 1, 0), 0)),
            pl.BlockSpec((n_b, halo, d), lambda t: (0, jnp.minimum((t + 1) * (ts // halo), n_halo - 1), 0)),
            pl.BlockSpec(mod_blk, lambda t: (mod_idx, 0, 0)),
            pl.BlockSpec(mod_blk, lambda t: (mod_idx, 0, 1)),
            pl.BlockSpec((1, d), lambda t: (0, 0)),
            pl.BlockSpec(w_in_bf.shape, lambda t: (0, 0), pipeline_mode=pl.Buffered(1)),
            pl.BlockSpec(b_gate2.shape, lambda t: (0, 0)),
            pl.BlockSpec(conv_w.shape, lambda t: (0, 0)),
            pl.BlockSpec(conv_b.shape, lambda t: (0, 0)),
        ],
        out_specs=[
            pl.BlockSpec((d // LANES, ts * n_b, LANES), lambda t: (0, t, 0)),
            pl.BlockSpec((n_b, ts, n_out), lambda t: (0, t, 0)),
        ],
        out_shape=[
            jax.ShapeDtypeStruct((d // LANES, seq * n_b, LANES), F32),
            jax.ShapeDtypeStruct((n_b, seq, n_out), BF16),
        ],
        scratch_shapes=[pltpu.VMEM((d // LANES, (ts + CONV_W - 1) * n_b, LANES), F32)],
        compiler_params=pltpu.CompilerParams(
            dimension_semantics=("arbitrary",), vmem_limit_bytes=VMEM_LIMIT),
        name="inproj",
    )(x3, x3, x3, mod3, mod3, g0, w_in_bf, b_gate2, conv_w, conv_b)


def _scan_kernel(xc_ref, wbd_ref, brg_ref, lam_ref, h0_ref, h_ref, hfin_ref, a_s, b_s, hst, *, tc):
    d = pl.program_id(0)
    i = pl.program_id(1)

    @pl.when(i == 0)
    def _():
        hst[...] = h0_ref[d]

    n_lc = a_s.shape[0]
    rows = tc * SUBLANES
    per = MXU_DIM // LANES
    for c in range(n_lc // per):
        xc2 = jnp.concatenate([xc_ref[k] for k in range(c * per, (c + 1) * per)], axis=1)
        sl = slice(c * MXU_DIM, (c + 1) * MXU_DIM)
        g = jnp.dot(xc2.astype(BF16), wbd_ref[0, c], preferred_element_type=F32) + brg_ref[0, c]
        r2 = jnp.tanh(g[:, :MXU_DIM]) + 1.0
        i2 = jnp.tanh(g[:, MXU_DIM:]) + 1.0
        z = -lam_ref[0, :, sl]
        softplus = jnp.maximum(z, 0.0) + jnp.log1p(jnp.exp(-jnp.abs(z)))
        c_nat = (0.5 * LRU_C) * softplus
        neg_log_a = c_nat * r2
        a = jnp.exp2((-LOG2E * c_nat) * r2)
        one_minus_a2 = jnp.tanh(neg_log_a) * (a * a + 1.0)
        root = jnp.where(one_minus_a2 > 0.0, one_minus_a2 * lax.rsqrt(one_minus_a2), 0.0)
        bb = root * (i2 * xc2)
        for kk in range(per):
            a_s[c * per + kk, :rows] = a[:, kk * LANES:(kk + 1) * LANES]
            b_s[c * per + kk, :rows] = bb[:, kk * LANES:(kk + 1) * LANES]

    def step(j, h):
        t = jnp.where(d == 0, j, tc - 1 - j)
        r0 = pl.multiple_of(t * SUBLANES, SUBLANES)
        h = a_s[:, pl.ds(r0, SUBLANES), :] * h + b_s[:, pl.ds(r0, SUBLANES), :]
        h_ref[0, :, pl.ds(r0, SUBLANES), :] = h
        return h

    h_last = lax.fori_loop(0, tc, step, hst[...], unroll=8)
    hst[...] = h_last
    hfin_ref[0] = h_last


def _scan_call(xc, wbd, brg, lam3, h0, *, tc):
    n_lc, n_rows, _ = xc.shape
    n_b = SUBLANES
    seq = n_rows // n_b
    d = n_lc * LANES
    n_chunks = seq // tc
    n_col = d // MXU_DIM
    rows = tc * n_b

    def chunk(dd, i):
        return jnp.where(dd == 0, i, n_chunks - 1 - i)

    kern = functools.partial(_scan_kernel, tc=tc)
    return pl.pallas_call(
        kern,
        grid=(2, n_chunks),
        in_specs=[
            pl.BlockSpec((n_lc, rows, LANES), lambda dd, i: (0, chunk(dd, i), 0)),
            pl.BlockSpec((1, n_col, MXU_DIM, 2 * MXU_DIM), lambda dd, i: (dd, 0, 0, 0)),
            pl.BlockSpec((1, n_col, 1, 2 * MXU_DIM), lambda dd, i: (dd, 0, 0, 0)),
            pl.BlockSpec((1, 1, d), lambda dd, i: (dd, 0, 0)),
            pl.BlockSpec((2, n_lc, n_b, LANES), lambda dd, i: (0, 0, 0, 0)),
        ],
        out_specs=[
            pl.BlockSpec((1, n_lc, rows, LANES), lambda dd, i: (dd, 0, chunk(dd, i), 0)),
            pl.BlockSpec((1, n_lc, n_b, LANES), lambda dd, i: (dd, 0, 0, 0)),
        ],
        out_shape=[
            jax.ShapeDtypeStruct((2, n_lc, n_rows, LANES), F32),
            jax.ShapeDtypeStruct((2, n_lc, n_b, LANES), F32),
        ],
        scratch_shapes=[
            pltpu.VMEM((n_lc, rows + n_b, LANES), F32),
            pltpu.VMEM((n_lc, rows + n_b, LANES), F32),
            pltpu.VMEM((n_lc, n_b, LANES), F32),
        ],
        compiler_params=pltpu.CompilerParams(
            dimension_semantics=("arbitrary", "arbitrary"), vmem_limit_bytes=VMEM_LIMIT),
        name="scan",
    )(xc, wbd, brg, lam3, h0)


def _window_plan(i, n_i, rows):
    wr = min(WIN_ROWS, rows)
    rng = []
    for rq in range(ATT_ROWS):
        r = ATT_ROWS * i + rq
        r0 = min(max(r - wr // 2, 0), rows - wr)
        lo = r0 - (ATT_ROWS * i - ATT_ROWS)
        rng.append((lo, lo + wr))
    s_lo = min(lo for lo, _ in rng)
    s_hi = max(hi for _, hi in rng)
    assert 0 <= s_lo and s_hi <= ATT_SLOTS
    rows_per_tile = LANES // GRID_W
    tiles = []
    for j in range(ATT_ROWS // rows_per_tile):
        rqs = range(j * rows_per_tile, (j + 1) * rows_per_tile)
        a = min(rng[rq][0] for rq in rqs)
        b = max(rng[rq][1] for rq in rqs)
        half = {}
        for s in range(a, b):
            ok = tuple(rng[rq][0] <= s < rng[rq][1] for rq in rqs)
            if not all(ok):
                half[s - s_lo] = ok
        tiles.append((a - s_lo, b - s_lo, half))
    return s_lo, s_hi, tiles


def _attn_block(q_ref, k_refs, v_refs, kc_ref, vc_ref, bias_ref, o_ref, s_buf, plan):
    s_lo, s_hi, tiles = plan
    n_slots = s_hi - s_lo
    n_loc = n_slots * GRID_W
    n_ctx = kc_ref.shape[0]
    lane = lax.broadcasted_iota(jnp.int32, (1, LANES), 1)
    lo = lane < HEAD_DIM
    heads_per_pair = LANES // HEAD_DIM
    n_pairs = N_HEADS // heads_per_pair
    dims_nt = (((1,), (1,)), ((), ()))
    dims_tn = (((0,), (0,)), ((), ()))

    def keys_of(refs, c_ref, ls):
        parts = []
        for blk, ref in enumerate(refs):
            a = max(s_lo, blk * ATT_ROWS) - blk * ATT_ROWS
            b = min(s_hi, (blk + 1) * ATT_ROWS) - blk * ATT_ROWS
            if b > a:
                parts.append(ref[a * GRID_W:b * GRID_W, ls])
        return jnp.concatenate(parts + [c_ref[:, ls]], axis=0)

    def scores(p):
        ls = slice(p * LANES, (p + 1) * LANES)
        qp = q_ref[:, ls]
        k_all = keys_of(k_refs, kc_ref, ls)
        for hh in range(heads_per_pair):
            msk = lo if hh == 0 else jnp.logical_not(lo)
            qm = jnp.where(msk, qp, jnp.zeros_like(qp))
            s_buf[p % 2, hh, :n_loc + n_ctx] = lax.dot_general(k_all, qm, dims_nt, preferred_element_type=F32)

    def softmax_pv(p):
        ls = slice(p * LANES, (p + 1) * LANES)
        v_all = keys_of(v_refs, vc_ref, ls)
        o_t = []
        for hh in range(heads_per_pair):
            own = lo if hh == 0 else jnp.logical_not(lo)
            head = p * heads_per_pair + hh
            cols = []
            for j, (a, b, half) in enumerate(tiles):
                qs = slice(j * LANES, (j + 1) * LANES)
                blocks = []
                for s in range(a, b):
                    blk = (s_buf[p % 2, hh, s * GRID_W:(s + 1) * GRID_W, qs]
                           + bias_ref[head, (s_lo + s) * GRID_W:(s_lo + s + 1) * GRID_W, qs])
                    if s in half:
                        blk = jnp.where(lo if half[s][0] else jnp.logical_not(lo), blk, NEG)
                    blocks.append(blk)
                s_loc = jnp.concatenate(blocks, axis=0)
                s_ctx = s_buf[p % 2, hh, n_loc:n_loc + n_ctx, qs]
                m = jnp.maximum(jnp.max(s_loc, axis=0, keepdims=True), jnp.max(s_ctx, axis=0, keepdims=True))
                col = [jnp.exp2(s_loc - m), jnp.exp2(s_ctx - m)]
                if a > 0:
                    col.insert(0, jnp.zeros((a * GRID_W, LANES), F32))
                if b < n_slots:
                    col.insert(-1, jnp.zeros(((n_slots - b) * GRID_W, LANES), F32))
                cols.append(jnp.concatenate(col, axis=0))
            p_all = jnp.concatenate(cols, axis=1).astype(BF16)
            v_own = jnp.where(own, v_all, jnp.ones_like(v_all))
            o = lax.dot_general(v_own, p_all, dims_tn, preferred_element_type=F32)
            den = o[(1 - hh) * HEAD_DIM:(1 - hh) * HEAD_DIM + 1]
            o_t.append(o[hh * HEAD_DIM:(hh + 1) * HEAD_DIM] * (1.0 / den))
        o_ref[:, ls] = jnp.concatenate(o_t, axis=0).T.astype(BF16)

    scores(0)
    for p in range(n_pairs):
        if p + 1 < n_pairs:
            scores(p + 1)
        softmax_pv(p)


def _attn_kernel(q_ref, kp_ref, km_ref, kn_ref, vp_ref, vm_ref, vn_ref, kc_ref, vc_ref, bias_ref, o_ref, s_buf,
                 *, plans):
    i = pl.program_id(0)
    n_i = pl.num_programs(0)
    variant = (i > 0).astype(jnp.int32) + (i == n_i - 1).astype(jnp.int32)
    for v, plan in enumerate(plans):
        @pl.when(variant == v)
        def _(plan=plan):
            _attn_block(q_ref, (kp_ref, km_ref, kn_ref), (vp_ref, vm_ref, vn_ref), kc_ref, vc_ref,
                        bias_ref, o_ref, s_buf, plan)


def _attn_call(proj, projc, bias, *, n_b, seq, ctx_len, q_col, k_col, v_col, kc_col, vc_col):
    d = N_HEADS * HEAD_DIM
    n_i = seq // ATT_Q
    rows = seq // GRID_W
    plans = tuple(_window_plan(i, n_i, rows) for i in (0, 1, n_i - 1))

    def row(i, b):
        return b * n_i + i

    def row_prev(i, b):
        return b * n_i + jnp.maximum(i - 1, 0)

    def row_next(i, b):
        return b * n_i + jnp.minimum(i + 1, n_i - 1)

    blk = (ATT_Q, d)
    return pl.pallas_call(
        functools.partial(_attn_kernel, plans=plans),
        grid=(n_i, n_b),
        in_specs=[
            pl.BlockSpec(blk, lambda i, b: (row(i, b), q_col)),
            pl.BlockSpec(blk, lambda i, b: (row_prev(i, b), k_col)),
            pl.BlockSpec(blk, lambda i, b: (row(i, b), k_col)),
            pl.BlockSpec(blk, lambda i, b: (row_next(i, b), k_col)),
            pl.BlockSpec(blk, lambda i, b: (row_prev(i, b), v_col)),
            pl.BlockSpec(blk, lambda i, b: (row(i, b), v_col)),
            pl.BlockSpec(blk, lambda i, b: (row_next(i, b), v_col)),
            pl.BlockSpec((ctx_len, d), lambda i, b: (b, kc_col)),
            pl.BlockSpec((ctx_len, d), lambda i, b: (b, vc_col)),
            pl.BlockSpec(bias.shape, lambda i, b: (0, 0, 0), pipeline_mode=pl.Buffered(1)),
        ],
        out_specs=pl.BlockSpec(blk, lambda i, b: (row(i, b), 0)),
        out_shape=jax.ShapeDtypeStruct((n_b * seq, d), BF16),
        scratch_shapes=[pltpu.VMEM((2, LANES // HEAD_DIM, ATT_SLOTS * GRID_W + ctx_len, ATT_Q), F32)],
        compiler_params=pltpu.CompilerParams(
            dimension_semantics=("arbitrary", "arbitrary"), vmem_limit_bytes=VMEM_LIMIT),
        name="attn",
    )(proj, proj, proj, proj, proj, proj, proj, projc, projc, bias)


def _attn_bias(rpb):
    qc = np.arange(GRID_W)[None, :]
    kc = np.arange(GRID_W)[:, None]
    col_start = np.clip(qc - WIN_COLS // 2, 0, GRID_W - WIN_COLS)
    col_ok = (kc >= col_start) & (kc < col_start + WIN_COLS)
    dc_idx = np.clip(kc - qc + WIN_COLS - 1, 0, 2 * WIN_COLS - 2)
    onehot = (dc_idx[None] == np.arange(2 * WIN_COLS - 1)[:, None, None]) & col_ok[None]
    col_tab = jnp.einsum('had,dkq->hakq', rpb.astype(F32), jnp.asarray(onehot, F32),
                         precision=lax.Precision.HIGHEST)
    col_tab = jnp.where(jnp.asarray(col_ok)[None, None], col_tab * LOG2E, NEG)
    slots = []
    for s in range(ATT_SLOTS):
        q_rows = [col_tab[:, s - ATT_ROWS - rq + WIN_ROWS - 1] for rq in range(ATT_ROWS)]
        slots.append(jnp.concatenate(q_rows, axis=-1))
    return jnp.concatenate(slots, axis=1)


def _out_kernel(hf_ref, hr_ref, ggr_ref, gl_ref, gna_ref, att_ref, x_ref, gt1_ref, sh2_ref, sc2_ref,
                gt2_ref, gn_ref, wl_ref, wn_ref, wo_ref, w1_ref, w2_ref, out_ref):
    n_b, ts, d = x_ref.shape
    group = n_b // OUT_GROUPS
    rows = group * ts
    groups = [slice(g * group, (g + 1) * group) for g in range(OUT_GROUPS)]

    y_na, y_lru = [], []
    for bs in groups:
        y_na.append(jnp.dot(att_ref[bs].reshape(rows, d), wn_ref[...], preferred_element_type=F32))
        hl = jnp.stack([
            jnp.concatenate([hf_ref[0, k, pl.ds(b, ts, stride=n_b), :] + hr_ref[0, k, pl.ds(b, ts, stride=n_b), :]
                             for k in range(d // LANES)], axis=1)
            for b in range(bs.start, bs.stop)], axis=0)
        hl = hl * ggr_ref[bs].astype(F32)
        y_lru.append(jnp.dot(hl.reshape(rows, d).astype(BF16), wl_ref[...], preferred_element_type=F32))

    ys = []
    for g, bs in enumerate(groups):
        y = (gl_ref[bs].astype(F32).reshape(rows, d) * y_lru[g]
             + gna_ref[bs].astype(F32).reshape(rows, d) * y_na[g])
        ys.append(jnp.dot(y.astype(BF16), wo_ref[...], preferred_element_type=F32).reshape(group, ts, d))

    x1s, u2s = [], []
    for g, bs in enumerate(groups):
        x1 = x_ref[bs] + gt1_ref[bs] * _rms(ys[g], gn_ref[1:2, :])
        u2 = _rms(x1, gn_ref[2:3, :]) * (1.0 + sc2_ref[bs]) + sh2_ref[bs]
        x1s.append(x1)
        u2s.append(u2.reshape(rows, d).astype(BF16))

    ms = [None] * OUT_GROUPS
    for f in range(0, w1_ref.shape[1], FF_CHUNK):
        h1s = []
        for g in range(OUT_GROUPS):
            h1 = jnp.dot(u2s[g], w1_ref[:, f:f + FF_CHUNK], preferred_element_type=F32)
            h1 = jnp.maximum(h1, 0.0)
            h1s.append((h1 * h1).astype(BF16))
        for g in range(OUT_GROUPS):
            part = jnp.dot(h1s[g], w2_ref[f:f + FF_CHUNK, :], preferred_element_type=F32)
            ms[g] = part if ms[g] is None else ms[g] + part

    for g, bs in enumerate(groups):
        out_ref[bs] = x1s[g] + gt2_ref[bs] * _rms(ms[g].reshape(group, ts, d), gn_ref[3:4, :])


def _out_call(h2, proj3, att3, x3, mod3, g_norm, wl, wn, wo, w1, w2, *, ts, ggr_col, gl_col, gna_col):
    n_b, seq, d = x3.shape

    def modspec(k):
        return pl.BlockSpec((n_b, 1, d), lambda t: (0, 0, k))

    def resident(w):
        return pl.BlockSpec(w.shape, lambda t: (0, 0), pipeline_mode=pl.Buffered(1))

    tile = (n_b, ts, d)
    return pl.pallas_call(
        _out_kernel,
        grid=(seq // ts,),
        in_specs=[
            pl.BlockSpec((1, d // LANES, ts * n_b, LANES), lambda t: (0, 0, t, 0)),
            pl.BlockSpec((1, d // LANES, ts * n_b, LANES), lambda t: (1, 0, t, 0)),
            pl.BlockSpec(tile, lambda t: (0, t, ggr_col)),
            pl.BlockSpec(tile, lambda t: (0, t, gl_col)),
            pl.BlockSpec(tile, lambda t: (0, t, gna_col)),
            pl.BlockSpec(tile, lambda t: (0, t, 0)),
            pl.BlockSpec(tile, lambda t: (0, t, 0)),
            modspec(2), modspec(3), modspec(4), modspec(5),
            pl.BlockSpec(g_norm.shape, lambda t: (0, 0)),
            resident(wl), resident(wn), resident(wo), resident(w1), resident(w2),
        ],
        out_specs=pl.BlockSpec(tile, lambda t: (0, t, 0)),
        out_shape=jax.ShapeDtypeStruct((n_b, seq, d), F32),
        compiler_params=pltpu.CompilerParams(
            dimension_semantics=("arbitrary",), vmem_limit_bytes=VMEM_LIMIT),
        name="out",
    )(h2, h2, proj3, proj3, proj3, att3, x3, mod3, mod3, mod3, mod3, g_norm, wl, wn, wo, w1, w2)


def _gate_weights(w_rg, b_rg):
    n_dir, n_gate, n_blk, bw, _ = w_rg.shape
    per = MXU_DIM // bw
    n_col = n_blk // per
    b_rg = 0.5 * b_rg
    w = w_rg.reshape(n_dir, n_gate, n_col, per, bw, bw)
    eye = jnp.eye(per, dtype=w.dtype)
    wbd = jnp.einsum('dgcpij,pq->dgcpiqj', w, eye).reshape(n_dir, n_gate, n_col, MXU_DIM, MXU_DIM)
    wbd = jnp.concatenate([wbd[:, 0], wbd[:, 1]], axis=-1).astype(BF16)
    b = b_rg.reshape(n_dir, n_gate, n_col, MXU_DIM)
    brg = jnp.concatenate([b[:, 0], b[:, 1]], axis=-1)[:, :, None, :]
    return wbd, brg


def kernel(x, c, ctx, c_ctx, w_ada, b_ada, g_norm, w_in, b_gate, conv_w, conv_b, w_rg, b_rg, lam, rpb,
           w_lru_out, w_na_out, w_o, w_mlp1, w_mlp2):
    n_b, seq, d = x.shape
    ctx_len = ctx.shape[1]
    rows = seq // GRID_W
    depth = w_ada.shape[0]
    assert depth == 1 and rows % ATT_ROWS == 0 and n_b == SUBLANES

    l = 0
    cs = jnp.concatenate([c, c_ctx[None], jnp.zeros((2 * SUBLANES - n_b - 1, d), F32)], axis=0)
    mod = _mod_call(cs, w_ada[l], b_ada[l][None])
    mod3 = mod[:, None, :]

    w_in_bf = w_in[l].astype(BF16)
    b_gate2 = b_gate[l][None]
    g0 = g_norm[l, 0:1]

    cw, cb = 0.5 * conv_w[l], 0.5 * conv_b[l][None]
    xc, proj3 = _inproj_call(x, mod3, None, g0, w_in_bf, b_gate2, cw, cb, ts=INPROJ_TS,
                             segs=(0, 1, 2, 3, 4, 5, 6))
    xc_c, projc3 = _inproj_call(ctx, mod3, n_b, g0, w_in_bf, b_gate2, cw, cb, ts=INPROJ_TS, segs=(0, 3, 4))

    wbd, brg = _gate_weights(w_rg[l], b_rg[l])
    lam3 = lam[l][:, None, :]
    _, h0 = _scan_call(xc_c, wbd, brg, lam3, jnp.zeros((2, d // LANES, n_b, LANES), F32), tc=SCAN_TC)
    h, _ = _scan_call(xc, wbd, brg, lam3, h0, tc=SCAN_TC)

    bias = _attn_bias(rpb[l])
    att = _attn_call(proj3.reshape(n_b * seq, -1), projc3.reshape(n_b * ctx_len, -1), bias,
                     n_b=n_b, seq=seq, ctx_len=ctx_len, q_col=1, k_col=2, v_col=3, kc_col=0, vc_col=1)

    return _out_call(h, proj3, att.reshape(n_b, seq, d), x, mod3, g_norm[l],
                     w_lru_out[l].astype(BF16), w_na_out[l].astype(BF16), w_o[l].astype(BF16),
                     w_mlp1[l].astype(BF16), w_mlp2[l].astype(BF16),
                     ts=OUT_TS, ggr_col=0, gl_col=4, gna_col=5)
```

```python
import functools

import numpy as np
import jax
import jax.numpy as jnp
from jax import lax
from jax.experimental import pallas as pl
from jax.experimental.pallas import tpu as pltpu

F32 = jnp.float32
BF16 = jnp.bfloat16

EPS = 1e-6
NEG = -1e30
LRU_C = 8.0
N_HEADS = 16
HEAD_DIM = 64
GRID_W = 64
WIN_ROWS = 8
WIN_COLS = 16
CONV_W = 4
CONV_PAD_LEFT = 2

LANES = 128
SUBLANES = 8
MXU_DIM = 256
VMEM_LIMIT = 56 * 1024 * 1024

INPROJ_TS = 64
SCAN_TC = 128
OUT_TS = 64

ATT_ROWS = 4
ATT_Q = ATT_ROWS * GRID_W
ATT_SLOTS = 3 * ATT_ROWS
INPROJ_GROUPS = 2
OUT_GROUPS = 2
FF_CHUNK = 1024
LOG2E = 1.4426950408889634


def _rms(x, g):
    return x * lax.rsqrt(jnp.mean(x * x, axis=-1, keepdims=True) + EPS) * g


def _mod_kernel(c_ref, w_ref, b_ref, o_ref):
    c = c_ref[...]
    s = c * jax.nn.sigmoid(c)
    o_ref[...] = jnp.dot(s, w_ref[...], preferred_element_type=F32) + b_ref[...]


def _mod_call(cs, w_ada, b_ada):
    rows, d = cs.shape
    n_out = w_ada.shape[1]
    return pl.pallas_call(
        _mod_kernel,
        grid=(n_out // d,),
        in_specs=[
            pl.BlockSpec((rows, d), lambda n: (0, 0)),
            pl.BlockSpec((d, d), lambda n: (0, n)),
            pl.BlockSpec((1, d), lambda n: (0, n)),
        ],
        out_specs=pl.BlockSpec((rows, d), lambda n: (0, n)),
        out_shape=jax.ShapeDtypeStruct((rows, n_out), F32),
        compiler_params=pltpu.CompilerParams(
            dimension_semantics=("arbitrary",), vmem_limit_bytes=VMEM_LIMIT),
        name="mod",
    )(cs, w_ada, b_ada)


def _inproj_kernel(x_ref, sh_ref, sc_ref, g_ref, w_ref, bg_ref, cw_ref, cb_ref, z_ref, ztail_ref, pr_ref,
                   xr_s, *, segs):
    n_b, ts, d = x_ref.shape
    hist = (CONV_W - 1) * n_b
    group = n_b // INPROJ_GROUPS
    groups = [slice(g * group, (g + 1) * group) for g in range(INPROJ_GROUPS)]

    @pl.when(pl.program_id(0) == 0)
    def _():
        xr_s[:, :hist, :] = jnp.zeros((xr_s.shape[0], hist, LANES), F32)

    def per_sample(ref, bs):
        return ref[bs] if ref.shape[0] == n_b else ref[...]

    us = []
    for bs in groups:
        u = _rms(x_ref[bs], g_ref[...]) * (1.0 + per_sample(sc_ref, bs)) + per_sample(sh_ref, bs)
        us.append(u.reshape(group * ts, d).astype(BF16))
    def conv():
        for k in range(d // LANES):
            ks = slice(k * LANES, (k + 1) * LANES)
            xe = xr_s[k]
            z = cb_ref[:, ks]
            tail = cb_ref[:, ks]
            for j in range(CONV_W):
                z = z + cw_ref[j:j + 1, ks] * xe[j * n_b:(j + ts) * n_b]
                if j < CONV_W - 1:
                    tail = tail + cw_ref[j:j + 1, ks] * xe[(ts + j) * n_b:(ts + j + 1) * n_b]
            z_ref[k] = z
            ztail_ref[k] = tail
            xr_s[k, :hist, :] = xe[ts * n_b:]

    conv_after = min(len(segs) - 1, 3)
    for idx, seg in enumerate(segs):
        for bs, u in zip(groups, us):
            acc = jnp.dot(u, w_ref[:, seg * d:(seg + 1) * d], preferred_element_type=F32)
            if seg == 0:
                for b in range(group):
                    for k in range(d // LANES):
                        xr_s[k, pl.ds(hist + bs.start + b, ts, stride=n_b), :] = (
                            acc[b * ts:(b + 1) * ts, k * LANES:(k + 1) * LANES])
                continue
            if seg == 1:
                val = jax.nn.gelu(acc)
            elif seg == 2:
                val = acc * (HEAD_DIM ** -0.5 * LOG2E)
            elif seg in (3, 4):
                val = acc
            else:
                val = jax.nn.sigmoid(acc + bg_ref[:, (seg - 5) * d:(seg - 4) * d])
            pr_ref[bs, :, (idx - 1) * d:idx * d] = val.astype(BF16).reshape(group, ts, d)
        if idx == conv_after:
            conv()


def _inproj_call(x3, mod3, mod_row, g0, w_in_bf, b_gate2, conv_w, conv_b, *, ts, segs):
    n_b, seq, d = x3.shape
    assert CONV_W == 4 and CONV_PAD_LEFT == 2
    segs = tuple(segs)
    n_out = (len(segs) - 1) * d
    if mod_row is None:
        mod_blk, mod_idx = (n_b, 1, d), 0
    else:
        mod_blk, mod_idx = (1, 1, d), mod_row
    kern = functools.partial(_inproj_kernel, segs=segs)
    return pl.pallas_call(
        kern,
        grid=(seq // ts,),
        in_specs=[
            pl.BlockSpec((n_b, ts, d), lambda t: (0, t, 0)),
            pl.BlockSpec(mod_blk, lambda t: (mod_idx, 0, 0)),
            pl.BlockSpec(mod_blk, lambda t: (mod_idx, 0, 1)),
            pl.BlockSpec((1, d), lambda t: (0, 0)),
            pl.BlockSpec(w_in_bf.shape, lambda t: (0, 0), pipeline_mode=pl.Buffered(1)),
            pl.BlockSpec(b_gate2.shape, lambda t: (0, 0)),
            pl.BlockSpec(conv_w.shape, lambda t: (0, 0)),
            pl.BlockSpec(conv_b.shape, lambda t: (0, 0)),
        ],
        out_specs=[
            pl.BlockSpec((d // LANES, ts * n_b, LANES), lambda t: (0, t, 0)),
            pl.BlockSpec((d // LANES, n_b, LANES), lambda t: (0, 0, 0)),
            pl.BlockSpec((n_b, ts, n_out), lambda t: (0, t, 0)),
        ],
        out_shape=[
            jax.ShapeDtypeStruct((d // LANES, seq * n_b, LANES), F32),
            jax.ShapeDtypeStruct((d // LANES, n_b, LANES), F32),
            jax.ShapeDtypeStruct((n_b, seq, n_out), BF16),
        ],
        scratch_shapes=[pltpu.VMEM((d // LANES, (ts + CONV_W - 1) * n_b, LANES), F32)],
        compiler_params=pltpu.CompilerParams(
            dimension_semantics=("arbitrary",), vmem_limit_bytes=VMEM_LIMIT),
        name="inproj",
    )(x3, mod3, mod3, g0, w_in_bf, b_gate2, conv_w, conv_b)


def _scan_kernel(z_ref, znext_ref, ztail_ref, wbd_ref, brg_ref, lam_ref, h0_ref, h_ref, hfin_ref, a_s, b_s, hst,
                 *, n_chunks, tc):
    d = pl.program_id(0)
    i = pl.program_id(1)
    ci = jnp.where(d == 0, i, n_chunks - 1 - i)
    is_last = ci == n_chunks - 1

    @pl.when(i == 0)
    def _():
        hst[...] = h0_ref[d]

    n_lc = a_s.shape[0]
    rows = tc * SUBLANES
    per = MXU_DIM // LANES
    for c in range(n_lc // per):
        xc2 = jnp.concatenate(
            [jnp.concatenate([z_ref[k, SUBLANES:], jnp.where(is_last, ztail_ref[k], znext_ref[k])], axis=0)
             for k in range(c * per, (c + 1) * per)], axis=1)
        sl = slice(c * MXU_DIM, (c + 1) * MXU_DIM)
        g = jnp.dot(xc2.astype(BF16), wbd_ref[0, c], preferred_element_type=F32) + brg_ref[0, c]
        r2 = jnp.tanh(g[:, :MXU_DIM]) + 1.0
        i2 = jnp.tanh(g[:, MXU_DIM:]) + 1.0
        z = -lam_ref[0, :, sl]
        softplus = jnp.maximum(z, 0.0) + jnp.log1p(jnp.exp(-jnp.abs(z)))
        c_nat = (0.5 * LRU_C) * softplus
        neg_log_a = c_nat * r2
        a = jnp.exp2((-LOG2E * c_nat) * r2)
        one_minus_a2 = jnp.tanh(neg_log_a) * (a * a + 1.0)
        root = jnp.where(one_minus_a2 > 0.0, one_minus_a2 * lax.rsqrt(one_minus_a2), 0.0)
        bb = root * (i2 * xc2)
        for kk in range(per):
            a_s[c * per + kk, :rows] = a[:, kk * LANES:(kk + 1) * LANES]
            b_s[c * per + kk, :rows] = bb[:, kk * LANES:(kk + 1) * LANES]

    def step(j, h):
        t = jnp.where(d == 0, j, tc - 1 - j)
        r0 = pl.multiple_of(t * SUBLANES, SUBLANES)
        h = a_s[:, pl.ds(r0, SUBLANES), :] * h + b_s[:, pl.ds(r0, SUBLANES), :]
        h_ref[0, :, pl.ds(r0, SUBLANES), :] = h
        return h

    h_last = lax.fori_loop(0, tc, step, hst[...], unroll=8)
    hst[...] = h_last
    hfin_ref[0] = h_last


def _scan_call(z, ztail, wbd, brg, lam3, h0, *, tc):
    n_lc, n_rows, _ = z.shape
    n_b = SUBLANES
    seq = n_rows // n_b
    d = n_lc * LANES
    n_chunks = seq // tc
    n_col = d // MXU_DIM
    rows = tc * n_b

    def chunk(dd, i):
        return jnp.where(dd == 0, i, n_chunks - 1 - i)

    kern = functools.partial(_scan_kernel, n_chunks=n_chunks, tc=tc)
    return pl.pallas_call(
        kern,
        grid=(2, n_chunks),
        in_specs=[
            pl.BlockSpec((n_lc, rows, LANES), lambda dd, i: (0, chunk(dd, i), 0)),
            pl.BlockSpec((n_lc, n_b, LANES),
                         lambda dd, i: (0, jnp.minimum((chunk(dd, i) + 1) * tc, seq - 1), 0)),
            pl.BlockSpec((n_lc, n_b, LANES), lambda dd, i: (0, 0, 0)),
            pl.BlockSpec((1, n_col, MXU_DIM, 2 * MXU_DIM), lambda dd, i: (dd, 0, 0, 0)),
            pl.BlockSpec((1, n_col, 1, 2 * MXU_DIM), lambda dd, i: (dd, 0, 0, 0)),
            pl.BlockSpec((1, 1, d), lambda dd, i: (dd, 0, 0)),
            pl.BlockSpec((2, n_lc, n_b, LANES), lambda dd, i: (0, 0, 0, 0)),
        ],
        out_specs=[
            pl.BlockSpec((1, n_lc, rows, LANES), lambda dd, i: (dd, 0, chunk(dd, i), 0)),
            pl.BlockSpec((1, n_lc, n_b, LANES), lambda dd, i: (dd, 0, 0, 0)),
        ],
        out_shape=[
            jax.ShapeDtypeStruct((2, n_lc, n_rows, LANES), F32),
            jax.ShapeDtypeStruct((2, n_lc, n_b, LANES), F32),
        ],
        scratch_shapes=[
            pltpu.VMEM((n_lc, rows + n_b, LANES), F32),
            pltpu.VMEM((n_lc, rows + n_b, LANES), F32),
            pltpu.VMEM((n_lc, n_b, LANES), F32),
        ],
        compiler_params=pltpu.CompilerParams(
            dimension_semantics=("arbitrary", "arbitrary"), vmem_limit_bytes=VMEM_LIMIT),
        name="scan",
    )(z, z, ztail, wbd, brg, lam3, h0)


def _window_plan(i, n_i, rows):
    wr = min(WIN_ROWS, rows)
    rng = []
    for rq in range(ATT_ROWS):
        r = ATT_ROWS * i + rq
        r0 = min(max(r - wr // 2, 0), rows - wr)
        lo = r0 - (ATT_ROWS * i - ATT_ROWS)
        rng.append((lo, lo + wr))
    s_lo = min(lo for lo, _ in rng)
    s_hi = max(hi for _, hi in rng)
    assert 0 <= s_lo and s_hi <= ATT_SLOTS
    rows_per_tile = LANES // GRID_W
    tiles = []
    for j in range(ATT_ROWS // rows_per_tile):
        rqs = range(j * rows_per_tile, (j + 1) * rows_per_tile)
        a = min(rng[rq][0] for rq in rqs)
        b = max(rng[rq][1] for rq in rqs)
        half = {}
        for s in range(a, b):
            ok = tuple(rng[rq][0] <= s < rng[rq][1] for rq in rqs)
            if not all(ok):
                half[s - s_lo] = ok
        tiles.append((a - s_lo, b - s_lo, half))
    return s_lo, s_hi, tiles


def _attn_block(q_ref, k_refs, v_refs, kc_ref, vc_ref, bias_ref, o_ref, s_buf, plan):
    s_lo, s_hi, tiles = plan
    n_slots = s_hi - s_lo
    n_loc = n_slots * GRID_W
    n_ctx = kc_ref.shape[0]
    lane = lax.broadcasted_iota(jnp.int32, (1, LANES), 1)
    lo = lane < HEAD_DIM
    heads_per_pair = LANES // HEAD_DIM
    n_pairs = N_HEADS // heads_per_pair
    dims_nt = (((1,), (1,)), ((), ()))
    dims_tn = (((0,), (0,)), ((), ()))

    def keys_of(refs, c_ref, ls):
        parts = []
        for blk, ref in enumerate(refs):
            a = max(s_lo, blk * ATT_ROWS) - blk * ATT_ROWS
            b = min(s_hi, (blk + 1) * ATT_ROWS) - blk * ATT_ROWS
            if b > a:
                parts.append(ref[a * GRID_W:b * GRID_W, ls])
        return jnp.concatenate(parts + [c_ref[:, ls]], axis=0)

    def scores(p):
        ls = slice(p * LANES, (p + 1) * LANES)
        qp = q_ref[:, ls]
        k_all = keys_of(k_refs, kc_ref, ls)
        for hh in range(heads_per_pair):
            msk = lo if hh == 0 else jnp.logical_not(lo)
            qm = jnp.where(msk, qp, jnp.zeros_like(qp))
            s_buf[p % 2, hh, :n_loc + n_ctx] = lax.dot_general(k_all, qm, dims_nt, preferred_element_type=F32)

    def softmax_pv(p):
        ls = slice(p * LANES, (p + 1) * LANES)
        v_all = keys_of(v_refs, vc_ref, ls)
        o_t = []
        for hh in range(heads_per_pair):
            own = lo if hh == 0 else jnp.logical_not(lo)
            head = p * heads_per_pair + hh
            cols = []
            for j, (a, b, half) in enumerate(tiles):
                qs = slice(j * LANES, (j + 1) * LANES)
                blocks = []
                for s in range(a, b):
                    blk = (s_buf[p % 2, hh, s * GRID_W:(s + 1) * GRID_W, qs]
                           + bias_ref[head, (s_lo + s) * GRID_W:(s_lo + s + 1) * GRID_W, qs])
                    if s in half:
                        blk = jnp.where(lo if half[s][0] else jnp.logical_not(lo), blk, NEG)
                    blocks.append(blk)
                s_loc = jnp.concatenate(blocks, axis=0)
                s_ctx = s_buf[p % 2, hh, n_loc:n_loc + n_ctx, qs]
                m = jnp.maximum(jnp.max(s_loc, axis=0, keepdims=True), jnp.max(s_ctx, axis=0, keepdims=True))
                col = [jnp.exp2(s_loc - m), jnp.exp2(s_ctx - m)]
                if a > 0:
                    col.insert(0, jnp.zeros((a * GRID_W, LANES), F32))
                if b < n_slots:
                    col.insert(-1, jnp.zeros(((n_slots - b) * GRID_W, LANES), F32))
                cols.append(jnp.concatenate(col, axis=0))
            p_all = jnp.concatenate(cols, axis=1).astype(BF16)
            v_own = jnp.where(own, v_all, jnp.ones_like(v_all))
            o = lax.dot_general(v_own, p_all, dims_tn, preferred_element_type=F32)
            den = o[(1 - hh) * HEAD_DIM:(1 - hh) * HEAD_DIM + 1]
            o_t.append(o[hh * HEAD_DIM:(hh + 1) * HEAD_DIM] * (1.0 / den))
        o_ref[:, ls] = jnp.concatenate(o_t, axis=0).T.astype(BF16)

    scores(0)
    for p in range(n_pairs):
        if p + 1 < n_pairs:
            scores(p + 1)
        softmax_pv(p)


def _attn_kernel(q_ref, kp_ref, km_ref, kn_ref, vp_ref, vm_ref, vn_ref, kc_ref, vc_ref, bias_ref, o_ref, s_buf,
                 *, plans):
    i = pl.program_id(0)
    n_i = pl.num_programs(0)
    variant = (i > 0).astype(jnp.int32) + (i == n_i - 1).astype(jnp.int32)
    for v, plan in enumerate(plans):
        @pl.when(variant == v)
        def _(plan=plan):
            _attn_block(q_ref, (kp_ref, km_ref, kn_ref), (vp_ref, vm_ref, vn_ref), kc_ref, vc_ref,
                        bias_ref, o_ref, s_buf, plan)


def _attn_call(proj, projc, bias, *, n_b, seq, ctx_len, q_col, k_col, v_col, kc_col, vc_col):
    d = N_HEADS * HEAD_DIM
    n_i = seq // ATT_Q
    rows = seq // GRID_W
    plans = tuple(_window_plan(i, n_i, rows) for i in (0, 1, n_i - 1))

    def row(i, b):
        return b * n_i + i

    def row_prev(i, b):
        return b * n_i + jnp.maximum(i - 1, 0)

    def row_next(i, b):
        return b * n_i + jnp.minimum(i + 1, n_i - 1)

    blk = (ATT_Q, d)
    return pl.pallas_call(
        functools.partial(_attn_kernel, plans=plans),
        grid=(n_i, n_b),
        in_specs=[
            pl.BlockSpec(blk, lambda i, b: (row(i, b), q_col)),
            pl.BlockSpec(blk, lambda i, b: (row_prev(i, b), k_col)),
            pl.BlockSpec(blk, lambda i, b: (row(i, b), k_col)),
            pl.BlockSpec(blk, lambda i, b: (row_next(i, b), k_col)),
            pl.BlockSpec(blk, lambda i, b: (row_prev(i, b), v_col)),
            pl.BlockSpec(blk, lambda i, b: (row(i, b), v_col)),
            pl.BlockSpec(blk, lambda i, b: (row_next(i, b), v_col)),
            pl.BlockSpec((ctx_len, d), lambda i, b: (b, kc_col)),
            pl.BlockSpec((ctx_len, d), lambda i, b: (b, vc_col)),
            pl.BlockSpec(bias.shape, lambda i, b: (0, 0, 0), pipeline_mode=pl.Buffered(1)),
        ],
        out_specs=pl.BlockSpec(blk, lambda i, b: (row(i, b), 0)),
        out_shape=jax.ShapeDtypeStruct((n_b * seq, d), BF16),
        scratch_shapes=[pltpu.VMEM((2, LANES // HEAD_DIM, ATT_SLOTS * GRID_W + ctx_len, ATT_Q), F32)],
        compiler_params=pltpu.CompilerParams(
            dimension_semantics=("arbitrary", "arbitrary"), vmem_limit_bytes=VMEM_LIMIT),
        name="attn",
    )(proj, proj, proj, proj, proj, proj, proj, projc, projc, bias)


def _attn_bias(rpb):
    qc = np.arange(GRID_W)[None, :]
    kc = np.arange(GRID_W)[:, None]
    col_start = np.clip(qc - WIN_COLS // 2, 0, GRID_W - WIN_COLS)
    col_ok = (kc >= col_start) & (kc < col_start + WIN_COLS)
    dc_idx = np.clip(kc - qc + WIN_COLS - 1, 0, 2 * WIN_COLS - 2)
    onehot = (dc_idx[None] == np.arange(2 * WIN_COLS - 1)[:, None, None]) & col_ok[None]
    col_tab = jnp.einsum('had,dkq->hakq', rpb.astype(F32), jnp.asarray(onehot, F32),
                         precision=lax.Precision.HIGHEST)
    col_tab = jnp.where(jnp.asarray(col_ok)[None, None], col_tab * LOG2E, NEG)
    slots = []
    for s in range(ATT_SLOTS):
        q_rows = [col_tab[:, s - ATT_ROWS - rq + WIN_ROWS - 1] for rq in range(ATT_ROWS)]
        slots.append(jnp.concatenate(q_rows, axis=-1))
    return jnp.concatenate(slots, axis=1)


def _out_kernel(hf_ref, hr_ref, ggr_ref, gl_ref, gna_ref, att_ref, x_ref, gt1_ref, sh2_ref, sc2_ref,
                gt2_ref, gn_ref, wl_ref, wn_ref, wo_ref, w1_ref, w2_ref, out_ref):
    n_b, ts, d = x_ref.shape
    group = n_b // OUT_GROUPS
    rows = group * ts
    groups = [slice(g * group, (g + 1) * group) for g in range(OUT_GROUPS)]

    y_na, y_lru = [], []
    for bs in groups:
        y_na.append(jnp.dot(att_ref[bs].reshape(rows, d), wn_ref[...], preferred_element_type=F32))
        hl = jnp.stack([
            jnp.concatenate([hf_ref[0, k, pl.ds(b, ts, stride=n_b), :] + hr_ref[0, k, pl.ds(b, ts, stride=n_b), :]
                             for k in range(d // LANES)], axis=1)
            for b in range(bs.start, bs.stop)], axis=0)
        hl = hl * ggr_ref[bs].astype(F32)
        y_lru.append(jnp.dot(hl.reshape(rows, d).astype(BF16), wl_ref[...], preferred_element_type=F32))

    ys = []
    for g, bs in enumerate(groups):
        y = (gl_ref[bs].astype(F32).reshape(rows, d) * y_lru[g]
             + gna_ref[bs].astype(F32).reshape(rows, d) * y_na[g])
        ys.append(jnp.dot(y.astype(BF16), wo_ref[...], preferred_element_type=F32).reshape(group, ts, d))

    x1s, u2s = [], []
    for g, bs in enumerate(groups):
        x1 = x_ref[bs] + gt1_ref[bs] * _rms(ys[g], gn_ref[1:2, :])
        u2 = _rms(x1, gn_ref[2:3, :]) * (1.0 + sc2_ref[bs]) + sh2_ref[bs]
        x1s.append(x1)
        u2s.append(u2.reshape(rows, d).astype(BF16))

    ms = [None] * OUT_GROUPS
    for f in range(0, w1_ref.shape[1], FF_CHUNK):
        h1s = []
        for g in range(OUT_GROUPS):
            h1 = jnp.dot(u2s[g], w1_ref[:, f:f + FF_CHUNK], preferred_element_type=F32)
            h1 = jnp.maximum(h1, 0.0)
            h1s.append((h1 * h1).astype(BF16))
        for g in range(OUT_GROUPS):
            part = jnp.dot(h1s[g], w2_ref[f:f + FF_CHUNK, :], preferred_element_type=F32)
            ms[g] = part if ms[g] is None else ms[g] + part

    for g, bs in enumerate(groups):
        out_ref[bs] = x1s[g] + gt2_ref[bs] * _rms(ms[g].reshape(group, ts, d), gn_ref[3:4, :])


def _out_call(h2, proj3, att3, x3, mod3, g_norm, wl, wn, wo, w1, w2, *, ts, ggr_col, gl_col, gna_col):
    n_b, seq, d = x3.shape

    def modspec(k):
        return pl.BlockSpec((n_b, 1, d), lambda t: (0, 0, k))

    def resident(w):
        return pl.BlockSpec(w.shape, lambda t: (0, 0), pipeline_mode=pl.Buffered(1))

    tile = (n_b, ts, d)
    return pl.pallas_call(
        _out_kernel,
        grid=(seq // ts,),
        in_specs=[
            pl.BlockSpec((1, d // LANES, ts * n_b, LANES), lambda t: (0, 0, t, 0)),
            pl.BlockSpec((1, d // LANES, ts * n_b, LANES), lambda t: (1, 0, t, 0)),
            pl.BlockSpec(tile, lambda t: (0, t, ggr_col)),
            pl.BlockSpec(tile, lambda t: (0, t, gl_col)),
            pl.BlockSpec(tile, lambda t: (0, t, gna_col)),
            pl.BlockSpec(tile, lambda t: (0, t, 0)),
            pl.BlockSpec(tile, lambda t: (0, t, 0)),
            modspec(2), modspec(3), modspec(4), modspec(5),
            pl.BlockSpec(g_norm.shape, lambda t: (0, 0)),
            resident(wl), resident(wn), resident(wo), resident(w1), resident(w2),
        ],
        out_specs=pl.BlockSpec(tile, lambda t: (0, t, 0)),
        out_shape=jax.ShapeDtypeStruct((n_b, seq, d), F32),
        compiler_params=pltpu.CompilerParams(
            dimension_semantics=("arbitrary",), vmem_limit_bytes=VMEM_LIMIT),
        name="out",
    )(h2, h2, proj3, proj3, proj3, att3, x3, mod3, mod3, mod3, mod3, g_norm, wl, wn, wo, w1, w2)


def _gate_weights(w_rg, b_rg):
    n_dir, n_gate, n_blk, bw, _ = w_rg.shape
    per = MXU_DIM // bw
    n_col = n_blk // per
    b_rg = 0.5 * b_rg
    w = w_rg.reshape(n_dir, n_gate, n_col, per, bw, bw)
    eye = jnp.eye(per, dtype=w.dtype)
    wbd = jnp.einsum('dgcpij,pq->dgcpiqj', w, eye).reshape(n_dir, n_gate, n_col, MXU_DIM, MXU_DIM)
    wbd = jnp.concatenate([wbd[:, 0], wbd[:, 1]], axis=-1).astype(BF16)
    b = b_rg.reshape(n_dir, n_gate, n_col, MXU_DIM)
    brg = jnp.concatenate([b[:, 0], b[:, 1]], axis=-1)[:, :, None, :]
    return wbd, brg


def kernel(x, c, ctx, c_ctx, w_ada, b_ada, g_norm, w_in, b_gate, conv_w, conv_b, w_rg, b_rg, lam, rpb,
           w_lru_out, w_na_out, w_o, w_mlp1, w_mlp2):
    n_b, seq, d = x.shape
    ctx_len = ctx.shape[1]
    rows = seq // GRID_W
    depth = w_ada.shape[0]
    assert depth == 1 and rows % ATT_ROWS == 0 and n_b == SUBLANES

    l = 0
    cs = jnp.concatenate([c, c_ctx[None], jnp.zeros((2 * SUBLANES - n_b - 1, d), F32)], axis=0)
    mod = _mod_call(cs, w_ada[l], b_ada[l][None])
    mod3 = mod[:, None, :]

    w_in_bf = w_in[l].astype(BF16)
    b_gate2 = b_gate[l][None]
    g0 = g_norm[l, 0:1]

    cw, cb = 0.5 * conv_w[l], 0.5 * conv_b[l][None]
    z, ztail, proj3 = _inproj_call(x, mod3, None, g0, w_in_bf, b_gate2, cw, cb, ts=INPROJ_TS,
                                   segs=(0, 1, 2, 3, 4, 5, 6))
    z_c, ztail_c, projc3 = _inproj_call(ctx, mod3, n_b, g0, w_in_bf, b_gate2, cw, cb, ts=INPROJ_TS,
                                        segs=(0, 3, 4))

    wbd, brg = _gate_weights(w_rg[l], b_rg[l])
    lam3 = lam[l][:, None, :]
    _, h0 = _scan_call(z_c, ztail_c, wbd, brg, lam3, jnp.zeros((2, d // LANES, n_b, LANES), F32), tc=SCAN_TC)
    h, _ = _scan_call(z, ztail, wbd, brg, lam3, h0, tc=SCAN_TC)

    bias = _attn_bias(rpb[l])
    att = _attn_call(proj3.reshape(n_b * seq, -1), projc3.reshape(n_b * ctx_len, -1), bias,
                     n_b=n_b, seq=seq, ctx_len=ctx_len, q_col=1, k_col=2, v_col=3, kc_col=0, vc_col=1)

    return _out_call(h, proj3, att.reshape(n_b, seq, d), x, mod3, g_norm[l],
                     w_lru_out[l].astype(BF16), w_na_out[l].astype(BF16), w_o[l].astype(BF16),
                     w_mlp1[l].astype(BF16), w_mlp2[l].astype(BF16),
                     ts=OUT_TS, ggr_col=0, gl_col=4, gna_col=5)
```

```python
import functools

import numpy as np
import jax
import jax.numpy as jnp
from jax import lax
from jax.experimental import pallas as pl
from jax.experimental.pallas import tpu as pltpu

F32 = jnp.float32
BF16 = jnp.bfloat16

EPS = 1e-6
NEG = -1e30
LRU_C = 8.0
N_HEADS = 16
HEAD_DIM = 64
GRID_W = 64
WIN_ROWS = 8
WIN_COLS = 16
CONV_W = 4
CONV_PAD_LEFT = 2

LANES = 128
SUBLANES = 8
MXU_DIM = 256
VMEM_LIMIT = 56 * 1024 * 1024

INPROJ_TS = 64
SCAN_TC = 128
OUT_TS = 64

ATT_ROWS = 4
ATT_Q = ATT_ROWS * GRID_W
ATT_SLOTS = 3 * ATT_ROWS
INPROJ_GROUPS = 2
OUT_GROUPS = 2
FF_CHUNK = 1024
LOG2E = 1.4426950408889634


def _rms(x, g):
    return x * lax.rsqrt(jnp.mean(x * x, axis=-1, keepdims=True) + EPS) * g


def _mod_kernel(c_ref, w_ref, b_ref, o_ref):
    c = c_ref[...]
    s = c * jax.nn.sigmoid(c)
    o_ref[...] = jnp.dot(s, w_ref[...], preferred_element_type=F32) + b_ref[...]


def _mod_call(cs, w_ada, b_ada):
    rows, d = cs.shape
    n_out = w_ada.shape[1]
    return pl.pallas_call(
        _mod_kernel,
        grid=(n_out // d,),
        in_specs=[
            pl.BlockSpec((rows, d), lambda n: (0, 0)),
            pl.BlockSpec((d, d), lambda n: (0, n)),
            pl.BlockSpec((1, d), lambda n: (0, n)),
        ],
        out_specs=pl.BlockSpec((rows, d), lambda n: (0, n)),
        out_shape=jax.ShapeDtypeStruct((rows, n_out), F32),
        compiler_params=pltpu.CompilerParams(
            dimension_semantics=("arbitrary",), vmem_limit_bytes=VMEM_LIMIT),
        name="mod",
    )(cs, w_ada, b_ada)


def _inproj_kernel(x_ref, sh_ref, sc_ref, g_ref, w_ref, bg_ref, cw_ref, cb_ref, z_ref, ztail_ref, pr_ref,
                   xr_s, *, segs):
    n_b, ts, d = x_ref.shape
    hist = (CONV_W - 1) * n_b
    group = n_b // INPROJ_GROUPS
    groups = [slice(g * group, (g + 1) * group) for g in range(INPROJ_GROUPS)]

    @pl.when(pl.program_id(0) == 0)
    def _():
        xr_s[:, :hist, :] = jnp.zeros((xr_s.shape[0], hist, LANES), F32)

    def per_sample(ref, bs):
        return ref[bs] if ref.shape[0] == n_b else ref[...]

    us = []
    for bs in groups:
        u = _rms(x_ref[bs], g_ref[...]) * (1.0 + per_sample(sc_ref, bs)) + per_sample(sh_ref, bs)
        us.append(u.reshape(group * ts, d).astype(BF16))
    def conv():
        for k in range(d // LANES):
            ks = slice(k * LANES, (k + 1) * LANES)
            xe = xr_s[k]
            z = cb_ref[:, ks]
            tail = cb_ref[:, ks]
            for j in range(CONV_W):
                z = z + cw_ref[j:j + 1, ks] * xe[j * n_b:(j + ts) * n_b]
                if j < CONV_W - 1:
                    tail = tail + cw_ref[j:j + 1, ks] * xe[(ts + j) * n_b:(ts + j + 1) * n_b]
            z_ref[k] = z
            ztail_ref[k] = tail
            xr_s[k, :hist, :] = xe[ts * n_b:]

    for idx, seg in sorted(enumerate(segs), key=lambda p: p[1] == 0):
        for bs, u in zip(groups, us):
            acc = jnp.dot(u, w_ref[:, seg * d:(seg + 1) * d], preferred_element_type=F32)
            if seg == 0:
                for b in range(group):
                    for k in range(d // LANES):
                        xr_s[k, pl.ds(hist + bs.start + b, ts, stride=n_b), :] = (
                            acc[b * ts:(b + 1) * ts, k * LANES:(k + 1) * LANES])
                continue
            if seg == 1:
                val = jax.nn.gelu(acc)
            elif seg == 2:
                val = acc * (HEAD_DIM ** -0.5 * LOG2E)
            elif seg in (3, 4):
                val = acc
            else:
                val = jax.nn.sigmoid(acc + bg_ref[:, (seg - 5) * d:(seg - 4) * d])
            pr_ref[bs, :, (idx - 1) * d:idx * d] = val.astype(BF16).reshape(group, ts, d)
    conv()


def _inproj_call(x3, mod3, mod_row, g0, w_in_bf, b_gate2, conv_w, conv_b, *, ts, segs):
    n_b, seq, d = x3.shape
    assert CONV_W == 4 and CONV_PAD_LEFT == 2
    segs = tuple(segs)
    n_out = (len(segs) - 1) * d
    if mod_row is None:
        mod_blk, mod_idx = (n_b, 1, d), 0
    else:
        mod_blk, mod_idx = (1, 1, d), mod_row
    kern = functools.partial(_inproj_kernel, segs=segs)
    return pl.pallas_call(
        kern,
        grid=(seq // ts,),
        in_specs=[
            pl.BlockSpec((n_b, ts, d), lambda t: (0, t, 0)),
            pl.BlockSpec(mod_blk, lambda t: (mod_idx, 0, 0)),
            pl.BlockSpec(mod_blk, lambda t: (mod_idx, 0, 1)),
            pl.BlockSpec((1, d), lambda t: (0, 0)),
            pl.BlockSpec(w_in_bf.shape, lambda t: (0, 0), pipeline_mode=pl.Buffered(1)),
            pl.BlockSpec(b_gate2.shape, lambda t: (0, 0)),
            pl.BlockSpec(conv_w.shape, lambda t: (0, 0)),
            pl.BlockSpec(conv_b.shape, lambda t: (0, 0)),
        ],
        out_specs=[
            pl.BlockSpec((d // LANES, ts * n_b, LANES), lambda t: (0, t, 0)),
            pl.BlockSpec((d // LANES, n_b, LANES), lambda t: (0, 0, 0)),
            pl.BlockSpec((n_b, ts, n_out), lambda t: (0, t, 0)),
        ],
        out_shape=[
            jax.ShapeDtypeStruct((d // LANES, seq * n_b, LANES), F32),
            jax.ShapeDtypeStruct((d // LANES, n_b, LANES), F32),
            jax.ShapeDtypeStruct((n_b, seq, n_out), BF16),
        ],
        scratch_shapes=[pltpu.VMEM((d // LANES, (ts + CONV_W - 1) * n_b, LANES), F32)],
        compiler_params=pltpu.CompilerParams(
            dimension_semantics=("arbitrary",), vmem_limit_bytes=VMEM_LIMIT),
        name="inproj",
    )(x3, mod3, mod3, g0, w_in_bf, b_gate2, conv_w, conv_b)


def _scan_kernel(z_ref, znext_ref, ztail_ref, wbd_ref, brg_ref, lam_ref, h0_ref, h_ref, hfin_ref, a_s, b_s, hst,
                 *, n_chunks, tc):
    d = pl.program_id(0)
    i = pl.program_id(1)
    ci = jnp.where(d == 0, i, n_chunks - 1 - i)
    is_last = ci == n_chunks - 1

    @pl.when(i == 0)
    def _():
        hst[...] = h0_ref[d]

    n_lc = a_s.shape[0]
    rows = tc * SUBLANES
    per = MXU_DIM // LANES
    for c in range(n_lc // per):
        xc2 = jnp.concatenate(
            [jnp.concatenate([z_ref[k, SUBLANES:], jnp.where(is_last, ztail_ref[k], znext_ref[k])], axis=0)
             for k in range(c * per, (c + 1) * per)], axis=1)
        sl = slice(c * MXU_DIM, (c + 1) * MXU_DIM)
        g = jnp.dot(xc2.astype(BF16), wbd_ref[0, c], preferred_element_type=F32) + brg_ref[0, c]
        r2 = jnp.tanh(g[:, :MXU_DIM]) + 1.0
        i2 = jnp.tanh(g[:, MXU_DIM:]) + 1.0
        z = -lam_ref[0, :, sl]
        softplus = jnp.maximum(z, 0.0) + jnp.log1p(jnp.exp(-jnp.abs(z)))
        c_nat = (0.5 * LRU_C) * softplus
        neg_log_a = c_nat * r2
        a = jnp.exp2((-LOG2E * c_nat) * r2)
        one_minus_a2 = jnp.tanh(neg_log_a) * (a * a + 1.0)
        root = jnp.where(one_minus_a2 > 0.0, one_minus_a2 * lax.rsqrt(one_minus_a2), 0.0)
        bb = root * (i2 * xc2)
        for kk in range(per):
            a_s[c * per + kk, :rows] = a[:, kk * LANES:(kk + 1) * LANES]
            b_s[c * per + kk, :rows] = bb[:, kk * LANES:(kk + 1) * LANES]

    def step(j, h):
        t = jnp.where(d == 0, j, tc - 1 - j)
        r0 = pl.multiple_of(t * SUBLANES, SUBLANES)
        h = a_s[:, pl.ds(r0, SUBLANES), :] * h + b_s[:, pl.ds(r0, SUBLANES), :]
        h_ref[0, :, pl.ds(r0, SUBLANES), :] = h
        return h

    h_last = lax.fori_loop(0, tc, step, hst[...], unroll=8)
    hst[...] = h_last
    hfin_ref[0] = h_last


def _scan_call(z, ztail, wbd, brg, lam3, h0, *, tc):
    n_lc, n_rows, _ = z.shape
    n_b = SUBLANES
    seq = n_rows // n_b
    d = n_lc * LANES
    n_chunks = seq // tc
    n_col = d // MXU_DIM
    rows = tc * n_b

    def chunk(dd, i):
        return jnp.where(dd == 0, i, n_chunks - 1 - i)

    kern = functools.partial(_scan_kernel, n_chunks=n_chunks, tc=tc)
    return pl.pallas_call(
        kern,
        grid=(2, n_chunks),
        in_specs=[
            pl.BlockSpec((n_lc, rows, LANES), lambda dd, i: (0, chunk(dd, i), 0)),
            pl.BlockSpec((n_lc, n_b, LANES),
                         lambda dd, i: (0, jnp.minimum((chunk(dd, i) + 1) * tc, seq - 1), 0)),
            pl.BlockSpec((n_lc, n_b, LANES), lambda dd, i: (0, 0, 0)),
            pl.BlockSpec((1, n_col, MXU_DIM, 2 * MXU_DIM), lambda dd, i: (dd, 0, 0, 0)),
            pl.BlockSpec((1, n_col, 1, 2 * MXU_DIM), lambda dd, i: (dd, 0, 0, 0)),
            pl.BlockSpec((1, 1, d), lambda dd, i: (dd, 0, 0)),
            pl.BlockSpec((2, n_lc, n_b, LANES), lambda dd, i: (0, 0, 0, 0)),
        ],
        out_specs=[
            pl.BlockSpec((1, n_lc, rows, LANES), lambda dd, i: (dd, 0, chunk(dd, i), 0)),
            pl.BlockSpec((1, n_lc, n_b, LANES), lambda dd, i: (dd, 0, 0, 0)),
        ],
        out_shape=[
            jax.ShapeDtypeStruct((2, n_lc, n_rows, LANES), F32),
            jax.ShapeDtypeStruct((2, n_lc, n_b, LANES), F32),
        ],
        scratch_shapes=[
            pltpu.VMEM((n_lc, rows + n_b, LANES), F32),
            pltpu.VMEM((n_lc, rows + n_b, LANES), F32),
            pltpu.VMEM((n_lc, n_b, LANES), F32),
        ],
        compiler_params=pltpu.CompilerParams(
            dimension_semantics=("arbitrary", "arbitrary"), vmem_limit_bytes=VMEM_LIMIT),
        name="scan",
    )(z, z, ztail, wbd, brg, lam3, h0)


def _window_plan(i, n_i, rows):
    wr = min(WIN_ROWS, rows)
    rng = []
    for rq in range(ATT_ROWS):
        r = ATT_ROWS * i + rq
        r0 = min(max(r - wr // 2, 0), rows - wr)
        lo = r0 - (ATT_ROWS * i - ATT_ROWS)
        rng.append((lo, lo + wr))
    s_lo = min(lo for lo, _ in rng)
    s_hi = max(hi for _, hi in rng)
    assert 0 <= s_lo and s_hi <= ATT_SLOTS
    rows_per_tile = LANES // GRID_W
    tiles = []
    for j in range(ATT_ROWS // rows_per_tile):
        rqs = range(j * rows_per_tile, (j + 1) * rows_per_tile)
        a = min(rng[rq][0] for rq in rqs)
        b = max(rng[rq][1] for rq in rqs)
        half = {}
        for s in range(a, b):
            ok = tuple(rng[rq][0] <= s < rng[rq][1] for rq in rqs)
            if not all(ok):
                half[s - s_lo] = ok
        tiles.append((a - s_lo, b - s_lo, half))
    return s_lo, s_hi, tiles


def _attn_block(q_ref, k_refs, v_refs, kc_ref, vc_ref, bias_ref, o_ref, s_buf, plan):
    s_lo, s_hi, tiles = plan
    n_slots = s_hi - s_lo
    n_loc = n_slots * GRID_W
    n_ctx = kc_ref.shape[0]
    lane = lax.broadcasted_iota(jnp.int32, (1, LANES), 1)
    lo = lane < HEAD_DIM
    heads_per_pair = LANES // HEAD_DIM
    n_pairs = N_HEADS // heads_per_pair
    dims_nt = (((1,), (1,)), ((), ()))
    dims_tn = (((0,), (0,)), ((), ()))

    def keys_of(refs, c_ref, ls):
        parts = []
        for blk, ref in enumerate(refs):
            a = max(s_lo, blk * ATT_ROWS) - blk * ATT_ROWS
            b = min(s_hi, (blk + 1) * ATT_ROWS) - blk * ATT_ROWS
            if b > a:
                parts.append(ref[a * GRID_W:b * GRID_W, ls])
        return jnp.concatenate(parts + [c_ref[:, ls]], axis=0)

    def scores(p):
        ls = slice(p * LANES, (p + 1) * LANES)
        qp = q_ref[:, ls]
        k_all = keys_of(k_refs, kc_ref, ls)
        for hh in range(heads_per_pair):
            msk = lo if hh == 0 else jnp.logical_not(lo)
            qm = jnp.where(msk, qp, jnp.zeros_like(qp))
            s_buf[p % 2, hh, :n_loc + n_ctx] = lax.dot_general(k_all, qm, dims_nt, preferred_element_type=F32)

    def softmax_pv(p):
        ls = slice(p * LANES, (p + 1) * LANES)
        v_all = keys_of(v_refs, vc_ref, ls)
        o_t = []
        for hh in range(heads_per_pair):
            own = lo if hh == 0 else jnp.logical_not(lo)
            head = p * heads_per_pair + hh
            cols = []
            for j, (a, b, half) in enumerate(tiles):
                qs = slice(j * LANES, (j + 1) * LANES)
                blocks = []
                for s in range(a, b):
                    blk = (s_buf[p % 2, hh, s * GRID_W:(s + 1) * GRID_W, qs]
                           + bias_ref[head, (s_lo + s) * GRID_W:(s_lo + s + 1) * GRID_W, qs])
                    if s in half:
                        blk = jnp.where(lo if half[s][0] else jnp.logical_not(lo), blk, NEG)
                    blocks.append(blk)
                s_loc = jnp.concatenate(blocks, axis=0)
                s_ctx = s_buf[p % 2, hh, n_loc:n_loc + n_ctx, qs]
                m = jnp.maximum(jnp.max(s_loc, axis=0, keepdims=True), jnp.max(s_ctx, axis=0, keepdims=True))
                col = [jnp.exp2(s_loc - m), jnp.exp2(s_ctx - m)]
                if a > 0:
                    col.insert(0, jnp.zeros((a * GRID_W, LANES), F32))
                if b < n_slots:
                    col.insert(-1, jnp.zeros(((n_slots - b) * GRID_W, LANES), F32))
                cols.append(jnp.concatenate(col, axis=0))
            p_all = jnp.concatenate(cols, axis=1).astype(BF16)
            v_own = jnp.where(own, v_all, jnp.ones_like(v_all))
            o = lax.dot_general(v_own, p_all, dims_tn, preferred_element_type=F32)
            den = o[(1 - hh) * HEAD_DIM:(1 - hh) * HEAD_DIM + 1]
            o_t.append(o[hh * HEAD_DIM:(hh + 1) * HEAD_DIM] * (1.0 / den))
        o_ref[:, ls] = jnp.concatenate(o_t, axis=0).T.astype(BF16)

    scores(0)
    for p in range(n_pairs):
        if p + 1 < n_pairs:
            scores(p + 1)
        softmax_pv(p)


def _attn_kernel(q_ref, kp_ref, km_ref, kn_ref, vp_ref, vm_ref, vn_ref, kc_ref, vc_ref, bias_ref, o_ref, s_buf,
                 *, plans):
    i = pl.program_id(0)
    n_i = pl.num_programs(0)
    variant = (i > 0).astype(jnp.int32) + (i == n_i - 1).astype(jnp.int32)
    for v, plan in enumerate(plans):
        @pl.when(variant == v)
        def _(plan=plan):
            _attn_block(q_ref, (kp_ref, km_ref, kn_ref), (vp_ref, vm_ref, vn_ref), kc_ref, vc_ref,
                        bias_ref, o_ref, s_buf, plan)


def _attn_call(proj, projc, bias, *, n_b, seq, ctx_len, q_col, k_col, v_col, kc_col, vc_col):
    d = N_HEADS * HEAD_DIM
    n_i = seq // ATT_Q
    rows = seq // GRID_W
    plans = tuple(_window_plan(i, n_i, rows) for i in (0, 1, n_i - 1))

    def row(i, b):
        return b * n_i + i

    def row_prev(i, b):
        return b * n_i + jnp.maximum(i - 1, 0)

    def row_next(i, b):
        return b * n_i + jnp.minimum(i + 1, n_i - 1)

    blk = (ATT_Q, d)
    return pl.pallas_call(
        functools.partial(_attn_kernel, plans=plans),
        grid=(n_i, n_b),
        in_specs=[
            pl.BlockSpec(blk, lambda i, b: (row(i, b), q_col)),
            pl.BlockSpec(blk, lambda i, b: (row_prev(i, b), k_col)),
            pl.BlockSpec(blk, lambda i, b: (row(i, b), k_col)),
            pl.BlockSpec(blk, lambda i, b: (row_next(i, b), k_col)),
            pl.BlockSpec(blk, lambda i, b: (row_prev(i, b), v_col)),
            pl.BlockSpec(blk, lambda i, b: (row(i, b), v_col)),
            pl.BlockSpec(blk, lambda i, b: (row_next(i, b), v_col)),
            pl.BlockSpec((ctx_len, d), lambda i, b: (b, kc_col)),
            pl.BlockSpec((ctx_len, d), lambda i, b: (b, vc_col)),
            pl.BlockSpec(bias.shape, lambda i, b: (0, 0, 0), pipeline_mode=pl.Buffered(1)),
        ],
        out_specs=pl.BlockSpec(blk, lambda i, b: (row(i, b), 0)),
        out_shape=jax.ShapeDtypeStruct((n_b * seq, d), BF16),
        scratch_shapes=[pltpu.VMEM((2, LANES // HEAD_DIM, ATT_SLOTS * GRID_W + ctx_len, ATT_Q), F32)],
        compiler_params=pltpu.CompilerParams(
            dimension_semantics=("arbitrary", "arbitrary"), vmem_limit_bytes=VMEM_LIMIT),
        name="attn",
    )(proj, proj, proj, proj, proj, proj, proj, projc, projc, bias)


def _attn_bias(rpb):
    qc = np.arange(GRID_W)[None, :]
    kc = np.arange(GRID_W)[:, None]
    col_start = np.clip(qc - WIN_COLS // 2, 0, GRID_W - WIN_COLS)
    col_ok = (kc >= col_start) & (kc < col_start + WIN_COLS)
    dc_idx = np.clip(kc - qc + WIN_COLS - 1, 0, 2 * WIN_COLS - 2)
    onehot = (dc_idx[None] == np.arange(2 * WIN_COLS - 1)[:, None, None]) & col_ok[None]
    col_tab = jnp.einsum('had,dkq->hakq', rpb.astype(F32), jnp.asarray(onehot, F32),
                         precision=lax.Precision.HIGHEST)
    col_tab = jnp.where(jnp.asarray(col_ok)[None, None], col_tab * LOG2E, NEG)
    slots = []
    for s in range(ATT_SLOTS):
        q_rows = [col_tab[:, s - ATT_ROWS - rq + WIN_ROWS - 1] for rq in range(ATT_ROWS)]
        slots.append(jnp.concatenate(q_rows, axis=-1))
    return jnp.concatenate(slots, axis=1)


def _out_kernel(hf_ref, hr_ref, ggr_ref, gl_ref, gna_ref, att_ref, x_ref, gt1_ref, sh2_ref, sc2_ref,
                gt2_ref, gn_ref, wl_ref, wn_ref, wo_ref, w1_ref, w2_ref, out_ref):
    n_b, ts, d = x_ref.shape
    group = n_b // OUT_GROUPS
    rows = group * ts
    groups = [slice(g * group, (g + 1) * group) for g in range(OUT_GROUPS)]

    y_na, y_lru = [], []
    for bs in groups:
        y_na.append(jnp.dot(att_ref[bs].reshape(rows, d), wn_ref[...], preferred_element_type=F32))
        hl = jnp.stack([
            jnp.concatenate([hf_ref[0, k, pl.ds(b, ts, stride=n_b), :] + hr_ref[0, k, pl.ds(b, ts, stride=n_b), :]
                             for k in range(d // LANES)], axis=1)
            for b in range(bs.start, bs.stop)], axis=0)
        hl = hl * ggr_ref[bs].astype(F32)
        y_lru.append(jnp.dot(hl.reshape(rows, d).astype(BF16), wl_ref[...], preferred_element_type=F32))

    ys = []
    for g, bs in enumerate(groups):
        y = (gl_ref[bs].astype(F32).reshape(rows, d) * y_lru[g]
             + gna_ref[bs].astype(F32).reshape(rows, d) * y_na[g])
        ys.append(jnp.dot(y.astype(BF16), wo_ref[...], preferred_element_type=F32).reshape(group, ts, d))

    x1s, u2s = [], []
    for g, bs in enumerate(groups):
        x1 = x_ref[bs] + gt1_ref[bs] * _rms(ys[g], gn_ref[1:2, :])
        u2 = _rms(x1, gn_ref[2:3, :]) * (1.0 + sc2_ref[bs]) + sh2_ref[bs]
        x1s.append(x1)
        u2s.append(u2.reshape(rows, d).astype(BF16))

    ms = [None] * OUT_GROUPS
    for f in range(0, w1_ref.shape[1], FF_CHUNK):
        h1s = []
        for g in range(OUT_GROUPS):
            h1 = jnp.dot(u2s[g], w1_ref[:, f:f + FF_CHUNK], preferred_element_type=F32)
            h1 = jnp.maximum(h1, 0.0)
            h1s.append((h1 * h1).astype(BF16))
        for g in range(OUT_GROUPS):
            part = jnp.dot(h1s[g], w2_ref[f:f + FF_CHUNK, :], preferred_element_type=F32)
            ms[g] = part if ms[g] is None else ms[g] + part

    for g, bs in enumerate(groups):
        out_ref[bs] = x1s[g] + gt2_ref[bs] * _rms(ms[g].reshape(group, ts, d), gn_ref[3:4, :])


def _out_call(h2, proj3, att3, x3, mod3, g_norm, wl, wn, wo, w1, w2, *, ts, ggr_col, gl_col, gna_col):
    n_b, seq, d = x3.shape

    def modspec(k):
        return pl.BlockSpec((n_b, 1, d), lambda t: (0, 0, k))

    def resident(w):
        return pl.BlockSpec(w.shape, lambda t: (0, 0), pipeline_mode=pl.Buffered(1))

    tile = (n_b, ts, d)
    return pl.pallas_call(
        _out_kernel,
        grid=(seq // ts,),
        in_specs=[
            pl.BlockSpec((1, d // LANES, ts * n_b, LANES), lambda t: (0, 0, t, 0)),
            pl.BlockSpec((1, d // LANES, ts * n_b, LANES), lambda t: (1, 0, t, 0)),
            pl.BlockSpec(tile, lambda t: (0, t, ggr_col)),
            pl.BlockSpec(tile, lambda t: (0, t, gl_col)),
            pl.BlockSpec(tile, lambda t: (0, t, gna_col)),
            pl.BlockSpec(tile, lambda t: (0, t, 0)),
            pl.BlockSpec(tile, lambda t: (0, t, 0)),
            modspec(2), modspec(3), modspec(4), modspec(5),
            pl.BlockSpec(g_norm.shape, lambda t: (0, 0)),
            resident(wl), resident(wn), resident(wo), resident(w1), resident(w2),
        ],
        out_specs=pl.BlockSpec(tile, lambda t: (0, t, 0)),
        out_shape=jax.ShapeDtypeStruct((n_b, seq, d), F32),
        compiler_params=pltpu.CompilerParams(
            dimension_semantics=("arbitrary",), vmem_limit_bytes=VMEM_LIMIT),
        name="out",
    )(h2, h2, proj3, proj3, proj3, att3, x3, mod3, mod3, mod3, mod3, g_norm, wl, wn, wo, w1, w2)


def _gate_weights(w_rg, b_rg):
    n_dir, n_gate, n_blk, bw, _ = w_rg.shape
    per = MXU_DIM // bw
    n_col = n_blk // per
    b_rg = 0.5 * b_rg
    w = w_rg.reshape(n_dir, n_gate, n_col, per, bw, bw)
    eye = jnp.eye(per, dtype=w.dtype)
    wbd = jnp.einsum('dgcpij,pq->dgcpiqj', w, eye).reshape(n_dir, n_gate, n_col, MXU_DIM, MXU_DIM)
    wbd = jnp.concatenate([wbd[:, 0], wbd[:, 1]], axis=-1).astype(BF16)
    b = b_rg.reshape(n_dir, n_gate, n_col, MXU_DIM)
    brg = jnp.concatenate([b[:, 0], b[:, 1]], axis=-1)[:, :, None, :]
    return wbd, brg


def kernel(x, c, ctx, c_ctx, w_ada, b_ada, g_norm, w_in, b_gate, conv_w, conv_b, w_rg, b_rg, lam, rpb,
           w_lru_out, w_na_out, w_o, w_mlp1, w_mlp2):
    n_b, seq, d = x.shape
    ctx_len = ctx.shape[1]
    rows = seq // GRID_W
    depth = w_ada.shape[0]
    assert depth == 1 and rows % ATT_ROWS == 0 and n_b == SUBLANES

    l = 0
    cs = jnp.concatenate([c, c_ctx[None], jnp.zeros((2 * SUBLANES - n_b - 1, d), F32)], axis=0)
    mod = _mod_call(cs, w_ada[l], b_ada[l][None])
    mod3 = mod[:, None, :]

    w_in_bf = w_in[l].astype(BF16)
    b_gate2 = b_gate[l][None]
    g0 = g_norm[l, 0:1]

    cw, cb = 0.5 * conv_w[l], 0.5 * conv_b[l][None]
    z, ztail, proj3 = _inproj_call(x, mod3, None, g0, w_in_bf, b_gate2, cw, cb, ts=INPROJ_TS,
                                   segs=(0, 1, 2, 3, 4, 5, 6))
    z_c, ztail_c, projc3 = _inproj_call(ctx, mod3, n_b, g0, w_in_bf, b_gate2, cw, cb, ts=INPROJ_TS,
                                        segs=(0, 3, 4))

    wbd, brg = _gate_weights(w_rg[l], b_rg[l])
    lam3 = lam[l][:, None, :]
    _, h0 = _scan_call(z_c, ztail_c, wbd, brg, lam3, jnp.zeros((2, d // LANES, n_b, LANES), F32), tc=SCAN_TC)
    h, _ = _scan_call(z, ztail, wbd, brg, lam3, h0, tc=SCAN_TC)

    bias = _attn_bias(rpb[l])
    att = _attn_call(proj3.reshape(n_b * seq, -1), projc3.reshape(n_b * ctx_len, -1), bias,
                     n_b=n_b, seq=seq, ctx_len=ctx_len, q_col=1, k_col=2, v_col=3, kc_col=0, vc_col=1)

    return _out_call(h, proj3, att.reshape(n_b, seq, d), x, mod3, g_norm[l],
                     w_lru_out[l].astype(BF16), w_na_out[l].astype(BF16), w_o[l].astype(BF16),
                     w_mlp1[l].astype(BF16), w_mlp2[l].astype(BF16),
                     ts=OUT_TS, ggr_col=0, gl_col=4, gna_col=5)
```

```python
import functools

import numpy as np
import jax
import jax.numpy as jnp
from jax import lax
from jax.experimental import pallas as pl
from jax.experimental.pallas import tpu as pltpu

F32 = jnp.float32
BF16 = jnp.bfloat16

EPS = 1e-6
NEG = -1e30
LRU_C = 8.0
N_HEADS = 16
HEAD_DIM = 64
GRID_W = 64
WIN_ROWS = 8
WIN_COLS = 16
CONV_W = 4
CONV_PAD_LEFT = 2

LANES = 128
SUBLANES = 8
MXU_DIM = 256
VMEM_LIMIT = 56 * 1024 * 1024

INPROJ_TS = 64
SCAN_TC = 128
OUT_TS = 64

ATT_ROWS = 4
ATT_Q = ATT_ROWS * GRID_W
ATT_SLOTS = 3 * ATT_ROWS
INPROJ_GROUPS = 2
OUT_GROUPS = 2
FF_CHUNK = 1024
LOG2E = 1.4426950408889634


def _rms(x, g):
    return x * lax.rsqrt(jnp.mean(x * x, axis=-1, keepdims=True) + EPS) * g


def _mod_kernel(c_ref, w_ref, b_ref, o_ref):
    c = c_ref[...]
    s = c * jax.nn.sigmoid(c)
    o_ref[...] = jnp.dot(s, w_ref[...], preferred_element_type=F32) + b_ref[...]


def _mod_call(cs, w_ada, b_ada):
    rows, d = cs.shape
    n_out = w_ada.shape[1]
    return pl.pallas_call(
        _mod_kernel,
        grid=(n_out // d,),
        in_specs=[
            pl.BlockSpec((rows, d), lambda n: (0, 0)),
            pl.BlockSpec((d, d), lambda n: (0, n)),
            pl.BlockSpec((1, d), lambda n: (0, n)),
        ],
        out_specs=pl.BlockSpec((rows, d), lambda n: (0, n)),
        out_shape=jax.ShapeDtypeStruct((rows, n_out), F32),
        compiler_params=pltpu.CompilerParams(
            dimension_semantics=("arbitrary",), vmem_limit_bytes=VMEM_LIMIT),
        name="mod",
    )(cs, w_ada, b_ada)


def _inproj_kernel(x_ref, sh_ref, sc_ref, g_ref, w_ref, bg_ref, cw_ref, cb_ref, z_ref, ztail_ref, pr_ref,
                   xr_s, *, segs):
    n_b, ts, d = x_ref.shape
    hist = (CONV_W - 1) * n_b
    group = n_b // INPROJ_GROUPS
    groups = [slice(g * group, (g + 1) * group) for g in range(INPROJ_GROUPS)]

    @pl.when(pl.program_id(0) == 0)
    def _():
        xr_s[:, :hist, :] = jnp.zeros((xr_s.shape[0], hist, LANES), F32)

    def per_sample(ref, bs):
        return ref[bs] if ref.shape[0] == n_b else ref[...]

    us = []
    for bs in groups:
        u = _rms(x_ref[bs], g_ref[...]) * (1.0 + per_sample(sc_ref, bs)) + per_sample(sh_ref, bs)
        us.append(u.reshape(group * ts, d).astype(BF16))
    def conv():
        for k in range(d // LANES):
            ks = slice(k * LANES, (k + 1) * LANES)
            xe = xr_s[k]
            z = cb_ref[:, ks]
            tail = cb_ref[:, ks]
            for j in range(CONV_W):
                z = z + cw_ref[j:j + 1, ks] * xe[j * n_b:(j + ts) * n_b]
                if j < CONV_W - 1:
                    tail = tail + cw_ref[j:j + 1, ks] * xe[(ts + j) * n_b:(ts + j + 1) * n_b]
            z_ref[k] = z
            ztail_ref[k] = tail
            xr_s[k, :hist, :] = xe[ts * n_b:]

    for idx, seg in sorted(enumerate(segs), key=lambda p: p[1] == 0):
        for bs, u in zip(groups, us):
            acc = jnp.dot(u, w_ref[:, seg * d:(seg + 1) * d], preferred_element_type=F32)
            if seg == 0:
                for b in range(group):
                    for k in range(d // LANES):
                        xr_s[k, pl.ds(hist + bs.start + b, ts, stride=n_b), :] = (
                            acc[b * ts:(b + 1) * ts, k * LANES:(k + 1) * LANES])
                continue
            if seg == 1:
                val = jax.nn.gelu(acc)
            elif seg == 2:
                val = acc * (HEAD_DIM ** -0.5 * LOG2E)
            elif seg in (3, 4):
                val = acc
            else:
                val = jax.nn.sigmoid(acc + bg_ref[:, (seg - 5) * d:(seg - 4) * d])
            pr_ref[bs, :, (idx - 1) * d:idx * d] = val.astype(BF16).reshape(group, ts, d)
    conv()


def _inproj_call(x3, mod3, mod_row, g0, w_in_bf, b_gate2, conv_w, conv_b, *, ts, segs):
    n_b, seq, d = x3.shape
    assert CONV_W == 4 and CONV_PAD_LEFT == 2
    segs = tuple(segs)
    n_out = (len(segs) - 1) * d
    if mod_row is None:
        mod_blk, mod_idx = (n_b, 1, d), 0
    else:
        mod_blk, mod_idx = (1, 1, d), mod_row
    kern = functools.partial(_inproj_kernel, segs=segs)
    return pl.pallas_call(
        kern,
        grid=(seq // ts,),
        in_specs=[
            pl.BlockSpec((n_b, ts, d), lambda t: (0, t, 0)),
            pl.BlockSpec(mod_blk, lambda t: (mod_idx, 0, 0)),
            pl.BlockSpec(mod_blk, lambda t: (mod_idx, 0, 1)),
            pl.BlockSpec((1, d), lambda t: (0, 0)),
            pl.BlockSpec(w_in_bf.shape, lambda t: (0, 0), pipeline_mode=pl.Buffered(1)),
            pl.BlockSpec(b_gate2.shape, lambda t: (0, 0)),
            pl.BlockSpec(conv_w.shape, lambda t: (0, 0)),
            pl.BlockSpec(conv_b.shape, lambda t: (0, 0)),
        ],
        out_specs=[
            pl.BlockSpec((d // LANES, ts * n_b, LANES), lambda t: (0, t, 0)),
            pl.BlockSpec((d // LANES, n_b, LANES), lambda t: (0, 0, 0)),
            pl.BlockSpec((n_b, ts, n_out), lambda t: (0, t, 0)),
        ],
        out_shape=[
            jax.ShapeDtypeStruct((d // LANES, seq * n_b, LANES), F32),
            jax.ShapeDtypeStruct((d // LANES, n_b, LANES), F32),
            jax.ShapeDtypeStruct((n_b, seq, n_out), BF16),
        ],
        scratch_shapes=[pltpu.VMEM((d // LANES, (ts + CONV_W - 1) * n_b, LANES), F32)],
        compiler_params=pltpu.CompilerParams(
            dimension_semantics=("arbitrary",), vmem_limit_bytes=VMEM_LIMIT),
        name="inproj",
    )(x3, mod3, mod3, g0, w_in_bf, b_gate2, conv_w, conv_b)


def _scan_chunk(z_ref, znext_ref, ztail_ref, wbd_ref, brg_ref, lam_ref, h_ref, hfin_ref, hst, *, is_last, tc,
                reverse):
    n_lc = hst.shape[0]
    per = MXU_DIM // LANES
    for c in range(n_lc // per):
        xc2 = jnp.concatenate(
            [jnp.concatenate([z_ref[k, SUBLANES:], jnp.where(is_last, ztail_ref[k], znext_ref[k])], axis=0)
             for k in range(c * per, (c + 1) * per)], axis=1)
        sl = slice(c * MXU_DIM, (c + 1) * MXU_DIM)
        g = jnp.dot(xc2.astype(BF16), wbd_ref[0, c], preferred_element_type=F32) + brg_ref[0, c]
        r2 = jnp.tanh(g[:, :MXU_DIM]) + 1.0
        i2 = jnp.tanh(g[:, MXU_DIM:]) + 1.0
        z = -lam_ref[0, :, sl]
        softplus = jnp.maximum(z, 0.0) + jnp.log1p(jnp.exp(-jnp.abs(z)))
        c_nat = (0.5 * LRU_C) * softplus
        neg_log_a = c_nat * r2
        a = jnp.exp2((-LOG2E * c_nat) * r2)
        one_minus_a2 = jnp.tanh(neg_log_a) * (a * a + 1.0)
        root = jnp.where(one_minus_a2 > 0.0, one_minus_a2 * lax.rsqrt(one_minus_a2), 0.0)
        bb = root * (i2 * xc2)
        for kk in range(per):
            k = c * per + kk
            lanes = slice(kk * LANES, (kk + 1) * LANES)
            h = hst[k]
            for j in range(tc):
                t = tc - 1 - j if reverse else j
                rs = slice(t * SUBLANES, (t + 1) * SUBLANES)
                h = a[rs, lanes] * h + bb[rs, lanes]
                h_ref[0, k, rs, :] = h
            hst[k] = h
            hfin_ref[0, k] = h


def _scan_kernel(z_ref, znext_ref, ztail_ref, wbd_ref, brg_ref, lam_ref, h0_ref, h_ref, hfin_ref, hst,
                 *, n_chunks, tc):
    d = pl.program_id(0)
    i = pl.program_id(1)
    ci = jnp.where(d == 0, i, n_chunks - 1 - i)
    is_last = ci == n_chunks - 1

    @pl.when(i == 0)
    def _():
        hst[...] = h0_ref[d]

    for reverse in (False, True):
        @pl.when(d == int(reverse))
        def _(reverse=reverse):
            _scan_chunk(z_ref, znext_ref, ztail_ref, wbd_ref, brg_ref, lam_ref, h_ref, hfin_ref, hst,
                        is_last=is_last, tc=tc, reverse=reverse)


def _scan_call(z, ztail, wbd, brg, lam3, h0, *, tc):
    n_lc, n_rows, _ = z.shape
    n_b = SUBLANES
    seq = n_rows // n_b
    d = n_lc * LANES
    n_chunks = seq // tc
    n_col = d // MXU_DIM
    rows = tc * n_b

    def chunk(dd, i):
        return jnp.where(dd == 0, i, n_chunks - 1 - i)

    kern = functools.partial(_scan_kernel, n_chunks=n_chunks, tc=tc)
    return pl.pallas_call(
        kern,
        grid=(2, n_chunks),
        in_specs=[
            pl.BlockSpec((n_lc, rows, LANES), lambda dd, i: (0, chunk(dd, i), 0)),
            pl.BlockSpec((n_lc, n_b, LANES),
                         lambda dd, i: (0, jnp.minimum((chunk(dd, i) + 1) * tc, seq - 1), 0)),
            pl.BlockSpec((n_lc, n_b, LANES), lambda dd, i: (0, 0, 0)),
            pl.BlockSpec((1, n_col, MXU_DIM, 2 * MXU_DIM), lambda dd, i: (dd, 0, 0, 0)),
            pl.BlockSpec((1, n_col, 1, 2 * MXU_DIM), lambda dd, i: (dd, 0, 0, 0)),
            pl.BlockSpec((1, 1, d), lambda dd, i: (dd, 0, 0)),
            pl.BlockSpec((2, n_lc, n_b, LANES), lambda dd, i: (0, 0, 0, 0)),
        ],
        out_specs=[
            pl.BlockSpec((1, n_lc, rows, LANES), lambda dd, i: (dd, 0, chunk(dd, i), 0)),
            pl.BlockSpec((1, n_lc, n_b, LANES), lambda dd, i: (dd, 0, 0, 0)),
        ],
        out_shape=[
            jax.ShapeDtypeStruct((2, n_lc, n_rows, LANES), F32),
            jax.ShapeDtypeStruct((2, n_lc, n_b, LANES), F32),
        ],
        scratch_shapes=[pltpu.VMEM((n_lc, n_b, LANES), F32)],
        compiler_params=pltpu.CompilerParams(
            dimension_semantics=("arbitrary", "arbitrary"), vmem_limit_bytes=VMEM_LIMIT),
        name="scan",
    )(z, z, ztail, wbd, brg, lam3, h0)


def _window_plan(i, n_i, rows):
    wr = min(WIN_ROWS, rows)
    rng = []
    for rq in range(ATT_ROWS):
        r = ATT_ROWS * i + rq
        r0 = min(max(r - wr // 2, 0), rows - wr)
        lo = r0 - (ATT_ROWS * i - ATT_ROWS)
        rng.append((lo, lo + wr))
    s_lo = min(lo for lo, _ in rng)
    s_hi = max(hi for _, hi in rng)
    assert 0 <= s_lo and s_hi <= ATT_SLOTS
    rows_per_tile = LANES // GRID_W
    tiles = []
    for j in range(ATT_ROWS // rows_per_tile):
        rqs = range(j * rows_per_tile, (j + 1) * rows_per_tile)
        a = min(rng[rq][0] for rq in rqs)
        b = max(rng[rq][1] for rq in rqs)
        half = {}
        for s in range(a, b):
            ok = tuple(rng[rq][0] <= s < rng[rq][1] for rq in rqs)
            if not all(ok):
                half[s - s_lo] = ok
        tiles.append((a - s_lo, b - s_lo, half))
    return s_lo, s_hi, tiles


def _attn_block(q_ref, k_refs, v_refs, kc_ref, vc_ref, bias_ref, o_ref, s_buf, plan):
    s_lo, s_hi, tiles = plan
    n_slots = s_hi - s_lo
    n_loc = n_slots * GRID_W
    n_ctx = kc_ref.shape[0]
    lane = lax.broadcasted_iota(jnp.int32, (1, LANES), 1)
    lo = lane < HEAD_DIM
    heads_per_pair = LANES // HEAD_DIM
    n_pairs = N_HEADS // heads_per_pair
    dims_nt = (((1,), (1,)), ((), ()))
    dims_tn = (((0,), (0,)), ((), ()))

    def keys_of(refs, c_ref, ls):
        parts = []
        for blk, ref in enumerate(refs):
            a = max(s_lo, blk * ATT_ROWS) - blk * ATT_ROWS
            b = min(s_hi, (blk + 1) * ATT_ROWS) - blk * ATT_ROWS
            if b > a:
                parts.append(ref[a * GRID_W:b * GRID_W, ls])
        return jnp.concatenate(parts + [c_ref[:, ls]], axis=0)

    def scores(p):
        ls = slice(p * LANES, (p + 1) * LANES)
        qp = q_ref[:, ls]
        k_all = keys_of(k_refs, kc_ref, ls)
        for hh in range(heads_per_pair):
            msk = lo if hh == 0 else jnp.logical_not(lo)
            qm = jnp.where(msk, qp, jnp.zeros_like(qp))
            s_buf[p % 2, hh, :n_loc + n_ctx] = lax.dot_general(k_all, qm, dims_nt, preferred_element_type=F32)

    def softmax_pv(p):
        ls = slice(p * LANES, (p + 1) * LANES)
        v_all = keys_of(v_refs, vc_ref, ls)
        o_t = []
        for hh in range(heads_per_pair):
            own = lo if hh == 0 else jnp.logical_not(lo)
            head = p * heads_per_pair + hh
            cols = []
            for j, (a, b, half) in enumerate(tiles):
                qs = slice(j * LANES, (j + 1) * LANES)
                blocks = []
                for s in range(a, b):
                    blk = (s_buf[p % 2, hh, s * GRID_W:(s + 1) * GRID_W, qs]
                           + bias_ref[head, (s_lo + s) * GRID_W:(s_lo + s + 1) * GRID_W, qs])
                    if s in half:
                        blk = jnp.where(lo if half[s][0] else jnp.logical_not(lo), blk, NEG)
                    blocks.append(blk)
                s_loc = jnp.concatenate(blocks, axis=0)
                s_ctx = s_buf[p % 2, hh, n_loc:n_loc + n_ctx, qs]
                m = jnp.maximum(jnp.max(s_loc, axis=0, keepdims=True), jnp.max(s_ctx, axis=0, keepdims=True))
                col = [jnp.exp2(s_loc - m), jnp.exp2(s_ctx - m)]
                if a > 0:
                    col.insert(0, jnp.zeros((a * GRID_W, LANES), F32))
                if b < n_slots:
                    col.insert(-1, jnp.zeros(((n_slots - b) * GRID_W, LANES), F32))
                cols.append(jnp.concatenate(col, axis=0))
            p_all = jnp.concatenate(cols, axis=1).astype(BF16)
            v_own = jnp.where(own, v_all, jnp.ones_like(v_all))
            o = lax.dot_general(v_own, p_all, dims_tn, preferred_element_type=F32)
            den = o[(1 - hh) * HEAD_DIM:(1 - hh) * HEAD_DIM + 1]
            o_t.append(o[hh * HEAD_DIM:(hh + 1) * HEAD_DIM] * (1.0 / den))
        o_ref[:, ls] = jnp.concatenate(o_t, axis=0).T.astype(BF16)

    scores(0)
    for p in range(n_pairs):
        if p + 1 < n_pairs:
            scores(p + 1)
        softmax_pv(p)


def _attn_kernel(q_ref, kp_ref, km_ref, kn_ref, vp_ref, vm_ref, vn_ref, kc_ref, vc_ref, bias_ref, o_ref, s_buf,
                 *, plans):
    i = pl.program_id(0)
    n_i = pl.num_programs(0)
    variant = (i > 0).astype(jnp.int32) + (i == n_i - 1).astype(jnp.int32)
    for v, plan in enumerate(plans):
        @pl.when(variant == v)
        def _(plan=plan):
            _attn_block(q_ref, (kp_ref, km_ref, kn_ref), (vp_ref, vm_ref, vn_ref), kc_ref, vc_ref,
                        bias_ref, o_ref, s_buf, plan)


def _attn_call(proj, projc, bias, *, n_b, seq, ctx_len, q_col, k_col, v_col, kc_col, vc_col):
    d = N_HEADS * HEAD_DIM
    n_i = seq // ATT_Q
    rows = seq // GRID_W
    plans = tuple(_window_plan(i, n_i, rows) for i in (0, 1, n_i - 1))

    def row(i, b):
        return b * n_i + i

    def row_prev(i, b):
        return b * n_i + jnp.maximum(i - 1, 0)

    def row_next(i, b):
        return b * n_i + jnp.minimum(i + 1, n_i - 1)

    blk = (ATT_Q, d)
    return pl.pallas_call(
        functools.partial(_attn_kernel, plans=plans),
        grid=(n_i, n_b),
        in_specs=[
            pl.BlockSpec(blk, lambda i, b: (row(i, b), q_col)),
            pl.BlockSpec(blk, lambda i, b: (row_prev(i, b), k_col)),
            pl.BlockSpec(blk, lambda i, b: (row(i, b), k_col)),
            pl.BlockSpec(blk, lambda i, b: (row_next(i, b), k_col)),
            pl.BlockSpec(blk, lambda i, b: (row_prev(i, b), v_col)),
            pl.BlockSpec(blk, lambda i, b: (row(i, b), v_col)),
            pl.BlockSpec(blk, lambda i, b: (row_next(i, b), v_col)),
            pl.BlockSpec((ctx_len, d), lambda i, b: (b, kc_col)),
            pl.BlockSpec((ctx_len, d), lambda i, b: (b, vc_col)),
            pl.BlockSpec(bias.shape, lambda i, b: (0, 0, 0), pipeline_mode=pl.Buffered(1)),
        ],
        out_specs=pl.BlockSpec(blk, lambda i, b: (row(i, b), 0)),
        out_shape=jax.ShapeDtypeStruct((n_b * seq, d), BF16),
        scratch_shapes=[pltpu.VMEM((2, LANES // HEAD_DIM, ATT_SLOTS * GRID_W + ctx_len, ATT_Q), F32)],
        compiler_params=pltpu.CompilerParams(
            dimension_semantics=("arbitrary", "arbitrary"), vmem_limit_bytes=VMEM_LIMIT),
        name="attn",
    )(proj, proj, proj, proj, proj, proj, proj, projc, projc, bias)


def _attn_bias(rpb):
    qc = np.arange(GRID_W)[None, :]
    kc = np.arange(GRID_W)[:, None]
    col_start = np.clip(qc - WIN_COLS // 2, 0, GRID_W - WIN_COLS)
    col_ok = (kc >= col_start) & (kc < col_start + WIN_COLS)
    dc_idx = np.clip(kc - qc + WIN_COLS - 1, 0, 2 * WIN_COLS - 2)
    onehot = (dc_idx[None] == np.arange(2 * WIN_COLS - 1)[:, None, None]) & col_ok[None]
    col_tab = jnp.einsum('had,dkq->hakq', rpb.astype(F32), jnp.asarray(onehot, F32),
                         precision=lax.Precision.HIGHEST)
    col_tab = jnp.where(jnp.asarray(col_ok)[None, None], col_tab * LOG2E, NEG)
    slots = []
    for s in range(ATT_SLOTS):
        q_rows = [col_tab[:, s - ATT_ROWS - rq + WIN_ROWS - 1] for rq in range(ATT_ROWS)]
        slots.append(jnp.concatenate(q_rows, axis=-1))
    return jnp.concatenate(slots, axis=1)


def _out_kernel(hf_ref, hr_ref, ggr_ref, gl_ref, gna_ref, att_ref, x_ref, gt1_ref, sh2_ref, sc2_ref,
                gt2_ref, gn_ref, wl_ref, wn_ref, wo_ref, w1_ref, w2_ref, out_ref):
    n_b, ts, d = x_ref.shape
    group = n_b // OUT_GROUPS
    rows = group * ts
    groups = [slice(g * group, (g + 1) * group) for g in range(OUT_GROUPS)]

    y_na, y_lru = [], []
    for bs in groups:
        y_na.append(jnp.dot(att_ref[bs].reshape(rows, d), wn_ref[...], preferred_element_type=F32))
        hl = jnp.stack([
            jnp.concatenate([hf_ref[0, k, pl.ds(b, ts, stride=n_b), :] + hr_ref[0, k, pl.ds(b, ts, stride=n_b), :]
                             for k in range(d // LANES)], axis=1)
            for b in range(bs.start, bs.stop)], axis=0)
        hl = hl * ggr_ref[bs].astype(F32)
        y_lru.append(jnp.dot(hl.reshape(rows, d).astype(BF16), wl_ref[...], preferred_element_type=F32))

    ys = []
    for g, bs in enumerate(groups):
        y = (gl_ref[bs].astype(F32).reshape(rows, d) * y_lru[g]
             + gna_ref[bs].astype(F32).reshape(rows, d) * y_na[g])
        ys.append(jnp.dot(y.astype(BF16), wo_ref[...], preferred_element_type=F32).reshape(group, ts, d))

    x1s, u2s = [], []
    for g, bs in enumerate(groups):
        x1 = x_ref[bs] + gt1_ref[bs] * _rms(ys[g], gn_ref[1:2, :])
        u2 = _rms(x1, gn_ref[2:3, :]) * (1.0 + sc2_ref[bs]) + sh2_ref[bs]
        x1s.append(x1)
        u2s.append(u2.reshape(rows, d).astype(BF16))

    ms = [None] * OUT_GROUPS
    for f in range(0, w1_ref.shape[1], FF_CHUNK):
        h1s = []
        for g in range(OUT_GROUPS):
            h1 = jnp.dot(u2s[g], w1_ref[:, f:f + FF_CHUNK], preferred_element_type=F32)
            h1 = jnp.maximum(h1, 0.0)
            h1s.append((h1 * h1).astype(BF16))
        for g in range(OUT_GROUPS):
            part = jnp.dot(h1s[g], w2_ref[f:f + FF_CHUNK, :], preferred_element_type=F32)
            ms[g] = part if ms[g] is None else ms[g] + part

    for g, bs in enumerate(groups):
        out_ref[bs] = x1s[g] + gt2_ref[bs] * _rms(ms[g].reshape(group, ts, d), gn_ref[3:4, :])


def _out_call(h2, proj3, att3, x3, mod3, g_norm, wl, wn, wo, w1, w2, *, ts, ggr_col, gl_col, gna_col):
    n_b, seq, d = x3.shape

    def modspec(k):
        return pl.BlockSpec((n_b, 1, d), lambda t: (0, 0, k))

    def resident(w):
        return pl.BlockSpec(w.shape, lambda t: (0, 0), pipeline_mode=pl.Buffered(1))

    tile = (n_b, ts, d)
    return pl.pallas_call(
        _out_kernel,
        grid=(seq // ts,),
        in_specs=[
            pl.BlockSpec((1, d // LANES, ts * n_b, LANES), lambda t: (0, 0, t, 0)),
            pl.BlockSpec((1, d // LANES, ts * n_b, LANES), lambda t: (1, 0, t, 0)),
            pl.BlockSpec(tile, lambda t: (0, t, ggr_col)),
            pl.BlockSpec(tile, lambda t: (0, t, gl_col)),
            pl.BlockSpec(tile, lambda t: (0, t, gna_col)),
            pl.BlockSpec(tile, lambda t: (0, t, 0)),
            pl.BlockSpec(tile, lambda t: (0, t, 0)),
            modspec(2), modspec(3), modspec(4), modspec(5),
            pl.BlockSpec(g_norm.shape, lambda t: (0, 0)),
            resident(wl), resident(wn), resident(wo), resident(w1), resident(w2),
        ],
        out_specs=pl.BlockSpec(tile, lambda t: (0, t, 0)),
        out_shape=jax.ShapeDtypeStruct((n_b, seq, d), F32),
        compiler_params=pltpu.CompilerParams(
            dimension_semantics=("arbitrary",), vmem_limit_bytes=VMEM_LIMIT),
        name="out",
    )(h2, h2, proj3, proj3, proj3, att3, x3, mod3, mod3, mod3, mod3, g_norm, wl, wn, wo, w1, w2)


def _gate_weights(w_rg, b_rg):
    n_dir, n_gate, n_blk, bw, _ = w_rg.shape
    per = MXU_DIM // bw
    n_col = n_blk // per
    b_rg = 0.5 * b_rg
    w = w_rg.reshape(n_dir, n_gate, n_col, per, bw, bw)
    eye = jnp.eye(per, dtype=w.dtype)
    wbd = jnp.einsum('dgcpij,pq->dgcpiqj', w, eye).reshape(n_dir, n_gate, n_col, MXU_DIM, MXU_DIM)
    wbd = jnp.concatenate([wbd[:, 0], wbd[:, 1]], axis=-1).astype(BF16)
    b = b_rg.reshape(n_dir, n_gate, n_col, MXU_DIM)
    brg = jnp.concatenate([b[:, 0], b[:, 1]], axis=-1)[:, :, None, :]
    return wbd, brg


def kernel(x, c, ctx, c_ctx, w_ada, b_ada, g_norm, w_in, b_gate, conv_w, conv_b, w_rg, b_rg, lam, rpb,
           w_lru_out, w_na_out, w_o, w_mlp1, w_mlp2):
    n_b, seq, d = x.shape
    ctx_len = ctx.shape[1]
    rows = seq // GRID_W
    depth = w_ada.shape[0]
    assert depth == 1 and rows % ATT_ROWS == 0 and n_b == SUBLANES

    l = 0
    cs = jnp.concatenate([c, c_ctx[None], jnp.zeros((2 * SUBLANES - n_b - 1, d), F32)], axis=0)
    mod = _mod_call(cs, w_ada[l], b_ada[l][None])
    mod3 = mod[:, None, :]

    w_in_bf = w_in[l].astype(BF16)
    b_gate2 = b_gate[l][None]
    g0 = g_norm[l, 0:1]

    cw, cb = 0.5 * conv_w[l], 0.5 * conv_b[l][None]
    z, ztail, proj3 = _inproj_call(x, mod3, None, g0, w_in_bf, b_gate2, cw, cb, ts=INPROJ_TS,
                                   segs=(0, 1, 2, 3, 4, 5, 6))
    z_c, ztail_c, projc3 = _inproj_call(ctx, mod3, n_b, g0, w_in_bf, b_gate2, cw, cb, ts=INPROJ_TS,
                                        segs=(0, 3, 4))

    wbd, brg = _gate_weights(w_rg[l], b_rg[l])
    lam3 = lam[l][:, None, :]
    _, h0 = _scan_call(z_c, ztail_c, wbd, brg, lam3, jnp.zeros((2, d // LANES, n_b, LANES), F32), tc=SCAN_TC)
    h, _ = _scan_call(z, ztail, wbd, brg, lam3, h0, tc=SCAN_TC)

    bias = _attn_bias(rpb[l])
    att = _attn_call(proj3.reshape(n_b * seq, -1), projc3.reshape(n_b * ctx_len, -1), bias,
                     n_b=n_b, seq=seq, ctx_len=ctx_len, q_col=1, k_col=2, v_col=3, kc_col=0, vc_col=1)

    return _out_call(h, proj3, att.reshape(n_b, seq, d), x, mod3, g_norm[l],
                     w_lru_out[l].astype(BF16), w_na_out[l].astype(BF16), w_o[l].astype(BF16),
                     w_mlp1[l].astype(BF16), w_mlp2[l].astype(BF16),
                     ts=OUT_TS, ggr_col=0, gl_col=4, gna_col=5)
```

```python
import functools

import numpy as np
import jax
import jax.numpy as jnp
from jax import lax
from jax.experimental import pallas as pl
from jax.experimental.pallas import tpu as pltpu

F32 = jnp.float32
BF16 = jnp.bfloat16

EPS = 1e-6
NEG = -1e30
LRU_C = 8.0
N_HEADS = 16
HEAD_DIM = 64
GRID_W = 64
WIN_ROWS = 8
WIN_COLS = 16
CONV_W = 4
CONV_PAD_LEFT = 2

LANES = 128
SUBLANES = 8
MXU_DIM = 256
VMEM_LIMIT = 56 * 1024 * 1024

INPROJ_TS = 64
SCAN_TC = 256
OUT_TS = 64

ATT_ROWS = 4
ATT_Q = ATT_ROWS * GRID_W
ATT_SLOTS = 3 * ATT_ROWS
INPROJ_GROUPS = 2
OUT_GROUPS = 2
FF_CHUNK = 1024
LOG2E = 1.4426950408889634


def _rms(x, g):
    return x * lax.rsqrt(jnp.mean(x * x, axis=-1, keepdims=True) + EPS) * g


def _mod_kernel(c_ref, w_ref, b_ref, o_ref):
    c = c_ref[...]
    s = c * jax.nn.sigmoid(c)
    o_ref[...] = jnp.dot(s, w_ref[...], preferred_element_type=F32) + b_ref[...]


def _mod_call(cs, w_ada, b_ada):
    rows, d = cs.shape
    n_out = w_ada.shape[1]
    return pl.pallas_call(
        _mod_kernel,
        grid=(n_out // d,),
        in_specs=[
            pl.BlockSpec((rows, d), lambda n: (0, 0)),
            pl.BlockSpec((d, d), lambda n: (0, n)),
            pl.BlockSpec((1, d), lambda n: (0, n)),
        ],
        out_specs=pl.BlockSpec((rows, d), lambda n: (0, n)),
        out_shape=jax.ShapeDtypeStruct((rows, n_out), F32),
        compiler_params=pltpu.CompilerParams(
            dimension_semantics=("arbitrary",), vmem_limit_bytes=VMEM_LIMIT),
        name="mod",
    )(cs, w_ada, b_ada)


def _inproj_kernel(x_ref, sh_ref, sc_ref, g_ref, w_ref, bg_ref, cw_ref, cb_ref, z_ref, ztail_ref, pr_ref,
                   xr_s, *, segs):
    n_b, ts, d = x_ref.shape
    hist = (CONV_W - 1) * n_b
    group = n_b // INPROJ_GROUPS
    groups = [slice(g * group, (g + 1) * group) for g in range(INPROJ_GROUPS)]

    @pl.when(pl.program_id(0) == 0)
    def _():
        xr_s[:, :hist, :] = jnp.zeros((xr_s.shape[0], hist, LANES), F32)

    def per_sample(ref, bs):
        return ref[bs] if ref.shape[0] == n_b else ref[...]

    us = []
    for bs in groups:
        u = _rms(x_ref[bs], g_ref[...]) * (1.0 + per_sample(sc_ref, bs)) + per_sample(sh_ref, bs)
        us.append(u.reshape(group * ts, d).astype(BF16))
    def conv():
        for k in range(d // LANES):
            ks = slice(k * LANES, (k + 1) * LANES)
            xe = xr_s[k]
            z = cb_ref[:, ks]
            tail = cb_ref[:, ks]
            for j in range(CONV_W):
                z = z + cw_ref[j:j + 1, ks] * xe[j * n_b:(j + ts) * n_b]
                if j < CONV_W - 1:
                    tail = tail + cw_ref[j:j + 1, ks] * xe[(ts + j) * n_b:(ts + j + 1) * n_b]
            z_ref[k] = z
            ztail_ref[k] = tail
            xr_s[k, :hist, :] = xe[ts * n_b:]

    for idx, seg in sorted(enumerate(segs), key=lambda p: p[1] == 0):
        for bs, u in zip(groups, us):
            acc = jnp.dot(u, w_ref[:, seg * d:(seg + 1) * d], preferred_element_type=F32)
            if seg == 0:
                for b in range(group):
                    for k in range(d // LANES):
                        xr_s[k, pl.ds(hist + bs.start + b, ts, stride=n_b), :] = (
                            acc[b * ts:(b + 1) * ts, k * LANES:(k + 1) * LANES])
                continue
            if seg == 1:
                val = jax.nn.gelu(acc)
            elif seg == 2:
                val = acc * (HEAD_DIM ** -0.5 * LOG2E)
            elif seg in (3, 4):
                val = acc
            else:
                val = jax.nn.sigmoid(acc + bg_ref[:, (seg - 5) * d:(seg - 4) * d])
            pr_ref[bs, :, (idx - 1) * d:idx * d] = val.astype(BF16).reshape(group, ts, d)
    conv()


def _inproj_call(x3, mod3, mod_row, g0, w_in_bf, b_gate2, conv_w, conv_b, *, ts, segs):
    n_b, seq, d = x3.shape
    assert CONV_W == 4 and CONV_PAD_LEFT == 2
    segs = tuple(segs)
    n_out = (len(segs) - 1) * d
    if mod_row is None:
        mod_blk, mod_idx = (n_b, 1, d), 0
    else:
        mod_blk, mod_idx = (1, 1, d), mod_row
    kern = functools.partial(_inproj_kernel, segs=segs)
    return pl.pallas_call(
        kern,
        grid=(seq // ts,),
        in_specs=[
            pl.BlockSpec((n_b, ts, d), lambda t: (0, t, 0)),
            pl.BlockSpec(mod_blk, lambda t: (mod_idx, 0, 0)),
            pl.BlockSpec(mod_blk, lambda t: (mod_idx, 0, 1)),
            pl.BlockSpec((1, d), lambda t: (0, 0)),
            pl.BlockSpec(w_in_bf.shape, lambda t: (0, 0), pipeline_mode=pl.Buffered(1)),
            pl.BlockSpec(b_gate2.shape, lambda t: (0, 0)),
            pl.BlockSpec(conv_w.shape, lambda t: (0, 0)),
            pl.BlockSpec(conv_b.shape, lambda t: (0, 0)),
        ],
        out_specs=[
            pl.BlockSpec((d // LANES, ts * n_b, LANES), lambda t: (0, t, 0)),
            pl.BlockSpec((d // LANES, n_b, LANES), lambda t: (0, 0, 0)),
            pl.BlockSpec((n_b, ts, n_out), lambda t: (0, t, 0)),
        ],
        out_shape=[
            jax.ShapeDtypeStruct((d // LANES, seq * n_b, LANES), F32),
            jax.ShapeDtypeStruct((d // LANES, n_b, LANES), F32),
            jax.ShapeDtypeStruct((n_b, seq, n_out), BF16),
        ],
        scratch_shapes=[pltpu.VMEM((d // LANES, (ts + CONV_W - 1) * n_b, LANES), F32)],
        compiler_params=pltpu.CompilerParams(
            dimension_semantics=("arbitrary",), vmem_limit_bytes=VMEM_LIMIT),
        name="inproj",
    )(x3, mod3, mod3, g0, w_in_bf, b_gate2, conv_w, conv_b)


def _scan_chunk(z_ref, znext_ref, ztail_ref, wbd_ref, brg_ref, lam_ref, h_ref, hfin_ref, hst, *, is_last, tc,
                reverse):
    n_lc = hst.shape[0]
    per = MXU_DIM // LANES
    for c in range(n_lc // per):
        xc2 = jnp.concatenate(
            [jnp.concatenate([z_ref[k, SUBLANES:], jnp.where(is_last, ztail_ref[k], znext_ref[k])], axis=0)
             for k in range(c * per, (c + 1) * per)], axis=1)
        sl = slice(c * MXU_DIM, (c + 1) * MXU_DIM)
        g = jnp.dot(xc2.astype(BF16), wbd_ref[0, c], preferred_element_type=F32) + brg_ref[0, c]
        r2 = jnp.tanh(g[:, :MXU_DIM]) + 1.0
        i2 = jnp.tanh(g[:, MXU_DIM:]) + 1.0
        z = -lam_ref[0, :, sl]
        softplus = jnp.maximum(z, 0.0) + jnp.log1p(jnp.exp(-jnp.abs(z)))
        c_nat = (0.5 * LRU_C) * softplus
        neg_log_a = c_nat * r2
        a = jnp.exp2((-LOG2E * c_nat) * r2)
        one_minus_a2 = jnp.tanh(neg_log_a) * (a * a + 1.0)
        root = jnp.where(one_minus_a2 > 0.0, one_minus_a2 * lax.rsqrt(one_minus_a2), 0.0)
        bb = root * (i2 * xc2)
        for kk in range(per):
            k = c * per + kk
            lanes = slice(kk * LANES, (kk + 1) * LANES)
            h = hst[k]
            for j in range(tc):
                t = tc - 1 - j if reverse else j
                rs = slice(t * SUBLANES, (t + 1) * SUBLANES)
                h = a[rs, lanes] * h + bb[rs, lanes]
                h_ref[0, k, rs, :] = h
            hst[k] = h
            hfin_ref[0, k] = h


def _scan_kernel(z_ref, znext_ref, ztail_ref, wbd_ref, brg_ref, lam_ref, h0_ref, h_ref, hfin_ref, hst,
                 *, n_chunks, tc):
    d = pl.program_id(0)
    i = pl.program_id(1)
    ci = jnp.where(d == 0, i, n_chunks - 1 - i)
    is_last = ci == n_chunks - 1

    @pl.when(i == 0)
    def _():
        hst[...] = h0_ref[d]

    for reverse in (False, True):
        @pl.when(d == int(reverse))
        def _(reverse=reverse):
            _scan_chunk(z_ref, znext_ref, ztail_ref, wbd_ref, brg_ref, lam_ref, h_ref, hfin_ref, hst,
                        is_last=is_last, tc=tc, reverse=reverse)


def _scan_call(z, ztail, wbd, brg, lam3, h0, *, tc):
    n_lc, n_rows, _ = z.shape
    n_b = SUBLANES
    seq = n_rows // n_b
    d = n_lc * LANES
    n_chunks = seq // tc
    n_col = d // MXU_DIM
    rows = tc * n_b

    def chunk(dd, i):
        return jnp.where(dd == 0, i, n_chunks - 1 - i)

    kern = functools.partial(_scan_kernel, n_chunks=n_chunks, tc=tc)
    return pl.pallas_call(
        kern,
        grid=(2, n_chunks),
        in_specs=[
            pl.BlockSpec((n_lc, rows, LANES), lambda dd, i: (0, chunk(dd, i), 0)),
            pl.BlockSpec((n_lc, n_b, LANES),
                         lambda dd, i: (0, jnp.minimum((chunk(dd, i) + 1) * tc, seq - 1), 0)),
            pl.BlockSpec((n_lc, n_b, LANES), lambda dd, i: (0, 0, 0)),
            pl.BlockSpec((1, n_col, MXU_DIM, 2 * MXU_DIM), lambda dd, i: (dd, 0, 0, 0)),
            pl.BlockSpec((1, n_col, 1, 2 * MXU_DIM), lambda dd, i: (dd, 0, 0, 0)),
            pl.BlockSpec((1, 1, d), lambda dd, i: (dd, 0, 0)),
            pl.BlockSpec((2, n_lc, n_b, LANES), lambda dd, i: (0, 0, 0, 0)),
        ],
        out_specs=[
            pl.BlockSpec((1, n_lc, rows, LANES), lambda dd, i: (dd, 0, chunk(dd, i), 0)),
            pl.BlockSpec((1, n_lc, n_b, LANES), lambda dd, i: (dd, 0, 0, 0)),
        ],
        out_shape=[
            jax.ShapeDtypeStruct((2, n_lc, n_rows, LANES), F32),
            jax.ShapeDtypeStruct((2, n_lc, n_b, LANES), F32),
        ],
        scratch_shapes=[pltpu.VMEM((n_lc, n_b, LANES), F32)],
        compiler_params=pltpu.CompilerParams(
            dimension_semantics=("arbitrary", "arbitrary"), vmem_limit_bytes=VMEM_LIMIT),
        name="scan",
    )(z, z, ztail, wbd, brg, lam3, h0)


def _window_plan(i, n_i, rows):
    wr = min(WIN_ROWS, rows)
    rng = []
    for rq in range(ATT_ROWS):
        r = ATT_ROWS * i + rq
        r0 = min(max(r - wr // 2, 0), rows - wr)
        lo = r0 - (ATT_ROWS * i - ATT_ROWS)
        rng.append((lo, lo + wr))
    s_lo = min(lo for lo, _ in rng)
    s_hi = max(hi for _, hi in rng)
    assert 0 <= s_lo and s_hi <= ATT_SLOTS
    rows_per_tile = LANES // GRID_W
    tiles = []
    for j in range(ATT_ROWS // rows_per_tile):
        rqs = range(j * rows_per_tile, (j + 1) * rows_per_tile)
        a = min(rng[rq][0] for rq in rqs)
        b = max(rng[rq][1] for rq in rqs)
        half = {}
        for s in range(a, b):
            ok = tuple(rng[rq][0] <= s < rng[rq][1] for rq in rqs)
            if not all(ok):
                half[s - s_lo] = ok
        tiles.append((a - s_lo, b - s_lo, half))
    return s_lo, s_hi, tiles


def _attn_block(q_ref, k_refs, v_refs, kc_ref, vc_ref, bias_ref, o_ref, s_buf, plan):
    s_lo, s_hi, tiles = plan
    n_slots = s_hi - s_lo
    n_loc = n_slots * GRID_W
    n_ctx = kc_ref.shape[0]
    lane = lax.broadcasted_iota(jnp.int32, (1, LANES), 1)
    lo = lane < HEAD_DIM
    heads_per_pair = LANES // HEAD_DIM
    n_pairs = N_HEADS // heads_per_pair
    dims_nt = (((1,), (1,)), ((), ()))
    dims_tn = (((0,), (0,)), ((), ()))

    def keys_of(refs, c_ref, ls):
        parts = []
        for blk, ref in enumerate(refs):
            a = max(s_lo, blk * ATT_ROWS) - blk * ATT_ROWS
            b = min(s_hi, (blk + 1) * ATT_ROWS) - blk * ATT_ROWS
            if b > a:
                parts.append(ref[a * GRID_W:b * GRID_W, ls])
        return jnp.concatenate(parts + [c_ref[:, ls]], axis=0)

    def scores(p):
        ls = slice(p * LANES, (p + 1) * LANES)
        qp = q_ref[:, ls]
        k_all = keys_of(k_refs, kc_ref, ls)
        for hh in range(heads_per_pair):
            msk = lo if hh == 0 else jnp.logical_not(lo)
            qm = jnp.where(msk, qp, jnp.zeros_like(qp))
            s_buf[p % 2, hh, :n_loc + n_ctx] = lax.dot_general(k_all, qm, dims_nt, preferred_element_type=F32)

    def softmax_pv(p):
        ls = slice(p * LANES, (p + 1) * LANES)
        v_all = keys_of(v_refs, vc_ref, ls)
        o_t = []
        for hh in range(heads_per_pair):
            own = lo if hh == 0 else jnp.logical_not(lo)
            head = p * heads_per_pair + hh
            cols = []
            for j, (a, b, half) in enumerate(tiles):
                qs = slice(j * LANES, (j + 1) * LANES)
                blocks = []
                for s in range(a, b):
                    blk = (s_buf[p % 2, hh, s * GRID_W:(s + 1) * GRID_W, qs]
                           + bias_ref[head, (s_lo + s) * GRID_W:(s_lo + s + 1) * GRID_W, qs])
                    if s in half:
                        blk = jnp.where(lo if half[s][0] else jnp.logical_not(lo), blk, NEG)
                    blocks.append(blk)
                s_loc = jnp.concatenate(blocks, axis=0)
                s_ctx = s_buf[p % 2, hh, n_loc:n_loc + n_ctx, qs]
                m = jnp.maximum(jnp.max(s_loc, axis=0, keepdims=True), jnp.max(s_ctx, axis=0, keepdims=True))
                col = [jnp.exp2(s_loc - m), jnp.exp2(s_ctx - m)]
                if a > 0:
                    col.insert(0, jnp.zeros((a * GRID_W, LANES), F32))
                if b < n_slots:
                    col.insert(-1, jnp.zeros(((n_slots - b) * GRID_W, LANES), F32))
                cols.append(jnp.concatenate(col, axis=0))
            p_all = jnp.concatenate(cols, axis=1).astype(BF16)
            v_own = jnp.where(own, v_all, jnp.ones_like(v_all))
            o = lax.dot_general(v_own, p_all, dims_tn, preferred_element_type=F32)
            den = o[(1 - hh) * HEAD_DIM:(1 - hh) * HEAD_DIM + 1]
            o_t.append(o[hh * HEAD_DIM:(hh + 1) * HEAD_DIM] * (1.0 / den))
        o_ref[:, ls] = jnp.concatenate(o_t, axis=0).T.astype(BF16)

    scores(0)
    for p in range(n_pairs):
        if p + 1 < n_pairs:
            scores(p + 1)
        softmax_pv(p)


def _attn_kernel(q_ref, kp_ref, km_ref, kn_ref, vp_ref, vm_ref, vn_ref, kc_ref, vc_ref, bias_ref, o_ref, s_buf,
                 *, plans):
    i = pl.program_id(0)
    n_i = pl.num_programs(0)
    variant = (i > 0).astype(jnp.int32) + (i == n_i - 1).astype(jnp.int32)
    for v, plan in enumerate(plans):
        @pl.when(variant == v)
        def _(plan=plan):
            _attn_block(q_ref, (kp_ref, km_ref, kn_ref), (vp_ref, vm_ref, vn_ref), kc_ref, vc_ref,
                        bias_ref, o_ref, s_buf, plan)


def _attn_call(proj, projc, bias, *, n_b, seq, ctx_len, q_col, k_col, v_col, kc_col, vc_col):
    d = N_HEADS * HEAD_DIM
    n_i = seq // ATT_Q
    rows = seq // GRID_W
    plans = tuple(_window_plan(i, n_i, rows) for i in (0, 1, n_i - 1))

    def row(i, b):
        return b * n_i + i

    def row_prev(i, b):
        return b * n_i + jnp.maximum(i - 1, 0)

    def row_next(i, b):
        return b * n_i + jnp.minimum(i + 1, n_i - 1)

    blk = (ATT_Q, d)
    return pl.pallas_call(
        functools.partial(_attn_kernel, plans=plans),
        grid=(n_i, n_b),
        in_specs=[
            pl.BlockSpec(blk, lambda i, b: (row(i, b), q_col)),
            pl.BlockSpec(blk, lambda i, b: (row_prev(i, b), k_col)),
            pl.BlockSpec(blk, lambda i, b: (row(i, b), k_col)),
            pl.BlockSpec(blk, lambda i, b: (row_next(i, b), k_col)),
            pl.BlockSpec(blk, lambda i, b: (row_prev(i, b), v_col)),
            pl.BlockSpec(blk, lambda i, b: (row(i, b), v_col)),
            pl.BlockSpec(blk, lambda i, b: (row_next(i, b), v_col)),
            pl.BlockSpec((ctx_len, d), lambda i, b: (b, kc_col)),
            pl.BlockSpec((ctx_len, d), lambda i, b: (b, vc_col)),
            pl.BlockSpec(bias.shape, lambda i, b: (0, 0, 0), pipeline_mode=pl.Buffered(1)),
        ],
        out_specs=pl.BlockSpec(blk, lambda i, b: (row(i, b), 0)),
        out_shape=jax.ShapeDtypeStruct((n_b * seq, d), BF16),
        scratch_shapes=[pltpu.VMEM((2, LANES // HEAD_DIM, ATT_SLOTS * GRID_W + ctx_len, ATT_Q), F32)],
        compiler_params=pltpu.CompilerParams(
            dimension_semantics=("arbitrary", "arbitrary"), vmem_limit_bytes=VMEM_LIMIT),
        name="attn",
    )(proj, proj, proj, proj, proj, proj, proj, projc, projc, bias)


def _attn_bias(rpb):
    qc = np.arange(GRID_W)[None, :]
    kc = np.arange(GRID_W)[:, None]
    col_start = np.clip(qc - WIN_COLS // 2, 0, GRID_W - WIN_COLS)
    col_ok = (kc >= col_start) & (kc < col_start + WIN_COLS)
    dc_idx = np.clip(kc - qc + WIN_COLS - 1, 0, 2 * WIN_COLS - 2)
    onehot = (dc_idx[None] == np.arange(2 * WIN_COLS - 1)[:, None, None]) & col_ok[None]
    col_tab = jnp.einsum('had,dkq->hakq', rpb.astype(F32), jnp.asarray(onehot, F32),
                         precision=lax.Precision.HIGHEST)
    col_tab = jnp.where(jnp.asarray(col_ok)[None, None], col_tab * LOG2E, NEG)
    slots = []
    for s in range(ATT_SLOTS):
        q_rows = [col_tab[:, s - ATT_ROWS - rq + WIN_ROWS - 1] for rq in range(ATT_ROWS)]
        slots.append(jnp.concatenate(q_rows, axis=-1))
    return jnp.concatenate(slots, axis=1)


def _out_kernel(hf_ref, hr_ref, ggr_ref, gl_ref, gna_ref, att_ref, x_ref, gt1_ref, sh2_ref, sc2_ref,
                gt2_ref, gn_ref, wl_ref, wn_ref, wo_ref, w1_ref, w2_ref, out_ref):
    n_b, ts, d = x_ref.shape
    group = n_b // OUT_GROUPS
    rows = group * ts
    groups = [slice(g * group, (g + 1) * group) for g in range(OUT_GROUPS)]

    y_na, y_lru = [], []
    for bs in groups:
        y_na.append(jnp.dot(att_ref[bs].reshape(rows, d), wn_ref[...], preferred_element_type=F32))
        hl = jnp.stack([
            jnp.concatenate([hf_ref[0, k, pl.ds(b, ts, stride=n_b), :] + hr_ref[0, k, pl.ds(b, ts, stride=n_b), :]
                             for k in range(d // LANES)], axis=1)
            for b in range(bs.start, bs.stop)], axis=0)
        hl = hl * ggr_ref[bs].astype(F32)
        y_lru.append(jnp.dot(hl.reshape(rows, d).astype(BF16), wl_ref[...], preferred_element_type=F32))

    ys = []
    for g, bs in enumerate(groups):
        y = (gl_ref[bs].astype(F32).reshape(rows, d) * y_lru[g]
             + gna_ref[bs].astype(F32).reshape(rows, d) * y_na[g])
        ys.append(jnp.dot(y.astype(BF16), wo_ref[...], preferred_element_type=F32).reshape(group, ts, d))

    x1s, u2s = [], []
    for g, bs in enumerate(groups):
        x1 = x_ref[bs] + gt1_ref[bs] * _rms(ys[g], gn_ref[1:2, :])
        u2 = _rms(x1, gn_ref[2:3, :]) * (1.0 + sc2_ref[bs]) + sh2_ref[bs]
        x1s.append(x1)
        u2s.append(u2.reshape(rows, d).astype(BF16))

    ms = [None] * OUT_GROUPS
    for f in range(0, w1_ref.shape[1], FF_CHUNK):
        h1s = []
        for g in range(OUT_GROUPS):
            h1 = jnp.dot(u2s[g], w1_ref[:, f:f + FF_CHUNK], preferred_element_type=F32)
            h1 = jnp.maximum(h1, 0.0)
            h1s.append((h1 * h1).astype(BF16))
        for g in range(OUT_GROUPS):
            part = jnp.dot(h1s[g], w2_ref[f:f + FF_CHUNK, :], preferred_element_type=F32)
            ms[g] = part if ms[g] is None else ms[g] + part

    for g, bs in enumerate(groups):
        out_ref[bs] = x1s[g] + gt2_ref[bs] * _rms(ms[g].reshape(group, ts, d), gn_ref[3:4, :])


def _out_call(h2, proj3, att3, x3, mod3, g_norm, wl, wn, wo, w1, w2, *, ts, ggr_col, gl_col, gna_col):
    n_b, seq, d = x3.shape

    def modspec(k):
        return pl.BlockSpec((n_b, 1, d), lambda t: (0, 0, k))

    def resident(w):
        return pl.BlockSpec(w.shape, lambda t: (0, 0), pipeline_mode=pl.Buffered(1))

    tile = (n_b, ts, d)
    return pl.pallas_call(
        _out_kernel,
        grid=(seq // ts,),
        in_specs=[
            pl.BlockSpec((1, d // LANES, ts * n_b, LANES), lambda t: (0, 0, t, 0)),
            pl.BlockSpec((1, d // LANES, ts * n_b, LANES), lambda t: (1, 0, t, 0)),
            pl.BlockSpec(tile, lambda t: (0, t, ggr_col)),
            pl.BlockSpec(tile, lambda t: (0, t, gl_col)),
            pl.BlockSpec(tile, lambda t: (0, t, gna_col)),
            pl.BlockSpec(tile, lambda t: (0, t, 0)),
            pl.BlockSpec(tile, lambda t: (0, t, 0)),
            modspec(2), modspec(3), modspec(4), modspec(5),
            pl.BlockSpec(g_norm.shape, lambda t: (0, 0)),
            resident(wl), resident(wn), resident(wo), resident(w1), resident(w2),
        ],
        out_specs=pl.BlockSpec(tile, lambda t: (0, t, 0)),
        out_shape=jax.ShapeDtypeStruct((n_b, seq, d), F32),
        compiler_params=pltpu.CompilerParams(
            dimension_semantics=("arbitrary",), vmem_limit_bytes=VMEM_LIMIT),
        name="out",
    )(h2, h2, proj3, proj3, proj3, att3, x3, mod3, mod3, mod3, mod3, g_norm, wl, wn, wo, w1, w2)


def _gate_weights(w_rg, b_rg):
    n_dir, n_gate, n_blk, bw, _ = w_rg.shape
    per = MXU_DIM // bw
    n_col = n_blk // per
    b_rg = 0.5 * b_rg
    w = w_rg.reshape(n_dir, n_gate, n_col, per, bw, bw)
    eye = jnp.eye(per, dtype=w.dtype)
    wbd = jnp.einsum('dgcpij,pq->dgcpiqj', w, eye).reshape(n_dir, n_gate, n_col, MXU_DIM, MXU_DIM)
    wbd = jnp.concatenate([wbd[:, 0], wbd[:, 1]], axis=-1).astype(BF16)
    b = b_rg.reshape(n_dir, n_gate, n_col, MXU_DIM)
    brg = jnp.concatenate([b[:, 0], b[:, 1]], axis=-1)[:, :, None, :]
    return wbd, brg


def kernel(x, c, ctx, c_ctx, w_ada, b_ada, g_norm, w_in, b_gate, conv_w, conv_b, w_rg, b_rg, lam, rpb,
           w_lru_out, w_na_out, w_o, w_mlp1, w_mlp2):
    n_b, seq, d = x.shape
    ctx_len = ctx.shape[1]
    rows = seq // GRID_W
    depth = w_ada.shape[0]
    assert depth == 1 and rows % ATT_ROWS == 0 and n_b == SUBLANES

    l = 0
    cs = jnp.concatenate([c, c_ctx[None], jnp.zeros((2 * SUBLANES - n_b - 1, d), F32)], axis=0)
    mod = _mod_call(cs, w_ada[l], b_ada[l][None])
    mod3 = mod[:, None, :]

    w_in_bf = w_in[l].astype(BF16)
    b_gate2 = b_gate[l][None]
    g0 = g_norm[l, 0:1]

    cw, cb = 0.5 * conv_w[l], 0.5 * conv_b[l][None]
    z, ztail, proj3 = _inproj_call(x, mod3, None, g0, w_in_bf, b_gate2, cw, cb, ts=INPROJ_TS,
                                   segs=(0, 1, 2, 3, 4, 5, 6))
    z_c, ztail_c, projc3 = _inproj_call(ctx, mod3, n_b, g0, w_in_bf, b_gate2, cw, cb, ts=INPROJ_TS,
                                        segs=(0, 3, 4))

    wbd, brg = _gate_weights(w_rg[l], b_rg[l])
    lam3 = lam[l][:, None, :]
    _, h0 = _scan_call(z_c, ztail_c, wbd, brg, lam3, jnp.zeros((2, d // LANES, n_b, LANES), F32), tc=SCAN_TC)
    h, _ = _scan_call(z, ztail, wbd, brg, lam3, h0, tc=SCAN_TC)

    bias = _attn_bias(rpb[l])
    att = _attn_call(proj3.reshape(n_b * seq, -1), projc3.reshape(n_b * ctx_len, -1), bias,
                     n_b=n_b, seq=seq, ctx_len=ctx_len, q_col=1, k_col=2, v_col=3, kc_col=0, vc_col=1)

    return _out_call(h, proj3, att.reshape(n_b, seq, d), x, mod3, g_norm[l],
                     w_lru_out[l].astype(BF16), w_na_out[l].astype(BF16), w_o[l].astype(BF16),
                     w_mlp1[l].astype(BF16), w_mlp2[l].astype(BF16),
                     ts=OUT_TS, ggr_col=0, gl_col=4, gna_col=5)
```

```python
import functools

import numpy as np
import jax
import jax.numpy as jnp
from jax import lax
from jax.experimental import pallas as pl
from jax.experimental.pallas import tpu as pltpu

F32 = jnp.float32
BF16 = jnp.bfloat16

EPS = 1e-6
NEG = -1e30
LRU_C = 8.0
N_HEADS = 16
HEAD_DIM = 64
GRID_W = 64
WIN_ROWS = 8
WIN_COLS = 16
CONV_W = 4
CONV_PAD_LEFT = 2

LANES = 128
SUBLANES = 8
MXU_DIM = 256
VMEM_LIMIT = 56 * 1024 * 1024

INPROJ_TS = 64
SCAN_TC = 256
SCAN_TC_CTX = 128
OUT_TS = 64

ATT_ROWS = 4
ATT_Q = ATT_ROWS * GRID_W
ATT_SLOTS = 3 * ATT_ROWS
INPROJ_GROUPS = 2
OUT_GROUPS = 2
FF_CHUNK = 1024
LOG2E = 1.4426950408889634


def _rms(x, g):
    return x * lax.rsqrt(jnp.mean(x * x, axis=-1, keepdims=True) + EPS) * g


def _mod_kernel(c_ref, w_ref, b_ref, o_ref):
    c = c_ref[...]
    s = c * jax.nn.sigmoid(c)
    o_ref[...] = jnp.dot(s, w_ref[...], preferred_element_type=F32) + b_ref[...]


def _mod_call(cs, w_ada, b_ada):
    rows, d = cs.shape
    n_out = w_ada.shape[1]
    return pl.pallas_call(
        _mod_kernel,
        grid=(n_out // d,),
        in_specs=[
            pl.BlockSpec((rows, d), lambda n: (0, 0)),
            pl.BlockSpec((d, d), lambda n: (0, n)),
            pl.BlockSpec((1, d), lambda n: (0, n)),
        ],
        out_specs=pl.BlockSpec((rows, d), lambda n: (0, n)),
        out_shape=jax.ShapeDtypeStruct((rows, n_out), F32),
        compiler_params=pltpu.CompilerParams(
            dimension_semantics=("arbitrary",), vmem_limit_bytes=VMEM_LIMIT),
        name="mod",
    )(cs, w_ada, b_ada)


def _inproj_kernel(x_ref, sh_ref, sc_ref, g_ref, w_ref, bg_ref, cw_ref, cb_ref, z_ref, ztail_ref, pr_ref,
                   xr_s, *, segs):
    n_b, ts, d = x_ref.shape
    hist = (CONV_W - 1) * n_b
    group = n_b // INPROJ_GROUPS
    groups = [slice(g * group, (g + 1) * group) for g in range(INPROJ_GROUPS)]

    @pl.when(pl.program_id(0) == 0)
    def _():
        xr_s[:, :hist, :] = jnp.zeros((xr_s.shape[0], hist, LANES), F32)

    def per_sample(ref, bs):
        return ref[bs] if ref.shape[0] == n_b else ref[...]

    us = []
    for bs in groups:
        u = _rms(x_ref[bs], g_ref[...]) * (1.0 + per_sample(sc_ref, bs)) + per_sample(sh_ref, bs)
        us.append(u.reshape(group * ts, d).astype(BF16))
    def conv():
        for k in range(d // LANES):
            ks = slice(k * LANES, (k + 1) * LANES)
            xe = xr_s[k]
            z = cb_ref[:, ks]
            tail = cb_ref[:, ks]
            for j in range(CONV_W):
                z = z + cw_ref[j:j + 1, ks] * xe[j * n_b:(j + ts) * n_b]
                if j < CONV_W - 1:
                    tail = tail + cw_ref[j:j + 1, ks] * xe[(ts + j) * n_b:(ts + j + 1) * n_b]
            z_ref[k] = z
            ztail_ref[k] = tail
            xr_s[k, :hist, :] = xe[ts * n_b:]

    for idx, seg in sorted(enumerate(segs), key=lambda p: p[1] == 0):
        for bs, u in zip(groups, us):
            acc = jnp.dot(u, w_ref[:, seg * d:(seg + 1) * d], preferred_element_type=F32)
            if seg == 0:
                for b in range(group):
                    for k in range(d // LANES):
                        xr_s[k, pl.ds(hist + bs.start + b, ts, stride=n_b), :] = (
                            acc[b * ts:(b + 1) * ts, k * LANES:(k + 1) * LANES])
                continue
            if seg == 1:
                val = jax.nn.gelu(acc)
            elif seg == 2:
                val = acc * (HEAD_DIM ** -0.5 * LOG2E)
            elif seg in (3, 4):
                val = acc
            else:
                val = jax.nn.sigmoid(acc + bg_ref[:, (seg - 5) * d:(seg - 4) * d])
            pr_ref[bs, :, (idx - 1) * d:idx * d] = val.astype(BF16).reshape(group, ts, d)
    conv()


def _inproj_call(x3, mod3, mod_row, g0, w_in_bf, b_gate2, conv_w, conv_b, *, ts, segs):
    n_b, seq, d = x3.shape
    assert CONV_W == 4 and CONV_PAD_LEFT == 2
    segs = tuple(segs)
    n_out = (len(segs) - 1) * d
    if mod_row is None:
        mod_blk, mod_idx = (n_b, 1, d), 0
    else:
        mod_blk, mod_idx = (1, 1, d), mod_row
    kern = functools.partial(_inproj_kernel, segs=segs)
    return pl.pallas_call(
        kern,
        grid=(seq // ts,),
        in_specs=[
            pl.BlockSpec((n_b, ts, d), lambda t: (0, t, 0)),
            pl.BlockSpec(mod_blk, lambda t: (mod_idx, 0, 0)),
            pl.BlockSpec(mod_blk, lambda t: (mod_idx, 0, 1)),
            pl.BlockSpec((1, d), lambda t: (0, 0)),
            pl.BlockSpec(w_in_bf.shape, lambda t: (0, 0), pipeline_mode=pl.Buffered(1)),
            pl.BlockSpec(b_gate2.shape, lambda t: (0, 0)),
            pl.BlockSpec(conv_w.shape, lambda t: (0, 0)),
            pl.BlockSpec(conv_b.shape, lambda t: (0, 0)),
        ],
        out_specs=[
            pl.BlockSpec((d // LANES, ts * n_b, LANES), lambda t: (0, t, 0)),
            pl.BlockSpec((d // LANES, n_b, LANES), lambda t: (0, 0, 0)),
            pl.BlockSpec((n_b, ts, n_out), lambda t: (0, t, 0)),
        ],
        out_shape=[
            jax.ShapeDtypeStruct((d // LANES, seq * n_b, LANES), F32),
            jax.ShapeDtypeStruct((d // LANES, n_b, LANES), F32),
            jax.ShapeDtypeStruct((n_b, seq, n_out), BF16),
        ],
        scratch_shapes=[pltpu.VMEM((d // LANES, (ts + CONV_W - 1) * n_b, LANES), F32)],
        compiler_params=pltpu.CompilerParams(
            dimension_semantics=("arbitrary",), vmem_limit_bytes=VMEM_LIMIT),
        name="inproj",
    )(x3, mod3, mod3, g0, w_in_bf, b_gate2, conv_w, conv_b)


def _scan_chunk(z_ref, znext_ref, ztail_ref, wbd_ref, brg_ref, lam_ref, h_ref, hfin_ref, hst, *, is_last, tc,
                reverse):
    n_lc = hst.shape[0]
    per = MXU_DIM // LANES
    for c in range(n_lc // per):
        xc2 = jnp.concatenate(
            [jnp.concatenate([z_ref[k, SUBLANES:], jnp.where(is_last, ztail_ref[k], znext_ref[k])], axis=0)
             for k in range(c * per, (c + 1) * per)], axis=1)
        sl = slice(c * MXU_DIM, (c + 1) * MXU_DIM)
        g = jnp.dot(xc2.astype(BF16), wbd_ref[0, c], preferred_element_type=F32) + brg_ref[0, c]
        r2 = jnp.tanh(g[:, :MXU_DIM]) + 1.0
        i2 = jnp.tanh(g[:, MXU_DIM:]) + 1.0
        z = -lam_ref[0, :, sl]
        softplus = jnp.maximum(z, 0.0) + jnp.log1p(jnp.exp(-jnp.abs(z)))
        c_nat = (0.5 * LRU_C) * softplus
        neg_log_a = c_nat * r2
        a = jnp.exp2((-LOG2E * c_nat) * r2)
        one_minus_a2 = jnp.tanh(neg_log_a) * (a * a + 1.0)
        root = jnp.where(one_minus_a2 > 0.0, one_minus_a2 * lax.rsqrt(one_minus_a2), 0.0)
        bb = root * (i2 * xc2)
        for kk in range(per):
            k = c * per + kk
            lanes = slice(kk * LANES, (kk + 1) * LANES)
            h = hst[k]
            for j in range(tc):
                t = tc - 1 - j if reverse else j
                rs = slice(t * SUBLANES, (t + 1) * SUBLANES)
                h = a[rs, lanes] * h + bb[rs, lanes]
                h_ref[0, k, rs, :] = h
            hst[k] = h
            hfin_ref[0, k] = h


def _scan_kernel(z_ref, znext_ref, ztail_ref, wbd_ref, brg_ref, lam_ref, h0_ref, h_ref, hfin_ref, hst,
                 *, n_chunks, tc):
    d = pl.program_id(0)
    i = pl.program_id(1)
    ci = jnp.where(d == 0, i, n_chunks - 1 - i)
    is_last = ci == n_chunks - 1

    @pl.when(i == 0)
    def _():
        hst[...] = h0_ref[d]

    for reverse in (False, True):
        @pl.when(d == int(reverse))
        def _(reverse=reverse):
            _scan_chunk(z_ref, znext_ref, ztail_ref, wbd_ref, brg_ref, lam_ref, h_ref, hfin_ref, hst,
                        is_last=is_last, tc=tc, reverse=reverse)


def _scan_call(z, ztail, wbd, brg, lam3, h0, *, tc):
    n_lc, n_rows, _ = z.shape
    n_b = SUBLANES
    seq = n_rows // n_b
    d = n_lc * LANES
    n_chunks = seq // tc
    n_col = d // MXU_DIM
    rows = tc * n_b

    def chunk(dd, i):
        return jnp.where(dd == 0, i, n_chunks - 1 - i)

    kern = functools.partial(_scan_kernel, n_chunks=n_chunks, tc=tc)
    return pl.pallas_call(
        kern,
        grid=(2, n_chunks),
        in_specs=[
            pl.BlockSpec((n_lc, rows, LANES), lambda dd, i: (0, chunk(dd, i), 0)),
            pl.BlockSpec((n_lc, n_b, LANES),
                         lambda dd, i: (0, jnp.minimum((chunk(dd, i) + 1) * tc, seq - 1), 0)),
            pl.BlockSpec((n_lc, n_b, LANES), lambda dd, i: (0, 0, 0)),
            pl.BlockSpec((1, n_col, MXU_DIM, 2 * MXU_DIM), lambda dd, i: (dd, 0, 0, 0)),
            pl.BlockSpec((1, n_col, 1, 2 * MXU_DIM), lambda dd, i: (dd, 0, 0, 0)),
            pl.BlockSpec((1, 1, d), lambda dd, i: (dd, 0, 0)),
            pl.BlockSpec((2, n_lc, n_b, LANES), lambda dd, i: (0, 0, 0, 0)),
        ],
        out_specs=[
            pl.BlockSpec((1, n_lc, rows, LANES), lambda dd, i: (dd, 0, chunk(dd, i), 0)),
            pl.BlockSpec((1, n_lc, n_b, LANES), lambda dd, i: (dd, 0, 0, 0)),
        ],
        out_shape=[
            jax.ShapeDtypeStruct((2, n_lc, n_rows, LANES), F32),
            jax.ShapeDtypeStruct((2, n_lc, n_b, LANES), F32),
        ],
        scratch_shapes=[pltpu.VMEM((n_lc, n_b, LANES), F32)],
        compiler_params=pltpu.CompilerParams(
            dimension_semantics=("arbitrary", "arbitrary"), vmem_limit_bytes=VMEM_LIMIT),
        name="scan",
    )(z, z, ztail, wbd, brg, lam3, h0)


def _window_plan(i, n_i, rows):
    wr = min(WIN_ROWS, rows)
    rng = []
    for rq in range(ATT_ROWS):
        r = ATT_ROWS * i + rq
        r0 = min(max(r - wr // 2, 0), rows - wr)
        lo = r0 - (ATT_ROWS * i - ATT_ROWS)
        rng.append((lo, lo + wr))
    s_lo = min(lo for lo, _ in rng)
    s_hi = max(hi for _, hi in rng)
    assert 0 <= s_lo and s_hi <= ATT_SLOTS
    rows_per_tile = LANES // GRID_W
    tiles = []
    for j in range(ATT_ROWS // rows_per_tile):
        rqs = range(j * rows_per_tile, (j + 1) * rows_per_tile)
        a = min(rng[rq][0] for rq in rqs)
        b = max(rng[rq][1] for rq in rqs)
        half = {}
        for s in range(a, b):
            ok = tuple(rng[rq][0] <= s < rng[rq][1] for rq in rqs)
            if not all(ok):
                half[s - s_lo] = ok
        tiles.append((a - s_lo, b - s_lo, half))
    return s_lo, s_hi, tiles


def _attn_block(q_ref, k_refs, v_refs, kc_ref, vc_ref, bias_ref, o_ref, s_buf, plan):
    s_lo, s_hi, tiles = plan
    n_slots = s_hi - s_lo
    n_loc = n_slots * GRID_W
    n_ctx = kc_ref.shape[0]
    lane = lax.broadcasted_iota(jnp.int32, (1, LANES), 1)
    lo = lane < HEAD_DIM
    heads_per_pair = LANES // HEAD_DIM
    n_pairs = N_HEADS // heads_per_pair
    dims_nt = (((1,), (1,)), ((), ()))
    dims_tn = (((0,), (0,)), ((), ()))

    def keys_of(refs, c_ref, ls):
        parts = []
        for blk, ref in enumerate(refs):
            a = max(s_lo, blk * ATT_ROWS) - blk * ATT_ROWS
            b = min(s_hi, (blk + 1) * ATT_ROWS) - blk * ATT_ROWS
            if b > a:
                parts.append(ref[a * GRID_W:b * GRID_W, ls])
        return jnp.concatenate(parts + [c_ref[:, ls]], axis=0)

    def scores(p):
        ls = slice(p * LANES, (p + 1) * LANES)
        qp = q_ref[:, ls]
        k_all = keys_of(k_refs, kc_ref, ls)
        for hh in range(heads_per_pair):
            msk = lo if hh == 0 else jnp.logical_not(lo)
            qm = jnp.where(msk, qp, jnp.zeros_like(qp))
            s_buf[p % 2, hh, :n_loc + n_ctx] = lax.dot_general(k_all, qm, dims_nt, preferred_element_type=F32)

    def softmax_pv(p):
        ls = slice(p * LANES, (p + 1) * LANES)
        v_all = keys_of(v_refs, vc_ref, ls)
        o_t = []
        for hh in range(heads_per_pair):
            own = lo if hh == 0 else jnp.logical_not(lo)
            head = p * heads_per_pair + hh
            cols = []
            for j, (a, b, half) in enumerate(tiles):
                qs = slice(j * LANES, (j + 1) * LANES)
                blocks = []
                for s in range(a, b):
                    blk = (s_buf[p % 2, hh, s * GRID_W:(s + 1) * GRID_W, qs]
                           + bias_ref[head, (s_lo + s) * GRID_W:(s_lo + s + 1) * GRID_W, qs])
                    if s in half:
                        blk = jnp.where(lo if half[s][0] else jnp.logical_not(lo), blk, NEG)
                    blocks.append(blk)
                s_loc = jnp.concatenate(blocks, axis=0)
                s_ctx = s_buf[p % 2, hh, n_loc:n_loc + n_ctx, qs]
                m = jnp.maximum(jnp.max(s_loc, axis=0, keepdims=True), jnp.max(s_ctx, axis=0, keepdims=True))
                col = [jnp.exp2(s_loc - m), jnp.exp2(s_ctx - m)]
                if a > 0:
                    col.insert(0, jnp.zeros((a * GRID_W, LANES), F32))
                if b < n_slots:
                    col.insert(-1, jnp.zeros(((n_slots - b) * GRID_W, LANES), F32))
                cols.append(jnp.concatenate(col, axis=0))
            p_all = jnp.concatenate(cols, axis=1).astype(BF16)
            v_own = jnp.where(own, v_all, jnp.ones_like(v_all))
            o = lax.dot_general(v_own, p_all, dims_tn, preferred_element_type=F32)
            den = o[(1 - hh) * HEAD_DIM:(1 - hh) * HEAD_DIM + 1]
            o_t.append(o[hh * HEAD_DIM:(hh + 1) * HEAD_DIM] * (1.0 / den))
        o_ref[:, ls] = jnp.concatenate(o_t, axis=0).T.astype(BF16)

    scores(0)
    for p in range(n_pairs):
        if p + 1 < n_pairs:
            scores(p + 1)
        softmax_pv(p)


def _attn_kernel(q_ref, kp_ref, km_ref, kn_ref, vp_ref, vm_ref, vn_ref, kc_ref, vc_ref, bias_ref, o_ref, s_buf,
                 *, plans):
    i = pl.program_id(0)
    n_i = pl.num_programs(0)
    variant = (i > 0).astype(jnp.int32) + (i == n_i - 1).astype(jnp.int32)
    for v, plan in enumerate(plans):
        @pl.when(variant == v)
        def _(plan=plan):
            _attn_block(q_ref, (kp_ref, km_ref, kn_ref), (vp_ref, vm_ref, vn_ref), kc_ref, vc_ref,
                        bias_ref, o_ref, s_buf, plan)


def _attn_call(proj, projc, bias, *, n_b, seq, ctx_len, q_col, k_col, v_col, kc_col, vc_col):
    d = N_HEADS * HEAD_DIM
    n_i = seq // ATT_Q
    rows = seq // GRID_W
    plans = tuple(_window_plan(i, n_i, rows) for i in (0, 1, n_i - 1))

    def row(i, b):
        return b * n_i + i

    def row_prev(i, b):
        return b * n_i + jnp.maximum(i - 1, 0)

    def row_next(i, b):
        return b * n_i + jnp.minimum(i + 1, n_i - 1)

    blk = (ATT_Q, d)
    return pl.pallas_call(
        functools.partial(_attn_kernel, plans=plans),
        grid=(n_i, n_b),
        in_specs=[
            pl.BlockSpec(blk, lambda i, b: (row(i, b), q_col)),
            pl.BlockSpec(blk, lambda i, b: (row_prev(i, b), k_col)),
            pl.BlockSpec(blk, lambda i, b: (row(i, b), k_col)),
            pl.BlockSpec(blk, lambda i, b: (row_next(i, b), k_col)),
            pl.BlockSpec(blk, lambda i, b: (row_prev(i, b), v_col)),
            pl.BlockSpec(blk, lambda i, b: (row(i, b), v_col)),
            pl.BlockSpec(blk, lambda i, b: (row_next(i, b), v_col)),
            pl.BlockSpec((ctx_len, d), lambda i, b: (b, kc_col)),
            pl.BlockSpec((ctx_len, d), lambda i, b: (b, vc_col)),
            pl.BlockSpec(bias.shape, lambda i, b: (0, 0, 0), pipeline_mode=pl.Buffered(1)),
        ],
        out_specs=pl.BlockSpec(blk, lambda i, b: (row(i, b), 0)),
        out_shape=jax.ShapeDtypeStruct((n_b * seq, d), BF16),
        scratch_shapes=[pltpu.VMEM((2, LANES // HEAD_DIM, ATT_SLOTS * GRID_W + ctx_len, ATT_Q), F32)],
        compiler_params=pltpu.CompilerParams(
            dimension_semantics=("arbitrary", "arbitrary"), vmem_limit_bytes=VMEM_LIMIT),
        name="attn",
    )(proj, proj, proj, proj, proj, proj, proj, projc, projc, bias)


def _attn_bias(rpb):
    qc = np.arange(GRID_W)[None, :]
    kc = np.arange(GRID_W)[:, None]
    col_start = np.clip(qc - WIN_COLS // 2, 0, GRID_W - WIN_COLS)
    col_ok = (kc >= col_start) & (kc < col_start + WIN_COLS)
    dc_idx = np.clip(kc - qc + WIN_COLS - 1, 0, 2 * WIN_COLS - 2)
    onehot = (dc_idx[None] == np.arange(2 * WIN_COLS - 1)[:, None, None]) & col_ok[None]
    col_tab = jnp.einsum('had,dkq->hakq', rpb.astype(F32), jnp.asarray(onehot, F32),
                         precision=lax.Precision.HIGHEST)
    col_tab = jnp.where(jnp.asarray(col_ok)[None, None], col_tab * LOG2E, NEG)
    slots = []
    for s in range(ATT_SLOTS):
        q_rows = [col_tab[:, s - ATT_ROWS - rq + WIN_ROWS - 1] for rq in range(ATT_ROWS)]
        slots.append(jnp.concatenate(q_rows, axis=-1))
    return jnp.concatenate(slots, axis=1)


def _out_kernel(hf_ref, hr_ref, ggr_ref, gl_ref, gna_ref, att_ref, x_ref, gt1_ref, sh2_ref, sc2_ref,
                gt2_ref, gn_ref, wl_ref, wn_ref, wo_ref, w1_ref, w2_ref, out_ref):
    n_b, ts, d = x_ref.shape
    group = n_b // OUT_GROUPS
    rows = group * ts
    groups = [slice(g * group, (g + 1) * group) for g in range(OUT_GROUPS)]

    y_na, y_lru = [], []
    for bs in groups:
        y_na.append(jnp.dot(att_ref[bs].reshape(rows, d), wn_ref[...], preferred_element_type=F32))
        hl = jnp.stack([
            jnp.concatenate([hf_ref[0, k, pl.ds(b, ts, stride=n_b), :] + hr_ref[0, k, pl.ds(b, ts, stride=n_b), :]
                             for k in range(d // LANES)], axis=1)
            for b in range(bs.start, bs.stop)], axis=0)
        hl = hl * ggr_ref[bs].astype(F32)
        y_lru.append(jnp.dot(hl.reshape(rows, d).astype(BF16), wl_ref[...], preferred_element_type=F32))

    ys = []
    for g, bs in enumerate(groups):
        y = (gl_ref[bs].astype(F32).reshape(rows, d) * y_lru[g]
             + gna_ref[bs].astype(F32).reshape(rows, d) * y_na[g])
        ys.append(jnp.dot(y.astype(BF16), wo_ref[...], preferred_element_type=F32).reshape(group, ts, d))

    x1s, u2s = [], []
    for g, bs in enumerate(groups):
        x1 = x_ref[bs] + gt1_ref[bs] * _rms(ys[g], gn_ref[1:2, :])
        u2 = _rms(x1, gn_ref[2:3, :]) * (1.0 + sc2_ref[bs]) + sh2_ref[bs]
        x1s.append(x1)
        u2s.append(u2.reshape(rows, d).astype(BF16))

    ms = [None] * OUT_GROUPS
    for f in range(0, w1_ref.shape[1], FF_CHUNK):
        h1s = []
        for g in range(OUT_GROUPS):
            h1 = jnp.dot(u2s[g], w1_ref[:, f:f + FF_CHUNK], preferred_element_type=F32)
            h1 = jnp.maximum(h1, 0.0)
            h1s.append((h1 * h1).astype(BF16))
        for g in range(OUT_GROUPS):
            part = jnp.dot(h1s[g], w2_ref[f:f + FF_CHUNK, :], preferred_element_type=F32)
            ms[g] = part if ms[g] is None else ms[g] + part

    for g, bs in enumerate(groups):
        out_ref[bs] = x1s[g] + gt2_ref[bs] * _rms(ms[g].reshape(group, ts, d), gn_ref[3:4, :])


def _out_call(h2, proj3, att3, x3, mod3, g_norm, wl, wn, wo, w1, w2, *, ts, ggr_col, gl_col, gna_col):
    n_b, seq, d = x3.shape

    def modspec(k):
        return pl.BlockSpec((n_b, 1, d), lambda t: (0, 0, k))

    def resident(w):
        return pl.BlockSpec(w.shape, lambda t: (0, 0), pipeline_mode=pl.Buffered(1))

    tile = (n_b, ts, d)
    return pl.pallas_call(
        _out_kernel,
        grid=(seq // ts,),
        in_specs=[
            pl.BlockSpec((1, d // LANES, ts * n_b, LANES), lambda t: (0, 0, t, 0)),
            pl.BlockSpec((1, d // LANES, ts * n_b, LANES), lambda t: (1, 0, t, 0)),
            pl.BlockSpec(tile, lambda t: (0, t, ggr_col)),
            pl.BlockSpec(tile, lambda t: (0, t, gl_col)),
            pl.BlockSpec(tile, lambda t: (0, t, gna_col)),
            pl.BlockSpec(tile, lambda t: (0, t, 0)),
            pl.BlockSpec(tile, lambda t: (0, t, 0)),
            modspec(2), modspec(3), modspec(4), modspec(5),
            pl.BlockSpec(g_norm.shape, lambda t: (0, 0)),
            resident(wl), resident(wn), resident(wo), resident(w1), resident(w2),
        ],
        out_specs=pl.BlockSpec(tile, lambda t: (0, t, 0)),
        out_shape=jax.ShapeDtypeStruct((n_b, seq, d), F32),
        compiler_params=pltpu.CompilerParams(
            dimension_semantics=("arbitrary",), vmem_limit_bytes=VMEM_LIMIT),
        name="out",
    )(h2, h2, proj3, proj3, proj3, att3, x3, mod3, mod3, mod3, mod3, g_norm, wl, wn, wo, w1, w2)


def _gate_weights(w_rg, b_rg):
    n_dir, n_gate, n_blk, bw, _ = w_rg.shape
    per = MXU_DIM // bw
    n_col = n_blk // per
    b_rg = 0.5 * b_rg
    w = w_rg.reshape(n_dir, n_gate, n_col, per, bw, bw)
    eye = jnp.eye(per, dtype=w.dtype)
    wbd = jnp.einsum('dgcpij,pq->dgcpiqj', w, eye).reshape(n_dir, n_gate, n_col, MXU_DIM, MXU_DIM)
    wbd = jnp.concatenate([wbd[:, 0], wbd[:, 1]], axis=-1).astype(BF16)
    b = b_rg.reshape(n_dir, n_gate, n_col, MXU_DIM)
    brg = jnp.concatenate([b[:, 0], b[:, 1]], axis=-1)[:, :, None, :]
    return wbd, brg


def kernel(x, c, ctx, c_ctx, w_ada, b_ada, g_norm, w_in, b_gate, conv_w, conv_b, w_rg, b_rg, lam, rpb,
           w_lru_out, w_na_out, w_o, w_mlp1, w_mlp2):
    n_b, seq, d = x.shape
    ctx_len = ctx.shape[1]
    rows = seq // GRID_W
    depth = w_ada.shape[0]
    assert depth == 1 and rows % ATT_ROWS == 0 and n_b == SUBLANES

    l = 0
    cs = jnp.concatenate([c, c_ctx[None], jnp.zeros((2 * SUBLANES - n_b - 1, d), F32)], axis=0)
    mod = _mod_call(cs, w_ada[l], b_ada[l][None])
    mod3 = mod[:, None, :]

    w_in_bf = w_in[l].astype(BF16)
    b_gate2 = b_gate[l][None]
    g0 = g_norm[l, 0:1]

    cw, cb = 0.5 * conv_w[l], 0.5 * conv_b[l][None]
    z, ztail, proj3 = _inproj_call(x, mod3, None, g0, w_in_bf, b_gate2, cw, cb, ts=INPROJ_TS,
                                   segs=(0, 1, 2, 3, 4, 5, 6))
    z_c, ztail_c, projc3 = _inproj_call(ctx, mod3, n_b, g0, w_in_bf, b_gate2, cw, cb, ts=INPROJ_TS,
                                        segs=(0, 3, 4))

    wbd, brg = _gate_weights(w_rg[l], b_rg[l])
    lam3 = lam[l][:, None, :]
    _, h0 = _scan_call(z_c, ztail_c, wbd, brg, lam3, jnp.zeros((2, d // LANES, n_b, LANES), F32),
                       tc=SCAN_TC_CTX)
    h, _ = _scan_call(z, ztail, wbd, brg, lam3, h0, tc=SCAN_TC)

    bias = _attn_bias(rpb[l])
    att = _attn_call(proj3.reshape(n_b * seq, -1), projc3.reshape(n_b * ctx_len, -1), bias,
                     n_b=n_b, seq=seq, ctx_len=ctx_len, q_col=1, k_col=2, v_col=3, kc_col=0, vc_col=1)

    return _out_call(h, proj3, att.reshape(n_b, seq, d), x, mod3, g_norm[l],
                     w_lru_out[l].astype(BF16), w_na_out[l].astype(BF16), w_o[l].astype(BF16),
                     w_mlp1[l].astype(BF16), w_mlp2[l].astype(BF16),
                     ts=OUT_TS, ggr_col=0, gl_col=4, gna_col=5)
```

```python
import functools

import numpy as np
import jax
import jax.numpy as jnp
from jax import lax
from jax.experimental import pallas as pl
from jax.experimental.pallas import tpu as pltpu

F32 = jnp.float32
BF16 = jnp.bfloat16

EPS = 1e-6
NEG = -1e30
LRU_C = 8.0
N_HEADS = 16
HEAD_DIM = 64
GRID_W = 64
WIN_ROWS = 8
WIN_COLS = 16
CONV_W = 4
CONV_PAD_LEFT = 2

LANES = 128
SUBLANES = 8
MXU_DIM = 256
VMEM_LIMIT = 56 * 1024 * 1024

INPROJ_TS = 64
SCAN_TC = 256
OUT_TS = 64

ATT_ROWS = 4
ATT_Q = ATT_ROWS * GRID_W
ATT_SLOTS = 3 * ATT_ROWS
INPROJ_GROUPS = 2
OUT_GROUPS = 2
FF_CHUNK = 1024
LOG2E = 1.4426950408889634


def _rms(x, g):
    return x * lax.rsqrt(jnp.mean(x * x, axis=-1, keepdims=True) + EPS) * g


def _mod_kernel(c_ref, w_ref, b_ref, o_ref):
    c = c_ref[...]
    s = c * jax.nn.sigmoid(c)
    o_ref[...] = jnp.dot(s, w_ref[...], preferred_element_type=F32) + b_ref[...]


def _mod_call(cs, w_ada, b_ada):
    rows, d = cs.shape
    n_out = w_ada.shape[1]
    return pl.pallas_call(
        _mod_kernel,
        grid=(n_out // d,),
        in_specs=[
            pl.BlockSpec((rows, d), lambda n: (0, 0)),
            pl.BlockSpec((d, d), lambda n: (0, n)),
            pl.BlockSpec((1, d), lambda n: (0, n)),
        ],
        out_specs=pl.BlockSpec((rows, d), lambda n: (0, n)),
        out_shape=jax.ShapeDtypeStruct((rows, n_out), F32),
        compiler_params=pltpu.CompilerParams(
            dimension_semantics=("arbitrary",), vmem_limit_bytes=VMEM_LIMIT),
        name="mod",
    )(cs, w_ada, b_ada)


def _inproj_kernel(x_ref, sh_ref, sc_ref, g_ref, w_ref, bg_ref, cw_ref, cb_ref, z_ref, ztail_ref, pr_ref,
                   xr_s, *, segs):
    n_b, ts, d = x_ref.shape
    hist = (CONV_W - 1) * n_b
    group = n_b // INPROJ_GROUPS
    groups = [slice(g * group, (g + 1) * group) for g in range(INPROJ_GROUPS)]

    @pl.when(pl.program_id(0) == 0)
    def _():
        xr_s[:, :hist, :] = jnp.zeros((xr_s.shape[0], hist, LANES), F32)

    def per_sample(ref, bs):
        return ref[bs] if ref.shape[0] == n_b else ref[...]

    us = []
    for bs in groups:
        u = _rms(x_ref[bs], g_ref[...]) * (1.0 + per_sample(sc_ref, bs)) + per_sample(sh_ref, bs)
        us.append(u.reshape(group * ts, d).astype(BF16))
    def conv():
        for k in range(d // LANES):
            ks = slice(k * LANES, (k + 1) * LANES)
            xe = xr_s[k]
            z = cb_ref[:, ks]
            tail = cb_ref[:, ks]
            for j in range(CONV_W):
                z = z + cw_ref[j:j + 1, ks] * xe[j * n_b:(j + ts) * n_b]
                if j < CONV_W - 1:
                    tail = tail + cw_ref[j:j + 1, ks] * xe[(ts + j) * n_b:(ts + j + 1) * n_b]
            z_ref[k] = z
            ztail_ref[k] = tail
            xr_s[k, :hist, :] = xe[ts * n_b:]

    for idx, seg in sorted(enumerate(segs), key=lambda p: p[1] == 0):
        for bs, u in zip(groups, us):
            acc = jnp.dot(u, w_ref[:, seg * d:(seg + 1) * d], preferred_element_type=F32)
            if seg == 0:
                for b in range(group):
                    for k in range(d // LANES):
                        xr_s[k, pl.ds(hist + bs.start + b, ts, stride=n_b), :] = (
                            acc[b * ts:(b + 1) * ts, k * LANES:(k + 1) * LANES])
                continue
            if seg == 1:
                val = jax.nn.gelu(acc)
            elif seg == 2:
                val = acc * (HEAD_DIM ** -0.5 * LOG2E)
            elif seg in (3, 4):
                val = acc
            else:
                val = jax.nn.sigmoid(acc + bg_ref[:, (seg - 5) * d:(seg - 4) * d])
            pr_ref[bs, :, (idx - 1) * d:idx * d] = val.astype(BF16).reshape(group, ts, d)
    conv()


def _inproj_call(x3, mod3, mod_row, g0, w_in_bf, b_gate2, conv_w, conv_b, *, ts, segs):
    n_b, seq, d = x3.shape
    assert CONV_W == 4 and CONV_PAD_LEFT == 2
    segs = tuple(segs)
    n_out = (len(segs) - 1) * d
    if mod_row is None:
        mod_blk, mod_idx = (n_b, 1, d), 0
    else:
        mod_blk, mod_idx = (1, 1, d), mod_row
    kern = functools.partial(_inproj_kernel, segs=segs)
    return pl.pallas_call(
        kern,
        grid=(seq // ts,),
        in_specs=[
            pl.BlockSpec((n_b, ts, d), lambda t: (0, t, 0)),
            pl.BlockSpec(mod_blk, lambda t: (mod_idx, 0, 0)),
            pl.BlockSpec(mod_blk, lambda t: (mod_idx, 0, 1)),
            pl.BlockSpec((1, d), lambda t: (0, 0)),
            pl.BlockSpec(w_in_bf.shape, lambda t: (0, 0), pipeline_mode=pl.Buffered(1)),
            pl.BlockSpec(b_gate2.shape, lambda t: (0, 0)),
            pl.BlockSpec(conv_w.shape, lambda t: (0, 0)),
            pl.BlockSpec(conv_b.shape, lambda t: (0, 0)),
        ],
        out_specs=[
            pl.BlockSpec((d // LANES, ts * n_b, LANES), lambda t: (0, t, 0)),
            pl.BlockSpec((d // LANES, n_b, LANES), lambda t: (0, 0, 0)),
            pl.BlockSpec((n_b, ts, n_out), lambda t: (0, t, 0)),
        ],
        out_shape=[
            jax.ShapeDtypeStruct((d // LANES, seq * n_b, LANES), F32),
            jax.ShapeDtypeStruct((d // LANES, n_b, LANES), F32),
            jax.ShapeDtypeStruct((n_b, seq, n_out), BF16),
        ],
        scratch_shapes=[pltpu.VMEM((d // LANES, (ts + CONV_W - 1) * n_b, LANES), F32)],
        compiler_params=pltpu.CompilerParams(
            dimension_semantics=("arbitrary",), vmem_limit_bytes=VMEM_LIMIT),
        name="inproj",
    )(x3, mod3, mod3, g0, w_in_bf, b_gate2, conv_w, conv_b)


def _scan_chunk(z_ref, znext_ref, ztail_ref, wbd_ref, brg_ref, lam_ref, h_ref, hfin_ref, hst, *, is_last, tc,
                reverse):
    n_lc = hst.shape[0]
    per = MXU_DIM // LANES
    for c in range(n_lc // per):
        xc2 = jnp.concatenate(
            [jnp.concatenate([z_ref[k, SUBLANES:], jnp.where(is_last, ztail_ref[k], znext_ref[k])], axis=0)
             for k in range(c * per, (c + 1) * per)], axis=1)
        sl = slice(c * MXU_DIM, (c + 1) * MXU_DIM)
        g = jnp.dot(xc2.astype(BF16), wbd_ref[0, c], preferred_element_type=F32) + brg_ref[0, c]
        r2 = jnp.tanh(g[:, :MXU_DIM]) + 1.0
        i2 = jnp.tanh(g[:, MXU_DIM:]) + 1.0
        z = -lam_ref[0, :, sl]
        softplus = jnp.maximum(z, 0.0) + jnp.log1p(jnp.exp(-jnp.abs(z)))
        c_nat = (0.5 * LRU_C) * softplus
        neg_log_a = c_nat * r2
        a = jnp.exp2((-LOG2E * c_nat) * r2)
        one_minus_a2 = jnp.tanh(neg_log_a) * (a * a + 1.0)
        root = jnp.where(one_minus_a2 > 0.0, one_minus_a2 * lax.rsqrt(one_minus_a2), 0.0)
        bb = root * (i2 * xc2)
        for kk in range(per):
            k = c * per + kk
            lanes = slice(kk * LANES, (kk + 1) * LANES)
            h = hst[k]
            for j in range(tc):
                t = tc - 1 - j if reverse else j
                rs = slice(t * SUBLANES, (t + 1) * SUBLANES)
                h = a[rs, lanes] * h + bb[rs, lanes]
                h_ref[0, k, rs, :] = h
            hst[k] = h
            hfin_ref[0, k] = h


def _scan_kernel(z_ref, znext_ref, ztail_ref, wbd_ref, brg_ref, lam_ref, h0_ref, h_ref, hfin_ref, hst,
                 *, n_chunks, tc):
    d = pl.program_id(0)
    i = pl.program_id(1)
    ci = jnp.where(d == 0, i, n_chunks - 1 - i)
    is_last = ci == n_chunks - 1

    @pl.when(i == 0)
    def _():
        hst[...] = h0_ref[d]

    for reverse in (False, True):
        @pl.when(d == int(reverse))
        def _(reverse=reverse):
            _scan_chunk(z_ref, znext_ref, ztail_ref, wbd_ref, brg_ref, lam_ref, h_ref, hfin_ref, hst,
                        is_last=is_last, tc=tc, reverse=reverse)


def _scan_call(z, ztail, wbd, brg, lam3, h0, *, tc):
    n_lc, n_rows, _ = z.shape
    n_b = SUBLANES
    seq = n_rows // n_b
    d = n_lc * LANES
    n_chunks = seq // tc
    n_col = d // MXU_DIM
    rows = tc * n_b

    def chunk(dd, i):
        return jnp.where(dd == 0, i, n_chunks - 1 - i)

    kern = functools.partial(_scan_kernel, n_chunks=n_chunks, tc=tc)
    return pl.pallas_call(
        kern,
        grid=(2, n_chunks),
        in_specs=[
            pl.BlockSpec((n_lc, rows, LANES), lambda dd, i: (0, chunk(dd, i), 0)),
            pl.BlockSpec((n_lc, n_b, LANES),
                         lambda dd, i: (0, jnp.minimum((chunk(dd, i) + 1) * tc, seq - 1), 0)),
            pl.BlockSpec((n_lc, n_b, LANES), lambda dd, i: (0, 0, 0)),
            pl.BlockSpec((1, n_col, MXU_DIM, 2 * MXU_DIM), lambda dd, i: (dd, 0, 0, 0)),
            pl.BlockSpec((1, n_col, 1, 2 * MXU_DIM), lambda dd, i: (dd, 0, 0, 0)),
            pl.BlockSpec((1, 1, d), lambda dd, i: (dd, 0, 0)),
            pl.BlockSpec((2, n_lc, n_b, LANES), lambda dd, i: (0, 0, 0, 0)),
        ],
        out_specs=[
            pl.BlockSpec((1, n_lc, rows, LANES), lambda dd, i: (dd, 0, chunk(dd, i), 0)),
            pl.BlockSpec((1, n_lc, n_b, LANES), lambda dd, i: (dd, 0, 0, 0)),
        ],
        out_shape=[
            jax.ShapeDtypeStruct((2, n_lc, n_rows, LANES), F32),
            jax.ShapeDtypeStruct((2, n_lc, n_b, LANES), F32),
        ],
        scratch_shapes=[pltpu.VMEM((n_lc, n_b, LANES), F32)],
        compiler_params=pltpu.CompilerParams(
            dimension_semantics=("arbitrary", "arbitrary"), vmem_limit_bytes=VMEM_LIMIT),
        name="scan",
    )(z, z, ztail, wbd, brg, lam3, h0)


def _window_plan(i, n_i, rows):
    wr = min(WIN_ROWS, rows)
    rng = []
    for rq in range(ATT_ROWS):
        r = ATT_ROWS * i + rq
        r0 = min(max(r - wr // 2, 0), rows - wr)
        lo = r0 - (ATT_ROWS * i - ATT_ROWS)
        rng.append((lo, lo + wr))
    s_lo = min(lo for lo, _ in rng)
    s_hi = max(hi for _, hi in rng)
    assert 0 <= s_lo and s_hi <= ATT_SLOTS
    rows_per_tile = LANES // GRID_W
    tiles = []
    for j in range(ATT_ROWS // rows_per_tile):
        rqs = range(j * rows_per_tile, (j + 1) * rows_per_tile)
        a = min(rng[rq][0] for rq in rqs)
        b = max(rng[rq][1] for rq in rqs)
        half = {}
        for s in range(a, b):
            ok = tuple(rng[rq][0] <= s < rng[rq][1] for rq in rqs)
            if not all(ok):
                half[s - s_lo] = ok
        tiles.append((a - s_lo, b - s_lo, half))
    return s_lo, s_hi, tiles


def _attn_block(q_ref, k_refs, v_refs, kc_ref, vc_ref, bias_ref, o_ref, s_buf, plan):
    s_lo, s_hi, tiles = plan
    n_slots = s_hi - s_lo
    n_loc = n_slots * GRID_W
    n_ctx = kc_ref.shape[0]
    lane = lax.broadcasted_iota(jnp.int32, (1, LANES), 1)
    lo = lane < HEAD_DIM
    heads_per_pair = LANES // HEAD_DIM
    n_pairs = N_HEADS // heads_per_pair
    dims_nt = (((1,), (1,)), ((), ()))
    dims_tn = (((0,), (0,)), ((), ()))

    def keys_of(refs, c_ref, ls):
        parts = []
        for blk, ref in enumerate(refs):
            a = max(s_lo, blk * ATT_ROWS) - blk * ATT_ROWS
            b = min(s_hi, (blk + 1) * ATT_ROWS) - blk * ATT_ROWS
            if b > a:
                parts.append(ref[a * GRID_W:b * GRID_W, ls])
        return jnp.concatenate(parts + [c_ref[:, ls]], axis=0)

    def scores(p):
        ls = slice(p * LANES, (p + 1) * LANES)
        qp = q_ref[:, ls]
        k_all = keys_of(k_refs, kc_ref, ls)
        for hh in range(heads_per_pair):
            msk = lo if hh == 0 else jnp.logical_not(lo)
            qm = jnp.where(msk, qp, jnp.zeros_like(qp))
            s_buf[p % 2, hh, :n_loc + n_ctx] = lax.dot_general(k_all, qm, dims_nt, preferred_element_type=F32)

    def softmax_pv(p):
        ls = slice(p * LANES, (p + 1) * LANES)
        v_all = keys_of(v_refs, vc_ref, ls)
        o_t = []
        for hh in range(heads_per_pair):
            own = lo if hh == 0 else jnp.logical_not(lo)
            head = p * heads_per_pair + hh
            cols = []
            for j, (a, b, half) in enumerate(tiles):
                qs = slice(j * LANES, (j + 1) * LANES)
                blocks = []
                for s in range(a, b):
                    blk = (s_buf[p % 2, hh, s * GRID_W:(s + 1) * GRID_W, qs]
                           + bias_ref[head, (s_lo + s) * GRID_W:(s_lo + s + 1) * GRID_W, qs])
                    if s in half:
                        blk = jnp.where(lo if half[s][0] else jnp.logical_not(lo), blk, NEG)
                    blocks.append(blk)
                s_loc = jnp.concatenate(blocks, axis=0)
                s_ctx = s_buf[p % 2, hh, n_loc:n_loc + n_ctx, qs]
                m = jnp.maximum(jnp.max(s_loc, axis=0, keepdims=True), jnp.max(s_ctx, axis=0, keepdims=True))
                col = [jnp.exp2(s_loc - m), jnp.exp2(s_ctx - m)]
                if a > 0:
                    col.insert(0, jnp.zeros((a * GRID_W, LANES), F32))
                if b < n_slots:
                    col.insert(-1, jnp.zeros(((n_slots - b) * GRID_W, LANES), F32))
                cols.append(jnp.concatenate(col, axis=0))
            p_all = jnp.concatenate(cols, axis=1).astype(BF16)
            v_own = jnp.where(own, v_all, jnp.ones_like(v_all))
            o = lax.dot_general(v_own, p_all, dims_tn, preferred_element_type=F32)
            den = o[(1 - hh) * HEAD_DIM:(1 - hh) * HEAD_DIM + 1]
            o_t.append(o[hh * HEAD_DIM:(hh + 1) * HEAD_DIM] * (1.0 / den))
        o_ref[:, ls] = jnp.concatenate(o_t, axis=0).T.astype(BF16)

    scores(0)
    for p in range(n_pairs):
        if p + 1 < n_pairs:
            scores(p + 1)
        softmax_pv(p)


def _attn_kernel(q_ref, kp_ref, km_ref, kn_ref, vp_ref, vm_ref, vn_ref, kc_ref, vc_ref, bias_ref, o_ref, s_buf,
                 *, plans):
    i = pl.program_id(0)
    n_i = pl.num_programs(0)
    variant = (i > 0).astype(jnp.int32) + (i == n_i - 1).astype(jnp.int32)
    for v, plan in enumerate(plans):
        @pl.when(variant == v)
        def _(plan=plan):
            _attn_block(q_ref, (kp_ref, km_ref, kn_ref), (vp_ref, vm_ref, vn_ref), kc_ref, vc_ref,
                        bias_ref, o_ref, s_buf, plan)


def _attn_call(proj, projc, bias, *, n_b, seq, ctx_len, q_col, k_col, v_col, kc_col, vc_col):
    d = N_HEADS * HEAD_DIM
    n_i = seq // ATT_Q
    rows = seq // GRID_W
    plans = tuple(_window_plan(i, n_i, rows) for i in (0, 1, n_i - 1))

    def row(i, b):
        return b * n_i + i

    def row_prev(i, b):
        return b * n_i + jnp.maximum(i - 1, 0)

    def row_next(i, b):
        return b * n_i + jnp.minimum(i + 1, n_i - 1)

    blk = (ATT_Q, d)
    return pl.pallas_call(
        functools.partial(_attn_kernel, plans=plans),
        grid=(n_i, n_b),
        in_specs=[
            pl.BlockSpec(blk, lambda i, b: (row(i, b), q_col)),
            pl.BlockSpec(blk, lambda i, b: (row_prev(i, b), k_col)),
            pl.BlockSpec(blk, lambda i, b: (row(i, b), k_col)),
            pl.BlockSpec(blk, lambda i, b: (row_next(i, b), k_col)),
            pl.BlockSpec(blk, lambda i, b: (row_prev(i, b), v_col)),
            pl.BlockSpec(blk, lambda i, b: (row(i, b), v_col)),
            pl.BlockSpec(blk, lambda i, b: (row_next(i, b), v_col)),
            pl.BlockSpec((ctx_len, d), lambda i, b: (b, kc_col)),
            pl.BlockSpec((ctx_len, d), lambda i, b: (b, vc_col)),
            pl.BlockSpec(bias.shape, lambda i, b: (0, 0, 0), pipeline_mode=pl.Buffered(1)),
        ],
        out_specs=pl.BlockSpec(blk, lambda i, b: (row(i, b), 0)),
        out_shape=jax.ShapeDtypeStruct((n_b * seq, d), BF16),
        scratch_shapes=[pltpu.VMEM((2, LANES // HEAD_DIM, ATT_SLOTS * GRID_W + ctx_len, ATT_Q), F32)],
        compiler_params=pltpu.CompilerParams(
            dimension_semantics=("arbitrary", "arbitrary"), vmem_limit_bytes=VMEM_LIMIT),
        name="attn",
    )(proj, proj, proj, proj, proj, proj, proj, projc, projc, bias)


def _attn_bias(rpb):
    qc = np.arange(GRID_W)[None, :]
    kc = np.arange(GRID_W)[:, None]
    col_start = np.clip(qc - WIN_COLS // 2, 0, GRID_W - WIN_COLS)
    col_ok = (kc >= col_start) & (kc < col_start + WIN_COLS)
    dc_idx = np.clip(kc - qc + WIN_COLS - 1, 0, 2 * WIN_COLS - 2)
    onehot = (dc_idx[None] == np.arange(2 * WIN_COLS - 1)[:, None, None]) & col_ok[None]
    col_tab = jnp.einsum('had,dkq->hakq', rpb.astype(F32), jnp.asarray(onehot, F32),
                         precision=lax.Precision.HIGHEST)
    col_tab = jnp.where(jnp.asarray(col_ok)[None, None], col_tab * LOG2E, NEG)
    first = WIN_ROWS - 1 - ATT_ROWS
    q_rows = [col_tab[:, first - rq:first - rq + ATT_SLOTS].reshape(rpb.shape[0], ATT_SLOTS * GRID_W, GRID_W)
              for rq in range(ATT_ROWS)]
    return jnp.concatenate(q_rows, axis=-1)


def _out_kernel(hf_ref, hr_ref, ggr_ref, gl_ref, gna_ref, att_ref, x_ref, gt1_ref, sh2_ref, sc2_ref,
                gt2_ref, gn_ref, wl_ref, wn_ref, wo_ref, w1_ref, w2_ref, out_ref):
    n_b, ts, d = x_ref.shape
    group = n_b // OUT_GROUPS
    rows = group * ts
    groups = [slice(g * group, (g + 1) * group) for g in range(OUT_GROUPS)]

    y_na, y_lru = [], []
    for bs in groups:
        y_na.append(jnp.dot(att_ref[bs].reshape(rows, d), wn_ref[...], preferred_element_type=F32))
        hl = jnp.stack([
            jnp.concatenate([hf_ref[0, k, pl.ds(b, ts, stride=n_b), :] + hr_ref[0, k, pl.ds(b, ts, stride=n_b), :]
                             for k in range(d // LANES)], axis=1)
            for b in range(bs.start, bs.stop)], axis=0)
        hl = hl * ggr_ref[bs].astype(F32)
        y_lru.append(jnp.dot(hl.reshape(rows, d).astype(BF16), wl_ref[...], preferred_element_type=F32))

    ys = []
    for g, bs in enumerate(groups):
        y = (gl_ref[bs].astype(F32).reshape(rows, d) * y_lru[g]
             + gna_ref[bs].astype(F32).reshape(rows, d) * y_na[g])
        ys.append(jnp.dot(y.astype(BF16), wo_ref[...], preferred_element_type=F32).reshape(group, ts, d))

    x1s, u2s = [], []
    for g, bs in enumerate(groups):
        x1 = x_ref[bs] + gt1_ref[bs] * _rms(ys[g], gn_ref[1:2, :])
        u2 = _rms(x1, gn_ref[2:3, :]) * (1.0 + sc2_ref[bs]) + sh2_ref[bs]
        x1s.append(x1)
        u2s.append(u2.reshape(rows, d).astype(BF16))

    ms = [None] * OUT_GROUPS
    for f in range(0, w1_ref.shape[1], FF_CHUNK):
        h1s = []
        for g in range(OUT_GROUPS):
            h1 = jnp.dot(u2s[g], w1_ref[:, f:f + FF_CHUNK], preferred_element_type=F32)
            h1 = jnp.maximum(h1, 0.0)
            h1s.append((h1 * h1).astype(BF16))
        for g in range(OUT_GROUPS):
            part = jnp.dot(h1s[g], w2_ref[f:f + FF_CHUNK, :], preferred_element_type=F32)
            ms[g] = part if ms[g] is None else ms[g] + part

    for g, bs in enumerate(groups):
        out_ref[bs] = x1s[g] + gt2_ref[bs] * _rms(ms[g].reshape(group, ts, d), gn_ref[3:4, :])


def _out_call(h2, proj3, att3, x3, mod3, g_norm, wl, wn, wo, w1, w2, *, ts, ggr_col, gl_col, gna_col):
    n_b, seq, d = x3.shape

    def modspec(k):
        return pl.BlockSpec((n_b, 1, d), lambda t: (0, 0, k))

    def resident(w):
        return pl.BlockSpec(w.shape, lambda t: (0, 0), pipeline_mode=pl.Buffered(1))

    tile = (n_b, ts, d)
    return pl.pallas_call(
        _out_kernel,
        grid=(seq // ts,),
        in_specs=[
            pl.BlockSpec((1, d // LANES, ts * n_b, LANES), lambda t: (0, 0, t, 0)),
            pl.BlockSpec((1, d // LANES, ts * n_b, LANES), lambda t: (1, 0, t, 0)),
            pl.BlockSpec(tile, lambda t: (0, t, ggr_col)),
            pl.BlockSpec(tile, lambda t: (0, t, gl_col)),
            pl.BlockSpec(tile, lambda t: (0, t, gna_col)),
            pl.BlockSpec(tile, lambda t: (0, t, 0)),
            pl.BlockSpec(tile, lambda t: (0, t, 0)),
            modspec(2), modspec(3), modspec(4), modspec(5),
            pl.BlockSpec(g_norm.shape, lambda t: (0, 0)),
            resident(wl), resident(wn), resident(wo), resident(w1), resident(w2),
        ],
        out_specs=pl.BlockSpec(tile, lambda t: (0, t, 0)),
        out_shape=jax.ShapeDtypeStruct((n_b, seq, d), F32),
        compiler_params=pltpu.CompilerParams(
            dimension_semantics=("arbitrary",), vmem_limit_bytes=VMEM_LIMIT),
        name="out",
    )(h2, h2, proj3, proj3, proj3, att3, x3, mod3, mod3, mod3, mod3, g_norm, wl, wn, wo, w1, w2)


def _gate_weights(w_rg, b_rg):
    n_dir, n_gate, n_blk, bw, _ = w_rg.shape
    per = MXU_DIM // bw
    n_col = n_blk // per
    b_rg = 0.5 * b_rg
    w = w_rg.reshape(n_dir, n_gate, n_col, per, bw, bw)
    eye = jnp.eye(per, dtype=w.dtype)
    wbd = jnp.einsum('dgcpij,pq->dgcpiqj', w, eye).reshape(n_dir, n_gate, n_col, MXU_DIM, MXU_DIM)
    wbd = jnp.concatenate([wbd[:, 0], wbd[:, 1]], axis=-1).astype(BF16)
    b = b_rg.reshape(n_dir, n_gate, n_col, MXU_DIM)
    brg = jnp.concatenate([b[:, 0], b[:, 1]], axis=-1)[:, :, None, :]
    return wbd, brg


def kernel(x, c, ctx, c_ctx, w_ada, b_ada, g_norm, w_in, b_gate, conv_w, conv_b, w_rg, b_rg, lam, rpb,
           w_lru_out, w_na_out, w_o, w_mlp1, w_mlp2):
    n_b, seq, d = x.shape
    ctx_len = ctx.shape[1]
    rows = seq // GRID_W
    depth = w_ada.shape[0]
    assert depth == 1 and rows % ATT_ROWS == 0 and n_b == SUBLANES

    l = 0
    cs = jnp.concatenate([c, c_ctx[None], jnp.zeros((2 * SUBLANES - n_b - 1, d), F32)], axis=0)
    mod = _mod_call(cs, w_ada[l], b_ada[l][None])
    mod3 = mod[:, None, :]

    w_in_bf = w_in[l].astype(BF16)
    b_gate2 = b_gate[l][None]
    g0 = g_norm[l, 0:1]

    cw, cb = 0.5 * conv_w[l], 0.5 * conv_b[l][None]
    z, ztail, proj3 = _inproj_call(x, mod3, None, g0, w_in_bf, b_gate2, cw, cb, ts=INPROJ_TS,
                                   segs=(0, 1, 2, 3, 4, 5, 6))
    z_c, ztail_c, projc3 = _inproj_call(ctx, mod3, n_b, g0, w_in_bf, b_gate2, cw, cb, ts=INPROJ_TS,
                                        segs=(0, 3, 4))

    wbd, brg = _gate_weights(w_rg[l], b_rg[l])
    lam3 = lam[l][:, None, :]
    _, h0 = _scan_call(z_c, ztail_c, wbd, brg, lam3, jnp.zeros((2, d // LANES, n_b, LANES), F32), tc=SCAN_TC)
    h, _ = _scan_call(z, ztail, wbd, brg, lam3, h0, tc=SCAN_TC)

    bias = _attn_bias(rpb[l])
    att = _attn_call(proj3.reshape(n_b * seq, -1), projc3.reshape(n_b * ctx_len, -1), bias,
                     n_b=n_b, seq=seq, ctx_len=ctx_len, q_col=1, k_col=2, v_col=3, kc_col=0, vc_col=1)

    return _out_call(h, proj3, att.reshape(n_b, seq, d), x, mod3, g_norm[l],
                     w_lru_out[l].astype(BF16), w_na_out[l].astype(BF16), w_o[l].astype(BF16),
                     w_mlp1[l].astype(BF16), w_mlp2[l].astype(BF16),
                     ts=OUT_TS, ggr_col=0, gl_col=4, gna_col=5)
```

```python
import functools

import numpy as np
import jax
import jax.numpy as jnp
from jax import lax
from jax.experimental import pallas as pl
from jax.experimental.pallas import tpu as pltpu

F32 = jnp.float32
BF16 = jnp.bfloat16

EPS = 1e-6
NEG = -1e30
LRU_C = 8.0
N_HEADS = 16
HEAD_DIM = 64
GRID_W = 64
WIN_ROWS = 8
WIN_COLS = 16
CONV_W = 4
CONV_PAD_LEFT = 2

LANES = 128
SUBLANES = 8
MXU_DIM = 256
VMEM_LIMIT = 56 * 1024 * 1024

INPROJ_TS = 64
SCAN_TC = 256
OUT_TS = 64

ATT_ROWS = 4
ATT_Q = ATT_ROWS * GRID_W
ATT_SLOTS = 3 * ATT_ROWS
INPROJ_GROUPS = 2
OUT_GROUPS = 2
FF_CHUNK = 1024
LOG2E = 1.4426950408889634


def _rms(x, g):
    return x * lax.rsqrt(jnp.mean(x * x, axis=-1, keepdims=True) + EPS) * g


def _mod_kernel(c_ref, w_ref, b_ref, o_ref):
    c = c_ref[...]
    s = c * jax.nn.sigmoid(c)
    o_ref[...] = jnp.dot(s, w_ref[...], preferred_element_type=F32) + b_ref[...]


def _mod_call(cs, w_ada, b_ada):
    rows, d = cs.shape
    n_out = w_ada.shape[1]
    return pl.pallas_call(
        _mod_kernel,
        grid=(n_out // d,),
        in_specs=[
            pl.BlockSpec((rows, d), lambda n: (0, 0)),
            pl.BlockSpec((d, d), lambda n: (0, n)),
            pl.BlockSpec((1, d), lambda n: (0, n)),
        ],
        out_specs=pl.BlockSpec((rows, d), lambda n: (0, n)),
        out_shape=jax.ShapeDtypeStruct((rows, n_out), F32),
        compiler_params=pltpu.CompilerParams(
            dimension_semantics=("arbitrary",), vmem_limit_bytes=VMEM_LIMIT),
        name="mod",
    )(cs, w_ada, b_ada)


def _inproj_kernel(x_ref, sh_ref, sc_ref, g_ref, w_ref, bg_ref, cw_ref, cb_ref, z_ref, ztail_ref, pr_ref,
                   xr_s, *, segs):
    n_b, ts, d = x_ref.shape
    hist = (CONV_W - 1) * n_b
    group = n_b // INPROJ_GROUPS
    groups = [slice(g * group, (g + 1) * group) for g in range(INPROJ_GROUPS)]

    @pl.when(pl.program_id(0) == 0)
    def _():
        xr_s[:, :hist, :] = jnp.zeros((xr_s.shape[0], hist, LANES), F32)

    def per_sample(ref, bs):
        return ref[bs] if ref.shape[0] == n_b else ref[...]

    us = []
    for bs in groups:
        u = _rms(x_ref[bs], g_ref[...]) * (1.0 + per_sample(sc_ref, bs)) + per_sample(sh_ref, bs)
        us.append(u.reshape(group * ts, d).astype(BF16))
    def conv():
        for k in range(d // LANES):
            ks = slice(k * LANES, (k + 1) * LANES)
            xe = xr_s[k]
            z = cb_ref[:, ks]
            tail = cb_ref[:, ks]
            for j in range(CONV_W):
                z = z + cw_ref[j:j + 1, ks] * xe[j * n_b:(j + ts) * n_b]
                if j < CONV_W - 1:
                    tail = tail + cw_ref[j:j + 1, ks] * xe[(ts + j) * n_b:(ts + j + 1) * n_b]
            z_ref[k] = z
            ztail_ref[k] = tail
            xr_s[k, :hist, :] = xe[ts * n_b:]

    for idx, seg in sorted(enumerate(segs), key=lambda p: p[1] == 0):
        for bs, u in zip(groups, us):
            acc = jnp.dot(u, w_ref[:, seg * d:(seg + 1) * d], preferred_element_type=F32)
            if seg == 0:
                for b in range(group):
                    for k in range(d // LANES):
                        xr_s[k, pl.ds(hist + bs.start + b, ts, stride=n_b), :] = (
                            acc[b * ts:(b + 1) * ts, k * LANES:(k + 1) * LANES])
                continue
            if seg == 1:
                val = jax.nn.gelu(acc)
            elif seg == 2:
                val = acc * (HEAD_DIM ** -0.5 * LOG2E)
            elif seg in (3, 4):
                val = acc
            else:
                val = jax.nn.sigmoid(acc + bg_ref[:, (seg - 5) * d:(seg - 4) * d])
            pr_ref[bs, :, (idx - 1) * d:idx * d] = val.astype(BF16).reshape(group, ts, d)
    conv()


def _inproj_call(x3, mod3, mod_row, g0, w_in_bf, b_gate2, conv_w, conv_b, *, ts, segs):
    n_b, seq, d = x3.shape
    assert CONV_W == 4 and CONV_PAD_LEFT == 2
    segs = tuple(segs)
    n_out = (len(segs) - 1) * d
    if mod_row is None:
        mod_blk, mod_idx = (n_b, 1, d), 0
    else:
        mod_blk, mod_idx = (1, 1, d), mod_row
    kern = functools.partial(_inproj_kernel, segs=segs)
    return pl.pallas_call(
        kern,
        grid=(seq // ts,),
        in_specs=[
            pl.BlockSpec((n_b, ts, d), lambda t: (0, t, 0)),
            pl.BlockSpec(mod_blk, lambda t: (mod_idx, 0, 0)),
            pl.BlockSpec(mod_blk, lambda t: (mod_idx, 0, 1)),
            pl.BlockSpec((1, d), lambda t: (0, 0)),
            pl.BlockSpec(w_in_bf.shape, lambda t: (0, 0), pipeline_mode=pl.Buffered(1)),
            pl.BlockSpec(b_gate2.shape, lambda t: (0, 0)),
            pl.BlockSpec(conv_w.shape, lambda t: (0, 0)),
            pl.BlockSpec(conv_b.shape, lambda t: (0, 0)),
        ],
        out_specs=[
            pl.BlockSpec((d // LANES, ts * n_b, LANES), lambda t: (0, t, 0)),
            pl.BlockSpec((d // LANES, n_b, LANES), lambda t: (0, 0, 0)),
            pl.BlockSpec((n_b, ts, n_out), lambda t: (0, t, 0)),
        ],
        out_shape=[
            jax.ShapeDtypeStruct((d // LANES, seq * n_b, LANES), F32),
            jax.ShapeDtypeStruct((d // LANES, n_b, LANES), F32),
            jax.ShapeDtypeStruct((n_b, seq, n_out), BF16),
        ],
        scratch_shapes=[pltpu.VMEM((d // LANES, (ts + CONV_W - 1) * n_b, LANES), F32)],
        compiler_params=pltpu.CompilerParams(
            dimension_semantics=("arbitrary",), vmem_limit_bytes=VMEM_LIMIT),
        name="inproj",
    )(x3, mod3, mod3, g0, w_in_bf, b_gate2, conv_w, conv_b)


def _scan_chunk(z_ref, znext_ref, ztail_ref, wbd_ref, brg_ref, lam_ref, h_ref, hfin_ref, hst, *, is_last, tc,
                reverse):
    n_lc = hst.shape[0]
    per = MXU_DIM // LANES
    for c in range(n_lc // per):
        xc2 = jnp.concatenate(
            [jnp.concatenate([z_ref[k, SUBLANES:], jnp.where(is_last, ztail_ref[k], znext_ref[k])], axis=0)
             for k in range(c * per, (c + 1) * per)], axis=1)
        sl = slice(c * MXU_DIM, (c + 1) * MXU_DIM)
        g = jnp.dot(xc2.astype(BF16), wbd_ref[0, c], preferred_element_type=F32) + brg_ref[0, c]
        r2 = jnp.tanh(g[:, :MXU_DIM]) + 1.0
        i2 = jnp.tanh(g[:, MXU_DIM:]) + 1.0
        z = -lam_ref[0, :, sl]
        softplus = jnp.maximum(z, 0.0) + jnp.log1p(jnp.exp(-jnp.abs(z)))
        c_nat = (0.5 * LRU_C) * softplus
        neg_log_a = c_nat * r2
        a = jnp.exp2((-LOG2E * c_nat) * r2)
        one_minus_a2 = jnp.tanh(neg_log_a) * (a * a + 1.0)
        root = jnp.where(one_minus_a2 > 0.0, one_minus_a2 * lax.rsqrt(one_minus_a2), 0.0)
        bb = root * (i2 * xc2)
        for kk in range(per):
            k = c * per + kk
            lanes = slice(kk * LANES, (kk + 1) * LANES)
            h = hst[k]
            for j in range(tc):
                t = tc - 1 - j if reverse else j
                rs = slice(t * SUBLANES, (t + 1) * SUBLANES)
                h = a[rs, lanes] * h + bb[rs, lanes]
                h_ref[0, k, rs, :] = h
            hst[k] = h
            hfin_ref[0, k] = h


def _scan_kernel(z_ref, znext_ref, ztail_ref, wbd_ref, brg_ref, lam_ref, h0_ref, h_ref, hfin_ref, hst,
                 *, n_chunks, tc):
    d = pl.program_id(0)
    i = pl.program_id(1)
    ci = jnp.where(d == 0, i, n_chunks - 1 - i)
    is_last = ci == n_chunks - 1

    @pl.when(i == 0)
    def _():
        hst[...] = h0_ref[d]

    for reverse in (False, True):
        @pl.when(d == int(reverse))
        def _(reverse=reverse):
            _scan_chunk(z_ref, znext_ref, ztail_ref, wbd_ref, brg_ref, lam_ref, h_ref, hfin_ref, hst,
                        is_last=is_last, tc=tc, reverse=reverse)


def _scan_call(z, ztail, wbd, brg, lam3, h0, *, tc):
    n_lc, n_rows, _ = z.shape
    n_b = SUBLANES
    seq = n_rows // n_b
    d = n_lc * LANES
    n_chunks = seq // tc
    n_col = d // MXU_DIM
    rows = tc * n_b

    def chunk(dd, i):
        return jnp.where(dd == 0, i, n_chunks - 1 - i)

    kern = functools.partial(_scan_kernel, n_chunks=n_chunks, tc=tc)
    return pl.pallas_call(
        kern,
        grid=(2, n_chunks),
        in_specs=[
            pl.BlockSpec((n_lc, rows, LANES), lambda dd, i: (0, chunk(dd, i), 0)),
            pl.BlockSpec((n_lc, n_b, LANES),
                         lambda dd, i: (0, jnp.minimum((chunk(dd, i) + 1) * tc, seq - 1), 0)),
            pl.BlockSpec((n_lc, n_b, LANES), lambda dd, i: (0, 0, 0)),
            pl.BlockSpec((1, n_col, MXU_DIM, 2 * MXU_DIM), lambda dd, i: (dd, 0, 0, 0)),
            pl.BlockSpec((1, n_col, 1, 2 * MXU_DIM), lambda dd, i: (dd, 0, 0, 0)),
            pl.BlockSpec((1, 1, d), lambda dd, i: (dd, 0, 0)),
            pl.BlockSpec((2, n_lc, n_b, LANES), lambda dd, i: (0, 0, 0, 0)),
        ],
        out_specs=[
            pl.BlockSpec((1, n_lc, rows, LANES), lambda dd, i: (dd, 0, chunk(dd, i), 0)),
            pl.BlockSpec((1, n_lc, n_b, LANES), lambda dd, i: (dd, 0, 0, 0)),
        ],
        out_shape=[
            jax.ShapeDtypeStruct((2, n_lc, n_rows, LANES), F32),
            jax.ShapeDtypeStruct((2, n_lc, n_b, LANES), F32),
        ],
        scratch_shapes=[pltpu.VMEM((n_lc, n_b, LANES), F32)],
        compiler_params=pltpu.CompilerParams(
            dimension_semantics=("arbitrary", "arbitrary"), vmem_limit_bytes=VMEM_LIMIT),
        name="scan",
    )(z, z, ztail, wbd, brg, lam3, h0)


def _window_plan(i, n_i, rows):
    wr = min(WIN_ROWS, rows)
    rng = []
    for rq in range(ATT_ROWS):
        r = ATT_ROWS * i + rq
        r0 = min(max(r - wr // 2, 0), rows - wr)
        lo = r0 - (ATT_ROWS * i - ATT_ROWS)
        rng.append((lo, lo + wr))
    s_lo = min(lo for lo, _ in rng)
    s_hi = max(hi for _, hi in rng)
    assert 0 <= s_lo and s_hi <= ATT_SLOTS
    rows_per_tile = LANES // GRID_W
    tiles = []
    for j in range(ATT_ROWS // rows_per_tile):
        rqs = range(j * rows_per_tile, (j + 1) * rows_per_tile)
        a = min(rng[rq][0] for rq in rqs)
        b = max(rng[rq][1] for rq in rqs)
        half = {}
        for s in range(a, b):
            ok = tuple(rng[rq][0] <= s < rng[rq][1] for rq in rqs)
            if not all(ok):
                half[s - s_lo] = ok
        tiles.append((a - s_lo, b - s_lo, half))
    return s_lo, s_hi, tiles


def _attn_block(q_ref, k_refs, v_refs, kc_ref, vc_ref, bias_ref, o_ref, s_buf, plan):
    s_lo, s_hi, tiles = plan
    n_slots = s_hi - s_lo
    n_loc = n_slots * GRID_W
    n_ctx = kc_ref.shape[0]
    lane = lax.broadcasted_iota(jnp.int32, (1, LANES), 1)
    lo = lane < HEAD_DIM
    heads_per_pair = LANES // HEAD_DIM
    n_pairs = N_HEADS // heads_per_pair
    dims_nt = (((1,), (1,)), ((), ()))
    dims_tn = (((0,), (0,)), ((), ()))

    def keys_of(refs, c_ref, ls):
        parts = []
        for blk, ref in enumerate(refs):
            a = max(s_lo, blk * ATT_ROWS) - blk * ATT_ROWS
            b = min(s_hi, (blk + 1) * ATT_ROWS) - blk * ATT_ROWS
            if b > a:
                parts.append(ref[a * GRID_W:b * GRID_W, ls])
        return jnp.concatenate(parts + [c_ref[:, ls]], axis=0)

    def scores(p):
        ls = slice(p * LANES, (p + 1) * LANES)
        qp = q_ref[:, ls]
        k_all = keys_of(k_refs, kc_ref, ls)
        for hh in range(heads_per_pair):
            msk = lo if hh == 0 else jnp.logical_not(lo)
            qm = jnp.where(msk, qp, jnp.zeros_like(qp))
            s_buf[p % 2, hh, :n_loc + n_ctx] = lax.dot_general(k_all, qm, dims_nt, preferred_element_type=F32)

    def softmax_pv(p):
        ls = slice(p * LANES, (p + 1) * LANES)
        v_all = keys_of(v_refs, vc_ref, ls)
        o_t = []
        for hh in range(heads_per_pair):
            own = lo if hh == 0 else jnp.logical_not(lo)
            head = p * heads_per_pair + hh
            cols = []
            for j, (a, b, half) in enumerate(tiles):
                qs = slice(j * LANES, (j + 1) * LANES)
                blocks = []
                for s in range(a, b):
                    blk = (s_buf[p % 2, hh, s * GRID_W:(s + 1) * GRID_W, qs]
                           + bias_ref[head, (s_lo + s) * GRID_W:(s_lo + s + 1) * GRID_W, qs])
                    if s in half:
                        blk = jnp.where(lo if half[s][0] else jnp.logical_not(lo), blk, NEG)
                    blocks.append(blk)
                s_loc = jnp.concatenate(blocks, axis=0)
                s_ctx = s_buf[p % 2, hh, n_loc:n_loc + n_ctx, qs]
                m = jnp.maximum(jnp.max(s_loc, axis=0, keepdims=True), jnp.max(s_ctx, axis=0, keepdims=True))
                col = [jnp.exp2(s_loc - m), jnp.exp2(s_ctx - m)]
                if a > 0:
                    col.insert(0, jnp.zeros((a * GRID_W, LANES), F32))
                if b < n_slots:
                    col.insert(-1, jnp.zeros(((n_slots - b) * GRID_W, LANES), F32))
                cols.append(jnp.concatenate(col, axis=0))
            p_all = jnp.concatenate(cols, axis=1).astype(BF16)
            v_own = jnp.where(own, v_all, jnp.ones_like(v_all))
            o = lax.dot_general(v_own, p_all, dims_tn, preferred_element_type=F32)
            den = o[(1 - hh) * HEAD_DIM:(1 - hh) * HEAD_DIM + 1]
            o_t.append(o[hh * HEAD_DIM:(hh + 1) * HEAD_DIM] * (1.0 / den))
        o_ref[:, ls] = jnp.concatenate(o_t, axis=0).T.astype(BF16)

    scores(0)
    for p in range(n_pairs):
        if p + 1 < n_pairs:
            scores(p + 1)
        softmax_pv(p)


def _attn_kernel(q_ref, kp_ref, km_ref, kn_ref, vp_ref, vm_ref, vn_ref, kc_ref, vc_ref, bias_ref, o_ref, s_buf,
                 *, plans):
    i = pl.program_id(0)
    n_i = pl.num_programs(0)
    variant = (i > 0).astype(jnp.int32) + (i == n_i - 1).astype(jnp.int32)
    for v, plan in enumerate(plans):
        @pl.when(variant == v)
        def _(plan=plan):
            _attn_block(q_ref, (kp_ref, km_ref, kn_ref), (vp_ref, vm_ref, vn_ref), kc_ref, vc_ref,
                        bias_ref, o_ref, s_buf, plan)


def _attn_call(proj, projc, bias, *, n_b, seq, ctx_len, q_col, k_col, v_col, kc_col, vc_col):
    d = N_HEADS * HEAD_DIM
    n_i = seq // ATT_Q
    rows = seq // GRID_W
    plans = tuple(_window_plan(i, n_i, rows) for i in (0, 1, n_i - 1))

    def row(i, b):
        return b * n_i + i

    def row_prev(i, b):
        return b * n_i + jnp.maximum(i - 1, 0)

    def row_next(i, b):
        return b * n_i + jnp.minimum(i + 1, n_i - 1)

    blk = (ATT_Q, d)
    return pl.pallas_call(
        functools.partial(_attn_kernel, plans=plans),
        grid=(n_i, n_b),
        in_specs=[
            pl.BlockSpec(blk, lambda i, b: (row(i, b), q_col)),
            pl.BlockSpec(blk, lambda i, b: (row_prev(i, b), k_col)),
            pl.BlockSpec(blk, lambda i, b: (row(i, b), k_col)),
            pl.BlockSpec(blk, lambda i, b: (row_next(i, b), k_col)),
            pl.BlockSpec(blk, lambda i, b: (row_prev(i, b), v_col)),
            pl.BlockSpec(blk, lambda i, b: (row(i, b), v_col)),
            pl.BlockSpec(blk, lambda i, b: (row_next(i, b), v_col)),
            pl.BlockSpec((ctx_len, d), lambda i, b: (b, kc_col)),
            pl.BlockSpec((ctx_len, d), lambda i, b: (b, vc_col)),
            pl.BlockSpec(bias.shape, lambda i, b: (0, 0, 0), pipeline_mode=pl.Buffered(1)),
        ],
        out_specs=pl.BlockSpec(blk, lambda i, b: (row(i, b), 0)),
        out_shape=jax.ShapeDtypeStruct((n_b * seq, d), BF16),
        scratch_shapes=[pltpu.VMEM((2, LANES // HEAD_DIM, ATT_SLOTS * GRID_W + ctx_len, ATT_Q), F32)],
        compiler_params=pltpu.CompilerParams(
            dimension_semantics=("arbitrary", "arbitrary"), vmem_limit_bytes=VMEM_LIMIT),
        name="attn",
    )(proj, proj, proj, proj, proj, proj, proj, projc, projc, bias)


def _attn_bias(rpb):
    qc = np.arange(GRID_W)[None, :]
    kc = np.arange(GRID_W)[:, None]
    col_start = np.clip(qc - WIN_COLS // 2, 0, GRID_W - WIN_COLS)
    col_ok = (kc >= col_start) & (kc < col_start + WIN_COLS)
    dc_idx = np.clip(kc - qc + WIN_COLS - 1, 0, 2 * WIN_COLS - 2)
    onehot = (dc_idx[None] == np.arange(2 * WIN_COLS - 1)[:, None, None]) & col_ok[None]
    col_tab = jnp.einsum('had,dkq->hakq', rpb.astype(F32), jnp.asarray(onehot, F32),
                         precision=lax.Precision.HIGHEST)
    col_tab = jnp.where(jnp.asarray(col_ok)[None, None], col_tab * LOG2E, NEG)
    slots = []
    for s in range(ATT_SLOTS):
        q_rows = [col_tab[:, s - ATT_ROWS - rq + WIN_ROWS - 1] for rq in range(ATT_ROWS)]
        slots.append(jnp.concatenate(q_rows, axis=-1))
    return jnp.concatenate(slots, axis=1)


def _out_kernel(hf_ref, hr_ref, ggr_ref, gl_ref, gna_ref, att_ref, x_ref, gt1_ref, sh2_ref, sc2_ref,
                gt2_ref, gn_ref, wl_ref, wn_ref, wo_ref, w1_ref, w2_ref, out_ref):
    n_b, ts, d = x_ref.shape
    group = n_b // OUT_GROUPS
    rows = group * ts
    groups = [slice(g * group, (g + 1) * group) for g in range(OUT_GROUPS)]

    y_na, y_lru = [], []
    for bs in groups:
        y_na.append(jnp.dot(att_ref[bs].reshape(rows, d), wn_ref[...], preferred_element_type=F32))
        hl = jnp.stack([
            jnp.concatenate([hf_ref[0, k, pl.ds(b, ts, stride=n_b), :] + hr_ref[0, k, pl.ds(b, ts, stride=n_b), :]
                             for k in range(d // LANES)], axis=1)
            for b in range(bs.start, bs.stop)], axis=0)
        hl = hl * ggr_ref[bs].astype(F32)
        y_lru.append(jnp.dot(hl.reshape(rows, d).astype(BF16), wl_ref[...], preferred_element_type=F32))

    ys = []
    for g, bs in enumerate(groups):
        y = (gl_ref[bs].astype(F32).reshape(rows, d) * y_lru[g]
             + gna_ref[bs].astype(F32).reshape(rows, d) * y_na[g])
        ys.append(jnp.dot(y.astype(BF16), wo_ref[...], preferred_element_type=F32).reshape(group, ts, d))

    x1s, u2s = [], []
    for g, bs in enumerate(groups):
        x1 = x_ref[bs] + gt1_ref[bs] * _rms(ys[g], gn_ref[1:2, :])
        u2 = _rms(x1, gn_ref[2:3, :]) * (1.0 + sc2_ref[bs]) + sh2_ref[bs]
        x1s.append(x1)
        u2s.append(u2.reshape(rows, d).astype(BF16))

    ms = [None] * OUT_GROUPS
    for f in range(0, w1_ref.shape[1], FF_CHUNK):
        h1s = []
        for g in range(OUT_GROUPS):
            h1 = jnp.dot(u2s[g], w1_ref[:, f:f + FF_CHUNK], preferred_element_type=F32)
            h1 = jnp.maximum(h1, 0.0)
            h1s.append((h1 * h1).astype(BF16))
        for g in range(OUT_GROUPS):
            part = jnp.dot(h1s[g], w2_ref[f:f + FF_CHUNK, :], preferred_element_type=F32)
            ms[g] = part if ms[g] is None else ms[g] + part

    for g, bs in enumerate(groups):
        out_ref[bs] = x1s[g] + gt2_ref[bs] * _rms(ms[g].reshape(group, ts, d), gn_ref[3:4, :])


def _out_call(h2, proj3, att3, x3, mod3, g_norm, wl, wn, wo, w1, w2, *, ts, ggr_col, gl_col, gna_col):
    n_b, seq, d = x3.shape

    def modspec(k):
        return pl.BlockSpec((n_b, 1, d), lambda t: (0, 0, k))

    def resident(w):
        return pl.BlockSpec(w.shape, lambda t: (0, 0), pipeline_mode=pl.Buffered(1))

    tile = (n_b, ts, d)
    return pl.pallas_call(
        _out_kernel,
        grid=(seq // ts,),
        in_specs=[
            pl.BlockSpec((1, d // LANES, ts * n_b, LANES), lambda t: (0, 0, t, 0)),
            pl.BlockSpec((1, d // LANES, ts * n_b, LANES), lambda t: (1, 0, t, 0)),
            pl.BlockSpec(tile, lambda t: (0, t, ggr_col)),
            pl.BlockSpec(tile, lambda t: (0, t, gl_col)),
            pl.BlockSpec(tile, lambda t: (0, t, gna_col)),
            pl.BlockSpec(tile, lambda t: (0, t, 0)),
            pl.BlockSpec(tile, lambda t: (0, t, 0)),
            modspec(2), modspec(3), modspec(4), modspec(5),
            pl.BlockSpec(g_norm.shape, lambda t: (0, 0)),
            resident(wl), resident(wn), resident(wo), resident(w1), resident(w2),
        ],
        out_specs=pl.BlockSpec(tile, lambda t: (0, t, 0)),
        out_shape=jax.ShapeDtypeStruct((n_b, seq, d), F32),
        compiler_params=pltpu.CompilerParams(
            dimension_semantics=("arbitrary",), vmem_limit_bytes=VMEM_LIMIT),
        name="out",
    )(h2, h2, proj3, proj3, proj3, att3, x3, mod3, mod3, mod3, mod3, g_norm, wl, wn, wo, w1, w2)


def _gate_weights(w_rg, b_rg):
    n_dir, n_gate, n_blk, bw, _ = w_rg.shape
    per = MXU_DIM // bw
    n_col = n_blk // per
    b_rg = 0.5 * b_rg
    w = jnp.tile(w_rg.reshape(n_dir, n_gate, n_col, MXU_DIM, bw), (1, 1, 1, 1, per))
    blk_r = lax.broadcasted_iota(jnp.int32, (MXU_DIM, MXU_DIM), 0) // bw
    blk_c = lax.broadcasted_iota(jnp.int32, (MXU_DIM, MXU_DIM), 1) // bw
    wbd = jnp.where(blk_r == blk_c, w, 0.0)
    wbd = jnp.concatenate([wbd[:, 0], wbd[:, 1]], axis=-1).astype(BF16)
    b = b_rg.reshape(n_dir, n_gate, n_col, MXU_DIM)
    brg = jnp.concatenate([b[:, 0], b[:, 1]], axis=-1)[:, :, None, :]
    return wbd, brg


def kernel(x, c, ctx, c_ctx, w_ada, b_ada, g_norm, w_in, b_gate, conv_w, conv_b, w_rg, b_rg, lam, rpb,
           w_lru_out, w_na_out, w_o, w_mlp1, w_mlp2):
    n_b, seq, d = x.shape
    ctx_len = ctx.shape[1]
    rows = seq // GRID_W
    depth = w_ada.shape[0]
    assert depth == 1 and rows % ATT_ROWS == 0 and n_b == SUBLANES

    l = 0
    cs = jnp.concatenate([c, c_ctx[None], jnp.zeros((2 * SUBLANES - n_b - 1, d), F32)], axis=0)
    mod = _mod_call(cs, w_ada[l], b_ada[l][None])
    mod3 = mod[:, None, :]

    w_in_bf = w_in[l].astype(BF16)
    b_gate2 = b_gate[l][None]
    g0 = g_norm[l, 0:1]

    cw, cb = 0.5 * conv_w[l], 0.5 * conv_b[l][None]
    z, ztail, proj3 = _inproj_call(x, mod3, None, g0, w_in_bf, b_gate2, cw, cb, ts=INPROJ_TS,
                                   segs=(0, 1, 2, 3, 4, 5, 6))
    z_c, ztail_c, projc3 = _inproj_call(ctx, mod3, n_b, g0, w_in_bf, b_gate2, cw, cb, ts=INPROJ_TS,
                                        segs=(0, 3, 4))

    wbd, brg = _gate_weights(w_rg[l], b_rg[l])
    lam3 = lam[l][:, None, :]
    _, h0 = _scan_call(z_c, ztail_c, wbd, brg, lam3, jnp.zeros((2, d // LANES, n_b, LANES), F32), tc=SCAN_TC)
    h, _ = _scan_call(z, ztail, wbd, brg, lam3, h0, tc=SCAN_TC)

    bias = _attn_bias(rpb[l])
    att = _attn_call(proj3.reshape(n_b * seq, -1), projc3.reshape(n_b * ctx_len, -1), bias,
                     n_b=n_b, seq=seq, ctx_len=ctx_len, q_col=1, k_col=2, v_col=3, kc_col=0, vc_col=1)

    return _out_call(h, proj3, att.reshape(n_b, seq, d), x, mod3, g_norm[l],
                     w_lru_out[l].astype(BF16), w_na_out[l].astype(BF16), w_o[l].astype(BF16),
                     w_mlp1[l].astype(BF16), w_mlp2[l].astype(BF16),
                     ts=OUT_TS, ggr_col=0, gl_col=4, gna_col=5)
```
